```python
import jax, jax.numpy as jnp
from jax import lax
import numpy as np


D_MODEL = 2048
BATCH = 4
SEQ = 2048
DEPTH = 4
DEC_BATCH = 128
DEC_SEQ = 4
PAST_LEN = 16384
PAGE_SIZE = 128

HG_HEADS = 8
HG_DK = 128
HG_DV = 128
HG_WIDTH = HG_HEADS * HG_DK
HG_VWIDTH = HG_HEADS * HG_DV
POOL_WINDOWS = (2, 4, 8, 16)
POOL_GROUPS = len(POOL_WINDOWS)
POOL_GC = 128
POOL_WIDTH = POOL_GROUPS * POOL_GC
POOL_BUF = max(POOL_WINDOWS) - 1
XA_HEADS = 4
XA_DH = 128
XA_WIDTH = XA_HEADS * XA_DH
N_MEM = 256
N_BRANCH = 3
MIX_WIDTH = HG_VWIDTH + POOL_WIDTH + XA_WIDTH
D_FF = 5632
CHUNK = 64
EPS = 1e-6

IN_SIZES = (HG_WIDTH, HG_WIDTH, HG_VWIDTH, HG_VWIDTH, POOL_WIDTH, XA_WIDTH, N_BRANCH * D_MODEL)
IN_WIDTH = int(sum(IN_SIZES))
IN_OFFSETS = [int(v) for v in np.cumsum(IN_SIZES)[:-1]]
BR_OFFSETS = [HG_VWIDTH, HG_VWIDTH + POOL_WIDTH]

kernel_name = "hgrn2_pool_memxattn_macaron_gated_trunk_step"


def rmsnorm(x, g):
    xf = x.astype(jnp.float32)
    y = xf * lax.rsqrt(jnp.mean(xf * xf, axis=-1, keepdims=True) + EPS)
    return (y * g.astype(jnp.float32)).astype(x.dtype)


def swiglu(x, w1, w3, w2):
    return (jax.nn.silu(x @ w1) * (x @ w3)) @ w2


def hgrn_scan(q, k, v, logf, S0):
    B, T, H, _ = q.shape
    C = CHUNK if T % CHUNK == 0 else T
    nc = T // C
    f32 = jnp.float32

    def to_chunks(a):
        return jnp.moveaxis(a.astype(f32).reshape(B, nc, C, *a.shape[2:]), 1, 0)

    mask = np.tril(np.ones((C, C), dtype=bool))[None, :, :, None, None]

    def step(S, inp):
        qc, kc, vc, gc = inp
        b = jnp.cumsum(gc, axis=1)
        diff = b[:, :, None] - b[:, None, :]
        dec = jnp.exp(jnp.where(mask, diff, -jnp.inf))
        A = jnp.einsum('bthk,bjhk,btjhk->bhtj', qc, kc, dec)
        o = (jnp.einsum('bhtj,bjhv->bthv', A, vc)
             + jnp.einsum('bthk,bhkv->bthv', qc * jnp.exp(b), S))
        bl = b[:, -1]
        kd = kc * jnp.exp(bl[:, None] - b)
        S = S * jnp.exp(bl)[..., None] + jnp.einsum('bjhk,bjhv->bhkv', kd, vc)
        return S, o

    S, o = lax.scan(step, S0.astype(f32), (to_chunks(q), to_chunks(k), to_chunks(v), to_chunks(logf)))
    o = jnp.moveaxis(o, 0, 1).reshape(B, T, H, v.shape[-1])
    return o, S


def pool_mix(u, buf, w_pool_l, scale_l):
    B, T, _ = u.shape
    P = buf.shape[1]
    uu = jnp.concatenate([buf.astype(u.dtype), u], axis=1)
    cs = jnp.pad(jnp.cumsum(uu.astype(jnp.float32), axis=1), ((0, 0), (1, 0), (0, 0)))
    n = P + np.arange(T)
    outs = []
    for g, w in enumerate(POOL_WINDOWS):
        lo = np.maximum(n + 1 - w, 0)
        cnt = (n + 1 - lo).astype(np.float32)
        sl = slice(g * POOL_GC, (g + 1) * POOL_GC)
        s = cs[:, n + 1, sl] - cs[:, lo, sl]
        outs.append(s / cnt[None, :, None] - u[..., sl].astype(jnp.float32))
    d = jnp.stack(outs, axis=2)
    y = jnp.einsum('btgc,gcd->btgd', d, w_pool_l.astype(jnp.float32)).reshape(B, T, POOL_WIDTH)
    y = y * scale_l.astype(jnp.float32)
    return y.astype(u.dtype), uu[:, -POOL_BUF:]


def mixer(h, S0, buf, mk, mv, w_in_l, lb_l, hg_norm_l, w_pool_l, pool_scale_l, w_branch_l, w_out_l):
    B, T, _ = h.shape
    z = h @ w_in_l
    zq, zf, zi, zog, zu, zx, zg = jnp.split(z, IN_OFFSETS, axis=-1)
    q = jax.nn.silu(zq).reshape(B, T, HG_HEADS, HG_DK)
    lb = lb_l.reshape(HG_HEADS, HG_DK)
    fg = lb + (1.0 - lb) * jax.nn.sigmoid(zf.astype(jnp.float32).reshape(B, T, HG_HEADS, HG_DK))
    k = 1.0 - fg
    logf = jnp.log(fg)
    v = zi.reshape(B, T, HG_HEADS, HG_DV)
    oh, S = hgrn_scan(q, k, v, logf, S0)
    oh = oh * lax.rsqrt(jnp.mean(oh * oh, axis=-1, keepdims=True) + EPS)
    oh = oh.reshape(B, T, HG_VWIDTH) * hg_norm_l.astype(jnp.float32)
    oh = (oh * jax.nn.silu(zog.astype(jnp.float32))).astype(h.dtype)
    op, new_buf = pool_mix(zu, buf, w_pool_l, pool_scale_l)
    xq = zx.reshape(B, T, XA_HEADS, XA_DH)
    s = jnp.einsum('bthd,bmhd->bhtm', xq, mk).astype(jnp.float32) * (XA_DH ** -0.5)
    p = jax.nn.softmax(s, axis=-1).astype(h.dtype)
    ox = jnp.einsum('bhtm,bmhd->bthd', p, mv).reshape(B, T, XA_WIDTH)
    gates = jax.nn.sigmoid(zg.astype(jnp.float32)).reshape(B, T, N_BRANCH, D_MODEL)
    wb_h, wb_p, wb_x = jnp.split(w_branch_l, BR_OFFSETS, axis=0)
    y = (gates[:, :, 0] * (oh @ wb_h).astype(jnp.float32)
         + gates[:, :, 1] * (op @ wb_p).astype(jnp.float32)
         + gates[:, :, 2] * (ox @ wb_x).astype(jnp.float32))
    return y.astype(h.dtype) @ w_out_l, S, new_buf


def setup_inputs(seed: int = 0) -> dict:
    key = jax.random.key(seed)
    ks = iter(jax.random.split(key, 40))
    nrm = lambda shape, s: jax.random.normal(next(ks), shape, jnp.float32) * s
    gain = lambda shape: 1.0 + 0.02 * jax.random.normal(next(ks), shape, jnp.float32)
    w_branch = jnp.concatenate([
        nrm((DEPTH, HG_VWIDTH, D_MODEL), HG_VWIDTH ** -0.5),
        nrm((DEPTH, POOL_WIDTH, D_MODEL), POOL_WIDTH ** -0.5),
        nrm((DEPTH, XA_WIDTH, D_MODEL), XA_WIDTH ** -0.5)], axis=1)
    return {
        "x_prompt": nrm((BATCH, SEQ, D_MODEL), 1.0),
        "x_sample": nrm((DEC_BATCH, DEC_SEQ, D_MODEL), 1.0),
        "state_hgrn": nrm((DEPTH, DEC_BATCH, HG_HEADS, HG_DK, HG_DV), 0.3),
        "state_pool": nrm((DEPTH, DEC_BATCH, POOL_BUF, POOL_WIDTH), 1.0),
        "cache_mem_k": nrm((DEPTH, DEC_BATCH, N_MEM, XA_HEADS, XA_DH), 1.0),
        "cache_mem_v": nrm((DEPTH, DEC_BATCH, N_MEM, XA_HEADS, XA_DH), 1.0),
        "mem_prompt": nrm((BATCH, N_MEM, D_MODEL), 1.0),
        "ffn1_norm": gain((DEPTH, D_MODEL)),
        "ffn1_w1": nrm((DEPTH, D_MODEL, D_FF), D_MODEL ** -0.5),
        "ffn1_w3": nrm((DEPTH, D_MODEL, D_FF), D_MODEL ** -0.5),
        "ffn1_w2": nrm((DEPTH, D_FF, D_MODEL), D_FF ** -0.5),
        "mix_norm": gain((DEPTH, D_MODEL)),
        "w_in": nrm((DEPTH, D_MODEL, IN_WIDTH), D_MODEL ** -0.5),
        "lb_logits": 1.0 + nrm((DEPTH, HG_WIDTH), 0.1),
        "hg_norm": gain((DEPTH, HG_VWIDTH)),
        "w_pool": nrm((DEPTH, POOL_GROUPS, POOL_GC, POOL_GC), POOL_GC ** -0.5),
        "pool_scale": 1.0 + nrm((DEPTH, POOL_WIDTH), 0.1),
        "mem_norm": gain((DEPTH, D_MODEL)),
        "w_mk": nrm((DEPTH, D_MODEL, XA_WIDTH), D_MODEL ** -0.5),
        "w_mv": nrm((DEPTH, D_MODEL, XA_WIDTH), D_MODEL ** -0.5),
        "w_branch": w_branch,
        "w_out": nrm((DEPTH, D_MODEL, D_MODEL), 0.5 * D_MODEL ** -0.5),
        "ffn2_norm": gain((DEPTH, D_MODEL)),
        "ffn2_w1": nrm((DEPTH, D_MODEL, D_FF), D_MODEL ** -0.5),
        "ffn2_w3": nrm((DEPTH, D_MODEL, D_FF), D_MODEL ** -0.5),
        "ffn2_w2": nrm((DEPTH, D_FF, D_MODEL), D_FF ** -0.5),
        "final_norm": gain((D_MODEL,)),
    }


def reference(x_prompt, x_sample, state_hgrn, state_pool, cache_mem_k, cache_mem_v, mem_prompt,
              ffn1_norm, ffn1_w1, ffn1_w3, ffn1_w2, mix_norm, w_in, lb_logits, hg_norm, w_pool,
              pool_scale, mem_norm, w_mk, w_mv, w_branch, w_out, ffn2_norm, ffn2_w1, ffn2_w3,
              ffn2_w2, final_norm):
    lbc = jnp.cumsum(jax.nn.softmax(lb_logits.astype(jnp.float32), axis=0), axis=0)
    lbs = lbc - lbc[0:1]

    xp, xs = x_prompt, x_sample
    Bp = x_prompt.shape[0]
    S0p = jnp.zeros((Bp, HG_HEADS, HG_DK, HG_DV), jnp.float32)
    buf0p = jnp.zeros((Bp, 0, POOL_WIDTH), x_prompt.dtype)
    hs_p, pb_p, mk_p, mv_p, hs_s, pb_s = [], [], [], [], [], []
    for l in range(DEPTH):
        mem_n = rmsnorm(mem_prompt, mem_norm[l])
        mk = (mem_n @ w_mk[l]).reshape(Bp, N_MEM, XA_HEADS, XA_DH)
        mv = (mem_n @ w_mv[l]).reshape(Bp, N_MEM, XA_HEADS, XA_DH)
        mix_w = (w_in[l], lbs[l], hg_norm[l], w_pool[l], pool_scale[l], w_branch[l], w_out[l])

        xp = xp + 0.5 * swiglu(rmsnorm(xp, ffn1_norm[l]), ffn1_w1[l], ffn1_w3[l], ffn1_w2[l])
        xs = xs + 0.5 * swiglu(rmsnorm(xs, ffn1_norm[l]), ffn1_w1[l], ffn1_w3[l], ffn1_w2[l])

        op, Sp, bp = mixer(rmsnorm(xp, mix_norm[l]), S0p, buf0p, mk, mv, *mix_w)
        os_, Ss, bs = mixer(rmsnorm(xs, mix_norm[l]), state_hgrn[l], state_pool[l],
                            cache_mem_k[l], cache_mem_v[l], *mix_w)
        xp = xp + op
        xs = xs + os_

        xp = xp + 0.5 * swiglu(rmsnorm(xp, ffn2_norm[l]), ffn2_w1[l], ffn2_w3[l], ffn2_w2[l])
        xs = xs + 0.5 * swiglu(rmsnorm(xs, ffn2_norm[l]), ffn2_w1[l], ffn2_w3[l], ffn2_w2[l])

        hs_p.append(Sp.astype(state_hgrn.dtype))
        pb_p.append(bp.astype(state_pool.dtype))
        mk_p.append(mk.astype(cache_mem_k.dtype))
        mv_p.append(mv.astype(cache_mem_v.dtype))
        hs_s.append(Ss.astype(state_hgrn.dtype))
        pb_s.append(bs.astype(state_pool.dtype))

    y_prompt = rmsnorm(xp, final_norm)
    y_sample = rmsnorm(xs, final_norm)
    return (y_prompt, y_sample, jnp.stack(hs_p), jnp.stack(pb_p), jnp.stack(mk_p), jnp.stack(mv_p),
            jnp.stack(hs_s), jnp.stack(pb_s))
```

```python
import functools

import jax
import jax.numpy as jnp
from jax import lax
from jax.experimental import pallas as pl
from jax.experimental.pallas import tpu as pltpu

F32 = jnp.float32
BF16 = jnp.bfloat16
EPS = 1e-6

LANE = 128
HG_HEADS = 8
POOL_WINDOWS = (2, 4, 8, 16)
POOL_BUF = max(POOL_WINDOWS) - 1
XA_HEADS = 4
N_BRANCH = 3

COL_Q, COL_F, COL_I, COL_OG = 0, HG_HEADS, 2 * HG_HEADS, 3 * HG_HEADS
COL_U = 4 * HG_HEADS
COL_X = COL_U + len(POOL_WINDOWS)
COL_G = COL_X + XA_HEADS

HGRN_CHUNK = 128
DEC_GROUP = 4


def _tile(n, preferred):
    t = preferred
    while n % t:
        t -= LANE
    return t


def _params(semantics, vmem_mib):
    return pltpu.CompilerParams(dimension_semantics=semantics, vmem_limit_bytes=vmem_mib * 1024 * 1024)


def _rms(x, g):
    return x * lax.rsqrt(jnp.mean(x * x, axis=-1, keepdims=True) + EPS) * g


def _silu(x):
    return x * jax.nn.sigmoid(x)


def _dot(a, b):
    return jnp.dot(a, b, preferred_element_type=F32)


def _dot_nt(a, b):
    return lax.dot_general(a, b, (((1,), (1,)), ((), ())), preferred_element_type=F32)


def _dot_tn(a, b):
    return lax.dot_general(a, b, (((0,), (0,)), ((), ())), preferred_element_type=F32)


def _ffn_kernel(x_ref, g_ref, w1_ref, w3_ref, w2_ref, o_ref, xn_ref):
    @pl.when(pl.program_id(1) == 0)
    def _():
        x = x_ref[...]
        xn_ref[...] = _rms(x, g_ref[...]).astype(BF16)
        o_ref[...] = x

    xn = xn_ref[...]
    h = _silu(_dot(xn, w1_ref[...])) * _dot(xn, w3_ref[...])
    o_ref[...] += 0.5 * _dot(h.astype(BF16), w2_ref[...])


def _ffn(x, g3, w1, w3, w2, layer, tm, tf):
    T, D = x.shape
    F = w1.shape[-1]
    return pl.pallas_call(
        _ffn_kernel,
        grid=(T // tm, F // tf),
        in_specs=[
            pl.BlockSpec((tm, D), lambda i, j: (i, 0)),
            pl.BlockSpec((None, 1, D), lambda i, j: (layer, 0, 0)),
            pl.BlockSpec((None, D, tf), lambda i, j: (layer, 0, j)),
            pl.BlockSpec((None, D, tf), lambda i, j: (layer, 0, j)),
            pl.BlockSpec((None, tf, D), lambda i, j: (layer, j, 0)),
        ],
        out_specs=pl.BlockSpec((tm, D), lambda i, j: (i, 0)),
        out_shape=jax.ShapeDtypeStruct((T, D), F32),
        scratch_shapes=[pltpu.VMEM((tm, D), BF16)],
        compiler_params=_params(("parallel", "arbitrary"), 48),
        name="ffn",
    )(x, g3, w1, w3, w2)


def _rms_matmul_kernel(x_ref, g_ref, w_ref, o_ref, xn_ref):
    @pl.when(pl.program_id(1) == 0)
    def _():
        xn_ref[...] = _rms(x_ref[...], g_ref[...]).astype(BF16)

    o_ref[...] = _dot(xn_ref[...], w_ref[...])


def _rms_matmul(x, g3, w, layer, tm, tn):
    T, D = x.shape
    N = w.shape[-1]
    return pl.pallas_call(
        _rms_matmul_kernel,
        grid=(T // tm, N // tn),
        in_specs=[
            pl.BlockSpec((tm, D), lambda i, j: (i, 0)),
            pl.BlockSpec((None, 1, D), lambda i, j: (layer, 0, 0)),
            pl.BlockSpec((None, D, tn), lambda i, j: (layer, 0, j)),
        ],
        out_specs=pl.BlockSpec((tm, tn), lambda i, j: (i, j)),
        out_shape=jax.ShapeDtypeStruct((T, N), F32),
        scratch_shapes=[pltpu.VMEM((tm, D), BF16)],
        compiler_params=_params(("parallel", "arbitrary"), 48),
        name="rms_matmul",
    )(x, g3, w)


def _lower_bound_kernel(lg_ref, o_ref):
    lg = lg_ref[...]
    depth = lg.shape[0]
    rows = [lg[i:i + 1, :] for i in range(depth)]
    m = rows[0]
    for r in rows[1:]:
        m = jnp.maximum(m, r)
    e = [jnp.exp(r - m) for r in rows]
    tot = e[0]
    for v in e[1:]:
        tot = tot + v
    c = e[0] / tot
    first = c
    o_ref[0:1, :] = c - first
    for i in range(1, depth):
        c = c + e[i] / tot
        o_ref[i:i + 1, :] = c - first


def _lower_bounds(lb_logits):
    return pl.pallas_call(
        _lower_bound_kernel,
        out_shape=jax.ShapeDtypeStruct(lb_logits.shape, F32),
        name="hgrn_lower_bounds",
    )(lb_logits)


def _group_ref_row(b, s):
    rows, width = b.shape
    gsz = 2 * s
    if gsz >= 8:
        parts = [jnp.broadcast_to(b[i * gsz + s - 1:i * gsz + s, :], (gsz, width)) for i in range(rows // gsz)]
        return parts[0] if len(parts) == 1 else jnp.concatenate(parts, axis=0)
    pos = lax.broadcasted_iota(jnp.int32, b.shape, 0) & (gsz - 1)
    out = b
    for p in range(gsz):
        d = p - (s - 1)
        if d != 0:
            out = jnp.where(pos == p, pltpu.roll(b, d % rows, 0), out)
    return out


def _hgrn_block(zq, zf, zi, lb, states, seg):
    C = zq.shape[0]
    q = _silu(zq)
    fg = lb + (1.0 - lb) * jax.nn.sigmoid(zf)
    kk = 1.0 - fg
    g = jnp.log(fg)

    row = lax.broadcasted_iota(jnp.int32, (C, LANE), 0)
    rr = lax.broadcasted_iota(jnp.int32, (C, C), 0)
    cc = lax.broadcasted_iota(jnp.int32, (C, C), 1)
    lseg = seg.bit_length() - 1
    tri = jnp.where((cc <= rr) & ((rr >> lseg) == (cc >> lseg)), 1.0, 0.0).astype(F32)
    b = jnp.dot(tri, g, precision=lax.Precision.HIGHEST, preferred_element_type=F32)

    vb = zi.astype(BF16)
    a = jnp.where(rr == cc, _dot_nt(q.astype(BF16), kk.astype(BF16)), 0.0)
    s = seg // 2
    while s >= 1:
        ls = s.bit_length() - 1
        right = ((row >> ls) & 1) == 1
        d = b - _group_ref_row(b, s)
        e = jnp.exp(jnp.where(right, d, -d))
        qs = jnp.where(right, q * e, 0.0).astype(BF16)
        ks = jnp.where(right, 0.0, kk * e).astype(BF16)
        a = a + jnp.where((rr >> (ls + 1)) == (cc >> (ls + 1)), _dot_nt(qs, ks), 0.0)
        s //= 2
    o = _dot(a.astype(BF16), vb)

    qe = q * jnp.exp(b)
    new_states = []
    for k, st in enumerate(states):
        bl = b[k * seg + seg - 1:k * seg + seg, :]
        if len(states) == 1:
            qk = qe
            ke = kk * jnp.exp(bl - b)
        else:
            mine = (row >> lseg) == k
            qk = jnp.where(mine, qe, 0.0)
            ke = jnp.where(mine, kk * jnp.exp(jnp.where(mine, bl - b, 0.0)), 0.0)
        o = o + _dot(qk.astype(BF16), st.astype(BF16))
        decay = jnp.transpose(jnp.broadcast_to(jnp.exp(bl), (LANE, LANE)))
        new_states.append(st * decay + _dot_tn(ke.astype(BF16), vb))
    return o, new_states


def _hgrn_finish(o, zog, hgn):
    return (_rms(o, hgn) * _silu(zog)).astype(BF16)


def _hgrn_prompt_kernel(zq_ref, zf_ref, zi_ref, zog_ref, lb_ref, hgn_ref, oh_ref, s_ref):
    @pl.when(pl.program_id(1) == 0)
    def _():
        s_ref[...] = jnp.zeros_like(s_ref)

    for h in range(HG_HEADS):
        cs = slice(h * LANE, (h + 1) * LANE)
        o, (st,) = _hgrn_block(zq_ref[:, cs], zf_ref[:, cs], zi_ref[:, cs], lb_ref[:, cs], [s_ref[h]], HGRN_CHUNK)
        s_ref[h] = st
        oh_ref[:, cs] = _hgrn_finish(o, zog_ref[:, cs], hgn_ref[:, cs])


def _hgrn_prompt(z, lbs3, hgn3, layer, batch):
    T = z.shape[0] // batch
    C = HGRN_CHUNK
    nc = T // C
    W = HG_HEADS * LANE
    zspec = lambda col: pl.BlockSpec((C, W), lambda b, c: (b * nc + c, col // HG_HEADS))
    vec = pl.BlockSpec((None, 1, W), lambda b, c: (layer, 0, 0))
    return pl.pallas_call(
        _hgrn_prompt_kernel,
        grid=(batch, nc),
        in_specs=[zspec(COL_Q), zspec(COL_F), zspec(COL_I), zspec(COL_OG), vec, vec],
        out_specs=[
            pl.BlockSpec((C, W), lambda b, c: (b * nc + c, 0)),
            pl.BlockSpec((None, HG_HEADS, LANE, LANE), lambda b, c: (b, 0, 0, 0)),
        ],
        out_shape=[
            jax.ShapeDtypeStruct((batch * T, W), BF16),
            jax.ShapeDtypeStruct((batch, HG_HEADS, LANE, LANE), F32),
        ],
        compiler_params=_params(("parallel", "arbitrary"), 32),
        name="hgrn_prompt",
    )(z, z, z, z, lbs3, hgn3)


def _hgrn_decode_kernel(seq_len, zq_ref, zf_ref, zi_ref, zog_ref, lb_ref, hgn_ref, s_ref, *rest):
    oh_ref, so_ref = rest[-2], rest[-1]
    rows = DEC_GROUP * seq_len
    for grp in range(s_ref.shape[0] // DEC_GROUP):
        rs = slice(grp * rows, (grp + 1) * rows)
        for h in range(HG_HEADS):
            cs = slice(h * LANE, (h + 1) * LANE)
            states = [s_ref[grp * DEC_GROUP + k, h] for k in range(DEC_GROUP)]
            o, new = _hgrn_block(zq_ref[rs, cs], zf_ref[rs, cs], zi_ref[rs, cs], lb_ref[:, cs], states, seq_len)
            for k in range(DEC_GROUP):
                so_ref[grp * DEC_GROUP + k, h] = new[k]
            oh_ref[rs, cs] = _hgrn_finish(o, zog_ref[rs, cs], hgn_ref[:, cs])


def _hgrn_decode(z, lbs3, hgn3, state, stacked, layer, nseq, nb):
    seq_len = z.shape[0] // nseq
    W = HG_HEADS * LANE
    zspec = lambda col: pl.BlockSpec((nb * seq_len, W), lambda i: (i, col // HG_HEADS))
    vec = pl.BlockSpec((None, 1, W), lambda i: (layer, 0, 0))
    sspec = pl.BlockSpec((None, nb, HG_HEADS, LANE, LANE), lambda i: (layer, i, 0, 0, 0))
    in_specs = [zspec(COL_Q), zspec(COL_F), zspec(COL_I), zspec(COL_OG), vec, vec, sspec]
    args = [z, z, z, z, lbs3, hgn3, state]
    aliases = {}
    if stacked is not None:
        in_specs.append(pl.BlockSpec(memory_space=pl.ANY))
        args.append(stacked)
        aliases = {len(args) - 1: 1}
    return pl.pallas_call(
        functools.partial(_hgrn_decode_kernel, seq_len),
        grid=(nseq // nb,),
        in_specs=in_specs,
        out_specs=[pl.BlockSpec((nb * seq_len, W), lambda i: (i, 0)), sspec],
        out_shape=[
            jax.ShapeDtypeStruct((nseq * seq_len, W), BF16),
            jax.ShapeDtypeStruct(state.shape, F32),
        ],
        input_output_aliases=aliases,
        compiler_params=_params(("parallel",), 48),
        name="hgrn_decode",
    )(*args)


def _pool_prompt_kernel(u_ref, wp_ref, sc_ref, op_ref, nb_ref):
    u = u_ref[...]
    T = u.shape[0]
    row = lax.broadcasted_iota(jnp.int32, (T, LANE), 0)
    for g, w in enumerate(POOL_WINDOWS):
        cs = slice(g * LANE, (g + 1) * LANE)
        ug = u[:, cs]
        s = ug
        d = 1
        while d < w:
            s = s + jnp.where(row >= d, pltpu.roll(s, d, 0), 0.0)
            d *= 2
        cnt = jnp.minimum(row + 1, w).astype(F32)
        dv = s / cnt - ug
        y = _dot(dv.astype(BF16), wp_ref[g].astype(BF16)) * sc_ref[:, cs]
        op_ref[:, cs] = y.astype(BF16)
    tail = u_ref[T - 16:T, :]
    nb_ref[...] = pltpu.roll(tail, 15, 0)[0:POOL_BUF, :]


def _pool_prompt(z, w_pool, scale3, layer, batch):
    T = z.shape[0] // batch
    G = len(POOL_WINDOWS)
    W = G * LANE
    return pl.pallas_call(
        _pool_prompt_kernel,
        grid=(batch,),
        in_specs=[
            pl.BlockSpec((T, W), lambda b: (b, COL_U // G)),
            pl.BlockSpec((None, G, LANE, LANE), lambda b: (layer, 0, 0, 0)),
            pl.BlockSpec((None, 1, W), lambda b: (layer, 0, 0)),
        ],
        out_specs=[
            pl.BlockSpec((T, W), lambda b: (b, 0)),
            pl.BlockSpec((None, POOL_BUF, W), lambda b: (b, 0, 0)),
        ],
        out_shape=[
            jax.ShapeDtypeStruct((batch * T, W), BF16),
            jax.ShapeDtypeStruct((batch, POOL_BUF, W), F32),
        ],
        compiler_params=_params(("parallel",), 48),
        name="pool_prompt",
    )(z, w_pool, scale3)


def _pool_decode_kernel(u_ref, buf_ref, wp_ref, sc_ref, op_ref, nb_ref):
    steps = u_ref.shape[0]
    for g, w in enumerate(POOL_WINDOWS):
        cs = slice(g * LANE, (g + 1) * LANE)
        wg = wp_ref[g].astype(BF16)
        for t in range(steps):
            n_u = min(t + 1, w)
            acc = u_ref[t, :, cs]
            for j in range(t - n_u + 1, t):
                acc = acc + u_ref[j, :, cs]
            for i in range(POOL_BUF - (w - n_u), POOL_BUF):
                acc = acc + buf_ref[i, :, cs]
            dv = acc * (1.0 / w) - u_ref[t, :, cs]
            op_ref[t, :, cs] = _dot(dv.astype(BF16), wg) * sc_ref[:, cs]
    for i in range(POOL_BUF - steps):
        nb_ref[i] = buf_ref[i + steps]
    for t in range(steps):
        nb_ref[POOL_BUF - steps + t] = u_ref[t]


def _pool_decode(u_t, buf_t, w_pool, scale3, layer):
    steps, nseq, W = u_t.shape
    G = len(POOL_WINDOWS)
    return pl.pallas_call(
        _pool_decode_kernel,
        grid=(1,),
        in_specs=[
            pl.BlockSpec((steps, nseq, W), lambda i: (0, 0, 0)),
            pl.BlockSpec((POOL_BUF, nseq, W), lambda i: (0, 0, 0)),
            pl.BlockSpec((None, G, LANE, LANE), lambda i: (layer, 0, 0, 0)),
            pl.BlockSpec((None, 1, W), lambda i: (layer, 0, 0)),
        ],
        out_specs=[
            pl.BlockSpec((steps, nseq, W), lambda i: (0, 0, 0)),
            pl.BlockSpec((POOL_BUF, nseq, W), lambda i: (0, 0, 0)),
        ],
        out_shape=[
            jax.ShapeDtypeStruct((steps, nseq, W), F32),
            jax.ShapeDtypeStruct((POOL_BUF, nseq, W), F32),
        ],
        compiler_params=_params(("arbitrary",), 32),
        name="pool_decode",
    )(u_t, buf_t, w_pool, scale3)


def _softmax_rows(s):
    e = jnp.exp(s - jnp.max(s, axis=-1, keepdims=True))
    return e / jnp.sum(e, axis=-1, keepdims=True)


def _xattn_prompt_kernel(q_ref, k_ref, v_ref, o_ref):
    scale = LANE ** -0.5
    for h in range(XA_HEADS):
        cs = slice(h * LANE, (h + 1) * LANE)
        s = _dot_nt(q_ref[:, cs].astype(BF16), k_ref[:, cs].astype(BF16)) * scale
        p = _softmax_rows(s)
        o_ref[:, cs] = _dot(p.astype(BF16), v_ref[:, cs].astype(BF16)).astype(BF16)


def _xattn_prompt(z, mkv, batch, tq):
    T = z.shape[0] // batch
    W = XA_HEADS * LANE
    nq = T // tq
    n_mem = mkv.shape[1]
    return pl.pallas_call(
        _xattn_prompt_kernel,
        grid=(batch, nq),
        in_specs=[
            pl.BlockSpec((tq, W), lambda b, i: (b * nq + i, COL_X // XA_HEADS)),
            pl.BlockSpec((None, n_mem, W), lambda b, i: (b, 0, 0)),
            pl.BlockSpec((None, n_mem, W), lambda b, i: (b, 0, 1)),
        ],
        out_specs=pl.BlockSpec((tq, W), lambda b, i: (b * nq + i, 0)),
        out_shape=jax.ShapeDtypeStruct((batch * T, W), BF16),
        compiler_params=_params(("parallel", "parallel"), 48),
        name="xattn_prompt",
    )(z, mkv, mkv)


def _xattn_decode_kernel(seq_len, q_ref, k_ref, v_ref, o_ref):
    scale = LANE ** -0.5
    rows = DEC_GROUP * seq_len
    lseq = seq_len.bit_length() - 1
    n_mem = k_ref.shape[1]
    seq_of_row = lax.broadcasted_iota(jnp.int32, (rows, n_mem), 0) >> lseq
    seq_of_row_o = lax.broadcasted_iota(jnp.int32, (rows, LANE), 0) >> lseq
    for grp in range(k_ref.shape[0] // DEC_GROUP):
        rs = slice(grp * rows, (grp + 1) * rows)
        for h in range(XA_HEADS):
            cs = slice(h * LANE, (h + 1) * LANE)
            qb = q_ref[rs, cs].astype(BF16)
            s = jnp.zeros((rows, n_mem), F32)
            for k in range(DEC_GROUP):
                sk = _dot_nt(qb, k_ref[grp * DEC_GROUP + k, :, cs].astype(BF16))
                s = jnp.where(seq_of_row == k, sk, s)
            p = _softmax_rows(s * scale).astype(BF16)
            o = jnp.zeros((rows, LANE), F32)
            for k in range(DEC_GROUP):
                ok = _dot(p, v_ref[grp * DEC_GROUP + k, :, cs].astype(BF16))
                o = jnp.where(seq_of_row_o == k, ok, o)
            o_ref[rs, cs] = o.astype(BF16)


def _xattn_decode(z, cache_k, cache_v, layer, nseq, nb):
    seq_len = z.shape[0] // nseq
    W = XA_HEADS * LANE
    n_mem = cache_k.shape[2]
    cspec = pl.BlockSpec((None, nb, n_mem, W), lambda i: (layer, i, 0, 0))
    return pl.pallas_call(
        functools.partial(_xattn_decode_kernel, seq_len),
        grid=(nseq // nb,),
        in_specs=[pl.BlockSpec((nb * seq_len, W), lambda i: (i, COL_X // XA_HEADS)), cspec, cspec],
        out_specs=pl.BlockSpec((nb * seq_len, W), lambda i: (i, 0)),
        out_shape=jax.ShapeDtypeStruct((nseq * seq_len, W), BF16),
        compiler_params=_params(("parallel",), 48),
        name="xattn_decode",
    )(z, cache_k, cache_v)


def _merge_kernel(x_ref, oh_ref, op_ref, ox_ref, g0_ref, g1_ref, g2_ref, wb_ref, wo_ref, o_ref):
    @pl.when(pl.program_id(1) == 0)
    def _():
        o_ref[...] = x_ref[...]

    wh = oh_ref.shape[1]
    wp = op_ref.shape[1]
    y = jax.nn.sigmoid(g0_ref[...]) * _dot(oh_ref[...].astype(BF16), wb_ref[0:wh, :])
    y += jax.nn.sigmoid(g1_ref[...]) * _dot(op_ref[...].astype(BF16), wb_ref[wh:wh + wp, :])
    y += jax.nn.sigmoid(g2_ref[...]) * _dot(ox_ref[...].astype(BF16), wb_ref[wh + wp:, :])
    o_ref[...] += _dot(y.astype(BF16), wo_ref[...])


def _merge(x, oh, op, ox, z, wb, wo, layer, tm, tn):
    T, D = x.shape
    gcol = COL_G * LANE // tn
    nj = D // tn
    full = lambda a: pl.BlockSpec((tm, a.shape[1]), lambda i, j: (i, 0))
    gate = lambda k: pl.BlockSpec((tm, tn), lambda i, j: (i, gcol + k * nj + j))
    return pl.pallas_call(
        _merge_kernel,
        grid=(T // tm, nj),
        in_specs=[
            full(x), full(oh), full(op), full(ox), gate(0), gate(1), gate(2),
            pl.BlockSpec((None, wb.shape[1], tn), lambda i, j: (layer, 0, j)),
            pl.BlockSpec((None, tn, D), lambda i, j: (layer, j, 0)),
        ],
        out_specs=pl.BlockSpec((tm, D), lambda i, j: (i, 0)),
        out_shape=jax.ShapeDtypeStruct((T, D), F32),
        compiler_params=_params(("parallel", "arbitrary"), 48),
        name="merge_out",
    )(x, oh, op, ox, z, z, z, wb, wo)


def _norm_kernel(x_ref, g_ref, o_ref):
    o_ref[...] = _rms(x_ref[...], g_ref[...])


def _final_norm(x, g2, tm):
    T, D = x.shape
    return pl.pallas_call(
        _norm_kernel,
        grid=(T // tm,),
        in_specs=[pl.BlockSpec((tm, D), lambda i: (i, 0)), pl.BlockSpec((1, D), lambda i: (0, 0))],
        out_specs=pl.BlockSpec((tm, D), lambda i: (i, 0)),
        out_shape=jax.ShapeDtypeStruct((T, D), F32),
        compiler_params=_params(("parallel",), 32),
        name="final_norm",
    )(x, g2)


def kernel(x_prompt, x_sample, state_hgrn, state_pool, cache_mem_k, cache_mem_v, mem_prompt, ffn1_norm, ffn1_w1, ffn1_w3, ffn1_w2, mix_norm, w_in, lb_logits, hg_norm, w_pool, pool_scale, mem_norm, w_mk, w_mv, w_branch, w_out, ffn2_norm, ffn2_w1, ffn2_w3, ffn2_w2, final_norm):
    B, T, D = x_prompt.shape
    nseq, steps, _ = x_sample.shape
    depth = w_in.shape[0]
    n_mem = mem_prompt.shape[1]
    xw = XA_HEADS * LANE
    pw = len(POOL_WINDOWS) * LANE

    tm_p = min(512, B * T)
    tm_s = min(512, nseq * steps)

    vec3 = lambda a: a.reshape(a.shape[0], 1, a.shape[1])
    bf = lambda a: a.astype(BF16)
    f1 = (vec3(ffn1_norm), bf(ffn1_w1), bf(ffn1_w3), bf(ffn1_w2))
    f2 = (vec3(ffn2_norm), bf(ffn2_w1), bf(ffn2_w3), bf(ffn2_w2))
    w_in_b, w_br_b, w_out_b = bf(w_in), bf(w_branch), bf(w_out)
    w_mkv = bf(jnp.concatenate([w_mk, w_mv], axis=-1))
    mix3, mem3, hgn3, psc3 = vec3(mix_norm), vec3(mem_norm), vec3(hg_norm), vec3(pool_scale)
    lbs3 = vec3(_lower_bounds(lb_logits))

    xp = x_prompt.reshape(B * T, D)
    xs = x_sample.reshape(nseq * steps, D)
    mem2 = mem_prompt.reshape(B * n_mem, D)
    ck = cache_mem_k.reshape(depth, nseq, n_mem, xw)
    cv = cache_mem_v.reshape(depth, nseq, n_mem, xw)

    tf = _tile(ffn1_w1.shape[-1], 512)
    tn_in = _tile(w_in.shape[-1], 1024)
    tn_mrg = _tile(D, 512)
    hs_p, pb_p, mk_p, mv_p, pb_s = [], [], [], [], []
    hs_s = None
    for l in range(depth):
        mkv = _rms_matmul(mem2, mem3, w_mkv, l, min(512, B * n_mem), 2 * xw)
        mk_p.append(mkv[:, :xw].reshape(B, n_mem, XA_HEADS, LANE))
        mv_p.append(mkv[:, xw:].reshape(B, n_mem, XA_HEADS, LANE))

        xp = _ffn(xp, *f1, l, tm_p, tf)
        xs = _ffn(xs, *f1, l, tm_s, tf)

        zp = _rms_matmul(xp, mix3, w_in_b, l, tm_p, tn_in)
        zs = _rms_matmul(xs, mix3, w_in_b, l, tm_s, tn_in)

        ohp, sp = _hgrn_prompt(zp, lbs3, hgn3, l, B)
        ohs, hs_s = _hgrn_decode(zs, lbs3, hgn3, state_hgrn, hs_s, l, nseq, 8)
        hs_p.append(sp)

        opp, bp = _pool_prompt(zp, w_pool, psc3, l, B)
        pb_p.append(bp)
        u_t = zs[:, COL_U * LANE:COL_U * LANE + pw].reshape(nseq, steps, pw).transpose(1, 0, 2)
        buf_t = state_pool[l].transpose(1, 0, 2)
        ops_t, nb_t = _pool_decode(u_t, buf_t, w_pool, psc3, l)
        ops = ops_t.transpose(1, 0, 2).reshape(nseq * steps, pw)
        pb_s.append(nb_t.transpose(1, 0, 2))

        oxp = _xattn_prompt(zp, mkv.reshape(B, n_mem, 2 * xw), B, min(1024, T))
        oxs = _xattn_decode(zs, ck, cv, l, nseq, 8)

        xp = _merge(xp, ohp, opp, oxp, zp, w_br_b, w_out_b, l, tm_p, tn_mrg)
        xs = _merge(xs, ohs, ops, oxs, zs, w_br_b, w_out_b, l, tm_s, tn_mrg)

        xp = _ffn(xp, *f2, l, tm_p, tf)
        xs = _ffn(xs, *f2, l, tm_s, tf)

    fn2 = final_norm.reshape(1, D)
    y_prompt = _final_norm(xp, fn2, tm_p).reshape(B, T, D)
    y_sample = _final_norm(xs, fn2, tm_s).reshape(nseq, steps, D)
    return (y_prompt, y_sample, jnp.stack(hs_p), jnp.stack(pb_p), jnp.stack(mk_p), jnp.stack(mv_p),
            hs_s, jnp.stack(pb_s))
```

```python
import functools

import jax
import jax.numpy as jnp
from jax import lax
from jax.experimental import pallas as pl
from jax.experimental.pallas import tpu as pltpu

F32 = jnp.float32
BF16 = jnp.bfloat16
EPS = 1e-6

LANE = 128
SUBLANE = 8
HG_HEADS = 8
POOL_WINDOWS = (2, 4, 8, 16)
POOL_BUF = max(POOL_WINDOWS) - 1
XA_HEADS = 4
N_BRANCH = 3

COL_Q, COL_F, COL_I, COL_OG = 0, HG_HEADS, 2 * HG_HEADS, 3 * HG_HEADS
COL_U = 4 * HG_HEADS
COL_X = COL_U + len(POOL_WINDOWS)
COL_G = COL_X + XA_HEADS

HGRN_CHUNK = 128
DEC_GROUP = 4


def _tile(n, preferred):
    t = preferred
    while n % t:
        t -= LANE
    return t


def _params(semantics, vmem_mib):
    return pltpu.CompilerParams(dimension_semantics=semantics, vmem_limit_bytes=vmem_mib * 1024 * 1024)


def _rms(x, g):
    return x * lax.rsqrt(jnp.mean(x * x, axis=-1, keepdims=True) + EPS) * g


def _silu(x):
    return x * jax.nn.sigmoid(x)


def _dot(a, b):
    return jnp.dot(a, b, preferred_element_type=F32)


def _dot_nt(a, b):
    return lax.dot_general(a, b, (((1,), (1,)), ((), ())), preferred_element_type=F32)


def _dot_tn(a, b):
    return lax.dot_general(a, b, (((0,), (0,)), ((), ())), preferred_element_type=F32)


def _ffn_kernel(x_ref, g_ref, w1_ref, w3_ref, w2_ref, o_ref, xn_ref):
    @pl.when(pl.program_id(1) == 0)
    def _():
        x = x_ref[...]
        xn_ref[...] = _rms(x, g_ref[...]).astype(BF16)
        o_ref[...] = x

    xn = xn_ref[...]
    h = (_silu(_dot(xn, w1_ref[...])) * _dot(xn, w3_ref[...])).astype(BF16)
    cw = min(w2_ref.shape[0], o_ref.shape[1])
    for n in range(o_ref.shape[1] // cw):
        cs = slice(n * cw, (n + 1) * cw)
        o_ref[:, cs] += 0.5 * _dot(h, w2_ref[:, cs])


def _ffn_cast_kernel(x_ref, g_ref, w1_ref, w3_ref, w2_ref, o_ref, w1b_ref, w3b_ref, w2b_ref, xn_ref):
    w1b_ref[...] = w1_ref[...].astype(BF16)
    w3b_ref[...] = w3_ref[...].astype(BF16)
    w2b_ref[...] = w2_ref[...].astype(BF16)
    _ffn_kernel(x_ref, g_ref, w1b_ref, w3b_ref, w2b_ref, o_ref, xn_ref)


def _layer_spec(w, layer, block, index):
    if w.ndim == 2:
        return pl.BlockSpec(block, index)
    return pl.BlockSpec((None,) + block, lambda i, j: (layer,) + index(i, j))


def _ffn(x, g3, w1, w3, w2, layer, tm, tf):
    T, D = x.shape
    F = w1.shape[-1]
    cast = w1.dtype != BF16
    assert not cast or T == tm
    col = lambda i, j: (0, j)
    row = lambda i, j: (j, 0)
    out_specs = [pl.BlockSpec((tm, D), lambda i, j: (i, 0))]
    out_shape = [jax.ShapeDtypeStruct((T, D), F32)]
    if cast:
        out_specs += [pl.BlockSpec((D, tf), col), pl.BlockSpec((D, tf), col), pl.BlockSpec((tf, D), row)]
        out_shape += [jax.ShapeDtypeStruct((D, F), BF16), jax.ShapeDtypeStruct((D, F), BF16),
                      jax.ShapeDtypeStruct((F, D), BF16)]
    out = pl.pallas_call(
        _ffn_cast_kernel if cast else _ffn_kernel,
        grid=(T // tm, F // tf),
        in_specs=[
            pl.BlockSpec((tm, D), lambda i, j: (i, 0)),
            pl.BlockSpec((None, 1, D), lambda i, j: (layer, 0, 0)),
            _layer_spec(w1, layer, (D, tf), col),
            _layer_spec(w3, layer, (D, tf), col),
            _layer_spec(w2, layer, (tf, D), row),
        ],
        out_specs=out_specs,
        out_shape=out_shape,
        scratch_shapes=[pltpu.VMEM((tm, D), BF16)],
        compiler_params=_params(("parallel", "arbitrary"), 58),
        name="ffn_cast" if cast else "ffn",
    )(x, g3, w1, w3, w2)
    return out if cast else out[0]


def _rms_matmul_kernel(x_ref, g_ref, w_ref, o_ref, *rest):
    xn_ref = rest[-1]

    @pl.when(pl.program_id(1) == 0)
    def _():
        xn_ref[...] = _rms(x_ref[...], g_ref[...]).astype(BF16)

    w = w_ref[...].astype(BF16)
    if len(rest) == 2:
        rest[0][...] = w
    o_ref[...] = _dot(xn_ref[...], w)


def _rms_matmul(x, g3, w, layer, tm, tn, emit=False):
    T, D = x.shape
    N = w.shape[-1]
    assert not emit or T == tm
    col = lambda i, j: (0, j)
    out_specs = [pl.BlockSpec((tm, tn), lambda i, j: (i, j))]
    out_shape = [jax.ShapeDtypeStruct((T, N), F32)]
    if emit:
        out_specs.append(pl.BlockSpec((D, tn), col))
        out_shape.append(jax.ShapeDtypeStruct((D, N), BF16))
    out = pl.pallas_call(
        _rms_matmul_kernel,
        grid=(T // tm, N // tn),
        in_specs=[
            pl.BlockSpec((tm, D), lambda i, j: (i, 0)),
            pl.BlockSpec((None, 1, D), lambda i, j: (layer, 0, 0)),
            _layer_spec(w, layer, (D, tn), col),
        ],
        out_specs=out_specs,
        out_shape=out_shape,
        scratch_shapes=[pltpu.VMEM((tm, D), BF16)],
        compiler_params=_params(("parallel", "arbitrary"), 48),
        name="rms_matmul",
    )(x, g3, w)
    return out if emit else out[0]


def _lower_bound_kernel(lg_ref, o_ref):
    lg = lg_ref[...]
    depth = lg.shape[0]
    rows = [lg[i:i + 1, :] for i in range(depth)]
    m = rows[0]
    for r in rows[1:]:
        m = jnp.maximum(m, r)
    e = [jnp.exp(r - m) for r in rows]
    tot = e[0]
    for v in e[1:]:
        tot = tot + v
    c = e[0] / tot
    first = c
    o_ref[0:1, :] = c - first
    for i in range(1, depth):
        c = c + e[i] / tot
        o_ref[i:i + 1, :] = c - first


def _lower_bounds(lb_logits):
    return pl.pallas_call(
        _lower_bound_kernel,
        out_shape=jax.ShapeDtypeStruct(lb_logits.shape, F32),
        name="hgrn_lower_bounds",
    )(lb_logits)


def _neg_abs(x):
    bits = lax.bitcast_convert_type(x, jnp.uint32) | jnp.uint32(0x80000000)
    return lax.bitcast_convert_type(bits, F32)


def _group_ref_row(b, s):
    rows, width = b.shape
    gsz = 2 * s
    if gsz >= 8:
        parts = [jnp.broadcast_to(b[i * gsz + s - 1:i * gsz + s, :], (gsz, width)) for i in range(rows // gsz)]
        return parts[0] if len(parts) == 1 else jnp.concatenate(parts, axis=0)
    pos = lax.broadcasted_iota(jnp.int32, b.shape, 0) & (gsz - 1)
    out = b
    for p in range(gsz):
        d = p - (s - 1)
        if d != 0:
            out = jnp.where(pos == p, pltpu.roll(b, d % rows, 0), out)
    return out


def _hgrn_block(zq, zf, zi, lb, states, seg):
    C = zq.shape[0]
    q = _silu(zq)
    fg = lb + (1.0 - lb) * jax.nn.sigmoid(zf)
    kk = 1.0 - fg
    g = jnp.log2(fg)

    row = lax.broadcasted_iota(jnp.int32, (C, LANE), 0)
    rr = lax.broadcasted_iota(jnp.int32, (C, C), 0)
    cc = lax.broadcasted_iota(jnp.int32, (C, C), 1)
    lseg = seg.bit_length() - 1
    tri = jnp.where((cc <= rr) & ((rr >> lseg) == (cc >> lseg)), 1.0, 0.0).astype(F32)
    b = jnp.dot(tri, g, precision=lax.Precision.HIGHEST, preferred_element_type=F32)

    vb = zi.astype(BF16)
    a = jnp.where(rr == cc, _dot_nt(q.astype(BF16), kk.astype(BF16)), 0.0)
    s = seg // 2
    while s >= 1:
        ls = s.bit_length() - 1
        e = jnp.exp2(_neg_abs(b - _group_ref_row(b, s)))
        same = (rr >> (ls + 1)) == (cc >> (ls + 1))
        if s % SUBLANE == 0:
            nblk = C // s
            zero = jnp.zeros((s, LANE), F32)
            blk = lambda x, i: x[i * s:(i + 1) * s, :]
            qs = jnp.concatenate([blk(q, i) * blk(e, i) if i % 2 else zero for i in range(nblk)], axis=0)
            ks = jnp.concatenate([zero if i % 2 else blk(kk, i) * blk(e, i) for i in range(nblk)], axis=0)
            p = _dot_nt(qs.astype(BF16), ks.astype(BF16))
            pieces = []
            for i in range(nblk):
                if i % 2 == 0:
                    pieces.append(blk(a, i))
                elif 2 * s == C:
                    pieces.append(blk(a, i) + blk(p, i))
                else:
                    pieces.append(blk(a, i) + jnp.where(blk(same, i), blk(p, i), 0.0))
            a = jnp.concatenate(pieces, axis=0)
        else:
            right = ((row >> ls) & 1) == 1
            eq = jnp.where(right, e, 0.0)
            p = _dot_nt((q * eq).astype(BF16), (kk * (e - eq)).astype(BF16))
            a = a + jnp.where(same, p, 0.0)
        s //= 2
    o = _dot(a.astype(BF16), vb)

    qe = q * jnp.exp2(b)
    new_states = []
    for k, st in enumerate(states):
        bl = b[k * seg + seg - 1:k * seg + seg, :]
        if len(states) == 1:
            qk = qe
            ke = kk * jnp.exp2(bl - b)
        else:
            mine = (row >> lseg) == k
            qk = jnp.where(mine, qe, 0.0)
            ke = jnp.where(mine, kk * jnp.exp2(jnp.where(mine, bl - b, 0.0)), 0.0)
        o = o + _dot(qk.astype(BF16), st.astype(BF16))
        decay = jnp.transpose(jnp.broadcast_to(jnp.exp2(bl), (LANE, LANE)))
        new_states.append(st * decay + _dot_tn(ke.astype(BF16), vb))
    return o, new_states


def _hgrn_finish(o, zog, hgn):
    return (_rms(o, hgn) * _silu(zog)).astype(BF16)


def _hgrn_prompt_kernel(zq_ref, zf_ref, zi_ref, zog_ref, lb_ref, hgn_ref, oh_ref, s_ref):
    @pl.when(pl.program_id(1) == 0)
    def _():
        s_ref[...] = jnp.zeros_like(s_ref)

    for h in range(HG_HEADS):
        cs = slice(h * LANE, (h + 1) * LANE)
        o, (st,) = _hgrn_block(zq_ref[:, cs], zf_ref[:, cs], zi_ref[:, cs], lb_ref[:, cs], [s_ref[h]], HGRN_CHUNK)
        s_ref[h] = st
        oh_ref[:, cs] = _hgrn_finish(o, zog_ref[:, cs], hgn_ref[:, cs])


def _hgrn_prompt(z, lbs3, hgn3, layer, batch):
    T = z.shape[0] // batch
    C = HGRN_CHUNK
    nc = T // C
    W = HG_HEADS * LANE
    zspec = lambda col: pl.BlockSpec((C, W), lambda b, c: (b * nc + c, col // HG_HEADS))
    vec = pl.BlockSpec((None, 1, W), lambda b, c: (layer, 0, 0))
    return pl.pallas_call(
        _hgrn_prompt_kernel,
        grid=(batch, nc),
        in_specs=[zspec(COL_Q), zspec(COL_F), zspec(COL_I), zspec(COL_OG), vec, vec],
        out_specs=[
            pl.BlockSpec((C, W), lambda b, c: (b * nc + c, 0)),
            pl.BlockSpec((None, HG_HEADS, LANE, LANE), lambda b, c: (b, 0, 0, 0)),
        ],
        out_shape=[
            jax.ShapeDtypeStruct((batch * T, W), BF16),
            jax.ShapeDtypeStruct((batch, HG_HEADS, LANE, LANE), F32),
        ],
        compiler_params=_params(("parallel", "arbitrary"), 32),
        name="hgrn_prompt",
    )(z, z, z, z, lbs3, hgn3)


def _hgrn_decode_kernel(seq_len, zq_ref, zf_ref, zi_ref, zog_ref, lb_ref, hgn_ref, s_ref, *rest):
    oh_ref, so_ref = rest[-2], rest[-1]
    rows = DEC_GROUP * seq_len
    for grp in range(s_ref.shape[0] // DEC_GROUP):
        rs = slice(grp * rows, (grp + 1) * rows)
        for h in range(HG_HEADS):
            cs = slice(h * LANE, (h + 1) * LANE)
            states = [s_ref[grp * DEC_GROUP + k, h] for k in range(DEC_GROUP)]
            o, new = _hgrn_block(zq_ref[rs, cs], zf_ref[rs, cs], zi_ref[rs, cs], lb_ref[:, cs], states, seq_len)
            for k in range(DEC_GROUP):
                so_ref[grp * DEC_GROUP + k, h] = new[k]
            oh_ref[rs, cs] = _hgrn_finish(o, zog_ref[rs, cs], hgn_ref[:, cs])


def _hgrn_decode(z, lbs3, hgn3, state, stacked, layer, nseq, nb):
    seq_len = z.shape[0] // nseq
    W = HG_HEADS * LANE
    zspec = lambda col: pl.BlockSpec((nb * seq_len, W), lambda i: (i, col // HG_HEADS))
    vec = pl.BlockSpec((None, 1, W), lambda i: (layer, 0, 0))
    sspec = pl.BlockSpec((None, nb, HG_HEADS, LANE, LANE), lambda i: (layer, i, 0, 0, 0))
    in_specs = [zspec(COL_Q), zspec(COL_F), zspec(COL_I), zspec(COL_OG), vec, vec, sspec]
    args = [z, z, z, z, lbs3, hgn3, state]
    aliases = {}
    if stacked is not None:
        in_specs.append(pl.BlockSpec(memory_space=pl.ANY))
        args.append(stacked)
        aliases = {len(args) - 1: 1}
    return pl.pallas_call(
        functools.partial(_hgrn_decode_kernel, seq_len),
        grid=(nseq // nb,),
        in_specs=in_specs,
        out_specs=[pl.BlockSpec((nb * seq_len, W), lambda i: (i, 0)), sspec],
        out_shape=[
            jax.ShapeDtypeStruct((nseq * seq_len, W), BF16),
            jax.ShapeDtypeStruct(state.shape, F32),
        ],
        input_output_aliases=aliases,
        compiler_params=_params(("parallel",), 48),
        name="hgrn_decode",
    )(*args)


def _pool_prompt_kernel(u_ref, wp_ref, sc_ref, op_ref, nb_ref):
    u = u_ref[...]
    T = u.shape[0]
    row = lax.broadcasted_iota(jnp.int32, (T, LANE), 0)
    for g, w in enumerate(POOL_WINDOWS):
        cs = slice(g * LANE, (g + 1) * LANE)
        ug = u[:, cs]
        s = ug
        d = 1
        while d < w:
            s = s + jnp.where(row >= d, pltpu.roll(s, d, 0), 0.0)
            d *= 2
        cnt = jnp.minimum(row + 1, w).astype(F32)
        dv = s / cnt - ug
        y = _dot(dv.astype(BF16), wp_ref[g].astype(BF16)) * sc_ref[:, cs]
        op_ref[:, cs] = y.astype(BF16)
    tail = u_ref[T - 16:T, :]
    nb_ref[...] = pltpu.roll(tail, 15, 0)[0:POOL_BUF, :]


def _pool_prompt(z, w_pool, scale3, layer, batch):
    T = z.shape[0] // batch
    G = len(POOL_WINDOWS)
    W = G * LANE
    return pl.pallas_call(
        _pool_prompt_kernel,
        grid=(batch,),
        in_specs=[
            pl.BlockSpec((T, W), lambda b: (b, COL_U // G)),
            pl.BlockSpec((None, G, LANE, LANE), lambda b: (layer, 0, 0, 0)),
            pl.BlockSpec((None, 1, W), lambda b: (layer, 0, 0)),
        ],
        out_specs=[
            pl.BlockSpec((T, W), lambda b: (b, 0)),
            pl.BlockSpec((None, POOL_BUF, W), lambda b: (b, 0, 0)),
        ],
        out_shape=[
            jax.ShapeDtypeStruct((batch * T, W), BF16),
            jax.ShapeDtypeStruct((batch, POOL_BUF, W), F32),
        ],
        compiler_params=_params(("parallel",), 48),
        name="pool_prompt",
    )(z, w_pool, scale3)


def _pool_decode_kernel(u_ref, buf_ref, wp_ref, sc_ref, op_ref, nb_ref):
    steps = u_ref.shape[0]
    for g, w in enumerate(POOL_WINDOWS):
        cs = slice(g * LANE, (g + 1) * LANE)
        wg = wp_ref[g].astype(BF16)
        for t in range(steps):
            n_u = min(t + 1, w)
            acc = u_ref[t, :, cs]
            for j in range(t - n_u + 1, t):
                acc = acc + u_ref[j, :, cs]
            for i in range(POOL_BUF - (w - n_u), POOL_BUF):
                acc = acc + buf_ref[i, :, cs]
            dv = acc * (1.0 / w) - u_ref[t, :, cs]
            op_ref[t, :, cs] = _dot(dv.astype(BF16), wg) * sc_ref[:, cs]
    for i in range(POOL_BUF - steps):
        nb_ref[i] = buf_ref[i + steps]
    for t in range(steps):
        nb_ref[POOL_BUF - steps + t] = u_ref[t]


def _pool_decode(u_t, buf_t, w_pool, scale3, layer):
    steps, nseq, W = u_t.shape
    G = len(POOL_WINDOWS)
    return pl.pallas_call(
        _pool_decode_kernel,
        grid=(1,),
        in_specs=[
            pl.BlockSpec((steps, nseq, W), lambda i: (0, 0, 0)),
            pl.BlockSpec((POOL_BUF, nseq, W), lambda i: (0, 0, 0)),
            pl.BlockSpec((None, G, LANE, LANE), lambda i: (layer, 0, 0, 0)),
            pl.BlockSpec((None, 1, W), lambda i: (layer, 0, 0)),
        ],
        out_specs=[
            pl.BlockSpec((steps, nseq, W), lambda i: (0, 0, 0)),
            pl.BlockSpec((POOL_BUF, nseq, W), lambda i: (0, 0, 0)),
        ],
        out_shape=[
            jax.ShapeDtypeStruct((steps, nseq, W), F32),
            jax.ShapeDtypeStruct((POOL_BUF, nseq, W), F32),
        ],
        compiler_params=_params(("arbitrary",), 32),
        name="pool_decode",
    )(u_t, buf_t, w_pool, scale3)


def _softmax_rows(s):
    e = jnp.exp(s - jnp.max(s, axis=-1, keepdims=True))
    return e / jnp.sum(e, axis=-1, keepdims=True)


def _xattn_prompt_kernel(q_ref, k_ref, v_ref, o_ref):
    scale = LANE ** -0.5
    for h in range(XA_HEADS):
        cs = slice(h * LANE, (h + 1) * LANE)
        s = _dot_nt(q_ref[:, cs].astype(BF16), k_ref[:, cs].astype(BF16)) * scale
        p = _softmax_rows(s)
        o_ref[:, cs] = _dot(p.astype(BF16), v_ref[:, cs].astype(BF16)).astype(BF16)


def _xattn_prompt(z, mk, mv, batch, tq):
    T = z.shape[0] // batch
    W = XA_HEADS * LANE
    nq = T // tq
    n_mem = mk.shape[1]
    mspec = pl.BlockSpec((None, n_mem, W), lambda b, i: (b, 0, 0))
    return pl.pallas_call(
        _xattn_prompt_kernel,
        grid=(batch, nq),
        in_specs=[pl.BlockSpec((tq, W), lambda b, i: (b * nq + i, COL_X // XA_HEADS)), mspec, mspec],
        out_specs=pl.BlockSpec((tq, W), lambda b, i: (b * nq + i, 0)),
        out_shape=jax.ShapeDtypeStruct((batch * T, W), BF16),
        compiler_params=_params(("parallel", "parallel"), 48),
        name="xattn_prompt",
    )(z, mk, mv)


def _xattn_decode_kernel(seq_len, q_ref, k_ref, v_ref, o_ref):
    scale = LANE ** -0.5
    rows = DEC_GROUP * seq_len
    lseq = seq_len.bit_length() - 1
    lrows = rows.bit_length() - 1
    nk = k_ref.shape[1]
    row_s = lax.broadcasted_iota(jnp.int32, (XA_HEADS * rows, nk), 0)
    col_s = lax.broadcasted_iota(jnp.int32, (XA_HEADS * rows, nk), 1)
    own_head = (col_s & (XA_HEADS - 1)) == (row_s >> lrows)
    seq_s = (row_s & (rows - 1)) >> lseq
    seq_o = (lax.broadcasted_iota(jnp.int32, (XA_HEADS * rows, LANE), 0) & (rows - 1)) >> lseq
    for grp in range(k_ref.shape[0] // DEC_GROUP):
        rs = slice(grp * rows, (grp + 1) * rows)
        qb = jnp.concatenate([q_ref[rs, h * LANE:(h + 1) * LANE] for h in range(XA_HEADS)], axis=0).astype(BF16)
        s = jnp.full((XA_HEADS * rows, nk), -jnp.inf, F32)
        for k in range(DEC_GROUP):
            sk = _dot_nt(qb, k_ref[grp * DEC_GROUP + k].astype(BF16))
            s = jnp.where(own_head & (seq_s == k), sk, s)
        p = _softmax_rows(s * scale).astype(BF16)
        o = jnp.zeros((XA_HEADS * rows, LANE), F32)
        for k in range(DEC_GROUP):
            o = jnp.where(seq_o == k, _dot(p, v_ref[grp * DEC_GROUP + k].astype(BF16)), o)
        for h in range(XA_HEADS):
            o_ref[rs, h * LANE:(h + 1) * LANE] = o[h * rows:(h + 1) * rows, :].astype(BF16)


def _xattn_decode(z, cache_k, cache_v, layer, nseq, nb):
    seq_len = z.shape[0] // nseq
    W = XA_HEADS * LANE
    cspec = pl.BlockSpec((None, nb) + cache_k.shape[2:], lambda i: (layer, i, 0, 0))
    return pl.pallas_call(
        functools.partial(_xattn_decode_kernel, seq_len),
        grid=(nseq // nb,),
        in_specs=[pl.BlockSpec((nb * seq_len, W), lambda i: (i, COL_X // XA_HEADS)), cspec, cspec],
        out_specs=pl.BlockSpec((nb * seq_len, W), lambda i: (i, 0)),
        out_shape=jax.ShapeDtypeStruct((nseq * seq_len, W), BF16),
        compiler_params=_params(("parallel",), 48),
        name="xattn_decode",
    )(z, cache_k, cache_v)


def _merge_kernel(x_ref, oh_ref, op_ref, ox_ref, g0_ref, g1_ref, g2_ref, wb_ref, wo_ref, o_ref, y_ref):
    j = pl.program_id(1)
    nj, _, tn = y_ref.shape

    @pl.when(j < nj)
    def _():
        wh = oh_ref.shape[1]
        wp = op_ref.shape[1]
        y = jax.nn.sigmoid(g0_ref[...]) * _dot(oh_ref[...].astype(BF16), wb_ref[0:wh, :])
        y += jax.nn.sigmoid(g1_ref[...]) * _dot(op_ref[...].astype(BF16), wb_ref[wh:wh + wp, :])
        y += jax.nn.sigmoid(g2_ref[...]) * _dot(ox_ref[...].astype(BF16), wb_ref[wh + wp:, :])
        y_ref[j] = y.astype(BF16)

    @pl.when(j >= nj)
    def _():
        acc = x_ref[...]
        for c in range(nj):
            acc += _dot(y_ref[c], wo_ref[c * tn:(c + 1) * tn, :])
        o_ref[...] = acc


def _merge(x, oh, op, ox, z, wb, wo, layer, tm, tn):
    T, D = x.shape
    gcol = COL_G * LANE // tn
    nj = D // tn
    first = lambda j: jnp.minimum(j, nj - 1)
    second = lambda j: jnp.maximum(j - nj, 0)
    full = lambda a: pl.BlockSpec((tm, a.shape[1]), lambda i, j: (i, 0))
    gate = lambda k: pl.BlockSpec((tm, tn), lambda i, j: (i, gcol + k * nj + first(j)))
    return pl.pallas_call(
        _merge_kernel,
        grid=(T // tm, 2 * nj),
        in_specs=[
            pl.BlockSpec((tm, tn), lambda i, j: (i, second(j))),
            full(oh), full(op), full(ox), gate(0), gate(1), gate(2),
            pl.BlockSpec((None, wb.shape[1], tn), lambda i, j: (layer, 0, first(j))),
            pl.BlockSpec((None, D, tn), lambda i, j: (layer, 0, second(j))),
        ],
        out_specs=pl.BlockSpec((tm, tn), lambda i, j: (i, second(j))),
        out_shape=jax.ShapeDtypeStruct((T, D), F32),
        scratch_shapes=[pltpu.VMEM((nj, tm, tn), BF16)],
        compiler_params=_params(("parallel", "arbitrary"), 48),
        name="merge_out",
    )(x, oh, op, ox, z, z, z, wb, wo)


def _norm_kernel(x_ref, g_ref, o_ref):
    o_ref[...] = _rms(x_ref[...], g_ref[...])


def _final_norm(x, g2, tm):
    T, D = x.shape
    return pl.pallas_call(
        _norm_kernel,
        grid=(T // tm,),
        in_specs=[pl.BlockSpec((tm, D), lambda i: (i, 0)), pl.BlockSpec((1, D), lambda i: (0, 0))],
        out_specs=pl.BlockSpec((tm, D), lambda i: (i, 0)),
        out_shape=jax.ShapeDtypeStruct((T, D), F32),
        compiler_params=_params(("parallel",), 40),
        name="final_norm",
    )(x, g2)


def kernel(x_prompt, x_sample, state_hgrn, state_pool, cache_mem_k, cache_mem_v, mem_prompt, ffn1_norm, ffn1_w1, ffn1_w3, ffn1_w2, mix_norm, w_in, lb_logits, hg_norm, w_pool, pool_scale, mem_norm, w_mk, w_mv, w_branch, w_out, ffn2_norm, ffn2_w1, ffn2_w3, ffn2_w2, final_norm):
    B, T, D = x_prompt.shape
    nseq, steps, _ = x_sample.shape
    depth = w_in.shape[0]
    n_mem = mem_prompt.shape[1]
    xw = XA_HEADS * LANE
    pw = len(POOL_WINDOWS) * LANE

    tm_p = min(1024, B * T)
    tm_s = min(512, nseq * steps)

    vec3 = lambda a: a.reshape(a.shape[0], 1, a.shape[1])
    f1n, f2n = vec3(ffn1_norm), vec3(ffn2_norm)
    w_br_b, w_out_b = w_branch.astype(BF16), w_out.astype(BF16)
    mix3, mem3, hgn3, psc3 = vec3(mix_norm), vec3(mem_norm), vec3(hg_norm), vec3(pool_scale)
    lbs3 = vec3(_lower_bounds(lb_logits))

    xp = x_prompt.reshape(B * T, D)
    xs = x_sample.reshape(nseq * steps, D)
    mem2 = mem_prompt.reshape(B * n_mem, D)
    ck = cache_mem_k.reshape(depth, nseq, n_mem * XA_HEADS, LANE)
    cv = cache_mem_v.reshape(depth, nseq, n_mem * XA_HEADS, LANE)

    tf = _tile(ffn1_w1.shape[-1], 512)
    tf_s = _tile(ffn1_w1.shape[-1], 256)
    tn_in = _tile(w_in.shape[-1], 1024)
    tn_mrg = _tile(D, 512)
    tm_mem = min(512, B * n_mem)
    hs_p, pb_p, mk_p, mv_p, pb_s = [], [], [], [], []
    hs_s = None
    for l in range(depth):
        mk = _rms_matmul(mem2, mem3, w_mk, l, tm_mem, xw)
        mv = _rms_matmul(mem2, mem3, w_mv, l, tm_mem, xw)
        mk_p.append(mk.reshape(B, n_mem, XA_HEADS, LANE))
        mv_p.append(mv.reshape(B, n_mem, XA_HEADS, LANE))

        xs, w1b, w3b, w2b = _ffn(xs, f1n, ffn1_w1, ffn1_w3, ffn1_w2, l, tm_s, tf_s)
        xp = _ffn(xp, f1n, w1b, w3b, w2b, l, tm_p, tf)

        zs, w_in_b = _rms_matmul(xs, mix3, w_in, l, tm_s, tn_in, emit=True)
        zp = _rms_matmul(xp, mix3, w_in_b, l, tm_p, tn_in)

        ohp, sp = _hgrn_prompt(zp, lbs3, hgn3, l, B)
        ohs, hs_s = _hgrn_decode(zs, lbs3, hgn3, state_hgrn, hs_s, l, nseq, 8)
        hs_p.append(sp)

        opp, bp = _pool_prompt(zp, w_pool, psc3, l, B)
        pb_p.append(bp)
        u_t = zs[:, COL_U * LANE:COL_U * LANE + pw].reshape(nseq, steps, pw).transpose(1, 0, 2)
        buf_t = state_pool[l].transpose(1, 0, 2)
        ops_t, nb_t = _pool_decode(u_t, buf_t, w_pool, psc3, l)
        ops = ops_t.transpose(1, 0, 2).reshape(nseq * steps, pw)
        pb_s.append(nb_t.transpose(1, 0, 2))

        oxp = _xattn_prompt(zp, mk.reshape(B, n_mem, xw), mv.reshape(B, n_mem, xw), B, min(1024, T))
        oxs = _xattn_decode(zs, ck, cv, l, nseq, 8)

        xp = _merge(xp, ohp, opp, oxp, zp, w_br_b, w_out_b, l, tm_p, tn_mrg)
        xs = _merge(xs, ohs, ops, oxs, zs, w_br_b, w_out_b, l, tm_s, tn_mrg)

        xs, w1b, w3b, w2b = _ffn(xs, f2n, ffn2_w1, ffn2_w3, ffn2_w2, l, tm_s, tf_s)
        xp = _ffn(xp, f2n, w1b, w3b, w2b, l, tm_p, tf)

    fn2 = final_norm.reshape(1, D)
    y_prompt = _final_norm(xp, fn2, tm_p).reshape(B, T, D)
    y_sample = _final_norm(xs, fn2, tm_s).reshape(nseq, steps, D)
    return (y_prompt, y_sample, jnp.stack(hs_p), jnp.stack(pb_p), jnp.stack(mk_p), jnp.stack(mv_p),
            hs_s, jnp.stack(pb_s))
```

```python
import functools
import math

import jax
import jax.numpy as jnp
from jax import lax
from jax.experimental import pallas as pl
from jax.experimental.pallas import tpu as pltpu

F32 = jnp.float32
BF16 = jnp.bfloat16
EPS = 1e-6

LANE = 128
SUBLANE = 8
HG_HEADS = 8
POOL_WINDOWS = (2, 4, 8, 16)
POOL_BUF = max(POOL_WINDOWS) - 1
XA_HEADS = 4
N_BRANCH = 3

COL_Q, COL_F, COL_I, COL_OG = 0, HG_HEADS, 2 * HG_HEADS, 3 * HG_HEADS
COL_U = 4 * HG_HEADS
COL_X = COL_U + len(POOL_WINDOWS)
COL_G = COL_X + XA_HEADS
ZB_Q, ZB_V, ZB_OG, ZB_GATES = 0, 1, 2, 3
ZF_F = 0
ZF_U, ZF_X = 2, 3

HGRN_CHUNK = 128
DEC_GROUP = 4


def _tile(n, preferred):
    t = preferred
    while n % t:
        t -= LANE
    return t


def _params(semantics, vmem_mib):
    return pltpu.CompilerParams(dimension_semantics=semantics, vmem_limit_bytes=vmem_mib * 1024 * 1024)


def _rms(x, g):
    return x * lax.rsqrt(jnp.mean(x * x, axis=-1, keepdims=True) + EPS) * g


def _silu(x):
    return x * jax.nn.sigmoid(x)


def _dot(a, b):
    return jnp.dot(a, b, preferred_element_type=F32)


def _dot_nt(a, b):
    return lax.dot_general(a, b, (((1,), (1,)), ((), ())), preferred_element_type=F32)


def _dot_tn(a, b):
    return lax.dot_general(a, b, (((0,), (0,)), ((), ())), preferred_element_type=F32)


def _ffn_kernel(x_ref, g_ref, w1_ref, w3_ref, w2_ref, o_ref, xn_ref):
    @pl.when(pl.program_id(1) == 0)
    def _():
        x = x_ref[...]
        xn_ref[...] = _rms(x, g_ref[...]).astype(BF16)
        o_ref[...] = x

    xn = xn_ref[...]
    h = (_silu(_dot(xn, w1_ref[...])) * _dot(xn, w3_ref[...])).astype(BF16)
    cw = min(w2_ref.shape[0], o_ref.shape[1])
    for n in range(o_ref.shape[1] // cw):
        cs = slice(n * cw, (n + 1) * cw)
        o_ref[:, cs] += 0.5 * _dot(h, w2_ref[:, cs])


def _ffn_cast_kernel(x_ref, g_ref, w1_ref, w3_ref, w2_ref, o_ref, w1b_ref, w3b_ref, w2b_ref, xn_ref):
    w1b_ref[...] = w1_ref[...].astype(BF16)
    w3b_ref[...] = w3_ref[...].astype(BF16)
    w2b_ref[...] = w2_ref[...].astype(BF16)
    _ffn_kernel(x_ref, g_ref, w1b_ref, w3b_ref, w2b_ref, o_ref, xn_ref)


def _layer_spec(w, layer, block, index):
    if w.ndim == 2:
        return pl.BlockSpec(block, index)
    return pl.BlockSpec((None,) + block, lambda i, j: (layer,) + index(i, j))


def _ffn(x, g3, w1, w3, w2, layer, tm, tf):
    T, D = x.shape
    F = w1.shape[-1]
    cast = w1.dtype != BF16
    assert not cast or T == tm
    col = lambda i, j: (0, j)
    row = lambda i, j: (j, 0)
    out_specs = [pl.BlockSpec((tm, D), lambda i, j: (i, 0))]
    out_shape = [jax.ShapeDtypeStruct((T, D), F32)]
    if cast:
        out_specs += [pl.BlockSpec((D, tf), col), pl.BlockSpec((D, tf), col), pl.BlockSpec((tf, D), row)]
        out_shape += [jax.ShapeDtypeStruct((D, F), BF16), jax.ShapeDtypeStruct((D, F), BF16),
                      jax.ShapeDtypeStruct((F, D), BF16)]
    out = pl.pallas_call(
        _ffn_cast_kernel if cast else _ffn_kernel,
        grid=(T // tm, F // tf),
        in_specs=[
            pl.BlockSpec((tm, D), lambda i, j: (i, 0)),
            pl.BlockSpec((None, 1, D), lambda i, j: (layer, 0, 0)),
            _layer_spec(w1, layer, (D, tf), col),
            _layer_spec(w3, layer, (D, tf), col),
            _layer_spec(w2, layer, (tf, D), row),
        ],
        out_specs=out_specs,
        out_shape=out_shape,
        scratch_shapes=[pltpu.VMEM((tm, D), BF16)],
        compiler_params=_params(("parallel", "arbitrary"), 58),
        name="ffn_cast" if cast else "ffn",
    )(x, g3, w1, w3, w2)
    return out if cast else out[0]


def _rms_matmul_kernel(x_ref, g_ref, w_ref, o_ref, *rest):
    xn_ref = rest[-1]

    @pl.when(pl.program_id(1) == 0)
    def _():
        xn_ref[...] = _rms(x_ref[...], g_ref[...]).astype(BF16)

    w = w_ref[...].astype(BF16)
    if len(rest) == 2:
        rest[0][...] = w
    o_ref[...] = _dot(xn_ref[...], w)


def _rms_matmul(x, g3, w, layer, tm, tn, emit=False):
    T, D = x.shape
    N = w.shape[-1]
    assert not emit or T == tm
    col = lambda i, j: (0, j)
    out_specs = [pl.BlockSpec((tm, tn), lambda i, j: (i, j))]
    out_shape = [jax.ShapeDtypeStruct((T, N), F32)]
    if emit:
        out_specs.append(pl.BlockSpec((D, tn), col))
        out_shape.append(jax.ShapeDtypeStruct((D, N), BF16))
    out = pl.pallas_call(
        _rms_matmul_kernel,
        grid=(T // tm, N // tn),
        in_specs=[
            pl.BlockSpec((tm, D), lambda i, j: (i, 0)),
            pl.BlockSpec((None, 1, D), lambda i, j: (layer, 0, 0)),
            _layer_spec(w, layer, (D, tn), col),
        ],
        out_specs=out_specs,
        out_shape=out_shape,
        scratch_shapes=[pltpu.VMEM((tm, D), BF16)],
        compiler_params=_params(("parallel", "arbitrary"), 48),
        name="rms_matmul",
    )(x, g3, w)
    return out if emit else out[0]


def _runs(seq):
    runs = []
    for j, v in enumerate(seq):
        if runs and runs[-1][2] == v - j:
            runs[-1][1] = j + 1
        else:
            runs.append([j, j + 1, v - j])
    return [tuple(r) for r in runs]


def _in_runs(j, runs):
    hit = None
    for lo, hi, _ in runs:
        c = (j >= lo) & (j < hi)
        hit = c if hit is None else hit | c
    return hit


def _lookup(j, runs):
    out = 0
    for lo, hi, off in runs:
        out = out + jnp.where((j >= lo) & (j < hi), j + off, 0)
    return out


def _in_proj_plan(d_model, tn):
    t = lambda blocks: blocks * LANE // tn
    nq, nux, ng = t(HG_HEADS), t(len(POOL_WINDOWS) + XA_HEADS), N_BRANCH * d_model // tn
    src = lambda start, n: list(range(t(start), t(start) + n))
    order = src(COL_Q, nq) + src(COL_I, nq) + src(COL_OG, nq) + src(COL_G, ng) + src(COL_F, nq) + src(COL_U, nux)
    n16 = 3 * nq + ng
    kinds = {
        "silu": [(0, nq, 0), (2 * nq, 3 * nq, 0)],
        "plain": [(nq, 2 * nq, 0)],
        "sigmoid": [(3 * nq, n16, 0)],
        "f32": [(n16, len(order), 0)],
    }
    return _runs(order), kinds, n16


def _in_proj_kernel(kinds, x_ref, g_ref, w_ref, zb_ref, zf_ref, *rest):
    xn_ref = rest[-1]
    j = pl.program_id(1)

    @pl.when(j == 0)
    def _():
        xn_ref[...] = _rms(x_ref[...], g_ref[...]).astype(BF16)

    w = w_ref[...].astype(BF16)
    if len(rest) == 2:
        rest[0][...] = w
    z = _dot(xn_ref[...], w)

    @pl.when(_in_runs(j, kinds["silu"]))
    def _():
        zb_ref[...] = _silu(z).astype(BF16)

    @pl.when(_in_runs(j, kinds["plain"]))
    def _():
        zb_ref[...] = z.astype(BF16)

    @pl.when(_in_runs(j, kinds["sigmoid"]))
    def _():
        zb_ref[...] = jax.nn.sigmoid(z).astype(BF16)

    @pl.when(_in_runs(j, kinds["f32"]))
    def _():
        zf_ref[...] = z


def _in_proj(x, g3, w, layer, tm, tn, emit=False):
    T, D = x.shape
    N = w.shape[-1]
    assert not emit or T == tm
    order, kinds, n16 = _in_proj_plan(D, tn)
    nsteps = N // tn
    wcol = lambda i, j: (0, _lookup(j, order))
    out_specs = [
        pl.BlockSpec((tm, tn), lambda i, j: (i, jnp.minimum(j, n16 - 1))),
        pl.BlockSpec((tm, tn), lambda i, j: (i, jnp.maximum(j - n16, 0))),
    ]
    out_shape = [
        jax.ShapeDtypeStruct((T, n16 * tn), BF16),
        jax.ShapeDtypeStruct((T, (nsteps - n16) * tn), F32),
    ]
    if emit:
        out_specs.append(pl.BlockSpec((D, tn), wcol))
        out_shape.append(jax.ShapeDtypeStruct((D, N), BF16))
    return pl.pallas_call(
        functools.partial(_in_proj_kernel, kinds),
        grid=(T // tm, nsteps),
        in_specs=[
            pl.BlockSpec((tm, D), lambda i, j: (i, 0)),
            pl.BlockSpec((None, 1, D), lambda i, j: (layer, 0, 0)),
            _layer_spec(w, layer, (D, tn), wcol),
        ],
        out_specs=out_specs,
        out_shape=out_shape,
        scratch_shapes=[pltpu.VMEM((tm, D), BF16)],
        compiler_params=_params(("parallel", "arbitrary"), 48),
        name="in_proj",
    )(x, g3, w)


def _lower_bound_kernel(lg_ref, o_ref):
    lg = lg_ref[...]
    depth = lg.shape[0]
    rows = [lg[i:i + 1, :] for i in range(depth)]
    m = rows[0]
    for r in rows[1:]:
        m = jnp.maximum(m, r)
    e = [jnp.exp(r - m) for r in rows]
    tot = e[0]
    for v in e[1:]:
        tot = tot + v
    c = e[0] / tot
    first = c
    o_ref[0:1, :] = c - first
    for i in range(1, depth):
        c = c + e[i] / tot
        o_ref[i:i + 1, :] = c - first


def _lower_bounds(lb_logits):
    return pl.pallas_call(
        _lower_bound_kernel,
        out_shape=jax.ShapeDtypeStruct(lb_logits.shape, F32),
        name="hgrn_lower_bounds",
    )(lb_logits)


def _neg_abs(x):
    bits = lax.bitcast_convert_type(x, jnp.uint32) | jnp.uint32(0x80000000)
    return lax.bitcast_convert_type(bits, F32)


def _group_ref_row(b, s):
    rows, width = b.shape
    gsz = 2 * s
    if gsz >= 8:
        parts = [jnp.broadcast_to(b[i * gsz + s - 1:i * gsz + s, :], (gsz, width)) for i in range(rows // gsz)]
        return parts[0] if len(parts) == 1 else jnp.concatenate(parts, axis=0)
    pos = lax.broadcasted_iota(jnp.int32, b.shape, 0) & (gsz - 1)
    out = b
    for p in range(gsz):
        d = p - (s - 1)
        if d != 0:
            out = jnp.where(pos == p, pltpu.roll(b, d % rows, 0), out)
    return out


def _hgrn_block(q, zf, vb, lb, states, seg):
    C = q.shape[0]
    fg = lb + (1.0 - lb) * jax.nn.sigmoid(zf)
    kk = 1.0 - fg
    g = jnp.log2(fg)

    row = lax.broadcasted_iota(jnp.int32, (C, LANE), 0)
    rr = lax.broadcasted_iota(jnp.int32, (C, C), 0)
    cc = lax.broadcasted_iota(jnp.int32, (C, C), 1)
    lseg = seg.bit_length() - 1
    tri = jnp.where((cc <= rr) & ((rr >> lseg) == (cc >> lseg)), 1.0, 0.0).astype(F32)
    b = jnp.dot(tri, g, precision=lax.Precision.HIGHEST, preferred_element_type=F32)

    a = jnp.where(rr == cc, _dot_nt(q.astype(BF16), kk.astype(BF16)), 0.0)
    s = seg // 2
    while s >= 1:
        ls = s.bit_length() - 1
        e = jnp.exp2(_neg_abs(b - _group_ref_row(b, s)))
        same = (rr >> (ls + 1)) == (cc >> (ls + 1))
        if s % SUBLANE == 0:
            nblk = C // s
            zero = jnp.zeros((s, LANE), F32)
            blk = lambda x, i: x[i * s:(i + 1) * s, :]
            qs = jnp.concatenate([blk(q, i) * blk(e, i) if i % 2 else zero for i in range(nblk)], axis=0)
            ks = jnp.concatenate([zero if i % 2 else blk(kk, i) * blk(e, i) for i in range(nblk)], axis=0)
            p = _dot_nt(qs.astype(BF16), ks.astype(BF16))
            pieces = []
            for i in range(nblk):
                if i % 2 == 0:
                    pieces.append(blk(a, i))
                elif 2 * s == C:
                    pieces.append(blk(a, i) + blk(p, i))
                else:
                    pieces.append(blk(a, i) + jnp.where(blk(same, i), blk(p, i), 0.0))
            a = jnp.concatenate(pieces, axis=0)
        else:
            right = ((row >> ls) & 1) == 1
            eq = jnp.where(right, e, 0.0)
            p = _dot_nt((q * eq).astype(BF16), (kk * (e - eq)).astype(BF16))
            a = a + jnp.where(same, p, 0.0)
        s //= 2
    o = _dot(a.astype(BF16), vb)

    qe = q * jnp.exp2(b)
    new_states = []
    for k, st in enumerate(states):
        bl = b[k * seg + seg - 1:k * seg + seg, :]
        if len(states) == 1:
            qk = qe
            ke = kk * jnp.exp2(bl - b)
        else:
            mine = (row >> lseg) == k
            qk = jnp.where(mine, qe, 0.0)
            ke = jnp.where(mine, kk * jnp.exp2(jnp.where(mine, bl - b, 0.0)), 0.0)
        o = o + _dot(qk.astype(BF16), st.astype(BF16))
        decay = jnp.transpose(jnp.broadcast_to(jnp.exp2(bl), (LANE, LANE)))
        new_states.append(st * decay + _dot_tn(ke.astype(BF16), vb))
    return o, new_states


def _hgrn_finish(o, og, hgn):
    return (_rms(o, hgn) * og.astype(F32)).astype(BF16)


def _hgrn_prompt_kernel(q_ref, zf_ref, v_ref, og_ref, lb_ref, hgn_ref, oh_ref, s_ref):
    @pl.when(pl.program_id(1) == 0)
    def _():
        s_ref[...] = jnp.zeros_like(s_ref)

    for h in range(HG_HEADS):
        cs = slice(h * LANE, (h + 1) * LANE)
        o, (st,) = _hgrn_block(q_ref[:, cs].astype(F32), zf_ref[:, cs], v_ref[:, cs], lb_ref[:, cs], [s_ref[h]],
                               HGRN_CHUNK)
        s_ref[h] = st
        oh_ref[:, cs] = _hgrn_finish(o, og_ref[:, cs], hgn_ref[:, cs])


def _hgrn_prompt(zb, zf, lbs3, hgn3, layer, batch):
    T = zb.shape[0] // batch
    C = HGRN_CHUNK
    nc = T // C
    W = HG_HEADS * LANE
    zspec = lambda col: pl.BlockSpec((C, W), lambda b, c: (b * nc + c, col))
    vec = pl.BlockSpec((None, 1, W), lambda b, c: (layer, 0, 0))
    return pl.pallas_call(
        _hgrn_prompt_kernel,
        grid=(batch, nc),
        in_specs=[zspec(ZB_Q), zspec(ZF_F), zspec(ZB_V), zspec(ZB_OG), vec, vec],
        out_specs=[
            pl.BlockSpec((C, W), lambda b, c: (b * nc + c, 0)),
            pl.BlockSpec((None, HG_HEADS, LANE, LANE), lambda b, c: (b, 0, 0, 0)),
        ],
        out_shape=[
            jax.ShapeDtypeStruct((batch * T, W), BF16),
            jax.ShapeDtypeStruct((batch, HG_HEADS, LANE, LANE), F32),
        ],
        compiler_params=_params(("parallel", "arbitrary"), 32),
        name="hgrn_prompt",
    )(zb, zf, zb, zb, lbs3, hgn3)


def _hgrn_decode_kernel(seq_len, q_ref, zf_ref, v_ref, og_ref, lb_ref, hgn_ref, s_ref, *rest):
    oh_ref, so_ref = rest[-2], rest[-1]
    rows = DEC_GROUP * seq_len
    for grp in range(s_ref.shape[0] // DEC_GROUP):
        rs = slice(grp * rows, (grp + 1) * rows)
        for h in range(HG_HEADS):
            cs = slice(h * LANE, (h + 1) * LANE)
            states = [s_ref[grp * DEC_GROUP + k, h] for k in range(DEC_GROUP)]
            o, new = _hgrn_block(q_ref[rs, cs].astype(F32), zf_ref[rs, cs], v_ref[rs, cs], lb_ref[:, cs], states,
                                 seq_len)
            for k in range(DEC_GROUP):
                so_ref[grp * DEC_GROUP + k, h] = new[k]
            oh_ref[rs, cs] = _hgrn_finish(o, og_ref[rs, cs], hgn_ref[:, cs])


def _hgrn_decode(zb, zf, lbs3, hgn3, state, stacked, layer, nseq, nb):
    seq_len = zb.shape[0] // nseq
    W = HG_HEADS * LANE
    zspec = lambda col: pl.BlockSpec((nb * seq_len, W), lambda i: (i, col))
    vec = pl.BlockSpec((None, 1, W), lambda i: (layer, 0, 0))
    sspec = pl.BlockSpec((None, nb, HG_HEADS, LANE, LANE), lambda i: (layer, i, 0, 0, 0))
    in_specs = [zspec(ZB_Q), zspec(ZF_F), zspec(ZB_V), zspec(ZB_OG), vec, vec, sspec]
    args = [zb, zf, zb, zb, lbs3, hgn3, state]
    aliases = {}
    if stacked is not None:
        in_specs.append(pl.BlockSpec(memory_space=pl.ANY))
        args.append(stacked)
        aliases = {len(args) - 1: 1}
    return pl.pallas_call(
        functools.partial(_hgrn_decode_kernel, seq_len),
        grid=(nseq // nb,),
        in_specs=in_specs,
        out_specs=[pl.BlockSpec((nb * seq_len, W), lambda i: (i, 0)), sspec],
        out_shape=[
            jax.ShapeDtypeStruct((nseq * seq_len, W), BF16),
            jax.ShapeDtypeStruct(state.shape, F32),
        ],
        input_output_aliases=aliases,
        compiler_params=_params(("parallel",), 48),
        name="hgrn_decode",
    )(*args)


def _pool_prompt_kernel(u_ref, wp_ref, sc_ref, op_ref, nb_ref):
    u = u_ref[...]
    T = u.shape[0]
    row = lax.broadcasted_iota(jnp.int32, (T, LANE), 0)
    for g, w in enumerate(POOL_WINDOWS):
        cs = slice(g * LANE, (g + 1) * LANE)
        ug = u[:, cs]
        s = ug
        d = 1
        while d < w:
            s = s + jnp.where(row >= d, pltpu.roll(s, d, 0), 0.0)
            d *= 2
        cnt = jnp.minimum(row + 1, w).astype(F32)
        dv = s / cnt - ug
        y = _dot(dv.astype(BF16), wp_ref[g].astype(BF16)) * sc_ref[:, cs]
        op_ref[:, cs] = y.astype(BF16)
    tail = u_ref[T - 16:T, :]
    nb_ref[...] = pltpu.roll(tail, 15, 0)[0:POOL_BUF, :]


def _pool_prompt(zf, w_pool, scale3, layer, batch):
    T = zf.shape[0] // batch
    G = len(POOL_WINDOWS)
    W = G * LANE
    return pl.pallas_call(
        _pool_prompt_kernel,
        grid=(batch,),
        in_specs=[
            pl.BlockSpec((T, W), lambda b: (b, ZF_U)),
            pl.BlockSpec((None, G, LANE, LANE), lambda b: (layer, 0, 0, 0)),
            pl.BlockSpec((None, 1, W), lambda b: (layer, 0, 0)),
        ],
        out_specs=[
            pl.BlockSpec((T, W), lambda b: (b, 0)),
            pl.BlockSpec((None, POOL_BUF, W), lambda b: (b, 0, 0)),
        ],
        out_shape=[
            jax.ShapeDtypeStruct((batch * T, W), BF16),
            jax.ShapeDtypeStruct((batch, POOL_BUF, W), F32),
        ],
        compiler_params=_params(("parallel",), 48),
        name="pool_prompt",
    )(zf, w_pool, scale3)


def _pool_decode_kernel(u_ref, buf_ref, wp_ref, sc_ref, op_ref, nb_ref):
    steps = u_ref.shape[0]
    for g, w in enumerate(POOL_WINDOWS):
        cs = slice(g * LANE, (g + 1) * LANE)
        wg = wp_ref[g].astype(BF16)
        for t in range(steps):
            n_u = min(t + 1, w)
            acc = u_ref[t, :, cs]
            for j in range(t - n_u + 1, t):
                acc = acc + u_ref[j, :, cs]
            for i in range(POOL_BUF - (w - n_u), POOL_BUF):
                acc = acc + buf_ref[i, :, cs]
            dv = acc * (1.0 / w) - u_ref[t, :, cs]
            op_ref[t, :, cs] = _dot(dv.astype(BF16), wg) * sc_ref[:, cs]
    for i in range(POOL_BUF - steps):
        nb_ref[i] = buf_ref[i + steps]
    for t in range(steps):
        nb_ref[POOL_BUF - steps + t] = u_ref[t]


def _pool_decode(u_t, buf_t, w_pool, scale3, layer):
    steps, nseq, W = u_t.shape
    G = len(POOL_WINDOWS)
    return pl.pallas_call(
        _pool_decode_kernel,
        grid=(1,),
        in_specs=[
            pl.BlockSpec((steps, nseq, W), lambda i: (0, 0, 0)),
            pl.BlockSpec((POOL_BUF, nseq, W), lambda i: (0, 0, 0)),
            pl.BlockSpec((None, G, LANE, LANE), lambda i: (layer, 0, 0, 0)),
            pl.BlockSpec((None, 1, W), lambda i: (layer, 0, 0)),
        ],
        out_specs=[
            pl.BlockSpec((steps, nseq, W), lambda i: (0, 0, 0)),
            pl.BlockSpec((POOL_BUF, nseq, W), lambda i: (0, 0, 0)),
        ],
        out_shape=[
            jax.ShapeDtypeStruct((steps, nseq, W), F32),
            jax.ShapeDtypeStruct((POOL_BUF, nseq, W), F32),
        ],
        compiler_params=_params(("arbitrary",), 32),
        name="pool_decode",
    )(u_t, buf_t, w_pool, scale3)


def _softmax_rows(s):
    e = jnp.exp(s - jnp.max(s, axis=-1, keepdims=True))
    return e / jnp.sum(e, axis=-1, keepdims=True)


def _xattn_prompt_kernel(q_ref, k_ref, v_ref, o_ref):
    scale = LANE ** -0.5
    for h in range(XA_HEADS):
        cs = slice(h * LANE, (h + 1) * LANE)
        s = _dot_nt(q_ref[:, cs].astype(BF16), k_ref[:, cs].astype(BF16)) * scale
        p = _softmax_rows(s)
        o_ref[:, cs] = _dot(p.astype(BF16), v_ref[:, cs].astype(BF16)).astype(BF16)


def _xattn_prompt(zf, mk, mv, batch, tq):
    T = zf.shape[0] // batch
    W = XA_HEADS * LANE
    nq = T // tq
    n_mem = mk.shape[1]
    mspec = pl.BlockSpec((None, n_mem, W), lambda b, i: (b, 0, 0))
    return pl.pallas_call(
        _xattn_prompt_kernel,
        grid=(batch, nq),
        in_specs=[pl.BlockSpec((tq, W), lambda b, i: (b * nq + i, ZF_X)), mspec, mspec],
        out_specs=pl.BlockSpec((tq, W), lambda b, i: (b * nq + i, 0)),
        out_shape=jax.ShapeDtypeStruct((batch * T, W), BF16),
        compiler_params=_params(("parallel", "parallel"), 48),
        name="xattn_prompt",
    )(zf, mk, mv)


def _xattn_decode_kernel(seq_len, q_ref, k_ref, v_ref, o_ref):
    scale = LANE ** -0.5
    rows = DEC_GROUP * seq_len
    lseq = seq_len.bit_length() - 1
    lrows = rows.bit_length() - 1
    nk = k_ref.shape[1]
    row_s = lax.broadcasted_iota(jnp.int32, (XA_HEADS * rows, nk), 0)
    col_s = lax.broadcasted_iota(jnp.int32, (XA_HEADS * rows, nk), 1)
    own_head = (col_s & (XA_HEADS - 1)) == (row_s >> lrows)
    seq_s = (row_s & (rows - 1)) >> lseq
    seq_o = (lax.broadcasted_iota(jnp.int32, (XA_HEADS * rows, LANE), 0) & (rows - 1)) >> lseq
    for grp in range(k_ref.shape[0] // DEC_GROUP):
        rs = slice(grp * rows, (grp + 1) * rows)
        qb = jnp.concatenate([q_ref[rs, h * LANE:(h + 1) * LANE] for h in range(XA_HEADS)], axis=0).astype(BF16)
        s = jnp.full((XA_HEADS * rows, nk), -jnp.inf, F32)
        for k in range(DEC_GROUP):
            sk = _dot_nt(qb, k_ref[grp * DEC_GROUP + k].astype(BF16))
            s = jnp.where(own_head & (seq_s == k), sk, s)
        p = _softmax_rows(s * scale).astype(BF16)
        o = jnp.zeros((XA_HEADS * rows, LANE), F32)
        for k in range(DEC_GROUP):
            o = jnp.where(seq_o == k, _dot(p, v_ref[grp * DEC_GROUP + k].astype(BF16)), o)
        for h in range(XA_HEADS):
            o_ref[rs, h * LANE:(h + 1) * LANE] = o[h * rows:(h + 1) * rows, :].astype(BF16)


def _xattn_decode(zf, cache_k, cache_v, layer, nseq, nb):
    seq_len = zf.shape[0] // nseq
    W = XA_HEADS * LANE
    cspec = pl.BlockSpec((None, nb) + cache_k.shape[2:], lambda i: (layer, i, 0, 0))
    return pl.pallas_call(
        functools.partial(_xattn_decode_kernel, seq_len),
        grid=(nseq // nb,),
        in_specs=[pl.BlockSpec((nb * seq_len, W), lambda i: (i, ZF_X)), cspec, cspec],
        out_specs=pl.BlockSpec((nb * seq_len, W), lambda i: (i, 0)),
        out_shape=jax.ShapeDtypeStruct((nseq * seq_len, W), BF16),
        compiler_params=_params(("parallel",), 48),
        name="xattn_decode",
    )(zf, cache_k, cache_v)


def _merge_kernel(x_ref, oh_ref, op_ref, ox_ref, g0_ref, g1_ref, g2_ref, wb_ref, wo_ref, o_ref, y_ref):
    j = pl.program_id(1)
    nj, _, tn = y_ref.shape

    @pl.when(j < nj)
    def _():
        wh = oh_ref.shape[1]
        wp = op_ref.shape[1]
        y = g0_ref[...].astype(F32) * _dot(oh_ref[...].astype(BF16), wb_ref[0:wh, :])
        y += g1_ref[...].astype(F32) * _dot(op_ref[...].astype(BF16), wb_ref[wh:wh + wp, :])
        y += g2_ref[...].astype(F32) * _dot(ox_ref[...].astype(BF16), wb_ref[wh + wp:, :])
        y_ref[j] = y.astype(BF16)

    @pl.when(j >= nj)
    def _():
        acc = x_ref[...]
        for c in range(nj):
            acc += _dot(y_ref[c], wo_ref[c * tn:(c + 1) * tn, :])
        o_ref[...] = acc


def _merge(x, oh, op, ox, zb, wb, wo, layer, tm, tn):
    T, D = x.shape
    gcol = ZB_GATES * HG_HEADS * LANE // tn
    nj = D // tn
    first = lambda j: jnp.minimum(j, nj - 1)
    second = lambda j: jnp.maximum(j - nj, 0)
    full = lambda a: pl.BlockSpec((tm, a.shape[1]), lambda i, j: (i, 0))
    gate = lambda k: pl.BlockSpec((tm, tn), lambda i, j: (i, gcol + k * nj + first(j)))
    return pl.pallas_call(
        _merge_kernel,
        grid=(T // tm, 2 * nj),
        in_specs=[
            pl.BlockSpec((tm, tn), lambda i, j: (i, second(j))),
            full(oh), full(op), full(ox), gate(0), gate(1), gate(2),
            pl.BlockSpec((None, wb.shape[1], tn), lambda i, j: (layer, 0, first(j))),
            pl.BlockSpec((None, D, tn), lambda i, j: (layer, 0, second(j))),
        ],
        out_specs=pl.BlockSpec((tm, tn), lambda i, j: (i, second(j))),
        out_shape=jax.ShapeDtypeStruct((T, D), F32),
        scratch_shapes=[pltpu.VMEM((nj, tm, tn), BF16)],
        compiler_params=_params(("parallel", "arbitrary"), 48),
        name="merge_out",
    )(x, oh, op, ox, zb, zb, zb, wb, wo)


def _norm_kernel(x_ref, g_ref, o_ref):
    o_ref[...] = _rms(x_ref[...], g_ref[...])


def _final_norm(x, g2, tm):
    T, D = x.shape
    return pl.pallas_call(
        _norm_kernel,
        grid=(T // tm,),
        in_specs=[pl.BlockSpec((tm, D), lambda i: (i, 0)), pl.BlockSpec((1, D), lambda i: (0, 0))],
        out_specs=pl.BlockSpec((tm, D), lambda i: (i, 0)),
        out_shape=jax.ShapeDtypeStruct((T, D), F32),
        compiler_params=_params(("parallel",), 40),
        name="final_norm",
    )(x, g2)


def kernel(x_prompt, x_sample, state_hgrn, state_pool, cache_mem_k, cache_mem_v, mem_prompt, ffn1_norm, ffn1_w1, ffn1_w3, ffn1_w2, mix_norm, w_in, lb_logits, hg_norm, w_pool, pool_scale, mem_norm, w_mk, w_mv, w_branch, w_out, ffn2_norm, ffn2_w1, ffn2_w3, ffn2_w2, final_norm):
    B, T, D = x_prompt.shape
    nseq, steps, _ = x_sample.shape
    depth = w_in.shape[0]
    n_mem = mem_prompt.shape[1]
    xw = XA_HEADS * LANE
    pw = len(POOL_WINDOWS) * LANE

    tm_p = min(1024, B * T)
    tm_s = min(512, nseq * steps)

    vec3 = lambda a: a.reshape(a.shape[0], 1, a.shape[1])
    f1n, f2n = vec3(ffn1_norm), vec3(ffn2_norm)
    w_br_b, w_out_b = w_branch.astype(BF16), w_out.astype(BF16)
    mix3, mem3, hgn3, psc3 = vec3(mix_norm), vec3(mem_norm), vec3(hg_norm), vec3(pool_scale)
    lbs3 = vec3(_lower_bounds(lb_logits))

    xp = x_prompt.reshape(B * T, D)
    xs = x_sample.reshape(nseq * steps, D)
    mem2 = mem_prompt.reshape(B * n_mem, D)
    ck = cache_mem_k.reshape(depth, nseq, n_mem * XA_HEADS, LANE)
    cv = cache_mem_v.reshape(depth, nseq, n_mem * XA_HEADS, LANE)

    tf = _tile(ffn1_w1.shape[-1], 512)
    tf_s = _tile(ffn1_w1.shape[-1], 256)
    tn_in = _tile(math.gcd(HG_HEADS * LANE, N_BRANCH * D), 1024)
    tn_mrg = _tile(D, 512)
    tm_mem = min(512, B * n_mem)
    hs_p, pb_p, mk_p, mv_p, pb_s = [], [], [], [], []
    hs_s = None
    for l in range(depth):
        mk = _rms_matmul(mem2, mem3, w_mk, l, tm_mem, xw)
        mv = _rms_matmul(mem2, mem3, w_mv, l, tm_mem, xw)
        mk_p.append(mk.reshape(B, n_mem, XA_HEADS, LANE))
        mv_p.append(mv.reshape(B, n_mem, XA_HEADS, LANE))

        xs, w1b, w3b, w2b = _ffn(xs, f1n, ffn1_w1, ffn1_w3, ffn1_w2, l, tm_s, tf_s)
        xp = _ffn(xp, f1n, w1b, w3b, w2b, l, tm_p, tf)

        zsb, zsf, w_in_b = _in_proj(xs, mix3, w_in, l, tm_s, tn_in, emit=True)
        zpb, zpf = _in_proj(xp, mix3, w_in_b, l, tm_p, tn_in)

        ohp, sp = _hgrn_prompt(zpb, zpf, lbs3, hgn3, l, B)
        ohs, hs_s = _hgrn_decode(zsb, zsf, lbs3, hgn3, state_hgrn, hs_s, l, nseq, 8)
        hs_p.append(sp)

        opp, bp = _pool_prompt(zpf, w_pool, psc3, l, B)
        pb_p.append(bp)
        u_t = zsf[:, ZF_U * pw:(ZF_U + 1) * pw].reshape(nseq, steps, pw).transpose(1, 0, 2)
        buf_t = state_pool[l].transpose(1, 0, 2)
        ops_t, nb_t = _pool_decode(u_t, buf_t, w_pool, psc3, l)
        ops = ops_t.transpose(1, 0, 2).reshape(nseq * steps, pw)
        pb_s.append(nb_t.transpose(1, 0, 2))

        oxp = _xattn_prompt(zpf, mk.reshape(B, n_mem, xw), mv.reshape(B, n_mem, xw), B, min(1024, T))
        oxs = _xattn_decode(zsf, ck, cv, l, nseq, 8)

        xp = _merge(xp, ohp, opp, oxp, zpb, w_br_b, w_out_b, l, tm_p, tn_mrg)
        xs = _merge(xs, ohs, ops, oxs, zsb, w_br_b, w_out_b, l, tm_s, tn_mrg)

        xs, w1b, w3b, w2b = _ffn(xs, f2n, ffn2_w1, ffn2_w3, ffn2_w2, l, tm_s, tf_s)
        xp = _ffn(xp, f2n, w1b, w3b, w2b, l, tm_p, tf)

    fn2 = final_norm.reshape(1, D)
    y_prompt = _final_norm(xp, fn2, tm_p).reshape(B, T, D)
    y_sample = _final_norm(xs, fn2, tm_s).reshape(nseq, steps, D)
    return (y_prompt, y_sample, jnp.stack(hs_p), jnp.stack(pb_p), jnp.stack(mk_p), jnp.stack(mv_p),
            hs_s, jnp.stack(pb_s))
```

```python
import functools
import math

import jax
import jax.numpy as jnp
from jax import lax
from jax.experimental import pallas as pl
from jax.experimental.pallas import tpu as pltpu

F32 = jnp.float32
BF16 = jnp.bfloat16
EPS = 1e-6

LANE = 128
SUBLANE = 8
HG_HEADS = 8
POOL_WINDOWS = (2, 4, 8, 16)
POOL_BUF = max(POOL_WINDOWS) - 1
XA_HEADS = 4
N_BRANCH = 3

COL_Q, COL_F, COL_I, COL_OG = 0, HG_HEADS, 2 * HG_HEADS, 3 * HG_HEADS
COL_U = 4 * HG_HEADS
COL_X = COL_U + len(POOL_WINDOWS)
COL_G = COL_X + XA_HEADS
ZB_Q, ZB_V, ZB_OG, ZB_GATES = 0, 1, 2, 3
ZF_F = 0
ZF_U, ZF_X = 2, 3

HGRN_CHUNK = 128
DEC_GROUP = 4


def _tile(n, preferred):
    t = preferred
    while n % t:
        t -= LANE
    return t


def _params(semantics, vmem_mib):
    return pltpu.CompilerParams(dimension_semantics=semantics, vmem_limit_bytes=vmem_mib * 1024 * 1024)


def _rms(x, g):
    return x * lax.rsqrt(jnp.mean(x * x, axis=-1, keepdims=True) + EPS) * g


def _sigmoid(x):
    return 0.5 * jnp.tanh(0.5 * x) + 0.5


def _silu(x):
    return x * _sigmoid(x)


def _dot(a, b):
    return jnp.dot(a, b, preferred_element_type=F32)


def _dot_nt(a, b):
    return lax.dot_general(a, b, (((1,), (1,)), ((), ())), preferred_element_type=F32)


def _dot_tn(a, b):
    return lax.dot_general(a, b, (((0,), (0,)), ((), ())), preferred_element_type=F32)


def _ffn_kernel(x_ref, g_ref, w1_ref, w3_ref, w2_ref, o_ref, xn_ref):
    @pl.when(pl.program_id(1) == 0)
    def _():
        x = x_ref[...]
        xn_ref[...] = _rms(x, g_ref[...]).astype(BF16)
        o_ref[...] = x

    xn = xn_ref[...]
    h = (_silu(_dot(xn, w1_ref[...])) * _dot(xn, w3_ref[...])).astype(BF16)
    cw = min(w2_ref.shape[0], o_ref.shape[1])
    for n in range(o_ref.shape[1] // cw):
        cs = slice(n * cw, (n + 1) * cw)
        o_ref[:, cs] += 0.5 * _dot(h, w2_ref[:, cs])


def _ffn_cast_kernel(x_ref, g_ref, w1_ref, w3_ref, w2_ref, o_ref, w1b_ref, w3b_ref, w2b_ref, xn_ref):
    w1b_ref[...] = w1_ref[...].astype(BF16)
    w3b_ref[...] = w3_ref[...].astype(BF16)
    w2b_ref[...] = w2_ref[...].astype(BF16)
    _ffn_kernel(x_ref, g_ref, w1b_ref, w3b_ref, w2b_ref, o_ref, xn_ref)


def _layer_spec(w, layer, block, index):
    if w.ndim == 2:
        return pl.BlockSpec(block, index)
    return pl.BlockSpec((None,) + block, lambda i, j: (layer,) + index(i, j))


def _ffn(x, g3, w1, w3, w2, layer, tm, tf):
    T, D = x.shape
    F = w1.shape[-1]
    cast = w1.dtype != BF16
    assert not cast or T == tm
    col = lambda i, j: (0, j)
    row = lambda i, j: (j, 0)
    out_specs = [pl.BlockSpec((tm, D), lambda i, j: (i, 0))]
    out_shape = [jax.ShapeDtypeStruct((T, D), F32)]
    if cast:
        out_specs += [pl.BlockSpec((D, tf), col), pl.BlockSpec((D, tf), col), pl.BlockSpec((tf, D), row)]
        out_shape += [jax.ShapeDtypeStruct((D, F), BF16), jax.ShapeDtypeStruct((D, F), BF16),
                      jax.ShapeDtypeStruct((F, D), BF16)]
    out = pl.pallas_call(
        _ffn_cast_kernel if cast else _ffn_kernel,
        grid=(T // tm, F // tf),
        in_specs=[
            pl.BlockSpec((tm, D), lambda i, j: (i, 0)),
            pl.BlockSpec((None, 1, D), lambda i, j: (layer, 0, 0)),
            _layer_spec(w1, layer, (D, tf), col),
            _layer_spec(w3, layer, (D, tf), col),
            _layer_spec(w2, layer, (tf, D), row),
        ],
        out_specs=out_specs,
        out_shape=out_shape,
        scratch_shapes=[pltpu.VMEM((tm, D), BF16)],
        compiler_params=_params(("parallel", "arbitrary"), 58),
        name="ffn_cast" if cast else "ffn",
    )(x, g3, w1, w3, w2)
    return out if cast else out[0]


def _rms_matmul_kernel(x_ref, g_ref, w_ref, o_ref, *rest):
    xn_ref = rest[-1]

    @pl.when(pl.program_id(1) == 0)
    def _():
        xn_ref[...] = _rms(x_ref[...], g_ref[...]).astype(BF16)

    w = w_ref[...].astype(BF16)
    if len(rest) == 2:
        rest[0][...] = w
    o_ref[...] = _dot(xn_ref[...], w)


def _rms_matmul(x, g3, w, layer, tm, tn, emit=False):
    T, D = x.shape
    N = w.shape[-1]
    assert not emit or T == tm
    col = lambda i, j: (0, j)
    out_specs = [pl.BlockSpec((tm, tn), lambda i, j: (i, j))]
    out_shape = [jax.ShapeDtypeStruct((T, N), F32)]
    if emit:
        out_specs.append(pl.BlockSpec((D, tn), col))
        out_shape.append(jax.ShapeDtypeStruct((D, N), BF16))
    out = pl.pallas_call(
        _rms_matmul_kernel,
        grid=(T // tm, N // tn),
        in_specs=[
            pl.BlockSpec((tm, D), lambda i, j: (i, 0)),
            pl.BlockSpec((None, 1, D), lambda i, j: (layer, 0, 0)),
            _layer_spec(w, layer, (D, tn), col),
        ],
        out_specs=out_specs,
        out_shape=out_shape,
        scratch_shapes=[pltpu.VMEM((tm, D), BF16)],
        compiler_params=_params(("parallel", "arbitrary"), 48),
        name="rms_matmul",
    )(x, g3, w)
    return out if emit else out[0]


def _runs(seq):
    runs = []
    for j, v in enumerate(seq):
        if runs and runs[-1][2] == v - j:
            runs[-1][1] = j + 1
        else:
            runs.append([j, j + 1, v - j])
    return [tuple(r) for r in runs]


def _in_runs(j, runs):
    hit = None
    for lo, hi, _ in runs:
        c = (j >= lo) & (j < hi)
        hit = c if hit is None else hit | c
    return hit


def _lookup(j, runs):
    out = 0
    for lo, hi, off in runs:
        out = out + jnp.where((j >= lo) & (j < hi), j + off, 0)
    return out


def _in_proj_plan(d_model, tn):
    t = lambda blocks: blocks * LANE // tn
    nq, nux, ng = t(HG_HEADS), t(len(POOL_WINDOWS) + XA_HEADS), N_BRANCH * d_model // tn
    src = lambda start, n: list(range(t(start), t(start) + n))
    order = src(COL_F, nq) + src(COL_U, nux) + src(COL_Q, nq) + src(COL_I, nq) + src(COL_OG, nq) + src(COL_G, ng)
    n32 = nq + nux
    kinds = {
        "plain": [(n32 + nq, n32 + 2 * nq, 0)],
        "sigmoid": [(n32 + 3 * nq, len(order), 0)],
    }
    return _runs(order), kinds, n32


def _in_proj_kernel(kinds, n32, x_ref, g_ref, w_ref, zb_ref, zf_ref, *rest):
    xn_ref = rest[-1]
    j = pl.program_id(1)

    @pl.when(j == 0)
    def _():
        xn_ref[...] = _rms(x_ref[...], g_ref[...]).astype(BF16)

    w = w_ref[...].astype(BF16)
    if len(rest) == 2:
        rest[0][...] = w
    z = _dot(xn_ref[...], w)

    s = _sigmoid(z)
    left = jnp.where(_in_runs(j, kinds["sigmoid"]), 1.0, z)
    right = jnp.where(_in_runs(j, kinds["plain"]), 1.0, s)
    zb_ref[...] = (left * right).astype(BF16)

    @pl.when(j < n32)
    def _():
        zf_ref[...] = z


def _in_proj(x, g3, w, layer, tm, tn, emit=False):
    T, D = x.shape
    N = w.shape[-1]
    assert not emit or T == tm
    order, kinds, n32 = _in_proj_plan(D, tn)
    nsteps = N // tn
    wcol = lambda i, j: (0, _lookup(j, order))
    out_specs = [
        pl.BlockSpec((tm, tn), lambda i, j: (i, jnp.maximum(j - n32, 0))),
        pl.BlockSpec((tm, tn), lambda i, j: (i, jnp.minimum(j, n32 - 1))),
    ]
    out_shape = [
        jax.ShapeDtypeStruct((T, (nsteps - n32) * tn), BF16),
        jax.ShapeDtypeStruct((T, n32 * tn), F32),
    ]
    if emit:
        out_specs.append(pl.BlockSpec((D, tn), wcol))
        out_shape.append(jax.ShapeDtypeStruct((D, N), BF16))
    return pl.pallas_call(
        functools.partial(_in_proj_kernel, kinds, n32),
        grid=(T // tm, nsteps),
        in_specs=[
            pl.BlockSpec((tm, D), lambda i, j: (i, 0)),
            pl.BlockSpec((None, 1, D), lambda i, j: (layer, 0, 0)),
            _layer_spec(w, layer, (D, tn), wcol),
        ],
        out_specs=out_specs,
        out_shape=out_shape,
        scratch_shapes=[pltpu.VMEM((tm, D), BF16)],
        compiler_params=_params(("parallel", "arbitrary"), 48),
        name="in_proj",
    )(x, g3, w)


def _lower_bound_kernel(lg_ref, o_ref):
    lg = lg_ref[...]
    depth = lg.shape[0]
    rows = [lg[i:i + 1, :] for i in range(depth)]
    m = rows[0]
    for r in rows[1:]:
        m = jnp.maximum(m, r)
    e = [jnp.exp(r - m) for r in rows]
    tot = e[0]
    for v in e[1:]:
        tot = tot + v
    c = e[0] / tot
    first = c
    o_ref[0:1, :] = c - first
    for i in range(1, depth):
        c = c + e[i] / tot
        o_ref[i:i + 1, :] = c - first


def _lower_bounds(lb_logits):
    return pl.pallas_call(
        _lower_bound_kernel,
        out_shape=jax.ShapeDtypeStruct(lb_logits.shape, F32),
        name="hgrn_lower_bounds",
    )(lb_logits)


def _neg_abs(x):
    bits = lax.bitcast_convert_type(x, jnp.uint32) | jnp.uint32(0x80000000)
    return lax.bitcast_convert_type(bits, F32)


def _group_ref_row(b, s):
    rows, width = b.shape
    gsz = 2 * s
    if gsz >= 8:
        parts = [jnp.broadcast_to(b[i * gsz + s - 1:i * gsz + s, :], (gsz, width)) for i in range(rows // gsz)]
        return parts[0] if len(parts) == 1 else jnp.concatenate(parts, axis=0)
    pos = lax.broadcasted_iota(jnp.int32, b.shape, 0) & (gsz - 1)
    out = b
    for p in range(gsz):
        d = p - (s - 1)
        if d != 0:
            out = jnp.where(pos == p, pltpu.roll(b, d % rows, 0), out)
    return out


def _hgrn_block(q, zf, vb, lb, states, seg):
    C = q.shape[0]
    fg = lb + (1.0 - lb) * jax.nn.sigmoid(zf)
    kk = 1.0 - fg
    g = jnp.log2(fg)

    row = lax.broadcasted_iota(jnp.int32, (C, LANE), 0)
    rr = lax.broadcasted_iota(jnp.int32, (C, C), 0)
    cc = lax.broadcasted_iota(jnp.int32, (C, C), 1)
    lseg = seg.bit_length() - 1
    tri = jnp.where((cc <= rr) & ((rr >> lseg) == (cc >> lseg)), 1.0, 0.0).astype(F32)
    b = jnp.dot(tri, g, precision=lax.Precision.HIGHEST, preferred_element_type=F32)

    a = jnp.where(rr == cc, _dot_nt(q.astype(BF16), kk.astype(BF16)), 0.0)
    s = seg // 2
    while s >= 1:
        ls = s.bit_length() - 1
        e = jnp.exp2(_neg_abs(b - _group_ref_row(b, s)))
        same = (rr >> (ls + 1)) == (cc >> (ls + 1))
        if s % SUBLANE == 0:
            nblk = C // s
            zero = jnp.zeros((s, LANE), F32)
            blk = lambda x, i: x[i * s:(i + 1) * s, :]
            qs = jnp.concatenate([blk(q, i) * blk(e, i) if i % 2 else zero for i in range(nblk)], axis=0)
            ks = jnp.concatenate([zero if i % 2 else blk(kk, i) * blk(e, i) for i in range(nblk)], axis=0)
            p = _dot_nt(qs.astype(BF16), ks.astype(BF16))
            pieces = []
            for i in range(nblk):
                if i % 2 == 0:
                    pieces.append(blk(a, i))
                elif 2 * s == C:
                    pieces.append(blk(a, i) + blk(p, i))
                else:
                    pieces.append(blk(a, i) + jnp.where(blk(same, i), blk(p, i), 0.0))
            a = jnp.concatenate(pieces, axis=0)
        else:
            right = ((row >> ls) & 1) == 1
            eq = jnp.where(right, e, 0.0)
            p = _dot_nt((q * eq).astype(BF16), (kk * (e - eq)).astype(BF16))
            a = a + jnp.where(same, p, 0.0)
        s //= 2
    o = _dot(a.astype(BF16), vb)

    qe = q * jnp.exp2(b)
    new_states = []
    for k, st in enumerate(states):
        bl = b[k * seg + seg - 1:k * seg + seg, :]
        if len(states) == 1:
            qk = qe
            ke = kk * jnp.exp2(bl - b)
        else:
            mine = (row >> lseg) == k
            qk = jnp.where(mine, qe, 0.0)
            ke = jnp.where(mine, kk * jnp.exp2(jnp.where(mine, bl - b, 0.0)), 0.0)
        o = o + _dot(qk.astype(BF16), st.astype(BF16))
        decay = jnp.transpose(jnp.broadcast_to(jnp.exp2(bl), (LANE, LANE)))
        new_states.append(st * decay + _dot_tn(ke.astype(BF16), vb))
    return o, new_states


def _hgrn_wide(q, zf, vb, lb, states, seg):
    C, W = q.shape
    heads = [slice(h * LANE, (h + 1) * LANE) for h in range(W // LANE)]
    nseg = C // seg
    fg = lb + (1.0 - lb) * jax.nn.sigmoid(zf)
    kk = 1.0 - fg
    g = jnp.log2(fg)

    row = lax.broadcasted_iota(jnp.int32, (C, W), 0)
    rr = lax.broadcasted_iota(jnp.int32, (C, C), 0)
    cc = lax.broadcasted_iota(jnp.int32, (C, C), 1)
    lseg = seg.bit_length() - 1
    tri = jnp.where((cc <= rr) & ((rr >> lseg) == (cc >> lseg)), 1.0, 0.0).astype(F32)
    b = jnp.dot(tri, g, precision=lax.Precision.HIGHEST, preferred_element_type=F32)

    qb, kb = q.astype(BF16), kk.astype(BF16)
    a = [jnp.where(rr == cc, _dot_nt(qb[:, hs], kb[:, hs]), 0.0) for hs in heads]
    s = seg // 2
    while s >= 1:
        ls = s.bit_length() - 1
        e = jnp.exp2(_neg_abs(b - _group_ref_row(b, s)))
        same = (rr >> (ls + 1)) == (cc >> (ls + 1))
        if s % SUBLANE == 0:
            nblk = C // s
            zero = jnp.zeros((s, W), F32)
            blk = lambda x, i: x[i * s:(i + 1) * s, :]
            qs = jnp.concatenate([blk(q, i) * blk(e, i) if i % 2 else zero for i in range(nblk)], axis=0).astype(BF16)
            ks = jnp.concatenate([zero if i % 2 else blk(kk, i) * blk(e, i) for i in range(nblk)], axis=0).astype(BF16)
            for h, hs in enumerate(heads):
                p = _dot_nt(qs[:, hs], ks[:, hs])
                pieces = []
                for i in range(nblk):
                    if i % 2 == 0:
                        pieces.append(blk(a[h], i))
                    elif 2 * s == C:
                        pieces.append(blk(a[h], i) + blk(p, i))
                    else:
                        pieces.append(blk(a[h], i) + jnp.where(blk(same, i), blk(p, i), 0.0))
                a[h] = jnp.concatenate(pieces, axis=0)
        else:
            right = ((row >> ls) & 1) == 1
            eq = jnp.where(right, e, 0.0)
            qs, ks = (q * eq).astype(BF16), (kk * (e - eq)).astype(BF16)
            for h, hs in enumerate(heads):
                a[h] = a[h] + jnp.where(same, _dot_nt(qs[:, hs], ks[:, hs]), 0.0)
        s //= 2
    o = [_dot(a[h].astype(BF16), vb[:, hs]) for h, hs in enumerate(heads)]

    qe = q * jnp.exp2(b)
    new_states = [[] for _ in heads]
    for k in range(nseg):
        bl = b[k * seg + seg - 1:k * seg + seg, :]
        if nseg == 1:
            qk = qe
            ke = kk * jnp.exp2(bl - b)
        else:
            mine = (row >> lseg) == k
            qk = jnp.where(mine, qe, 0.0)
            ke = jnp.where(mine, kk * jnp.exp2(jnp.where(mine, bl - b, 0.0)), 0.0)
        qkb, keb, ebl = qk.astype(BF16), ke.astype(BF16), jnp.exp2(bl)
        for h, hs in enumerate(heads):
            st = states[h][k]
            o[h] = o[h] + _dot(qkb[:, hs], st.astype(BF16))
            decay = jnp.transpose(jnp.broadcast_to(ebl[:, hs], (LANE, LANE)))
            new_states[h].append(st * decay + _dot_tn(keb[:, hs], vb[:, hs]))
    return o, new_states


def _hgrn_finish(o, og, hgn):
    return (_rms(o, hgn) * og.astype(F32)).astype(BF16)


def _hgrn_prompt_kernel(q_ref, zf_ref, v_ref, og_ref, lb_ref, hgn_ref, oh_ref, s_ref):
    @pl.when(pl.program_id(1) == 0)
    def _():
        s_ref[...] = jnp.zeros_like(s_ref)

    states = [[s_ref[h]] for h in range(HG_HEADS)]
    o, new = _hgrn_wide(q_ref[...].astype(F32), zf_ref[...], v_ref[...], lb_ref[...], states, HGRN_CHUNK)
    for h in range(HG_HEADS):
        cs = slice(h * LANE, (h + 1) * LANE)
        s_ref[h] = new[h][0]
        oh_ref[:, cs] = _hgrn_finish(o[h], og_ref[:, cs], hgn_ref[:, cs])


def _hgrn_prompt(zb, zf, lbs3, hgn3, layer, batch):
    T = zb.shape[0] // batch
    C = HGRN_CHUNK
    nc = T // C
    W = HG_HEADS * LANE
    zspec = lambda col: pl.BlockSpec((C, W), lambda b, c: (b * nc + c, col))
    vec = pl.BlockSpec((None, 1, W), lambda b, c: (layer, 0, 0))
    return pl.pallas_call(
        _hgrn_prompt_kernel,
        grid=(batch, nc),
        in_specs=[zspec(ZB_Q), zspec(ZF_F), zspec(ZB_V), zspec(ZB_OG), vec, vec],
        out_specs=[
            pl.BlockSpec((C, W), lambda b, c: (b * nc + c, 0)),
            pl.BlockSpec((None, HG_HEADS, LANE, LANE), lambda b, c: (b, 0, 0, 0)),
        ],
        out_shape=[
            jax.ShapeDtypeStruct((batch * T, W), BF16),
            jax.ShapeDtypeStruct((batch, HG_HEADS, LANE, LANE), F32),
        ],
        compiler_params=_params(("parallel", "arbitrary"), 32),
        name="hgrn_prompt",
    )(zb, zf, zb, zb, lbs3, hgn3)


def _hgrn_decode_kernel(seq_len, q_ref, zf_ref, v_ref, og_ref, lb_ref, hgn_ref, s_ref, *rest):
    oh_ref, so_ref = rest[-2], rest[-1]
    rows = DEC_GROUP * seq_len
    for grp in range(s_ref.shape[0] // DEC_GROUP):
        rs = slice(grp * rows, (grp + 1) * rows)
        states = [[s_ref[grp * DEC_GROUP + k, h] for k in range(DEC_GROUP)] for h in range(HG_HEADS)]
        o, new = _hgrn_wide(q_ref[rs, :].astype(F32), zf_ref[rs, :], v_ref[rs, :], lb_ref[...], states, seq_len)
        for h in range(HG_HEADS):
            cs = slice(h * LANE, (h + 1) * LANE)
            for k in range(DEC_GROUP):
                so_ref[grp * DEC_GROUP + k, h] = new[h][k]
            oh_ref[rs, cs] = _hgrn_finish(o[h], og_ref[rs, cs], hgn_ref[:, cs])


def _hgrn_decode(zb, zf, lbs3, hgn3, state, stacked, layer, nseq, nb):
    seq_len = zb.shape[0] // nseq
    W = HG_HEADS * LANE
    zspec = lambda col: pl.BlockSpec((nb * seq_len, W), lambda i: (i, col))
    vec = pl.BlockSpec((None, 1, W), lambda i: (layer, 0, 0))
    sspec = pl.BlockSpec((None, nb, HG_HEADS, LANE, LANE), lambda i: (layer, i, 0, 0, 0))
    in_specs = [zspec(ZB_Q), zspec(ZF_F), zspec(ZB_V), zspec(ZB_OG), vec, vec, sspec]
    args = [zb, zf, zb, zb, lbs3, hgn3, state]
    aliases = {}
    if stacked is not None:
        in_specs.append(pl.BlockSpec(memory_space=pl.ANY))
        args.append(stacked)
        aliases = {len(args) - 1: 1}
    return pl.pallas_call(
        functools.partial(_hgrn_decode_kernel, seq_len),
        grid=(nseq // nb,),
        in_specs=in_specs,
        out_specs=[pl.BlockSpec((nb * seq_len, W), lambda i: (i, 0)), sspec],
        out_shape=[
            jax.ShapeDtypeStruct((nseq * seq_len, W), BF16),
            jax.ShapeDtypeStruct(state.shape, F32),
        ],
        input_output_aliases=aliases,
        compiler_params=_params(("parallel",), 48),
        name="hgrn_decode",
    )(*args)


def _pool_prompt_kernel(u_ref, wp_ref, sc_ref, op_ref, nb_ref):
    u = u_ref[...]
    T = u.shape[0]
    row = lax.broadcasted_iota(jnp.int32, (T, LANE), 0)
    for g, w in enumerate(POOL_WINDOWS):
        cs = slice(g * LANE, (g + 1) * LANE)
        ug = u[:, cs]
        s = ug
        d = 1
        while d < w:
            s = s + jnp.where(row >= d, pltpu.roll(s, d, 0), 0.0)
            d *= 2
        cnt = jnp.minimum(row + 1, w).astype(F32)
        dv = s / cnt - ug
        y = _dot(dv.astype(BF16), wp_ref[g].astype(BF16)) * sc_ref[:, cs]
        op_ref[:, cs] = y.astype(BF16)
    tail = u_ref[T - 16:T, :]
    nb_ref[...] = pltpu.roll(tail, 15, 0)[0:POOL_BUF, :]


def _pool_prompt(zf, w_pool, scale3, layer, batch):
    T = zf.shape[0] // batch
    G = len(POOL_WINDOWS)
    W = G * LANE
    return pl.pallas_call(
        _pool_prompt_kernel,
        grid=(batch,),
        in_specs=[
            pl.BlockSpec((T, W), lambda b: (b, ZF_U)),
            pl.BlockSpec((None, G, LANE, LANE), lambda b: (layer, 0, 0, 0)),
            pl.BlockSpec((None, 1, W), lambda b: (layer, 0, 0)),
        ],
        out_specs=[
            pl.BlockSpec((T, W), lambda b: (b, 0)),
            pl.BlockSpec((None, POOL_BUF, W), lambda b: (b, 0, 0)),
        ],
        out_shape=[
            jax.ShapeDtypeStruct((batch * T, W), BF16),
            jax.ShapeDtypeStruct((batch, POOL_BUF, W), F32),
        ],
        compiler_params=_params(("parallel",), 48),
        name="pool_prompt",
    )(zf, w_pool, scale3)


def _pool_decode_kernel(u_ref, buf_ref, wp_ref, sc_ref, op_ref, nb_ref):
    steps = u_ref.shape[0]
    for g, w in enumerate(POOL_WINDOWS):
        cs = slice(g * LANE, (g + 1) * LANE)
        wg = wp_ref[g].astype(BF16)
        for t in range(steps):
            n_u = min(t + 1, w)
            acc = u_ref[t, :, cs]
            for j in range(t - n_u + 1, t):
                acc = acc + u_ref[j, :, cs]
            for i in range(POOL_BUF - (w - n_u), POOL_BUF):
                acc = acc + buf_ref[i, :, cs]
            dv = acc * (1.0 / w) - u_ref[t, :, cs]
            op_ref[t, :, cs] = _dot(dv.astype(BF16), wg) * sc_ref[:, cs]
    for i in range(POOL_BUF - steps):
        nb_ref[i] = buf_ref[i + steps]
    for t in range(steps):
        nb_ref[POOL_BUF - steps + t] = u_ref[t]


def _pool_decode(u_t, buf_t, w_pool, scale3, layer):
    steps, nseq, W = u_t.shape
    G = len(POOL_WINDOWS)
    return pl.pallas_call(
        _pool_decode_kernel,
        grid=(1,),
        in_specs=[
            pl.BlockSpec((steps, nseq, W), lambda i: (0, 0, 0)),
            pl.BlockSpec((POOL_BUF, nseq, W), lambda i: (0, 0, 0)),
            pl.BlockSpec((None, G, LANE, LANE), lambda i: (layer, 0, 0, 0)),
            pl.BlockSpec((None, 1, W), lambda i: (layer, 0, 0)),
        ],
        out_specs=[
            pl.BlockSpec((steps, nseq, W), lambda i: (0, 0, 0)),
            pl.BlockSpec((POOL_BUF, nseq, W), lambda i: (0, 0, 0)),
        ],
        out_shape=[
            jax.ShapeDtypeStruct((steps, nseq, W), F32),
            jax.ShapeDtypeStruct((POOL_BUF, nseq, W), F32),
        ],
        compiler_params=_params(("arbitrary",), 32),
        name="pool_decode",
    )(u_t, buf_t, w_pool, scale3)


def _softmax_rows(s):
    e = jnp.exp(s - jnp.max(s, axis=-1, keepdims=True))
    return e / jnp.sum(e, axis=-1, keepdims=True)


def _xattn_prompt_kernel(q_ref, k_ref, v_ref, o_ref):
    scale = LANE ** -0.5
    for h in range(XA_HEADS):
        cs = slice(h * LANE, (h + 1) * LANE)
        s = _dot_nt(q_ref[:, cs].astype(BF16), k_ref[:, cs].astype(BF16)) * scale
        p = _softmax_rows(s)
        o_ref[:, cs] = _dot(p.astype(BF16), v_ref[:, cs].astype(BF16)).astype(BF16)


def _xattn_prompt(zf, mk, mv, batch, tq):
    T = zf.shape[0] // batch
    W = XA_HEADS * LANE
    nq = T // tq
    n_mem = mk.shape[1]
    mspec = pl.BlockSpec((None, n_mem, W), lambda b, i: (b, 0, 0))
    return pl.pallas_call(
        _xattn_prompt_kernel,
        grid=(batch, nq),
        in_specs=[pl.BlockSpec((tq, W), lambda b, i: (b * nq + i, ZF_X)), mspec, mspec],
        out_specs=pl.BlockSpec((tq, W), lambda b, i: (b * nq + i, 0)),
        out_shape=jax.ShapeDtypeStruct((batch * T, W), BF16),
        compiler_params=_params(("parallel", "parallel"), 48),
        name="xattn_prompt",
    )(zf, mk, mv)


def _xattn_decode_kernel(seq_len, q_ref, k_ref, v_ref, o_ref):
    scale = LANE ** -0.5
    rows = DEC_GROUP * seq_len
    lseq = seq_len.bit_length() - 1
    lrows = rows.bit_length() - 1
    nk = k_ref.shape[1]
    row_s = lax.broadcasted_iota(jnp.int32, (XA_HEADS * rows, nk), 0)
    col_s = lax.broadcasted_iota(jnp.int32, (XA_HEADS * rows, nk), 1)
    own_head = (col_s & (XA_HEADS - 1)) == (row_s >> lrows)
    seq_s = (row_s & (rows - 1)) >> lseq
    seq_o = (lax.broadcasted_iota(jnp.int32, (XA_HEADS * rows, LANE), 0) & (rows - 1)) >> lseq
    for grp in range(k_ref.shape[0] // DEC_GROUP):
        rs = slice(grp * rows, (grp + 1) * rows)
        qb = jnp.concatenate([q_ref[rs, h * LANE:(h + 1) * LANE] for h in range(XA_HEADS)], axis=0).astype(BF16)
        s = jnp.full((XA_HEADS * rows, nk), -jnp.inf, F32)
        for k in range(DEC_GROUP):
            sk = _dot_nt(qb, k_ref[grp * DEC_GROUP + k].astype(BF16))
            s = jnp.where(own_head & (seq_s == k), sk, s)
        p = _softmax_rows(s * scale).astype(BF16)
        o = jnp.zeros((XA_HEADS * rows, LANE), F32)
        for k in range(DEC_GROUP):
            o = jnp.where(seq_o == k, _dot(p, v_ref[grp * DEC_GROUP + k].astype(BF16)), o)
        for h in range(XA_HEADS):
            o_ref[rs, h * LANE:(h + 1) * LANE] = o[h * rows:(h + 1) * rows, :].astype(BF16)


def _xattn_decode(zf, cache_k, cache_v, layer, nseq, nb):
    seq_len = zf.shape[0] // nseq
    W = XA_HEADS * LANE
    cspec = pl.BlockSpec((None, nb) + cache_k.shape[2:], lambda i: (layer, i, 0, 0))
    return pl.pallas_call(
        functools.partial(_xattn_decode_kernel, seq_len),
        grid=(nseq // nb,),
        in_specs=[pl.BlockSpec((nb * seq_len, W), lambda i: (i, ZF_X)), cspec, cspec],
        out_specs=pl.BlockSpec((nb * seq_len, W), lambda i: (i, 0)),
        out_shape=jax.ShapeDtypeStruct((nseq * seq_len, W), BF16),
        compiler_params=_params(("parallel",), 48),
        name="xattn_decode",
    )(zf, cache_k, cache_v)


def _merge_kernel(x_ref, oh_ref, op_ref, ox_ref, g0_ref, g1_ref, g2_ref, wb_ref, wo_ref, o_ref, y_ref):
    j = pl.program_id(1)
    nj, _, tn = y_ref.shape

    @pl.when(j < nj)
    def _():
        wh = oh_ref.shape[1]
        wp = op_ref.shape[1]
        y = g0_ref[...].astype(F32) * _dot(oh_ref[...].astype(BF16), wb_ref[0:wh, :])
        y += g1_ref[...].astype(F32) * _dot(op_ref[...].astype(BF16), wb_ref[wh:wh + wp, :])
        y += g2_ref[...].astype(F32) * _dot(ox_ref[...].astype(BF16), wb_ref[wh + wp:, :])
        y_ref[j] = y.astype(BF16)

    @pl.when(j >= nj)
    def _():
        acc = x_ref[...]
        for c in range(nj):
            acc += _dot(y_ref[c], wo_ref[c * tn:(c + 1) * tn, :])
        o_ref[...] = acc


def _merge(x, oh, op, ox, zb, wb, wo, layer, tm, tn):
    T, D = x.shape
    gcol = ZB_GATES * HG_HEADS * LANE // tn
    nj = D // tn
    first = lambda j: jnp.minimum(j, nj - 1)
    second = lambda j: jnp.maximum(j - nj, 0)
    full = lambda a: pl.BlockSpec((tm, a.shape[1]), lambda i, j: (i, 0))
    gate = lambda k: pl.BlockSpec((tm, tn), lambda i, j: (i, gcol + k * nj + first(j)))
    return pl.pallas_call(
        _merge_kernel,
        grid=(T // tm, 2 * nj),
        in_specs=[
            pl.BlockSpec((tm, tn), lambda i, j: (i, second(j))),
            full(oh), full(op), full(ox), gate(0), gate(1), gate(2),
            pl.BlockSpec((None, wb.shape[1], tn), lambda i, j: (layer, 0, first(j))),
            pl.BlockSpec((None, D, tn), lambda i, j: (layer, 0, second(j))),
        ],
        out_specs=pl.BlockSpec((tm, tn), lambda i, j: (i, second(j))),
        out_shape=jax.ShapeDtypeStruct((T, D), F32),
        scratch_shapes=[pltpu.VMEM((nj, tm, tn), BF16)],
        compiler_params=_params(("parallel", "arbitrary"), 48),
        name="merge_out",
    )(x, oh, op, ox, zb, zb, zb, wb, wo)


def _norm_kernel(x_ref, g_ref, o_ref):
    o_ref[...] = _rms(x_ref[...], g_ref[...])


def _final_norm(x, g2, tm):
    T, D = x.shape
    return pl.pallas_call(
        _norm_kernel,
        grid=(T // tm,),
        in_specs=[pl.BlockSpec((tm, D), lambda i: (i, 0)), pl.BlockSpec((1, D), lambda i: (0, 0))],
        out_specs=pl.BlockSpec((tm, D), lambda i: (i, 0)),
        out_shape=jax.ShapeDtypeStruct((T, D), F32),
        compiler_params=_params(("parallel",), 40),
        name="final_norm",
    )(x, g2)


def kernel(x_prompt, x_sample, state_hgrn, state_pool, cache_mem_k, cache_mem_v, mem_prompt, ffn1_norm, ffn1_w1, ffn1_w3, ffn1_w2, mix_norm, w_in, lb_logits, hg_norm, w_pool, pool_scale, mem_norm, w_mk, w_mv, w_branch, w_out, ffn2_norm, ffn2_w1, ffn2_w3, ffn2_w2, final_norm):
    B, T, D = x_prompt.shape
    nseq, steps, _ = x_sample.shape
    depth = w_in.shape[0]
    n_mem = mem_prompt.shape[1]
    xw = XA_HEADS * LANE
    pw = len(POOL_WINDOWS) * LANE

    tm_p = min(1024, B * T)
    tm_s = min(512, nseq * steps)

    vec3 = lambda a: a.reshape(a.shape[0], 1, a.shape[1])
    f1n, f2n = vec3(ffn1_norm), vec3(ffn2_norm)
    w_br_b, w_out_b = w_branch.astype(BF16), w_out.astype(BF16)
    mix3, mem3, hgn3, psc3 = vec3(mix_norm), vec3(mem_norm), vec3(hg_norm), vec3(pool_scale)
    lbs3 = vec3(_lower_bounds(lb_logits))

    xp = x_prompt.reshape(B * T, D)
    xs = x_sample.reshape(nseq * steps, D)
    mem2 = mem_prompt.reshape(B * n_mem, D)
    ck = cache_mem_k.reshape(depth, nseq, n_mem * XA_HEADS, LANE)
    cv = cache_mem_v.reshape(depth, nseq, n_mem * XA_HEADS, LANE)

    tf = _tile(ffn1_w1.shape[-1], 512)
    tf_s = _tile(ffn1_w1.shape[-1], 256)
    tn_in = _tile(math.gcd(HG_HEADS * LANE, N_BRANCH * D), 1024)
    tn_mrg = _tile(D, 512)
    tm_mem = min(512, B * n_mem)
    hs_p, pb_p, mk_p, mv_p, pb_s = [], [], [], [], []
    hs_s = None
    for l in range(depth):
        mk = _rms_matmul(mem2, mem3, w_mk, l, tm_mem, xw)
        mv = _rms_matmul(mem2, mem3, w_mv, l, tm_mem, xw)
        mk_p.append(mk.reshape(B, n_mem, XA_HEADS, LANE))
        mv_p.append(mv.reshape(B, n_mem, XA_HEADS, LANE))

        xs, w1b, w3b, w2b = _ffn(xs, f1n, ffn1_w1, ffn1_w3, ffn1_w2, l, tm_s, tf_s)
        xp = _ffn(xp, f1n, w1b, w3b, w2b, l, tm_p, tf)

        zsb, zsf, w_in_b = _in_proj(xs, mix3, w_in, l, tm_s, tn_in, emit=True)
        zpb, zpf = _in_proj(xp, mix3, w_in_b, l, tm_p, tn_in)

        ohp, sp = _hgrn_prompt(zpb, zpf, lbs3, hgn3, l, B)
        ohs, hs_s = _hgrn_decode(zsb, zsf, lbs3, hgn3, state_hgrn, hs_s, l, nseq, 8)
        hs_p.append(sp)

        opp, bp = _pool_prompt(zpf, w_pool, psc3, l, B)
        pb_p.append(bp)
        u_t = zsf[:, ZF_U * pw:(ZF_U + 1) * pw].reshape(nseq, steps, pw).transpose(1, 0, 2)
        buf_t = state_pool[l].transpose(1, 0, 2)
        ops_t, nb_t = _pool_decode(u_t, buf_t, w_pool, psc3, l)
        ops = ops_t.transpose(1, 0, 2).reshape(nseq * steps, pw)
        pb_s.append(nb_t.transpose(1, 0, 2))

        oxp = _xattn_prompt(zpf, mk.reshape(B, n_mem, xw), mv.reshape(B, n_mem, xw), B, min(1024, T))
        oxs = _xattn_decode(zsf, ck, cv, l, nseq, 8)

        xp = _merge(xp, ohp, opp, oxp, zpb, w_br_b, w_out_b, l, tm_p, tn_mrg)
        xs = _merge(xs, ohs, ops, oxs, zsb, w_br_b, w_out_b, l, tm_s, tn_mrg)

        xs, w1b, w3b, w2b = _ffn(xs, f2n, ffn2_w1, ffn2_w3, ffn2_w2, l, tm_s, tf_s)
        xp = _ffn(xp, f2n, w1b, w3b, w2b, l, tm_p, tf)

    fn2 = final_norm.reshape(1, D)
    y_prompt = _final_norm(xp, fn2, tm_p).reshape(B, T, D)
    y_sample = _final_norm(xs, fn2, tm_s).reshape(nseq, steps, D)
    return (y_prompt, y_sample, jnp.stack(hs_p), jnp.stack(pb_p), jnp.stack(mk_p), jnp.stack(mv_p),
            hs_s, jnp.stack(pb_s))
```

```python
import functools
import math

import jax
import jax.numpy as jnp
from jax import lax
from jax.experimental import pallas as pl
from jax.experimental.pallas import tpu as pltpu

F32 = jnp.float32
BF16 = jnp.bfloat16
EPS = 1e-6

LANE = 128
SUBLANE = 8
MXU_COLS = 256
HG_HEADS = 8
POOL_WINDOWS = (2, 4, 8, 16)
POOL_BUF = max(POOL_WINDOWS) - 1
XA_HEADS = 4
N_BRANCH = 3

COL_Q, COL_F, COL_I, COL_OG = 0, HG_HEADS, 2 * HG_HEADS, 3 * HG_HEADS
COL_U = 4 * HG_HEADS
COL_X = COL_U + len(POOL_WINDOWS)
COL_G = COL_X + XA_HEADS
ZB_Q, ZB_V, ZB_OG, ZB_GATES = 0, 1, 2, 3
ZF_F = 0
ZF_U, ZF_X = 2, 3

HGRN_CHUNK = 128
DEC_GROUP = 4


def _tile(n, preferred):
    t = preferred
    while n % t:
        t -= LANE
    return t


def _params(semantics, vmem_mib):
    return pltpu.CompilerParams(dimension_semantics=semantics, vmem_limit_bytes=vmem_mib * 1024 * 1024)


def _rms(x, g):
    return x * lax.rsqrt(jnp.mean(x * x, axis=-1, keepdims=True) + EPS) * g


def _sigmoid(x):
    return 0.5 * jnp.tanh(0.5 * x) + 0.5


def _silu(x):
    return x * _sigmoid(x)


def _dot(a, b):
    return jnp.dot(a, b, preferred_element_type=F32)


def _dot_nt(a, b):
    return lax.dot_general(a, b, (((1,), (1,)), ((), ())), preferred_element_type=F32)


def _dot_tn(a, b):
    return lax.dot_general(a, b, (((0,), (0,)), ((), ())), preferred_element_type=F32)


def _ffn_kernel(final, cast, x_ref, g_ref, w1_ref, w3_ref, w2_ref, *rest):
    fn_ref = rest[0] if final else None
    o_ref, xn_ref = rest[int(final)], rest[-1]
    if cast:
        src = (w1_ref, w3_ref, w2_ref)
        w1_ref, w3_ref, w2_ref = rest[int(final) + 1:int(final) + 4]
        for dst, s in zip((w1_ref, w3_ref, w2_ref), src):
            dst[...] = s[...].astype(BF16)
    j = pl.program_id(1)

    @pl.when(j == 0)
    def _():
        x = x_ref[...]
        xn_ref[...] = _rms(x, g_ref[...]).astype(BF16)
        o_ref[...] = x

    xn = xn_ref[...]
    h = (_silu(_dot(xn, w1_ref[...])) * _dot(xn, w3_ref[...])).astype(BF16)
    cw = min(w2_ref.shape[0], o_ref.shape[1])
    for n in range(o_ref.shape[1] // cw):
        cs = slice(n * cw, (n + 1) * cw)
        o_ref[:, cs] += 0.5 * _dot(h, w2_ref[:, cs])

    if final:
        @pl.when(j == pl.num_programs(1) - 1)
        def _():
            o_ref[...] = _rms(o_ref[...], fn_ref[...])


def _layer_spec(w, layer, block, index):
    if w.ndim == 2:
        return pl.BlockSpec(block, index)
    return pl.BlockSpec((None,) + block, lambda i, j: (layer,) + index(i, j))


def _ffn(x, g3, w1, w3, w2, layer, tm, tf, final_g=None):
    T, D = x.shape
    F = w1.shape[-1]
    cast = w1.dtype != BF16
    final = final_g is not None
    assert not cast or T == tm
    col = lambda i, j: (0, j)
    row = lambda i, j: (j, 0)
    out_specs = [pl.BlockSpec((tm, D), lambda i, j: (i, 0))]
    out_shape = [jax.ShapeDtypeStruct((T, D), F32)]
    if cast:
        out_specs += [pl.BlockSpec((D, tf), col), pl.BlockSpec((D, tf), col), pl.BlockSpec((tf, D), row)]
        out_shape += [jax.ShapeDtypeStruct((D, F), BF16), jax.ShapeDtypeStruct((D, F), BF16),
                      jax.ShapeDtypeStruct((F, D), BF16)]
    in_specs = [
        pl.BlockSpec((tm, D), lambda i, j: (i, 0)),
        pl.BlockSpec((None, 1, D), lambda i, j: (layer, 0, 0)),
        _layer_spec(w1, layer, (D, tf), col),
        _layer_spec(w3, layer, (D, tf), col),
        _layer_spec(w2, layer, (tf, D), row),
    ]
    args = [x, g3, w1, w3, w2]
    if final:
        in_specs.append(pl.BlockSpec((1, D), lambda i, j: (0, 0)))
        args.append(final_g)
    out = pl.pallas_call(
        functools.partial(_ffn_kernel, final, cast),
        grid=(T // tm, F // tf),
        in_specs=in_specs,
        out_specs=out_specs,
        out_shape=out_shape,
        scratch_shapes=[pltpu.VMEM((tm, D), BF16)],
        compiler_params=_params(("parallel", "arbitrary"), 58),
        name="ffn_cast" if cast else "ffn",
    )(*args)
    return out if cast else out[0]


def _rms_matmul_kernel(x_ref, g_ref, w_ref, o_ref, *rest):
    xn_ref = rest[-1]

    @pl.when(pl.program_id(1) == 0)
    def _():
        xn_ref[...] = _rms(x_ref[...], g_ref[...]).astype(BF16)

    w = w_ref[...].astype(BF16)
    if len(rest) == 2:
        rest[0][...] = w
    o_ref[...] = _dot(xn_ref[...], w)


def _rms_matmul(x, g3, w, layer, tm, tn, emit=False):
    T, D = x.shape
    N = w.shape[-1]
    assert not emit or T == tm
    col = lambda i, j: (0, j)
    out_specs = [pl.BlockSpec((tm, tn), lambda i, j: (i, j))]
    out_shape = [jax.ShapeDtypeStruct((T, N), F32)]
    if emit:
        out_specs.append(pl.BlockSpec((D, tn), col))
        out_shape.append(jax.ShapeDtypeStruct((D, N), BF16))
    out = pl.pallas_call(
        _rms_matmul_kernel,
        grid=(T // tm, N // tn),
        in_specs=[
            pl.BlockSpec((tm, D), lambda i, j: (i, 0)),
            pl.BlockSpec((None, 1, D), lambda i, j: (layer, 0, 0)),
            _layer_spec(w, layer, (D, tn), col),
        ],
        out_specs=out_specs,
        out_shape=out_shape,
        scratch_shapes=[pltpu.VMEM((tm, D), BF16)],
        compiler_params=_params(("parallel", "arbitrary"), 48),
        name="rms_matmul",
    )(x, g3, w)
    return out if emit else out[0]


def _runs(seq):
    runs = []
    for j, v in enumerate(seq):
        if runs and runs[-1][2] == v - j:
            runs[-1][1] = j + 1
        else:
            runs.append([j, j + 1, v - j])
    return [tuple(r) for r in runs]


def _in_runs(j, runs):
    hit = None
    for lo, hi, _ in runs:
        c = (j >= lo) & (j < hi)
        hit = c if hit is None else hit | c
    return hit


def _lookup(j, runs):
    out = 0
    for lo, hi, off in runs:
        out = out + jnp.where((j >= lo) & (j < hi), j + off, 0)
    return out


def _in_proj_plan(d_model, tn):
    t = lambda blocks: blocks * LANE // tn
    nq, nux, ng = t(HG_HEADS), t(len(POOL_WINDOWS) + XA_HEADS), N_BRANCH * d_model // tn
    src = lambda start, n: list(range(t(start), t(start) + n))
    order = src(COL_F, nq) + src(COL_U, nux) + src(COL_Q, nq) + src(COL_I, nq) + src(COL_OG, nq) + src(COL_G, ng)
    n32 = nq + nux
    kinds = {
        "plain": [(n32 + nq, n32 + 2 * nq, 0)],
        "sigmoid": [(n32 + 3 * nq, len(order), 0)],
    }
    return _runs(order), kinds, n32


def _in_proj_kernel(kinds, n32, x_ref, g_ref, w_ref, zb_ref, zf_ref, *rest):
    xn_ref = rest[-1]
    j = pl.program_id(1)

    @pl.when(j == 0)
    def _():
        xn_ref[...] = _rms(x_ref[...], g_ref[...]).astype(BF16)

    is_sigmoid = _in_runs(j, kinds["sigmoid"])
    is_plain = _in_runs(j, kinds["plain"])
    xn = xn_ref[...]
    tn = w_ref.shape[1]
    cw = min(MXU_COLS, tn)
    zs = []
    for c in range(tn // cw):
        cs = slice(c * cw, (c + 1) * cw)
        w = w_ref[:, cs].astype(BF16)
        if len(rest) == 2:
            rest[0][:, cs] = w
        z = _dot(xn, w)
        zs.append(z)
        s = _sigmoid(z)
        left = jnp.where(is_sigmoid, 1.0, z)
        right = jnp.where(is_plain, 1.0, s)
        zb_ref[:, cs] = (left * right).astype(BF16)

    @pl.when(j < n32)
    def _():
        for c, z in enumerate(zs):
            zf_ref[:, c * cw:(c + 1) * cw] = z


def _in_proj(x, g3, w, layer, tm, tn, emit=False):
    T, D = x.shape
    N = w.shape[-1]
    assert not emit or T == tm
    order, kinds, n32 = _in_proj_plan(D, tn)
    nsteps = N // tn
    wcol = lambda i, j: (0, _lookup(j, order))
    out_specs = [
        pl.BlockSpec((tm, tn), lambda i, j: (i, jnp.maximum(j - n32, 0))),
        pl.BlockSpec((tm, tn), lambda i, j: (i, jnp.minimum(j, n32 - 1))),
    ]
    out_shape = [
        jax.ShapeDtypeStruct((T, (nsteps - n32) * tn), BF16),
        jax.ShapeDtypeStruct((T, n32 * tn), F32),
    ]
    if emit:
        out_specs.append(pl.BlockSpec((D, tn), wcol))
        out_shape.append(jax.ShapeDtypeStruct((D, N), BF16))
    return pl.pallas_call(
        functools.partial(_in_proj_kernel, kinds, n32),
        grid=(T // tm, nsteps),
        in_specs=[
            pl.BlockSpec((tm, D), lambda i, j: (i, 0)),
            pl.BlockSpec((None, 1, D), lambda i, j: (layer, 0, 0)),
            _layer_spec(w, layer, (D, tn), wcol),
        ],
        out_specs=out_specs,
        out_shape=out_shape,
        scratch_shapes=[pltpu.VMEM((tm, D), BF16)],
        compiler_params=_params(("parallel", "arbitrary"), 48),
        name="in_proj",
    )(x, g3, w)


def _lower_bound_kernel(lg_ref, o_ref):
    lg = lg_ref[...]
    depth = lg.shape[0]
    rows = [lg[i:i + 1, :] for i in range(depth)]
    m = rows[0]
    for r in rows[1:]:
        m = jnp.maximum(m, r)
    e = [jnp.exp(r - m) for r in rows]
    tot = e[0]
    for v in e[1:]:
        tot = tot + v
    c = e[0] / tot
    first = c
    o_ref[0:1, :] = c - first
    for i in range(1, depth):
        c = c + e[i] / tot
        o_ref[i:i + 1, :] = c - first


def _lower_bounds(lb_logits):
    return pl.pallas_call(
        _lower_bound_kernel,
        out_shape=jax.ShapeDtypeStruct(lb_logits.shape, F32),
        name="hgrn_lower_bounds",
    )(lb_logits)


def _neg_abs(x):
    bits = lax.bitcast_convert_type(x, jnp.uint32) | jnp.uint32(0x80000000)
    return lax.bitcast_convert_type(bits, F32)


def _group_ref_row(b, s):
    rows, width = b.shape
    gsz = 2 * s
    if gsz >= 8:
        parts = [jnp.broadcast_to(b[i * gsz + s - 1:i * gsz + s, :], (gsz, width)) for i in range(rows // gsz)]
        return parts[0] if len(parts) == 1 else jnp.concatenate(parts, axis=0)
    pos = lax.broadcasted_iota(jnp.int32, b.shape, 0) & (gsz - 1)
    out = b
    for p in range(gsz):
        d = p - (s - 1)
        if d != 0:
            out = jnp.where(pos == p, pltpu.roll(b, d % rows, 0), out)
    return out


def _hgrn_block(q, zf, vb, lb, states, seg):
    C = q.shape[0]
    fg = lb + (1.0 - lb) * jax.nn.sigmoid(zf)
    kk = 1.0 - fg
    g = jnp.log2(fg)

    row = lax.broadcasted_iota(jnp.int32, (C, LANE), 0)
    rr = lax.broadcasted_iota(jnp.int32, (C, C), 0)
    cc = lax.broadcasted_iota(jnp.int32, (C, C), 1)
    lseg = seg.bit_length() - 1
    tri = jnp.where((cc <= rr) & ((rr >> lseg) == (cc >> lseg)), 1.0, 0.0).astype(F32)
    b = jnp.dot(tri, g, precision=lax.Precision.HIGHEST, preferred_element_type=F32)

    a = jnp.where(rr == cc, _dot_nt(q.astype(BF16), kk.astype(BF16)), 0.0)
    s = seg // 2
    while s >= 1:
        ls = s.bit_length() - 1
        e = jnp.exp2(_neg_abs(b - _group_ref_row(b, s)))
        same = (rr >> (ls + 1)) == (cc >> (ls + 1))
        if s % SUBLANE == 0:
            nblk = C // s
            zero = jnp.zeros((s, LANE), F32)
            blk = lambda x, i: x[i * s:(i + 1) * s, :]
            qs = jnp.concatenate([blk(q, i) * blk(e, i) if i % 2 else zero for i in range(nblk)], axis=0)
            ks = jnp.concatenate([zero if i % 2 else blk(kk, i) * blk(e, i) for i in range(nblk)], axis=0)
            p = _dot_nt(qs.astype(BF16), ks.astype(BF16))
            pieces = []
            for i in range(nblk):
                if i % 2 == 0:
                    pieces.append(blk(a, i))
                elif 2 * s == C:
                    pieces.append(blk(a, i) + blk(p, i))
                else:
                    pieces.append(blk(a, i) + jnp.where(blk(same, i), blk(p, i), 0.0))
            a = jnp.concatenate(pieces, axis=0)
        else:
            right = ((row >> ls) & 1) == 1
            eq = jnp.where(right, e, 0.0)
            p = _dot_nt((q * eq).astype(BF16), (kk * (e - eq)).astype(BF16))
            a = a + jnp.where(same, p, 0.0)
        s //= 2
    o = _dot(a.astype(BF16), vb)

    qe = q * jnp.exp2(b)
    new_states = []
    for k, st in enumerate(states):
        bl = b[k * seg + seg - 1:k * seg + seg, :]
        if len(states) == 1:
            qk = qe
            ke = kk * jnp.exp2(bl - b)
        else:
            mine = (row >> lseg) == k
            qk = jnp.where(mine, qe, 0.0)
            ke = jnp.where(mine, kk * jnp.exp2(jnp.where(mine, bl - b, 0.0)), 0.0)
        o = o + _dot(qk.astype(BF16), st.astype(BF16))
        decay = jnp.transpose(jnp.broadcast_to(jnp.exp2(bl), (LANE, LANE)))
        new_states.append(st * decay + _dot_tn(ke.astype(BF16), vb))
    return o, new_states


def _hgrn_wide(q, zf, vb, lb, states, seg):
    C, W = q.shape
    heads = [slice(h * LANE, (h + 1) * LANE) for h in range(W // LANE)]
    nseg = C // seg
    fg = lb + (1.0 - lb) * jax.nn.sigmoid(zf)
    kk = 1.0 - fg
    g = jnp.log2(fg)

    row = lax.broadcasted_iota(jnp.int32, (C, W), 0)
    rr = lax.broadcasted_iota(jnp.int32, (C, C), 0)
    cc = lax.broadcasted_iota(jnp.int32, (C, C), 1)
    lseg = seg.bit_length() - 1
    tri = jnp.where((cc <= rr) & ((rr >> lseg) == (cc >> lseg)), 1.0, 0.0).astype(F32)
    b = jnp.dot(tri, g, precision=lax.Precision.HIGHEST, preferred_element_type=F32)

    qb, kb = q.astype(BF16), kk.astype(BF16)
    a = [jnp.where(rr == cc, _dot_nt(qb[:, hs], kb[:, hs]), 0.0) for hs in heads]
    s = seg // 2
    while s >= 1:
        ls = s.bit_length() - 1
        same = (rr >> (ls + 1)) == (cc >> (ls + 1))
        if s == 1:
            right = (row & 1) == 1
            qs, ks = jnp.where(right, q * fg, 0.0).astype(BF16), jnp.where(right, 0.0, kk).astype(BF16)
            for h, hs in enumerate(heads):
                a[h] = a[h] + jnp.where(same, _dot_nt(qs[:, hs], ks[:, hs]), 0.0)
            break
        e = jnp.exp2(_neg_abs(b - _group_ref_row(b, s)))
        if s % SUBLANE == 0:
            nblk = C // s
            zero = jnp.zeros((s, W), F32)
            blk = lambda x, i: x[i * s:(i + 1) * s, :]
            qs = jnp.concatenate([blk(q, i) * blk(e, i) if i % 2 else zero for i in range(nblk)], axis=0).astype(BF16)
            ks = jnp.concatenate([zero if i % 2 else blk(kk, i) * blk(e, i) for i in range(nblk)], axis=0).astype(BF16)
            for h, hs in enumerate(heads):
                p = _dot_nt(qs[:, hs], ks[:, hs])
                pieces = []
                for i in range(nblk):
                    if i % 2 == 0:
                        pieces.append(blk(a[h], i))
                    elif 2 * s == C:
                        pieces.append(blk(a[h], i) + blk(p, i))
                    else:
                        pieces.append(blk(a[h], i) + jnp.where(blk(same, i), blk(p, i), 0.0))
                a[h] = jnp.concatenate(pieces, axis=0)
        else:
            right = ((row >> ls) & 1) == 1
            eq = jnp.where(right, e, 0.0)
            qs, ks = (q * eq).astype(BF16), (kk * (e - eq)).astype(BF16)
            for h, hs in enumerate(heads):
                a[h] = a[h] + jnp.where(same, _dot_nt(qs[:, hs], ks[:, hs]), 0.0)
        s //= 2
    o = [_dot(a[h].astype(BF16), vb[:, hs]) for h, hs in enumerate(heads)]

    qe = q * jnp.exp2(b)
    new_states = [[] for _ in heads]
    for k in range(nseg):
        bl = b[k * seg + seg - 1:k * seg + seg, :]
        if nseg == 1:
            qk = qe
            ke = kk * jnp.exp2(bl - b)
        else:
            mine = (row >> lseg) == k
            qk = jnp.where(mine, qe, 0.0)
            ke = jnp.where(mine, kk * jnp.exp2(jnp.where(mine, bl - b, 0.0)), 0.0)
        qkb, keb, ebl = qk.astype(BF16), ke.astype(BF16), jnp.exp2(bl)
        for h, hs in enumerate(heads):
            st = states[h][k]
            o[h] = o[h] + _dot(qkb[:, hs], st.astype(BF16))
            decay = jnp.transpose(jnp.broadcast_to(ebl[:, hs], (LANE, LANE)))
            new_states[h].append(st * decay + _dot_tn(keb[:, hs], vb[:, hs]))
    return o, new_states


def _hgrn_finish(o, og, hgn):
    return (_rms(o, hgn) * og.astype(F32)).astype(BF16)


def _hgrn_prompt_kernel(q_ref, zf_ref, v_ref, og_ref, lb_ref, hgn_ref, oh_ref, s_ref):
    @pl.when(pl.program_id(1) == 0)
    def _():
        s_ref[...] = jnp.zeros_like(s_ref)

    states = [[s_ref[h]] for h in range(HG_HEADS)]
    o, new = _hgrn_wide(q_ref[...].astype(F32), zf_ref[...], v_ref[...], lb_ref[...], states, HGRN_CHUNK)
    for h in range(HG_HEADS):
        cs = slice(h * LANE, (h + 1) * LANE)
        s_ref[h] = new[h][0]
        oh_ref[:, cs] = _hgrn_finish(o[h], og_ref[:, cs], hgn_ref[:, cs])


def _hgrn_prompt(zb, zf, lbs3, hgn3, layer, batch):
    T = zb.shape[0] // batch
    C = HGRN_CHUNK
    nc = T // C
    W = HG_HEADS * LANE
    zspec = lambda col: pl.BlockSpec((C, W), lambda b, c: (b * nc + c, col))
    vec = pl.BlockSpec((None, 1, W), lambda b, c: (layer, 0, 0))
    return pl.pallas_call(
        _hgrn_prompt_kernel,
        grid=(batch, nc),
        in_specs=[zspec(ZB_Q), zspec(ZF_F), zspec(ZB_V), zspec(ZB_OG), vec, vec],
        out_specs=[
            pl.BlockSpec((C, W), lambda b, c: (b * nc + c, 0)),
            pl.BlockSpec((None, HG_HEADS, LANE, LANE), lambda b, c: (b, 0, 0, 0)),
        ],
        out_shape=[
            jax.ShapeDtypeStruct((batch * T, W), BF16),
            jax.ShapeDtypeStruct((batch, HG_HEADS, LANE, LANE), F32),
        ],
        compiler_params=_params(("parallel", "arbitrary"), 32),
        name="hgrn_prompt",
    )(zb, zf, zb, zb, lbs3, hgn3)


def _hgrn_decode_kernel(seq_len, q_ref, zf_ref, v_ref, og_ref, lb_ref, hgn_ref, s_ref, *rest):
    oh_ref, so_ref = rest[-2], rest[-1]
    rows = DEC_GROUP * seq_len
    for grp in range(s_ref.shape[0] // DEC_GROUP):
        rs = slice(grp * rows, (grp + 1) * rows)
        states = [[s_ref[grp * DEC_GROUP + k, h] for k in range(DEC_GROUP)] for h in range(HG_HEADS)]
        o, new = _hgrn_wide(q_ref[rs, :].astype(F32), zf_ref[rs, :], v_ref[rs, :], lb_ref[...], states, seq_len)
        for h in range(HG_HEADS):
            cs = slice(h * LANE, (h + 1) * LANE)
            for k in range(DEC_GROUP):
                so_ref[grp * DEC_GROUP + k, h] = new[h][k]
            oh_ref[rs, cs] = _hgrn_finish(o[h], og_ref[rs, cs], hgn_ref[:, cs])


def _hgrn_decode(zb, zf, lbs3, hgn3, state, stacked, layer, nseq, nb):
    seq_len = zb.shape[0] // nseq
    W = HG_HEADS * LANE
    zspec = lambda col: pl.BlockSpec((nb * seq_len, W), lambda i: (i, col))
    vec = pl.BlockSpec((None, 1, W), lambda i: (layer, 0, 0))
    sspec = pl.BlockSpec((None, nb, HG_HEADS, LANE, LANE), lambda i: (layer, i, 0, 0, 0))
    in_specs = [zspec(ZB_Q), zspec(ZF_F), zspec(ZB_V), zspec(ZB_OG), vec, vec, sspec]
    args = [zb, zf, zb, zb, lbs3, hgn3, state]
    aliases = {}
    if stacked is not None:
        in_specs.append(pl.BlockSpec(memory_space=pl.ANY))
        args.append(stacked)
        aliases = {len(args) - 1: 1}
    return pl.pallas_call(
        functools.partial(_hgrn_decode_kernel, seq_len),
        grid=(nseq // nb,),
        in_specs=in_specs,
        out_specs=[pl.BlockSpec((nb * seq_len, W), lambda i: (i, 0)), sspec],
        out_shape=[
            jax.ShapeDtypeStruct((nseq * seq_len, W), BF16),
            jax.ShapeDtypeStruct(state.shape, F32),
        ],
        input_output_aliases=aliases,
        compiler_params=_params(("parallel",), 48),
        name="hgrn_decode",
    )(*args)


def _pool_prompt_kernel(u_ref, wp_ref, sc_ref, op_ref, nb_ref):
    u = u_ref[...]
    T = u.shape[0]
    row = lax.broadcasted_iota(jnp.int32, (T, LANE), 0)
    for g, w in enumerate(POOL_WINDOWS):
        cs = slice(g * LANE, (g + 1) * LANE)
        ug = u[:, cs]
        s = ug
        d = 1
        while d < w:
            s = s + jnp.where(row >= d, pltpu.roll(s, d, 0), 0.0)
            d *= 2
        cnt = jnp.minimum(row + 1, w).astype(F32)
        dv = s / cnt - ug
        y = _dot(dv.astype(BF16), wp_ref[g].astype(BF16)) * sc_ref[:, cs]
        op_ref[:, cs] = y.astype(BF16)
    tail = u_ref[T - 16:T, :]
    nb_ref[...] = pltpu.roll(tail, 15, 0)[0:POOL_BUF, :]


def _pool_prompt(zf, w_pool, scale3, layer, batch):
    T = zf.shape[0] // batch
    G = len(POOL_WINDOWS)
    W = G * LANE
    return pl.pallas_call(
        _pool_prompt_kernel,
        grid=(batch,),
        in_specs=[
            pl.BlockSpec((T, W), lambda b: (b, ZF_U)),
            pl.BlockSpec((None, G, LANE, LANE), lambda b: (layer, 0, 0, 0)),
            pl.BlockSpec((None, 1, W), lambda b: (layer, 0, 0)),
        ],
        out_specs=[
            pl.BlockSpec((T, W), lambda b: (b, 0)),
            pl.BlockSpec((None, POOL_BUF, W), lambda b: (b, 0, 0)),
        ],
        out_shape=[
            jax.ShapeDtypeStruct((batch * T, W), BF16),
            jax.ShapeDtypeStruct((batch, POOL_BUF, W), F32),
        ],
        compiler_params=_params(("parallel",), 48),
        name="pool_prompt",
    )(zf, w_pool, scale3)


def _pool_decode_kernel(u_ref, buf_ref, wp_ref, sc_ref, op_ref, nb_ref):
    steps = u_ref.shape[0]
    for g, w in enumerate(POOL_WINDOWS):
        cs = slice(g * LANE, (g + 1) * LANE)
        wg = wp_ref[g].astype(BF16)
        for t in range(steps):
            n_u = min(t + 1, w)
            acc = u_ref[t, :, cs]
            for j in range(t - n_u + 1, t):
                acc = acc + u_ref[j, :, cs]
            for i in range(POOL_BUF - (w - n_u), POOL_BUF):
                acc = acc + buf_ref[i, :, cs]
            dv = acc * (1.0 / w) - u_ref[t, :, cs]
            op_ref[t, :, cs] = _dot(dv.astype(BF16), wg) * sc_ref[:, cs]
    for i in range(POOL_BUF - steps):
        nb_ref[i] = buf_ref[i + steps]
    for t in range(steps):
        nb_ref[POOL_BUF - steps + t] = u_ref[t]


def _pool_decode(u_t, buf_t, w_pool, scale3, layer):
    steps, nseq, W = u_t.shape
    G = len(POOL_WINDOWS)
    return pl.pallas_call(
        _pool_decode_kernel,
        grid=(1,),
        in_specs=[
            pl.BlockSpec((steps, nseq, W), lambda i: (0, 0, 0)),
            pl.BlockSpec((POOL_BUF, nseq, W), lambda i: (0, 0, 0)),
            pl.BlockSpec((None, G, LANE, LANE), lambda i: (layer, 0, 0, 0)),
            pl.BlockSpec((None, 1, W), lambda i: (layer, 0, 0)),
        ],
        out_specs=[
            pl.BlockSpec((steps, nseq, W), lambda i: (0, 0, 0)),
            pl.BlockSpec((POOL_BUF, nseq, W), lambda i: (0, 0, 0)),
        ],
        out_shape=[
            jax.ShapeDtypeStruct((steps, nseq, W), F32),
            jax.ShapeDtypeStruct((POOL_BUF, nseq, W), F32),
        ],
        compiler_params=_params(("arbitrary",), 32),
        name="pool_decode",
    )(u_t, buf_t, w_pool, scale3)


def _softmax_rows(s):
    e = jnp.exp(s - jnp.max(s, axis=-1, keepdims=True))
    return e / jnp.sum(e, axis=-1, keepdims=True)


def _xattn_prompt_kernel(q_ref, k_ref, v_ref, o_ref):
    scale = LANE ** -0.5
    for h in range(XA_HEADS):
        cs = slice(h * LANE, (h + 1) * LANE)
        s = _dot_nt(q_ref[:, cs].astype(BF16), k_ref[:, cs].astype(BF16)) * scale
        p = _softmax_rows(s)
        o_ref[:, cs] = _dot(p.astype(BF16), v_ref[:, cs].astype(BF16)).astype(BF16)


def _xattn_prompt(zf, mk, mv, batch, tq):
    T = zf.shape[0] // batch
    W = XA_HEADS * LANE
    nq = T // tq
    n_mem = mk.shape[1]
    mspec = pl.BlockSpec((None, n_mem, W), lambda b, i: (b, 0, 0))
    return pl.pallas_call(
        _xattn_prompt_kernel,
        grid=(batch, nq),
        in_specs=[pl.BlockSpec((tq, W), lambda b, i: (b * nq + i, ZF_X)), mspec, mspec],
        out_specs=pl.BlockSpec((tq, W), lambda b, i: (b * nq + i, 0)),
        out_shape=jax.ShapeDtypeStruct((batch * T, W), BF16),
        compiler_params=_params(("parallel", "parallel"), 48),
        name="xattn_prompt",
    )(zf, mk, mv)


def _xattn_decode_kernel(seq_len, q_ref, k_ref, v_ref, o_ref):
    scale = LANE ** -0.5
    rows = DEC_GROUP * seq_len
    lseq = seq_len.bit_length() - 1
    lrows = rows.bit_length() - 1
    nk = k_ref.shape[1]
    row_s = lax.broadcasted_iota(jnp.int32, (XA_HEADS * rows, nk), 0)
    col_s = lax.broadcasted_iota(jnp.int32, (XA_HEADS * rows, nk), 1)
    own_head = (col_s & (XA_HEADS - 1)) == (row_s >> lrows)
    seq_s = (row_s & (rows - 1)) >> lseq
    seq_o = (lax.broadcasted_iota(jnp.int32, (XA_HEADS * rows, LANE), 0) & (rows - 1)) >> lseq
    for grp in range(k_ref.shape[0] // DEC_GROUP):
        rs = slice(grp * rows, (grp + 1) * rows)
        qb = jnp.concatenate([q_ref[rs, h * LANE:(h + 1) * LANE] for h in range(XA_HEADS)], axis=0).astype(BF16)
        s = jnp.full((XA_HEADS * rows, nk), -jnp.inf, F32)
        for k in range(DEC_GROUP):
            sk = _dot_nt(qb, k_ref[grp * DEC_GROUP + k].astype(BF16))
            s = jnp.where(own_head & (seq_s == k), sk, s)
        p = _softmax_rows(s * scale).astype(BF16)
        o = jnp.zeros((XA_HEADS * rows, LANE), F32)
        for k in range(DEC_GROUP):
            o = jnp.where(seq_o == k, _dot(p, v_ref[grp * DEC_GROUP + k].astype(BF16)), o)
        for h in range(XA_HEADS):
            o_ref[rs, h * LANE:(h + 1) * LANE] = o[h * rows:(h + 1) * rows, :].astype(BF16)


def _xattn_decode(zf, cache_k, cache_v, layer, nseq, nb):
    seq_len = zf.shape[0] // nseq
    W = XA_HEADS * LANE
    cspec = pl.BlockSpec((None, nb) + cache_k.shape[2:], lambda i: (layer, i, 0, 0))
    return pl.pallas_call(
        functools.partial(_xattn_decode_kernel, seq_len),
        grid=(nseq // nb,),
        in_specs=[pl.BlockSpec((nb * seq_len, W), lambda i: (i, ZF_X)), cspec, cspec],
        out_specs=pl.BlockSpec((nb * seq_len, W), lambda i: (i, 0)),
        out_shape=jax.ShapeDtypeStruct((nseq * seq_len, W), BF16),
        compiler_params=_params(("parallel",), 48),
        name="xattn_decode",
    )(zf, cache_k, cache_v)


def _merge_kernel(x_ref, oh_ref, op_ref, ox_ref, g0_ref, g1_ref, g2_ref, wb_ref, wo_ref, o_ref, *rest):
    y_ref = rest[-1]
    wbb_ref, wob_ref = rest[:2] if len(rest) == 3 else (None, None)
    j = pl.program_id(1)
    nj, _, tn = y_ref.shape

    @pl.when(j < nj)
    def _():
        wh = oh_ref.shape[1]
        wp = op_ref.shape[1]
        oh, op, ox = oh_ref[...].astype(BF16), op_ref[...].astype(BF16), ox_ref[...].astype(BF16)
        cw = min(MXU_COLS, tn)
        for c in range(tn // cw):
            cs = slice(c * cw, (c + 1) * cw)
            w = wb_ref[:, cs].astype(BF16)
            if wbb_ref is not None:
                wbb_ref[:, cs] = w
            y = g0_ref[:, cs].astype(F32) * _dot(oh, w[0:wh, :])
            y += g1_ref[:, cs].astype(F32) * _dot(op, w[wh:wh + wp, :])
            y += g2_ref[:, cs].astype(F32) * _dot(ox, w[wh + wp:, :])
            y_ref[j, :, cs] = y.astype(BF16)

    @pl.when(j >= nj)
    def _():
        acc = x_ref[...]
        for c in range(nj):
            rs = slice(c * tn, (c + 1) * tn)
            w = wo_ref[rs, :].astype(BF16)
            if wob_ref is not None:
                wob_ref[rs, :] = w
            acc += _dot(y_ref[c], w)
        o_ref[...] = acc


def _merge(x, oh, op, ox, zb, wb, wo, layer, tm, tn, emit=False):
    T, D = x.shape
    assert not emit or T == tm
    gcol = ZB_GATES * HG_HEADS * LANE // tn
    nj = D // tn
    first = lambda j: jnp.minimum(j, nj - 1)
    second = lambda j: jnp.maximum(j - nj, 0)
    full = lambda a: pl.BlockSpec((tm, a.shape[1]), lambda i, j: (i, 0))
    gate = lambda k: pl.BlockSpec((tm, tn), lambda i, j: (i, gcol + k * nj + first(j)))
    wb_idx = lambda i, j: (0, first(j))
    wo_idx = lambda i, j: (0, second(j))
    rows_b = wb.shape[-2]
    out_specs = [pl.BlockSpec((tm, tn), lambda i, j: (i, second(j)))]
    out_shape = [jax.ShapeDtypeStruct((T, D), F32)]
    if emit:
        out_specs += [pl.BlockSpec((rows_b, tn), wb_idx), pl.BlockSpec((D, tn), wo_idx)]
        out_shape += [jax.ShapeDtypeStruct((rows_b, D), BF16), jax.ShapeDtypeStruct((D, D), BF16)]
    out = pl.pallas_call(
        _merge_kernel,
        grid=(T // tm, 2 * nj),
        in_specs=[
            pl.BlockSpec((tm, tn), lambda i, j: (i, second(j))),
            full(oh), full(op), full(ox), gate(0), gate(1), gate(2),
            _layer_spec(wb, layer, (rows_b, tn), wb_idx),
            _layer_spec(wo, layer, (D, tn), wo_idx),
        ],
        out_specs=out_specs,
        out_shape=out_shape,
        scratch_shapes=[pltpu.VMEM((nj, tm, tn), BF16)],
        compiler_params=_params(("parallel", "arbitrary"), 48),
        name="merge_out",
    )(x, oh, op, ox, zb, zb, zb, wb, wo)
    return out if emit else out[0]


def _norm_kernel(x_ref, g_ref, o_ref):
    o_ref[...] = _rms(x_ref[...], g_ref[...])


def _final_norm(x, g2, tm):
    T, D = x.shape
    return pl.pallas_call(
        _norm_kernel,
        grid=(T // tm,),
        in_specs=[pl.BlockSpec((tm, D), lambda i: (i, 0)), pl.BlockSpec((1, D), lambda i: (0, 0))],
        out_specs=pl.BlockSpec((tm, D), lambda i: (i, 0)),
        out_shape=jax.ShapeDtypeStruct((T, D), F32),
        compiler_params=_params(("parallel",), 40),
        name="final_norm",
    )(x, g2)


def kernel(x_prompt, x_sample, state_hgrn, state_pool, cache_mem_k, cache_mem_v, mem_prompt, ffn1_norm, ffn1_w1, ffn1_w3, ffn1_w2, mix_norm, w_in, lb_logits, hg_norm, w_pool, pool_scale, mem_norm, w_mk, w_mv, w_branch, w_out, ffn2_norm, ffn2_w1, ffn2_w3, ffn2_w2, final_norm):
    B, T, D = x_prompt.shape
    nseq, steps, _ = x_sample.shape
    depth = w_in.shape[0]
    n_mem = mem_prompt.shape[1]
    xw = XA_HEADS * LANE
    pw = len(POOL_WINDOWS) * LANE

    tm_p = min(1024, B * T)
    tm_s = min(512, nseq * steps)

    vec3 = lambda a: a.reshape(a.shape[0], 1, a.shape[1])
    f1n, f2n = vec3(ffn1_norm), vec3(ffn2_norm)
    mix3, mem3, hgn3, psc3 = vec3(mix_norm), vec3(mem_norm), vec3(hg_norm), vec3(pool_scale)
    lbs3 = vec3(_lower_bounds(lb_logits))

    xp = x_prompt.reshape(B * T, D)
    xs = x_sample.reshape(nseq * steps, D)
    mem2 = mem_prompt.reshape(B * n_mem, D)
    ck = cache_mem_k.reshape(depth, nseq, n_mem * XA_HEADS, LANE)
    cv = cache_mem_v.reshape(depth, nseq, n_mem * XA_HEADS, LANE)

    tf = _tile(ffn1_w1.shape[-1], 512)
    tf_s = _tile(ffn1_w1.shape[-1], 256)
    tn_in = _tile(math.gcd(HG_HEADS * LANE, N_BRANCH * D), 1024)
    tn_mrg = _tile(D, 512)
    tm_mem = min(512, B * n_mem)
    hs_p, pb_p, mk_p, mv_p, pb_s = [], [], [], [], []
    hs_s = None
    for l in range(depth):
        mk = _rms_matmul(mem2, mem3, w_mk, l, tm_mem, xw)
        mv = _rms_matmul(mem2, mem3, w_mv, l, tm_mem, xw)
        mk_p.append(mk.reshape(B, n_mem, XA_HEADS, LANE))
        mv_p.append(mv.reshape(B, n_mem, XA_HEADS, LANE))

        xs, w1b, w3b, w2b = _ffn(xs, f1n, ffn1_w1, ffn1_w3, ffn1_w2, l, tm_s, tf_s)
        xp = _ffn(xp, f1n, w1b, w3b, w2b, l, tm_p, tf)

        zsb, zsf, w_in_b = _in_proj(xs, mix3, w_in, l, tm_s, tn_in, emit=True)
        zpb, zpf = _in_proj(xp, mix3, w_in_b, l, tm_p, tn_in)

        ohp, sp = _hgrn_prompt(zpb, zpf, lbs3, hgn3, l, B)
        ohs, hs_s = _hgrn_decode(zsb, zsf, lbs3, hgn3, state_hgrn, hs_s, l, nseq, 8)
        hs_p.append(sp)

        opp, bp = _pool_prompt(zpf, w_pool, psc3, l, B)
        pb_p.append(bp)
        u_t = zsf[:, ZF_U * pw:(ZF_U + 1) * pw].reshape(nseq, steps, pw).transpose(1, 0, 2)
        buf_t = state_pool[l].transpose(1, 0, 2)
        ops_t, nb_t = _pool_decode(u_t, buf_t, w_pool, psc3, l)
        ops = ops_t.transpose(1, 0, 2).reshape(nseq * steps, pw)
        pb_s.append(nb_t.transpose(1, 0, 2))

        oxp = _xattn_prompt(zpf, mk.reshape(B, n_mem, xw), mv.reshape(B, n_mem, xw), B, min(1024, T))
        oxs = _xattn_decode(zsf, ck, cv, l, nseq, 8)

        xs, w_br_b, w_out_b = _merge(xs, ohs, ops, oxs, zsb, w_branch, w_out, l, tm_s, tn_mrg, emit=True)
        xp = _merge(xp, ohp, opp, oxp, zpb, w_br_b, w_out_b, l, tm_p, tn_mrg)

        fn2 = final_norm.reshape(1, D) if l == depth - 1 else None
        xs, w1b, w3b, w2b = _ffn(xs, f2n, ffn2_w1, ffn2_w3, ffn2_w2, l, tm_s, tf_s, fn2)
        xp = _ffn(xp, f2n, w1b, w3b, w2b, l, tm_p, tf, fn2)

    y_prompt = xp.reshape(B, T, D)
    y_sample = xs.reshape(nseq, steps, D)
    return (y_prompt, y_sample, jnp.stack(hs_p), jnp.stack(pb_p), jnp.stack(mk_p), jnp.stack(mv_p),
            hs_s, jnp.stack(pb_s))
```

```python
import functools
import math

import jax
import jax.numpy as jnp
from jax import lax
from jax.experimental import pallas as pl
from jax.experimental.pallas import tpu as pltpu

F32 = jnp.float32
BF16 = jnp.bfloat16
EPS = 1e-6

LANE = 128
SUBLANE = 8
MXU_COLS = 256
HG_HEADS = 8
POOL_WINDOWS = (2, 4, 8, 16)
POOL_BUF = max(POOL_WINDOWS) - 1
XA_HEADS = 4
N_BRANCH = 3

COL_Q, COL_F, COL_I, COL_OG = 0, HG_HEADS, 2 * HG_HEADS, 3 * HG_HEADS
COL_U = 4 * HG_HEADS
COL_X = COL_U + len(POOL_WINDOWS)
COL_G = COL_X + XA_HEADS
ZB_Q, ZB_V, ZB_OG, ZB_GATES = 0, 1, 2, 3
ZF_F = 0
ZF_U, ZF_X = 2, 3

HGRN_CHUNK = 128
DEC_GROUP = 4


def _tile(n, preferred):
    t = preferred
    while n % t:
        t -= LANE
    return t


def _params(semantics, vmem_mib):
    return pltpu.CompilerParams(dimension_semantics=semantics, vmem_limit_bytes=vmem_mib * 1024 * 1024)


def _rms(x, g):
    return x * lax.rsqrt(jnp.mean(x * x, axis=-1, keepdims=True) + EPS) * g


def _sigmoid(x):
    return 0.5 * jnp.tanh(0.5 * x) + 0.5


def _silu(x):
    return x * _sigmoid(x)


def _dot(a, b):
    return jnp.dot(a, b, preferred_element_type=F32)


def _dot_nt(a, b):
    return lax.dot_general(a, b, (((1,), (1,)), ((), ())), preferred_element_type=F32)


def _dot_tn(a, b):
    return lax.dot_general(a, b, (((0,), (0,)), ((), ())), preferred_element_type=F32)


def _ffn_kernel(final, cast, x_ref, g_ref, w1_ref, w3_ref, w2_ref, *rest):
    fn_ref = rest[0] if final else None
    o_ref, xn_ref = rest[int(final)], rest[-1]
    if cast:
        src = (w1_ref, w3_ref, w2_ref)
        w1_ref, w3_ref, w2_ref = rest[int(final) + 1:int(final) + 4]
        for dst, s in zip((w1_ref, w3_ref, w2_ref), src):
            dst[...] = s[...].astype(BF16)
    j = pl.program_id(1)

    @pl.when(j == 0)
    def _():
        x = x_ref[...]
        xn_ref[...] = _rms(x, g_ref[...]).astype(BF16)
        o_ref[...] = x

    xn = xn_ref[...]
    h = (_silu(_dot(xn, w1_ref[...])) * _dot(xn, w3_ref[...])).astype(BF16)
    cw = min(w2_ref.shape[0], o_ref.shape[1])
    for n in range(o_ref.shape[1] // cw):
        cs = slice(n * cw, (n + 1) * cw)
        o_ref[:, cs] += 0.5 * _dot(h, w2_ref[:, cs])

    if final:
        @pl.when(j == pl.num_programs(1) - 1)
        def _():
            o_ref[...] = _rms(o_ref[...], fn_ref[...])


def _layer_spec(w, layer, block, index):
    if w.ndim == 2:
        return pl.BlockSpec(block, index)
    return pl.BlockSpec((None,) + block, lambda i, j: (layer,) + index(i, j))


def _ffn(x, g3, w1, w3, w2, layer, tm, tf, final_g=None):
    T, D = x.shape
    F = w1.shape[-1]
    cast = w1.dtype != BF16
    final = final_g is not None
    assert not cast or T == tm
    col = lambda i, j: (0, j)
    row = lambda i, j: (j, 0)
    out_specs = [pl.BlockSpec((tm, D), lambda i, j: (i, 0))]
    out_shape = [jax.ShapeDtypeStruct((T, D), F32)]
    if cast:
        out_specs += [pl.BlockSpec((D, tf), col), pl.BlockSpec((D, tf), col), pl.BlockSpec((tf, D), row)]
        out_shape += [jax.ShapeDtypeStruct((D, F), BF16), jax.ShapeDtypeStruct((D, F), BF16),
                      jax.ShapeDtypeStruct((F, D), BF16)]
    in_specs = [
        pl.BlockSpec((tm, D), lambda i, j: (i, 0)),
        pl.BlockSpec((None, 1, D), lambda i, j: (layer, 0, 0)),
        _layer_spec(w1, layer, (D, tf), col),
        _layer_spec(w3, layer, (D, tf), col),
        _layer_spec(w2, layer, (tf, D), row),
    ]
    args = [x, g3, w1, w3, w2]
    if final:
        in_specs.append(pl.BlockSpec((1, D), lambda i, j: (0, 0)))
        args.append(final_g)
    out = pl.pallas_call(
        functools.partial(_ffn_kernel, final, cast),
        grid=(T // tm, F // tf),
        in_specs=in_specs,
        out_specs=out_specs,
        out_shape=out_shape,
        scratch_shapes=[pltpu.VMEM((tm, D), BF16)],
        compiler_params=_params(("parallel", "arbitrary"), 58),
        name="ffn_cast" if cast else "ffn",
    )(*args)
    return out if cast else out[0]


def _rms_matmul_kernel(x_ref, g_ref, w_ref, o_ref, *rest):
    xn_ref = rest[-1]

    @pl.when(pl.program_id(1) == 0)
    def _():
        xn_ref[...] = _rms(x_ref[...], g_ref[...]).astype(BF16)

    w = w_ref[...].astype(BF16)
    if len(rest) == 2:
        rest[0][...] = w
    o_ref[...] = _dot(xn_ref[...], w)


def _rms_matmul(x, g3, w, layer, tm, tn, emit=False):
    T, D = x.shape
    N = w.shape[-1]
    assert not emit or T == tm
    col = lambda i, j: (0, j)
    out_specs = [pl.BlockSpec((tm, tn), lambda i, j: (i, j))]
    out_shape = [jax.ShapeDtypeStruct((T, N), F32)]
    if emit:
        out_specs.append(pl.BlockSpec((D, tn), col))
        out_shape.append(jax.ShapeDtypeStruct((D, N), BF16))
    out = pl.pallas_call(
        _rms_matmul_kernel,
        grid=(T // tm, N // tn),
        in_specs=[
            pl.BlockSpec((tm, D), lambda i, j: (i, 0)),
            pl.BlockSpec((None, 1, D), lambda i, j: (layer, 0, 0)),
            _layer_spec(w, layer, (D, tn), col),
        ],
        out_specs=out_specs,
        out_shape=out_shape,
        scratch_shapes=[pltpu.VMEM((tm, D), BF16)],
        compiler_params=_params(("parallel", "arbitrary"), 48),
        name="rms_matmul",
    )(x, g3, w)
    return out if emit else out[0]


def _runs(seq):
    runs = []
    for j, v in enumerate(seq):
        if runs and runs[-1][2] == v - j:
            runs[-1][1] = j + 1
        else:
            runs.append([j, j + 1, v - j])
    return [tuple(r) for r in runs]


def _in_runs(j, runs):
    hit = None
    for lo, hi, _ in runs:
        c = (j >= lo) & (j < hi)
        hit = c if hit is None else hit | c
    return hit


def _lookup(j, runs):
    out = 0
    for lo, hi, off in runs:
        out = out + jnp.where((j >= lo) & (j < hi), j + off, 0)
    return out


def _in_proj_plan(d_model, tn):
    t = lambda blocks: blocks * LANE // tn
    nq, nux, ng = t(HG_HEADS), t(len(POOL_WINDOWS) + XA_HEADS), N_BRANCH * d_model // tn
    src = lambda start, n: list(range(t(start), t(start) + n))
    order = src(COL_F, nq) + src(COL_U, nux) + src(COL_Q, nq) + src(COL_I, nq) + src(COL_OG, nq) + src(COL_G, ng)
    n32 = nq + nux
    kinds = {
        "plain": [(n32 + nq, n32 + 2 * nq, 0)],
        "sigmoid": [(n32 + 3 * nq, len(order), 0)],
    }
    return _runs(order), kinds, n32


def _in_proj_kernel(kinds, n32, x_ref, g_ref, w_ref, zb_ref, zf_ref, *rest):
    xn_ref = rest[-1]
    j = pl.program_id(1)

    @pl.when(j == 0)
    def _():
        xn_ref[...] = _rms(x_ref[...], g_ref[...]).astype(BF16)

    is_sigmoid = _in_runs(j, kinds["sigmoid"])
    is_plain = _in_runs(j, kinds["plain"])
    xn = xn_ref[...]
    tn = w_ref.shape[1]
    cw = min(MXU_COLS, tn)
    zs = []
    for c in range(tn // cw):
        cs = slice(c * cw, (c + 1) * cw)
        w = w_ref[:, cs].astype(BF16)
        if len(rest) == 2:
            rest[0][:, cs] = w
        z = _dot(xn, w)
        zs.append(z)
        s = _sigmoid(z)
        left = jnp.where(is_sigmoid, 1.0, z)
        right = jnp.where(is_plain, 1.0, s)
        zb_ref[:, cs] = (left * right).astype(BF16)

    @pl.when(j < n32)
    def _():
        for c, z in enumerate(zs):
            zf_ref[:, c * cw:(c + 1) * cw] = z


def _in_proj(x, g3, w, layer, tm, tn, emit=False):
    T, D = x.shape
    N = w.shape[-1]
    assert not emit or T == tm
    order, kinds, n32 = _in_proj_plan(D, tn)
    nsteps = N // tn
    wcol = lambda i, j: (0, _lookup(j, order))
    out_specs = [
        pl.BlockSpec((tm, tn), lambda i, j: (i, jnp.maximum(j - n32, 0))),
        pl.BlockSpec((tm, tn), lambda i, j: (i, jnp.minimum(j, n32 - 1))),
    ]
    out_shape = [
        jax.ShapeDtypeStruct((T, (nsteps - n32) * tn), BF16),
        jax.ShapeDtypeStruct((T, n32 * tn), F32),
    ]
    if emit:
        out_specs.append(pl.BlockSpec((D, tn), wcol))
        out_shape.append(jax.ShapeDtypeStruct((D, N), BF16))
    return pl.pallas_call(
        functools.partial(_in_proj_kernel, kinds, n32),
        grid=(T // tm, nsteps),
        in_specs=[
            pl.BlockSpec((tm, D), lambda i, j: (i, 0)),
            pl.BlockSpec((None, 1, D), lambda i, j: (layer, 0, 0)),
            _layer_spec(w, layer, (D, tn), wcol),
        ],
        out_specs=out_specs,
        out_shape=out_shape,
        scratch_shapes=[pltpu.VMEM((tm, D), BF16)],
        compiler_params=_params(("parallel", "arbitrary"), 48),
        name="in_proj",
    )(x, g3, w)


def _lower_bound_kernel(lg_ref, o_ref):
    lg = lg_ref[...]
    depth = lg.shape[0]
    rows = [lg[i:i + 1, :] for i in range(depth)]
    m = rows[0]
    for r in rows[1:]:
        m = jnp.maximum(m, r)
    e = [jnp.exp(r - m) for r in rows]
    tot = e[0]
    for v in e[1:]:
        tot = tot + v
    c = e[0] / tot
    first = c
    o_ref[0:1, :] = c - first
    for i in range(1, depth):
        c = c + e[i] / tot
        o_ref[i:i + 1, :] = c - first


def _lower_bounds(lb_logits):
    return pl.pallas_call(
        _lower_bound_kernel,
        out_shape=jax.ShapeDtypeStruct(lb_logits.shape, F32),
        name="hgrn_lower_bounds",
    )(lb_logits)


def _neg_abs(x):
    bits = lax.bitcast_convert_type(x, jnp.uint32) | jnp.uint32(0x80000000)
    return lax.bitcast_convert_type(bits, F32)


def _group_ref_row(b, s):
    rows, width = b.shape
    gsz = 2 * s
    if gsz >= 8:
        parts = [jnp.broadcast_to(b[i * gsz + s - 1:i * gsz + s, :], (gsz, width)) for i in range(rows // gsz)]
        return parts[0] if len(parts) == 1 else jnp.concatenate(parts, axis=0)
    pos = lax.broadcasted_iota(jnp.int32, b.shape, 0) & (gsz - 1)
    out = b
    for p in range(gsz):
        d = p - (s - 1)
        if d != 0:
            out = jnp.where(pos == p, pltpu.roll(b, d % rows, 0), out)
    return out


def _hgrn_block(q, zf, vb, lb, states, seg):
    C = q.shape[0]
    fg = lb + (1.0 - lb) * jax.nn.sigmoid(zf)
    kk = 1.0 - fg
    g = jnp.log2(fg)

    row = lax.broadcasted_iota(jnp.int32, (C, LANE), 0)
    rr = lax.broadcasted_iota(jnp.int32, (C, C), 0)
    cc = lax.broadcasted_iota(jnp.int32, (C, C), 1)
    lseg = seg.bit_length() - 1
    tri = jnp.where((cc <= rr) & ((rr >> lseg) == (cc >> lseg)), 1.0, 0.0).astype(F32)
    b = jnp.dot(tri, g, precision=lax.Precision.HIGHEST, preferred_element_type=F32)

    a = jnp.where(rr == cc, _dot_nt(q.astype(BF16), kk.astype(BF16)), 0.0)
    s = seg // 2
    while s >= 1:
        ls = s.bit_length() - 1
        e = jnp.exp2(_neg_abs(b - _group_ref_row(b, s)))
        same = (rr >> (ls + 1)) == (cc >> (ls + 1))
        if s % SUBLANE == 0:
            nblk = C // s
            zero = jnp.zeros((s, LANE), F32)
            blk = lambda x, i: x[i * s:(i + 1) * s, :]
            qs = jnp.concatenate([blk(q, i) * blk(e, i) if i % 2 else zero for i in range(nblk)], axis=0)
            ks = jnp.concatenate([zero if i % 2 else blk(kk, i) * blk(e, i) for i in range(nblk)], axis=0)
            p = _dot_nt(qs.astype(BF16), ks.astype(BF16))
            pieces = []
            for i in range(nblk):
                if i % 2 == 0:
                    pieces.append(blk(a, i))
                elif 2 * s == C:
                    pieces.append(blk(a, i) + blk(p, i))
                else:
                    pieces.append(blk(a, i) + jnp.where(blk(same, i), blk(p, i), 0.0))
            a = jnp.concatenate(pieces, axis=0)
        else:
            right = ((row >> ls) & 1) == 1
            eq = jnp.where(right, e, 0.0)
            p = _dot_nt((q * eq).astype(BF16), (kk * (e - eq)).astype(BF16))
            a = a + jnp.where(same, p, 0.0)
        s //= 2
    o = _dot(a.astype(BF16), vb)

    qe = q * jnp.exp2(b)
    new_states = []
    for k, st in enumerate(states):
        bl = b[k * seg + seg - 1:k * seg + seg, :]
        if len(states) == 1:
            qk = qe
            ke = kk * jnp.exp2(bl - b)
        else:
            mine = (row >> lseg) == k
            qk = jnp.where(mine, qe, 0.0)
            ke = jnp.where(mine, kk * jnp.exp2(jnp.where(mine, bl - b, 0.0)), 0.0)
        o = o + _dot(qk.astype(BF16), st.astype(BF16))
        decay = jnp.transpose(jnp.broadcast_to(jnp.exp2(bl), (LANE, LANE)))
        new_states.append(st * decay + _dot_tn(ke.astype(BF16), vb))
    return o, new_states


def _hgrn_wide(q, zf, vb, lb, states, seg):
    C, W = q.shape
    heads = [slice(h * LANE, (h + 1) * LANE) for h in range(W // LANE)]
    nseg = C // seg
    fg = lb + (1.0 - lb) * jax.nn.sigmoid(zf)
    kk = 1.0 - fg
    g = jnp.log2(fg)

    row = lax.broadcasted_iota(jnp.int32, (C, W), 0)
    rr = lax.broadcasted_iota(jnp.int32, (C, C), 0)
    cc = lax.broadcasted_iota(jnp.int32, (C, C), 1)
    lseg = seg.bit_length() - 1
    tri = jnp.where((cc <= rr) & ((rr >> lseg) == (cc >> lseg)), 1.0, 0.0).astype(F32)
    b = jnp.dot(tri, g, precision=lax.Precision.HIGHEST, preferred_element_type=F32)

    qb, kb = q.astype(BF16), kk.astype(BF16)
    a = [jnp.where(rr == cc, _dot_nt(qb[:, hs], kb[:, hs]), 0.0) for hs in heads]
    s = seg // 2
    while s >= 1:
        ls = s.bit_length() - 1
        same = (rr >> (ls + 1)) == (cc >> (ls + 1))
        if s == 1:
            right = (row & 1) == 1
            qs, ks = jnp.where(right, q * fg, 0.0).astype(BF16), jnp.where(right, 0.0, kk).astype(BF16)
            for h, hs in enumerate(heads):
                a[h] = a[h] + jnp.where(same, _dot_nt(qs[:, hs], ks[:, hs]), 0.0)
            break
        e = jnp.exp2(_neg_abs(b - _group_ref_row(b, s)))
        if s % SUBLANE == 0:
            nblk = C // s
            zero = jnp.zeros((s, W), F32)
            blk = lambda x, i: x[i * s:(i + 1) * s, :]
            qs = jnp.concatenate([blk(q, i) * blk(e, i) if i % 2 else zero for i in range(nblk)], axis=0).astype(BF16)
            ks = jnp.concatenate([zero if i % 2 else blk(kk, i) * blk(e, i) for i in range(nblk)], axis=0).astype(BF16)
            for h, hs in enumerate(heads):
                p = _dot_nt(qs[:, hs], ks[:, hs])
                pieces = []
                for i in range(nblk):
                    if i % 2 == 0:
                        pieces.append(blk(a[h], i))
                    elif 2 * s == C:
                        pieces.append(blk(a[h], i) + blk(p, i))
                    else:
                        pieces.append(blk(a[h], i) + jnp.where(blk(same, i), blk(p, i), 0.0))
                a[h] = jnp.concatenate(pieces, axis=0)
        else:
            right = ((row >> ls) & 1) == 1
            eq = jnp.where(right, e, 0.0)
            qs, ks = (q * eq).astype(BF16), (kk * (e - eq)).astype(BF16)
            for h, hs in enumerate(heads):
                a[h] = a[h] + jnp.where(same, _dot_nt(qs[:, hs], ks[:, hs]), 0.0)
        s //= 2
    o = [_dot(a[h].astype(BF16), vb[:, hs]) for h, hs in enumerate(heads)]

    qe = q * jnp.exp2(b)
    new_states = [[] for _ in heads]
    for k in range(nseg):
        bl = b[k * seg + seg - 1:k * seg + seg, :]
        if nseg == 1:
            qk = qe
            ke = kk * jnp.exp2(bl - b)
        else:
            mine = (row >> lseg) == k
            qk = jnp.where(mine, qe, 0.0)
            ke = jnp.where(mine, kk * jnp.exp2(jnp.where(mine, bl - b, 0.0)), 0.0)
        qkb, keb, ebl = qk.astype(BF16), ke.astype(BF16), jnp.exp2(bl)
        for h, hs in enumerate(heads):
            st = states[h][k]
            o[h] = o[h] + _dot(qkb[:, hs], st.astype(BF16))
            decay = jnp.transpose(jnp.broadcast_to(ebl[:, hs], (LANE, LANE)))
            new_states[h].append(st * decay + _dot_tn(keb[:, hs], vb[:, hs]))
    return o, new_states


def _hgrn_finish(o, og, hgn):
    return (_rms(o, hgn) * og.astype(F32)).astype(BF16)


def _hgrn_prompt_kernel(q_ref, zf_ref, v_ref, og_ref, lb_ref, hgn_ref, oh_ref, s_ref):
    @pl.when(pl.program_id(1) == 0)
    def _():
        s_ref[...] = jnp.zeros_like(s_ref)

    states = [[s_ref[h]] for h in range(HG_HEADS)]
    o, new = _hgrn_wide(q_ref[...].astype(F32), zf_ref[...], v_ref[...], lb_ref[...], states, HGRN_CHUNK)
    for h in range(HG_HEADS):
        cs = slice(h * LANE, (h + 1) * LANE)
        s_ref[h] = new[h][0]
        oh_ref[:, cs] = _hgrn_finish(o[h], og_ref[:, cs], hgn_ref[:, cs])


def _hgrn_prompt(zb, zf, lbs3, hgn3, layer, batch):
    T = zb.shape[0] // batch
    C = HGRN_CHUNK
    nc = T // C
    W = HG_HEADS * LANE
    zspec = lambda col: pl.BlockSpec((C, W), lambda b, c: (b * nc + c, col))
    vec = pl.BlockSpec((None, 1, W), lambda b, c: (layer, 0, 0))
    return pl.pallas_call(
        _hgrn_prompt_kernel,
        grid=(batch, nc),
        in_specs=[zspec(ZB_Q), zspec(ZF_F), zspec(ZB_V), zspec(ZB_OG), vec, vec],
        out_specs=[
            pl.BlockSpec((C, W), lambda b, c: (b * nc + c, 0)),
            pl.BlockSpec((None, HG_HEADS, LANE, LANE), lambda b, c: (b, 0, 0, 0)),
        ],
        out_shape=[
            jax.ShapeDtypeStruct((batch * T, W), BF16),
            jax.ShapeDtypeStruct((batch, HG_HEADS, LANE, LANE), F32),
        ],
        compiler_params=_params(("parallel", "arbitrary"), 32),
        name="hgrn_prompt",
    )(zb, zf, zb, zb, lbs3, hgn3)


def _hgrn_decode_kernel(seq_len, q_ref, zf_ref, v_ref, og_ref, lb_ref, hgn_ref, s_ref, *rest):
    oh_ref, so_ref = rest[-2], rest[-1]
    rows = DEC_GROUP * seq_len
    for grp in range(s_ref.shape[0] // DEC_GROUP):
        rs = slice(grp * rows, (grp + 1) * rows)
        states = [[s_ref[grp * DEC_GROUP + k, h] for k in range(DEC_GROUP)] for h in range(HG_HEADS)]
        o, new = _hgrn_wide(q_ref[rs, :].astype(F32), zf_ref[rs, :], v_ref[rs, :], lb_ref[...], states, seq_len)
        for h in range(HG_HEADS):
            cs = slice(h * LANE, (h + 1) * LANE)
            for k in range(DEC_GROUP):
                so_ref[grp * DEC_GROUP + k, h] = new[h][k]
            oh_ref[rs, cs] = _hgrn_finish(o[h], og_ref[rs, cs], hgn_ref[:, cs])


def _hgrn_decode(zb, zf, lbs3, hgn3, state, stacked, layer, nseq, nb):
    seq_len = zb.shape[0] // nseq
    W = HG_HEADS * LANE
    zspec = lambda col: pl.BlockSpec((nb * seq_len, W), lambda i: (i, col))
    vec = pl.BlockSpec((None, 1, W), lambda i: (layer, 0, 0))
    sspec = pl.BlockSpec((None, nb, HG_HEADS, LANE, LANE), lambda i: (layer, i, 0, 0, 0))
    in_specs = [zspec(ZB_Q), zspec(ZF_F), zspec(ZB_V), zspec(ZB_OG), vec, vec, sspec]
    args = [zb, zf, zb, zb, lbs3, hgn3, state]
    aliases = {}
    if stacked is not None:
        in_specs.append(pl.BlockSpec(memory_space=pl.ANY))
        args.append(stacked)
        aliases = {len(args) - 1: 1}
    return pl.pallas_call(
        functools.partial(_hgrn_decode_kernel, seq_len),
        grid=(nseq // nb,),
        in_specs=in_specs,
        out_specs=[pl.BlockSpec((nb * seq_len, W), lambda i: (i, 0)), sspec],
        out_shape=[
            jax.ShapeDtypeStruct((nseq * seq_len, W), BF16),
            jax.ShapeDtypeStruct(state.shape, F32),
        ],
        input_output_aliases=aliases,
        compiler_params=_params(("parallel",), 48),
        name="hgrn_decode",
    )(*args)


def _pool_prompt_kernel(u_ref, wp_ref, sc_ref, op_ref, nb_ref):
    u = u_ref[...]
    T = u.shape[0]
    row = lax.broadcasted_iota(jnp.int32, (T, LANE), 0)
    for g, w in enumerate(POOL_WINDOWS):
        cs = slice(g * LANE, (g + 1) * LANE)
        ug = u[:, cs]
        s = ug
        d = 1
        while d < w:
            s = s + jnp.where(row >= d, pltpu.roll(s, d, 0), 0.0)
            d *= 2
        cnt = jnp.minimum(row + 1, w).astype(F32)
        dv = s / cnt - ug
        y = _dot(dv.astype(BF16), wp_ref[g].astype(BF16)) * sc_ref[:, cs]
        op_ref[:, cs] = y.astype(BF16)
    tail = u_ref[T - 16:T, :]
    nb_ref[...] = pltpu.roll(tail, 15, 0)[0:POOL_BUF, :]


def _pool_prompt(zf, w_pool, scale3, layer, batch):
    T = zf.shape[0] // batch
    G = len(POOL_WINDOWS)
    W = G * LANE
    return pl.pallas_call(
        _pool_prompt_kernel,
        grid=(batch,),
        in_specs=[
            pl.BlockSpec((T, W), lambda b: (b, ZF_U)),
            pl.BlockSpec((None, G, LANE, LANE), lambda b: (layer, 0, 0, 0)),
            pl.BlockSpec((None, 1, W), lambda b: (layer, 0, 0)),
        ],
        out_specs=[
            pl.BlockSpec((T, W), lambda b: (b, 0)),
            pl.BlockSpec((None, POOL_BUF, W), lambda b: (b, 0, 0)),
        ],
        out_shape=[
            jax.ShapeDtypeStruct((batch * T, W), BF16),
            jax.ShapeDtypeStruct((batch, POOL_BUF, W), F32),
        ],
        compiler_params=_params(("parallel",), 48),
        name="pool_prompt",
    )(zf, w_pool, scale3)


def _pool_decode_kernel(u_ref, buf_ref, wp_ref, sc_ref, op_ref, nb_ref):
    steps = u_ref.shape[0]
    for g, w in enumerate(POOL_WINDOWS):
        cs = slice(g * LANE, (g + 1) * LANE)
        wg = wp_ref[g].astype(BF16)
        for t in range(steps):
            n_u = min(t + 1, w)
            acc = u_ref[t, :, cs]
            for j in range(t - n_u + 1, t):
                acc = acc + u_ref[j, :, cs]
            for i in range(POOL_BUF - (w - n_u), POOL_BUF):
                acc = acc + buf_ref[i, :, cs]
            dv = acc * (1.0 / w) - u_ref[t, :, cs]
            op_ref[t, :, cs] = _dot(dv.astype(BF16), wg) * sc_ref[:, cs]
    for i in range(POOL_BUF - steps):
        nb_ref[i] = buf_ref[i + steps]
    for t in range(steps):
        nb_ref[POOL_BUF - steps + t] = u_ref[t]


def _pool_decode(u_t, buf_t, w_pool, scale3, layer):
    steps, nseq, W = u_t.shape
    G = len(POOL_WINDOWS)
    return pl.pallas_call(
        _pool_decode_kernel,
        grid=(1,),
        in_specs=[
            pl.BlockSpec((steps, nseq, W), lambda i: (0, 0, 0)),
            pl.BlockSpec((POOL_BUF, nseq, W), lambda i: (0, 0, 0)),
            pl.BlockSpec((None, G, LANE, LANE), lambda i: (layer, 0, 0, 0)),
            pl.BlockSpec((None, 1, W), lambda i: (layer, 0, 0)),
        ],
        out_specs=[
            pl.BlockSpec((steps, nseq, W), lambda i: (0, 0, 0)),
            pl.BlockSpec((POOL_BUF, nseq, W), lambda i: (0, 0, 0)),
        ],
        out_shape=[
            jax.ShapeDtypeStruct((steps, nseq, W), F32),
            jax.ShapeDtypeStruct((POOL_BUF, nseq, W), F32),
        ],
        compiler_params=_params(("arbitrary",), 32),
        name="pool_decode",
    )(u_t, buf_t, w_pool, scale3)


def _softmax_rows(s):
    e = jnp.exp(s - jnp.max(s, axis=-1, keepdims=True))
    return e / jnp.sum(e, axis=-1, keepdims=True)


def _xattn_prompt_kernel(q_ref, k_ref, v_ref, o_ref):
    scale = LANE ** -0.5
    for h in range(XA_HEADS):
        cs = slice(h * LANE, (h + 1) * LANE)
        s = _dot_nt(q_ref[:, cs].astype(BF16), k_ref[:, cs].astype(BF16)) * scale
        p = _softmax_rows(s)
        o_ref[:, cs] = _dot(p.astype(BF16), v_ref[:, cs].astype(BF16)).astype(BF16)


def _xattn_prompt(zf, mk, mv, batch, tq):
    T = zf.shape[0] // batch
    W = XA_HEADS * LANE
    nq = T // tq
    n_mem = mk.shape[1]
    mspec = pl.BlockSpec((None, n_mem, W), lambda b, i: (b, 0, 0))
    return pl.pallas_call(
        _xattn_prompt_kernel,
        grid=(batch, nq),
        in_specs=[pl.BlockSpec((tq, W), lambda b, i: (b * nq + i, ZF_X)), mspec, mspec],
        out_specs=pl.BlockSpec((tq, W), lambda b, i: (b * nq + i, 0)),
        out_shape=jax.ShapeDtypeStruct((batch * T, W), BF16),
        compiler_params=_params(("parallel", "parallel"), 48),
        name="xattn_prompt",
    )(zf, mk, mv)


def _xattn_decode_kernel(seq_len, q_ref, k_ref, v_ref, o_ref):
    scale = LANE ** -0.5
    rows = DEC_GROUP * seq_len
    lseq = seq_len.bit_length() - 1
    lrows = rows.bit_length() - 1
    nk = k_ref.shape[1]
    row_s = lax.broadcasted_iota(jnp.int32, (XA_HEADS * rows, nk), 0)
    col_s = lax.broadcasted_iota(jnp.int32, (XA_HEADS * rows, nk), 1)
    own_head = (col_s & (XA_HEADS - 1)) == (row_s >> lrows)
    seq_s = (row_s & (rows - 1)) >> lseq
    seq_o = (lax.broadcasted_iota(jnp.int32, (XA_HEADS * rows, LANE), 0) & (rows - 1)) >> lseq
    for grp in range(k_ref.shape[0] // DEC_GROUP):
        rs = slice(grp * rows, (grp + 1) * rows)
        qb = jnp.concatenate([q_ref[rs, h * LANE:(h + 1) * LANE] for h in range(XA_HEADS)], axis=0).astype(BF16)
        s = jnp.full((XA_HEADS * rows, nk), -jnp.inf, F32)
        for k in range(DEC_GROUP):
            sk = _dot_nt(qb, k_ref[grp * DEC_GROUP + k].astype(BF16))
            s = jnp.where(own_head & (seq_s == k), sk, s)
        p = _softmax_rows(s * scale).astype(BF16)
        o = jnp.zeros((XA_HEADS * rows, LANE), F32)
        for k in range(DEC_GROUP):
            o = jnp.where(seq_o == k, _dot(p, v_ref[grp * DEC_GROUP + k].astype(BF16)), o)
        for h in range(XA_HEADS):
            o_ref[rs, h * LANE:(h + 1) * LANE] = o[h * rows:(h + 1) * rows, :].astype(BF16)


def _xattn_decode(zf, cache_k, cache_v, layer, nseq, nb):
    seq_len = zf.shape[0] // nseq
    W = XA_HEADS * LANE
    cspec = pl.BlockSpec((None, nb) + cache_k.shape[2:], lambda i: (layer, i, 0, 0))
    return pl.pallas_call(
        functools.partial(_xattn_decode_kernel, seq_len),
        grid=(nseq // nb,),
        in_specs=[pl.BlockSpec((nb * seq_len, W), lambda i: (i, ZF_X)), cspec, cspec],
        out_specs=pl.BlockSpec((nb * seq_len, W), lambda i: (i, 0)),
        out_shape=jax.ShapeDtypeStruct((nseq * seq_len, W), BF16),
        compiler_params=_params(("parallel",), 48),
        name="xattn_decode",
    )(zf, cache_k, cache_v)


def _merge_kernel(x_ref, oh_ref, op_ref, ox_ref, g0_ref, g1_ref, g2_ref, wb_ref, wo_ref, o_ref, *rest):
    y_ref = rest[-1]
    wbb_ref, wob_ref = rest[:2] if len(rest) == 3 else (None, None)
    j = pl.program_id(1)
    nj, _, tn = y_ref.shape

    @pl.when(j < nj)
    def _():
        wh = oh_ref.shape[1]
        wp = op_ref.shape[1]
        oh, op, ox = oh_ref[...].astype(BF16), op_ref[...].astype(BF16), ox_ref[...].astype(BF16)
        cw = min(MXU_COLS, tn)
        for c in range(tn // cw):
            cs = slice(c * cw, (c + 1) * cw)
            w = wb_ref[:, cs].astype(BF16)
            if wbb_ref is not None:
                wbb_ref[:, cs] = w
            y = g0_ref[:, cs].astype(F32) * _dot(oh, w[0:wh, :])
            y += g1_ref[:, cs].astype(F32) * _dot(op, w[wh:wh + wp, :])
            y += g2_ref[:, cs].astype(F32) * _dot(ox, w[wh + wp:, :])
            y_ref[j, :, cs] = y.astype(BF16)

    @pl.when(j >= nj)
    def _():
        acc = x_ref[...]
        for c in range(nj):
            rs = slice(c * tn, (c + 1) * tn)
            w = wo_ref[rs, :].astype(BF16)
            if wob_ref is not None:
                wob_ref[rs, :] = w
            acc += _dot(y_ref[c], w)
        o_ref[...] = acc


def _merge(x, oh, op, ox, zb, wb, wo, layer, tm, tn, emit=False):
    T, D = x.shape
    assert not emit or T == tm
    gcol = ZB_GATES * HG_HEADS * LANE // tn
    nj = D // tn
    first = lambda j: jnp.minimum(j, nj - 1)
    second = lambda j: jnp.maximum(j - nj, 0)
    full = lambda a: pl.BlockSpec((tm, a.shape[1]), lambda i, j: (i, 0))
    gate = lambda k: pl.BlockSpec((tm, tn), lambda i, j: (i, gcol + k * nj + first(j)))
    wb_idx = lambda i, j: (0, first(j))
    wo_idx = lambda i, j: (0, second(j))
    rows_b = wb.shape[-2]
    out_specs = [pl.BlockSpec((tm, tn), lambda i, j: (i, second(j)))]
    out_shape = [jax.ShapeDtypeStruct((T, D), F32)]
    if emit:
        out_specs += [pl.BlockSpec((rows_b, tn), wb_idx), pl.BlockSpec((D, tn), wo_idx)]
        out_shape += [jax.ShapeDtypeStruct((rows_b, D), BF16), jax.ShapeDtypeStruct((D, D), BF16)]
    out = pl.pallas_call(
        _merge_kernel,
        grid=(T // tm, 2 * nj),
        in_specs=[
            pl.BlockSpec((tm, tn), lambda i, j: (i, second(j))),
            full(oh), full(op), full(ox), gate(0), gate(1), gate(2),
            _layer_spec(wb, layer, (rows_b, tn), wb_idx),
            _layer_spec(wo, layer, (D, tn), wo_idx),
        ],
        out_specs=out_specs,
        out_shape=out_shape,
        scratch_shapes=[pltpu.VMEM((nj, tm, tn), BF16)],
        compiler_params=_params(("parallel", "arbitrary"), 63),
        name="merge_out",
    )(x, oh, op, ox, zb, zb, zb, wb, wo)
    return out if emit else out[0]


def _norm_kernel(x_ref, g_ref, o_ref):
    o_ref[...] = _rms(x_ref[...], g_ref[...])


def _final_norm(x, g2, tm):
    T, D = x.shape
    return pl.pallas_call(
        _norm_kernel,
        grid=(T // tm,),
        in_specs=[pl.BlockSpec((tm, D), lambda i: (i, 0)), pl.BlockSpec((1, D), lambda i: (0, 0))],
        out_specs=pl.BlockSpec((tm, D), lambda i: (i, 0)),
        out_shape=jax.ShapeDtypeStruct((T, D), F32),
        compiler_params=_params(("parallel",), 40),
        name="final_norm",
    )(x, g2)


def kernel(x_prompt, x_sample, state_hgrn, state_pool, cache_mem_k, cache_mem_v, mem_prompt, ffn1_norm, ffn1_w1, ffn1_w3, ffn1_w2, mix_norm, w_in, lb_logits, hg_norm, w_pool, pool_scale, mem_norm, w_mk, w_mv, w_branch, w_out, ffn2_norm, ffn2_w1, ffn2_w3, ffn2_w2, final_norm):
    B, T, D = x_prompt.shape
    nseq, steps, _ = x_sample.shape
    depth = w_in.shape[0]
    n_mem = mem_prompt.shape[1]
    xw = XA_HEADS * LANE
    pw = len(POOL_WINDOWS) * LANE

    tm_p = min(1024, B * T)
    tm_s = min(512, nseq * steps)

    vec3 = lambda a: a.reshape(a.shape[0], 1, a.shape[1])
    f1n, f2n = vec3(ffn1_norm), vec3(ffn2_norm)
    mix3, mem3, hgn3, psc3 = vec3(mix_norm), vec3(mem_norm), vec3(hg_norm), vec3(pool_scale)
    lbs3 = vec3(_lower_bounds(lb_logits))

    xp = x_prompt.reshape(B * T, D)
    xs = x_sample.reshape(nseq * steps, D)
    mem2 = mem_prompt.reshape(B * n_mem, D)
    ck = cache_mem_k.reshape(depth, nseq, n_mem * XA_HEADS, LANE)
    cv = cache_mem_v.reshape(depth, nseq, n_mem * XA_HEADS, LANE)

    tf = _tile(ffn1_w1.shape[-1], 512)
    tf_s = _tile(ffn1_w1.shape[-1], 256)
    tn_in = _tile(math.gcd(HG_HEADS * LANE, N_BRANCH * D), 1024)
    tn_mrg = _tile(D, 512)
    tm_mem = min(512, B * n_mem)
    hs_p, pb_p, mk_p, mv_p, pb_s = [], [], [], [], []
    hs_s = None
    for l in range(depth):
        mk = _rms_matmul(mem2, mem3, w_mk, l, tm_mem, xw)
        mv = _rms_matmul(mem2, mem3, w_mv, l, tm_mem, xw)
        mk_p.append(mk.reshape(B, n_mem, XA_HEADS, LANE))
        mv_p.append(mv.reshape(B, n_mem, XA_HEADS, LANE))

        xs, w1b, w3b, w2b = _ffn(xs, f1n, ffn1_w1, ffn1_w3, ffn1_w2, l, tm_s, tf_s)
        xp = _ffn(xp, f1n, w1b, w3b, w2b, l, tm_p, tf)

        zsb, zsf, w_in_b = _in_proj(xs, mix3, w_in, l, tm_s, tn_in, emit=True)
        zpb, zpf = _in_proj(xp, mix3, w_in_b, l, tm_p, tn_in)

        ohp, sp = _hgrn_prompt(zpb, zpf, lbs3, hgn3, l, B)
        ohs, hs_s = _hgrn_decode(zsb, zsf, lbs3, hgn3, state_hgrn, hs_s, l, nseq, 8)
        hs_p.append(sp)

        opp, bp = _pool_prompt(zpf, w_pool, psc3, l, B)
        pb_p.append(bp)
        u_t = zsf[:, ZF_U * pw:(ZF_U + 1) * pw].reshape(nseq, steps, pw).transpose(1, 0, 2)
        buf_t = state_pool[l].transpose(1, 0, 2)
        ops_t, nb_t = _pool_decode(u_t, buf_t, w_pool, psc3, l)
        ops = ops_t.transpose(1, 0, 2).reshape(nseq * steps, pw)
        pb_s.append(nb_t.transpose(1, 0, 2))

        oxp = _xattn_prompt(zpf, mk.reshape(B, n_mem, xw), mv.reshape(B, n_mem, xw), B, min(1024, T))
        oxs = _xattn_decode(zsf, ck, cv, l, nseq, 8)

        xs, w_br_b, w_out_b = _merge(xs, ohs, ops, oxs, zsb, w_branch, w_out, l, tm_s, tn_mrg, emit=True)
        xp = _merge(xp, ohp, opp, oxp, zpb, w_br_b, w_out_b, l, tm_p, _tile(D, 1024))

        fn2 = final_norm.reshape(1, D) if l == depth - 1 else None
        xs, w1b, w3b, w2b = _ffn(xs, f2n, ffn2_w1, ffn2_w3, ffn2_w2, l, tm_s, tf_s, fn2)
        xp = _ffn(xp, f2n, w1b, w3b, w2b, l, tm_p, tf, fn2)

    y_prompt = xp.reshape(B, T, D)
    y_sample = xs.reshape(nseq, steps, D)
    return (y_prompt, y_sample, jnp.stack(hs_p), jnp.stack(pb_p), jnp.stack(mk_p), jnp.stack(mv_p),
            hs_s, jnp.stack(pb_s))
```

```python
import functools
import math

import jax
import jax.numpy as jnp
from jax import lax
from jax.experimental import pallas as pl
from jax.experimental.pallas import tpu as pltpu

F32 = jnp.float32
BF16 = jnp.bfloat16
EPS = 1e-6

LANE = 128
SUBLANE = 8
MXU_COLS = 256
HG_HEADS = 8
POOL_WINDOWS = (2, 4, 8, 16)
POOL_BUF = max(POOL_WINDOWS) - 1
XA_HEADS = 4
N_BRANCH = 3

COL_Q, COL_F, COL_I, COL_OG = 0, HG_HEADS, 2 * HG_HEADS, 3 * HG_HEADS
COL_U = 4 * HG_HEADS
COL_X = COL_U + len(POOL_WINDOWS)
COL_G = COL_X + XA_HEADS
ZB_Q, ZB_V, ZB_OG, ZB_GATES = 0, 1, 2, 3
ZF_F = 0
ZF_U, ZF_X = 2, 3

HGRN_CHUNK = 128
DEC_GROUP = 4


def _tile(n, preferred):
    t = preferred
    while n % t:
        t -= LANE
    return t


def _params(semantics, vmem_mib):
    return pltpu.CompilerParams(dimension_semantics=semantics, vmem_limit_bytes=vmem_mib * 1024 * 1024)


def _rms(x, g):
    return x * lax.rsqrt(jnp.mean(x * x, axis=-1, keepdims=True) + EPS) * g


def _sigmoid(x):
    return 0.5 * jnp.tanh(0.5 * x) + 0.5


def _silu(x):
    return x * _sigmoid(x)


def _dot(a, b):
    return jnp.dot(a, b, preferred_element_type=F32)


def _dot_nt(a, b):
    return lax.dot_general(a, b, (((1,), (1,)), ((), ())), preferred_element_type=F32)


def _dot_tn(a, b):
    return lax.dot_general(a, b, (((0,), (0,)), ((), ())), preferred_element_type=F32)


def _ffn_kernel(final, cast, x_ref, g_ref, w1_ref, w3_ref, w2_ref, *rest):
    fn_ref = rest[0] if final else None
    o_ref, xn_ref = rest[int(final)], rest[-1]
    if cast:
        src = (w1_ref, w3_ref, w2_ref)
        w1_ref, w3_ref, w2_ref = rest[int(final) + 1:int(final) + 4]
        for dst, s in zip((w1_ref, w3_ref, w2_ref), src):
            dst[...] = s[...].astype(BF16)
    j = pl.program_id(1)

    @pl.when(j == 0)
    def _():
        x = x_ref[...]
        xn_ref[...] = _rms(x, g_ref[...]).astype(BF16)
        o_ref[...] = x

    xn = xn_ref[...]
    h = (_silu(_dot(xn, w1_ref[...])) * _dot(xn, w3_ref[...])).astype(BF16)
    cw = min(w2_ref.shape[0], o_ref.shape[1])
    for n in range(o_ref.shape[1] // cw):
        cs = slice(n * cw, (n + 1) * cw)
        o_ref[:, cs] += 0.5 * _dot(h, w2_ref[:, cs])

    if final:
        @pl.when(j == pl.num_programs(1) - 1)
        def _():
            o_ref[...] = _rms(o_ref[...], fn_ref[...])


def _layer_spec(w, layer, block, index):
    if w.ndim == 2:
        return pl.BlockSpec(block, index)
    return pl.BlockSpec((None,) + block, lambda i, j: (layer,) + index(i, j))


def _ffn(x, g3, w1, w3, w2, layer, tm, tf, final_g=None):
    T, D = x.shape
    F = w1.shape[-1]
    cast = w1.dtype != BF16
    final = final_g is not None
    assert not cast or T == tm
    col = lambda i, j: (0, j)
    row = lambda i, j: (j, 0)
    out_specs = [pl.BlockSpec((tm, D), lambda i, j: (i, 0))]
    out_shape = [jax.ShapeDtypeStruct((T, D), F32)]
    if cast:
        out_specs += [pl.BlockSpec((D, tf), col), pl.BlockSpec((D, tf), col), pl.BlockSpec((tf, D), row)]
        out_shape += [jax.ShapeDtypeStruct((D, F), BF16), jax.ShapeDtypeStruct((D, F), BF16),
                      jax.ShapeDtypeStruct((F, D), BF16)]
    in_specs = [
        pl.BlockSpec((tm, D), lambda i, j: (i, 0)),
        pl.BlockSpec((None, 1, D), lambda i, j: (layer, 0, 0)),
        _layer_spec(w1, layer, (D, tf), col),
        _layer_spec(w3, layer, (D, tf), col),
        _layer_spec(w2, layer, (tf, D), row),
    ]
    args = [x, g3, w1, w3, w2]
    if final:
        in_specs.append(pl.BlockSpec((1, D), lambda i, j: (0, 0)))
        args.append(final_g)
    out = pl.pallas_call(
        functools.partial(_ffn_kernel, final, cast),
        grid=(T // tm, F // tf),
        in_specs=in_specs,
        out_specs=out_specs,
        out_shape=out_shape,
        scratch_shapes=[pltpu.VMEM((tm, D), BF16)],
        compiler_params=_params(("parallel", "arbitrary"), 58),
        name="ffn_cast" if cast else "ffn",
    )(*args)
    return out if cast else out[0]


def _rms_matmul_kernel(x_ref, g_ref, w_ref, o_ref, *rest):
    xn_ref = rest[-1]

    @pl.when(pl.program_id(1) == 0)
    def _():
        xn_ref[...] = _rms(x_ref[...], g_ref[...]).astype(BF16)

    w = w_ref[...].astype(BF16)
    if len(rest) == 2:
        rest[0][...] = w
    o_ref[...] = _dot(xn_ref[...], w)


def _rms_matmul(x, g3, w, layer, tm, tn, emit=False):
    T, D = x.shape
    N = w.shape[-1]
    assert not emit or T == tm
    col = lambda i, j: (0, j)
    out_specs = [pl.BlockSpec((tm, tn), lambda i, j: (i, j))]
    out_shape = [jax.ShapeDtypeStruct((T, N), F32)]
    if emit:
        out_specs.append(pl.BlockSpec((D, tn), col))
        out_shape.append(jax.ShapeDtypeStruct((D, N), BF16))
    out = pl.pallas_call(
        _rms_matmul_kernel,
        grid=(T // tm, N // tn),
        in_specs=[
            pl.BlockSpec((tm, D), lambda i, j: (i, 0)),
            pl.BlockSpec((None, 1, D), lambda i, j: (layer, 0, 0)),
            _layer_spec(w, layer, (D, tn), col),
        ],
        out_specs=out_specs,
        out_shape=out_shape,
        scratch_shapes=[pltpu.VMEM((tm, D), BF16)],
        compiler_params=_params(("parallel", "arbitrary"), 48),
        name="rms_matmul",
    )(x, g3, w)
    return out if emit else out[0]


def _runs(seq):
    runs = []
    for j, v in enumerate(seq):
        if runs and runs[-1][2] == v - j:
            runs[-1][1] = j + 1
        else:
            runs.append([j, j + 1, v - j])
    return [tuple(r) for r in runs]


def _in_runs(j, runs):
    hit = None
    for lo, hi, _ in runs:
        c = (j >= lo) & (j < hi)
        hit = c if hit is None else hit | c
    return hit


def _lookup(j, runs):
    out = 0
    for lo, hi, off in runs:
        out = out + jnp.where((j >= lo) & (j < hi), j + off, 0)
    return out


def _in_proj_plan(d_model, tn):
    t = lambda blocks: blocks * LANE // tn
    nq, nux, ng = t(HG_HEADS), t(len(POOL_WINDOWS) + XA_HEADS), N_BRANCH * d_model // tn
    src = lambda start, n: list(range(t(start), t(start) + n))
    order = src(COL_Q, nq) + src(COL_I, nq) + src(COL_OG, nq) + src(COL_F, nq) + src(COL_U, nux) + src(COL_G, ng)
    nh, n32 = 3 * nq, nq + nux
    kinds = {
        "plain": [(nq, 2 * nq, 0)],
        "sigmoid": [(nh + n32, len(order), 0)],
        "f32": [(nh, nh + n32, 0)],
    }
    return _runs(order), kinds, nh, n32


def _in_proj_kernel(kinds, x_ref, g_ref, w_ref, zh_ref, zf_ref, zg_ref, *rest):
    xn_ref = rest[-1]
    j = pl.program_id(1)

    @pl.when(j == 0)
    def _():
        xn_ref[...] = _rms(x_ref[...], g_ref[...]).astype(BF16)

    is_sigmoid = _in_runs(j, kinds["sigmoid"])
    is_plain = _in_runs(j, kinds["plain"])
    xn = xn_ref[...]
    tn = w_ref.shape[1]
    cw = min(MXU_COLS, tn)
    zs = []
    for c in range(tn // cw):
        cs = slice(c * cw, (c + 1) * cw)
        w = w_ref[:, cs].astype(BF16)
        if len(rest) == 2:
            rest[0][:, cs] = w
        z = _dot(xn, w)
        zs.append(z)
        s = _sigmoid(z)
        left = jnp.where(is_sigmoid, 1.0, z)
        right = jnp.where(is_plain, 1.0, s)
        act = (left * right).astype(BF16)
        zh_ref[:, cs] = act
        zg_ref[:, cs] = act

    @pl.when(_in_runs(j, kinds["f32"]))
    def _():
        for c, z in enumerate(zs):
            zf_ref[:, c * cw:(c + 1) * cw] = z


def _in_proj(x, g3, w, layer, tm, tn, emit=False):
    T, D = x.shape
    N = w.shape[-1]
    assert not emit or T == tm
    order, kinds, nh, n32 = _in_proj_plan(D, tn)
    nsteps = N // tn
    wcol = lambda i, j: (0, _lookup(j, order))
    out_specs = [
        pl.BlockSpec((tm, tn), lambda i, j: (i, jnp.minimum(j, nh))),
        pl.BlockSpec((tm, tn), lambda i, j: (i, jnp.clip(j - nh, 0, n32 - 1))),
        pl.BlockSpec((tm, tn), lambda i, j: (i, jnp.maximum(j - nh - n32, 0))),
    ]
    out_shape = [
        jax.ShapeDtypeStruct((T, (nh + 1) * tn), BF16),
        jax.ShapeDtypeStruct((T, n32 * tn), F32),
        jax.ShapeDtypeStruct((T, (nsteps - nh - n32) * tn), BF16),
    ]
    if emit:
        out_specs.append(pl.BlockSpec((D, tn), wcol))
        out_shape.append(jax.ShapeDtypeStruct((D, N), BF16))
    return pl.pallas_call(
        functools.partial(_in_proj_kernel, kinds),
        grid=(T // tm, nsteps),
        in_specs=[
            pl.BlockSpec((tm, D), lambda i, j: (i, 0)),
            pl.BlockSpec((None, 1, D), lambda i, j: (layer, 0, 0)),
            _layer_spec(w, layer, (D, tn), wcol),
        ],
        out_specs=out_specs,
        out_shape=out_shape,
        scratch_shapes=[pltpu.VMEM((tm, D), BF16)],
        compiler_params=_params(("parallel", "arbitrary"), 56),
        name="in_proj",
    )(x, g3, w)


def _lower_bound_kernel(lg_ref, o_ref):
    lg = lg_ref[...]
    depth = lg.shape[0]
    rows = [lg[i:i + 1, :] for i in range(depth)]
    m = rows[0]
    for r in rows[1:]:
        m = jnp.maximum(m, r)
    e = [jnp.exp(r - m) for r in rows]
    tot = e[0]
    for v in e[1:]:
        tot = tot + v
    c = e[0] / tot
    first = c
    o_ref[0:1, :] = c - first
    for i in range(1, depth):
        c = c + e[i] / tot
        o_ref[i:i + 1, :] = c - first


def _lower_bounds(lb_logits):
    return pl.pallas_call(
        _lower_bound_kernel,
        out_shape=jax.ShapeDtypeStruct(lb_logits.shape, F32),
        name="hgrn_lower_bounds",
    )(lb_logits)


def _neg_abs(x):
    bits = lax.bitcast_convert_type(x, jnp.uint32) | jnp.uint32(0x80000000)
    return lax.bitcast_convert_type(bits, F32)


def _group_ref_row(b, s):
    rows, width = b.shape
    gsz = 2 * s
    if gsz >= 8:
        parts = [jnp.broadcast_to(b[i * gsz + s - 1:i * gsz + s, :], (gsz, width)) for i in range(rows // gsz)]
        return parts[0] if len(parts) == 1 else jnp.concatenate(parts, axis=0)
    pos = lax.broadcasted_iota(jnp.int32, b.shape, 0) & (gsz - 1)
    out = b
    for p in range(gsz):
        d = p - (s - 1)
        if d != 0:
            out = jnp.where(pos == p, pltpu.roll(b, d % rows, 0), out)
    return out


def _hgrn_block(q, zf, vb, lb, states, seg):
    C = q.shape[0]
    fg = lb + (1.0 - lb) * jax.nn.sigmoid(zf)
    kk = 1.0 - fg
    g = jnp.log2(fg)

    row = lax.broadcasted_iota(jnp.int32, (C, LANE), 0)
    rr = lax.broadcasted_iota(jnp.int32, (C, C), 0)
    cc = lax.broadcasted_iota(jnp.int32, (C, C), 1)
    lseg = seg.bit_length() - 1
    tri = jnp.where((cc <= rr) & ((rr >> lseg) == (cc >> lseg)), 1.0, 0.0).astype(F32)
    b = jnp.dot(tri, g, precision=lax.Precision.HIGHEST, preferred_element_type=F32)

    a = jnp.where(rr == cc, _dot_nt(q.astype(BF16), kk.astype(BF16)), 0.0)
    s = seg // 2
    while s >= 1:
        ls = s.bit_length() - 1
        e = jnp.exp2(_neg_abs(b - _group_ref_row(b, s)))
        same = (rr >> (ls + 1)) == (cc >> (ls + 1))
        if s % SUBLANE == 0:
            nblk = C // s
            zero = jnp.zeros((s, LANE), F32)
            blk = lambda x, i: x[i * s:(i + 1) * s, :]
            qs = jnp.concatenate([blk(q, i) * blk(e, i) if i % 2 else zero for i in range(nblk)], axis=0)
            ks = jnp.concatenate([zero if i % 2 else blk(kk, i) * blk(e, i) for i in range(nblk)], axis=0)
            p = _dot_nt(qs.astype(BF16), ks.astype(BF16))
            pieces = []
            for i in range(nblk):
                if i % 2 == 0:
                    pieces.append(blk(a, i))
                elif 2 * s == C:
                    pieces.append(blk(a, i) + blk(p, i))
                else:
                    pieces.append(blk(a, i) + jnp.where(blk(same, i), blk(p, i), 0.0))
            a = jnp.concatenate(pieces, axis=0)
        else:
            right = ((row >> ls) & 1) == 1
            eq = jnp.where(right, e, 0.0)
            p = _dot_nt((q * eq).astype(BF16), (kk * (e - eq)).astype(BF16))
            a = a + jnp.where(same, p, 0.0)
        s //= 2
    o = _dot(a.astype(BF16), vb)

    qe = q * jnp.exp2(b)
    new_states = []
    for k, st in enumerate(states):
        bl = b[k * seg + seg - 1:k * seg + seg, :]
        if len(states) == 1:
            qk = qe
            ke = kk * jnp.exp2(bl - b)
        else:
            mine = (row >> lseg) == k
            qk = jnp.where(mine, qe, 0.0)
            ke = jnp.where(mine, kk * jnp.exp2(jnp.where(mine, bl - b, 0.0)), 0.0)
        o = o + _dot(qk.astype(BF16), st.astype(BF16))
        decay = jnp.transpose(jnp.broadcast_to(jnp.exp2(bl), (LANE, LANE)))
        new_states.append(st * decay + _dot_tn(ke.astype(BF16), vb))
    return o, new_states


def _hgrn_wide(q, zf, vb, lb, states, seg):
    C, W = q.shape
    heads = [slice(h * LANE, (h + 1) * LANE) for h in range(W // LANE)]
    nseg = C // seg
    fg = lb + (1.0 - lb) * jax.nn.sigmoid(zf)
    kk = 1.0 - fg
    g = jnp.log2(fg)

    row = lax.broadcasted_iota(jnp.int32, (C, W), 0)
    rr = lax.broadcasted_iota(jnp.int32, (C, C), 0)
    cc = lax.broadcasted_iota(jnp.int32, (C, C), 1)
    lseg = seg.bit_length() - 1
    tri = jnp.where((cc <= rr) & ((rr >> lseg) == (cc >> lseg)), 1.0, 0.0).astype(F32)
    b = jnp.dot(tri, g, precision=lax.Precision.HIGHEST, preferred_element_type=F32)

    qb, kb = q.astype(BF16), kk.astype(BF16)
    a = [jnp.where(rr == cc, _dot_nt(qb[:, hs], kb[:, hs]), 0.0) for hs in heads]
    s = seg // 2
    while s >= 1:
        ls = s.bit_length() - 1
        same = (rr >> (ls + 1)) == (cc >> (ls + 1))
        if s == 1:
            right = (row & 1) == 1
            qs, ks = jnp.where(right, q * fg, 0.0).astype(BF16), jnp.where(right, 0.0, kk).astype(BF16)
            for h, hs in enumerate(heads):
                a[h] = a[h] + jnp.where(same, _dot_nt(qs[:, hs], ks[:, hs]), 0.0)
            break
        e = jnp.exp2(_neg_abs(b - _group_ref_row(b, s)))
        if s % SUBLANE == 0:
            nblk = C // s
            zero = jnp.zeros((s, W), F32)
            blk = lambda x, i: x[i * s:(i + 1) * s, :]
            qs = jnp.concatenate([blk(q, i) * blk(e, i) if i % 2 else zero for i in range(nblk)], axis=0).astype(BF16)
            ks = jnp.concatenate([zero if i % 2 else blk(kk, i) * blk(e, i) for i in range(nblk)], axis=0).astype(BF16)
            for h, hs in enumerate(heads):
                p = _dot_nt(qs[:, hs], ks[:, hs])
                pieces = []
                for i in range(nblk):
                    if i % 2 == 0:
                        pieces.append(blk(a[h], i))
                    elif 2 * s == C:
                        pieces.append(blk(a[h], i) + blk(p, i))
                    else:
                        pieces.append(blk(a[h], i) + jnp.where(blk(same, i), blk(p, i), 0.0))
                a[h] = jnp.concatenate(pieces, axis=0)
        else:
            right = ((row >> ls) & 1) == 1
            eq = jnp.where(right, e, 0.0)
            qs, ks = (q * eq).astype(BF16), (kk * (e - eq)).astype(BF16)
            for h, hs in enumerate(heads):
                a[h] = a[h] + jnp.where(same, _dot_nt(qs[:, hs], ks[:, hs]), 0.0)
        s //= 2
    o = [_dot(a[h].astype(BF16), vb[:, hs]) for h, hs in enumerate(heads)]

    qe = q * jnp.exp2(b)
    new_states = [[] for _ in heads]
    for k in range(nseg):
        bl = b[k * seg + seg - 1:k * seg + seg, :]
        if nseg == 1:
            qk = qe
            ke = kk * jnp.exp2(bl - b)
        else:
            mine = (row >> lseg) == k
            qk = jnp.where(mine, qe, 0.0)
            ke = jnp.where(mine, kk * jnp.exp2(jnp.where(mine, bl - b, 0.0)), 0.0)
        qkb, keb, ebl = qk.astype(BF16), ke.astype(BF16), jnp.exp2(bl)
        for h, hs in enumerate(heads):
            st = states[h][k]
            o[h] = o[h] + _dot(qkb[:, hs], st.astype(BF16))
            decay = jnp.transpose(jnp.broadcast_to(ebl[:, hs], (LANE, LANE)))
            new_states[h].append(st * decay + _dot_tn(keb[:, hs], vb[:, hs]))
    return o, new_states


def _hgrn_finish(o, og, hgn):
    return (_rms(o, hgn) * og.astype(F32)).astype(BF16)


def _hgrn_prompt_kernel(q_ref, zf_ref, v_ref, og_ref, lb_ref, hgn_ref, oh_ref, s_ref):
    @pl.when(pl.program_id(1) == 0)
    def _():
        s_ref[...] = jnp.zeros_like(s_ref)

    states = [[s_ref[h]] for h in range(HG_HEADS)]
    o, new = _hgrn_wide(q_ref[...].astype(F32), zf_ref[...], v_ref[...], lb_ref[...], states, HGRN_CHUNK)
    for h in range(HG_HEADS):
        cs = slice(h * LANE, (h + 1) * LANE)
        s_ref[h] = new[h][0]
        oh_ref[:, cs] = _hgrn_finish(o[h], og_ref[:, cs], hgn_ref[:, cs])


def _hgrn_prompt(zb, zf, lbs3, hgn3, layer, batch):
    T = zb.shape[0] // batch
    C = HGRN_CHUNK
    nc = T // C
    W = HG_HEADS * LANE
    zspec = lambda col: pl.BlockSpec((C, W), lambda b, c: (b * nc + c, col))
    vec = pl.BlockSpec((None, 1, W), lambda b, c: (layer, 0, 0))
    return pl.pallas_call(
        _hgrn_prompt_kernel,
        grid=(batch, nc),
        in_specs=[zspec(ZB_Q), zspec(ZF_F), zspec(ZB_V), zspec(ZB_OG), vec, vec],
        out_specs=[
            pl.BlockSpec((C, W), lambda b, c: (b * nc + c, 0)),
            pl.BlockSpec((None, HG_HEADS, LANE, LANE), lambda b, c: (b, 0, 0, 0)),
        ],
        out_shape=[
            jax.ShapeDtypeStruct((batch * T, W), BF16),
            jax.ShapeDtypeStruct((batch, HG_HEADS, LANE, LANE), F32),
        ],
        compiler_params=_params(("parallel", "arbitrary"), 32),
        name="hgrn_prompt",
    )(zb, zf, zb, zb, lbs3, hgn3)


def _hgrn_decode_kernel(seq_len, q_ref, zf_ref, v_ref, og_ref, lb_ref, hgn_ref, s_ref, *rest):
    oh_ref, so_ref = rest[-2], rest[-1]
    rows = DEC_GROUP * seq_len
    for grp in range(s_ref.shape[0] // DEC_GROUP):
        rs = slice(grp * rows, (grp + 1) * rows)
        states = [[s_ref[grp * DEC_GROUP + k, h] for k in range(DEC_GROUP)] for h in range(HG_HEADS)]
        o, new = _hgrn_wide(q_ref[rs, :].astype(F32), zf_ref[rs, :], v_ref[rs, :], lb_ref[...], states, seq_len)
        for h in range(HG_HEADS):
            cs = slice(h * LANE, (h + 1) * LANE)
            for k in range(DEC_GROUP):
                so_ref[grp * DEC_GROUP + k, h] = new[h][k]
            oh_ref[rs, cs] = _hgrn_finish(o[h], og_ref[rs, cs], hgn_ref[:, cs])


def _hgrn_decode(zb, zf, lbs3, hgn3, state, stacked, layer, nseq, nb):
    seq_len = zb.shape[0] // nseq
    W = HG_HEADS * LANE
    zspec = lambda col: pl.BlockSpec((nb * seq_len, W), lambda i: (i, col))
    vec = pl.BlockSpec((None, 1, W), lambda i: (layer, 0, 0))
    sspec = pl.BlockSpec((None, nb, HG_HEADS, LANE, LANE), lambda i: (layer, i, 0, 0, 0))
    in_specs = [zspec(ZB_Q), zspec(ZF_F), zspec(ZB_V), zspec(ZB_OG), vec, vec, sspec]
    args = [zb, zf, zb, zb, lbs3, hgn3, state]
    aliases = {}
    if stacked is not None:
        in_specs.append(pl.BlockSpec(memory_space=pl.ANY))
        args.append(stacked)
        aliases = {len(args) - 1: 1}
    return pl.pallas_call(
        functools.partial(_hgrn_decode_kernel, seq_len),
        grid=(nseq // nb,),
        in_specs=in_specs,
        out_specs=[pl.BlockSpec((nb * seq_len, W), lambda i: (i, 0)), sspec],
        out_shape=[
            jax.ShapeDtypeStruct((nseq * seq_len, W), BF16),
            jax.ShapeDtypeStruct(state.shape, F32),
        ],
        input_output_aliases=aliases,
        compiler_params=_params(("parallel",), 48),
        name="hgrn_decode",
    )(*args)


def _pool_prompt_kernel(u_ref, wp_ref, sc_ref, op_ref, nb_ref):
    u = u_ref[...]
    T = u.shape[0]
    row = lax.broadcasted_iota(jnp.int32, (T, LANE), 0)
    for g, w in enumerate(POOL_WINDOWS):
        cs = slice(g * LANE, (g + 1) * LANE)
        ug = u[:, cs]
        s = ug
        d = 1
        while d < w:
            s = s + jnp.where(row >= d, pltpu.roll(s, d, 0), 0.0)
            d *= 2
        cnt = jnp.minimum(row + 1, w).astype(F32)
        dv = s / cnt - ug
        y = _dot(dv.astype(BF16), wp_ref[g].astype(BF16)) * sc_ref[:, cs]
        op_ref[:, cs] = y.astype(BF16)
    tail = u_ref[T - 16:T, :]
    nb_ref[...] = pltpu.roll(tail, 15, 0)[0:POOL_BUF, :]


def _pool_prompt(zf, w_pool, scale3, layer, batch):
    T = zf.shape[0] // batch
    G = len(POOL_WINDOWS)
    W = G * LANE
    return pl.pallas_call(
        _pool_prompt_kernel,
        grid=(batch,),
        in_specs=[
            pl.BlockSpec((T, W), lambda b: (b, ZF_U)),
            pl.BlockSpec((None, G, LANE, LANE), lambda b: (layer, 0, 0, 0)),
            pl.BlockSpec((None, 1, W), lambda b: (layer, 0, 0)),
        ],
        out_specs=[
            pl.BlockSpec((T, W), lambda b: (b, 0)),
            pl.BlockSpec((None, POOL_BUF, W), lambda b: (b, 0, 0)),
        ],
        out_shape=[
            jax.ShapeDtypeStruct((batch * T, W), BF16),
            jax.ShapeDtypeStruct((batch, POOL_BUF, W), F32),
        ],
        compiler_params=_params(("parallel",), 48),
        name="pool_prompt",
    )(zf, w_pool, scale3)


def _pool_decode_kernel(u_ref, buf_ref, wp_ref, sc_ref, op_ref, nb_ref):
    steps = u_ref.shape[0]
    for g, w in enumerate(POOL_WINDOWS):
        cs = slice(g * LANE, (g + 1) * LANE)
        wg = wp_ref[g].astype(BF16)
        for t in range(steps):
            n_u = min(t + 1, w)
            acc = u_ref[t, :, cs]
            for j in range(t - n_u + 1, t):
                acc = acc + u_ref[j, :, cs]
            for i in range(POOL_BUF - (w - n_u), POOL_BUF):
                acc = acc + buf_ref[i, :, cs]
            dv = acc * (1.0 / w) - u_ref[t, :, cs]
            op_ref[t, :, cs] = _dot(dv.astype(BF16), wg) * sc_ref[:, cs]
    for i in range(POOL_BUF - steps):
        nb_ref[i] = buf_ref[i + steps]
    for t in range(steps):
        nb_ref[POOL_BUF - steps + t] = u_ref[t]


def _pool_decode(u_t, buf_t, w_pool, scale3, layer):
    steps, nseq, W = u_t.shape
    G = len(POOL_WINDOWS)
    return pl.pallas_call(
        _pool_decode_kernel,
        grid=(1,),
        in_specs=[
            pl.BlockSpec((steps, nseq, W), lambda i: (0, 0, 0)),
            pl.BlockSpec((POOL_BUF, nseq, W), lambda i: (0, 0, 0)),
            pl.BlockSpec((None, G, LANE, LANE), lambda i: (layer, 0, 0, 0)),
            pl.BlockSpec((None, 1, W), lambda i: (layer, 0, 0)),
        ],
        out_specs=[
            pl.BlockSpec((steps, nseq, W), lambda i: (0, 0, 0)),
            pl.BlockSpec((POOL_BUF, nseq, W), lambda i: (0, 0, 0)),
        ],
        out_shape=[
            jax.ShapeDtypeStruct((steps, nseq, W), F32),
            jax.ShapeDtypeStruct((POOL_BUF, nseq, W), F32),
        ],
        compiler_params=_params(("arbitrary",), 32),
        name="pool_decode",
    )(u_t, buf_t, w_pool, scale3)


def _softmax_rows(s):
    e = jnp.exp(s - jnp.max(s, axis=-1, keepdims=True))
    return e / jnp.sum(e, axis=-1, keepdims=True)


def _xattn_prompt_kernel(q_ref, k_ref, v_ref, o_ref):
    scale = LANE ** -0.5
    for h in range(XA_HEADS):
        cs = slice(h * LANE, (h + 1) * LANE)
        s = _dot_nt(q_ref[:, cs].astype(BF16), k_ref[:, cs].astype(BF16)) * scale
        p = _softmax_rows(s)
        o_ref[:, cs] = _dot(p.astype(BF16), v_ref[:, cs].astype(BF16)).astype(BF16)


def _xattn_prompt(zf, mk, mv, batch, tq):
    T = zf.shape[0] // batch
    W = XA_HEADS * LANE
    nq = T // tq
    n_mem = mk.shape[1]
    mspec = pl.BlockSpec((None, n_mem, W), lambda b, i: (b, 0, 0))
    return pl.pallas_call(
        _xattn_prompt_kernel,
        grid=(batch, nq),
        in_specs=[pl.BlockSpec((tq, W), lambda b, i: (b * nq + i, ZF_X)), mspec, mspec],
        out_specs=pl.BlockSpec((tq, W), lambda b, i: (b * nq + i, 0)),
        out_shape=jax.ShapeDtypeStruct((batch * T, W), BF16),
        compiler_params=_params(("parallel", "parallel"), 48),
        name="xattn_prompt",
    )(zf, mk, mv)


def _xattn_decode_kernel(seq_len, q_ref, k_ref, v_ref, o_ref):
    scale = LANE ** -0.5
    rows = DEC_GROUP * seq_len
    lseq = seq_len.bit_length() - 1
    lrows = rows.bit_length() - 1
    nk = k_ref.shape[1]
    row_s = lax.broadcasted_iota(jnp.int32, (XA_HEADS * rows, nk), 0)
    col_s = lax.broadcasted_iota(jnp.int32, (XA_HEADS * rows, nk), 1)
    own_head = (col_s & (XA_HEADS - 1)) == (row_s >> lrows)
    seq_s = (row_s & (rows - 1)) >> lseq
    seq_o = (lax.broadcasted_iota(jnp.int32, (XA_HEADS * rows, LANE), 0) & (rows - 1)) >> lseq
    for grp in range(k_ref.shape[0] // DEC_GROUP):
        rs = slice(grp * rows, (grp + 1) * rows)
        qb = jnp.concatenate([q_ref[rs, h * LANE:(h + 1) * LANE] for h in range(XA_HEADS)], axis=0).astype(BF16)
        s = jnp.full((XA_HEADS * rows, nk), -jnp.inf, F32)
        for k in range(DEC_GROUP):
            sk = _dot_nt(qb, k_ref[grp * DEC_GROUP + k].astype(BF16))
            s = jnp.where(own_head & (seq_s == k), sk, s)
        p = _softmax_rows(s * scale).astype(BF16)
        o = jnp.zeros((XA_HEADS * rows, LANE), F32)
        for k in range(DEC_GROUP):
            o = jnp.where(seq_o == k, _dot(p, v_ref[grp * DEC_GROUP + k].astype(BF16)), o)
        for h in range(XA_HEADS):
            o_ref[rs, h * LANE:(h + 1) * LANE] = o[h * rows:(h + 1) * rows, :].astype(BF16)


def _xattn_decode(zf, cache_k, cache_v, layer, nseq, nb):
    seq_len = zf.shape[0] // nseq
    W = XA_HEADS * LANE
    cspec = pl.BlockSpec((None, nb) + cache_k.shape[2:], lambda i: (layer, i, 0, 0))
    return pl.pallas_call(
        functools.partial(_xattn_decode_kernel, seq_len),
        grid=(nseq // nb,),
        in_specs=[pl.BlockSpec((nb * seq_len, W), lambda i: (i, ZF_X)), cspec, cspec],
        out_specs=pl.BlockSpec((nb * seq_len, W), lambda i: (i, 0)),
        out_shape=jax.ShapeDtypeStruct((nseq * seq_len, W), BF16),
        compiler_params=_params(("parallel",), 48),
        name="xattn_decode",
    )(zf, cache_k, cache_v)


def _merge_kernel(x_ref, oh_ref, op_ref, ox_ref, g0_ref, g1_ref, g2_ref, wb_ref, wo_ref, o_ref, *rest):
    y_ref = rest[-1]
    wbb_ref, wob_ref = rest[:2] if len(rest) == 3 else (None, None)
    j = pl.program_id(1)
    nj, _, tn = y_ref.shape

    @pl.when(j < nj)
    def _():
        wh = oh_ref.shape[1]
        wp = op_ref.shape[1]
        oh, op, ox = oh_ref[...].astype(BF16), op_ref[...].astype(BF16), ox_ref[...].astype(BF16)
        cw = min(MXU_COLS, tn)
        for c in range(tn // cw):
            cs = slice(c * cw, (c + 1) * cw)
            w = wb_ref[:, cs].astype(BF16)
            if wbb_ref is not None:
                wbb_ref[:, cs] = w
            y = g0_ref[:, cs].astype(F32) * _dot(oh, w[0:wh, :])
            y += g1_ref[:, cs].astype(F32) * _dot(op, w[wh:wh + wp, :])
            y += g2_ref[:, cs].astype(F32) * _dot(ox, w[wh + wp:, :])
            y_ref[j, :, cs] = y.astype(BF16)

    @pl.when(j >= nj)
    def _():
        acc = x_ref[...]
        for c in range(nj):
            rs = slice(c * tn, (c + 1) * tn)
            w = wo_ref[rs, :].astype(BF16)
            if wob_ref is not None:
                wob_ref[rs, :] = w
            acc += _dot(y_ref[c], w)
        o_ref[...] = acc


def _merge_resident_kernel(x_ref, oh_ref, op_ref, ox_ref, g_ref, wb_ref, wo_ref, o_ref, y_ref):
    D = x_ref.shape[1]
    wh = oh_ref.shape[1]
    wp = op_ref.shape[1]
    oh, op, ox = oh_ref[...].astype(BF16), op_ref[...].astype(BF16), ox_ref[...].astype(BF16)
    cw = min(2 * MXU_COLS, D)
    for c in range(D // cw):
        cs = slice(c * cw, (c + 1) * cw)
        y = g_ref[:, cs].astype(F32) * _dot(oh, wb_ref[0:wh, cs])
        y += g_ref[:, D + c * cw:D + (c + 1) * cw].astype(F32) * _dot(op, wb_ref[wh:wh + wp, cs])
        y += g_ref[:, 2 * D + c * cw:2 * D + (c + 1) * cw].astype(F32) * _dot(ox, wb_ref[wh + wp:, cs])
        y_ref[:, cs] = y.astype(BF16)
    for c in range(D // cw):
        cs = slice(c * cw, (c + 1) * cw)
        o_ref[:, cs] = x_ref[:, cs] + _dot(y_ref[...], wo_ref[:, cs])


def _merge_resident(x, oh, op, ox, zg, wb, wo, tm):
    T, D = x.shape
    row = lambda a: pl.BlockSpec((tm, a.shape[1]), lambda i: (i, 0))
    held = lambda a: pl.BlockSpec(a.shape, lambda i: (0, 0), pipeline_mode=pl.Buffered(1))
    return pl.pallas_call(
        _merge_resident_kernel,
        grid=(T // tm,),
        in_specs=[row(x), row(oh), row(op), row(ox), row(zg), held(wb), held(wo)],
        out_specs=row(x),
        out_shape=jax.ShapeDtypeStruct((T, D), F32),
        scratch_shapes=[pltpu.VMEM((tm, D), BF16)],
        compiler_params=_params(("parallel",), 56),
        name="merge_resident",
    )(x, oh, op, ox, zg, wb, wo)


def _merge(x, oh, op, ox, zb, wb, wo, layer, tm, tn, emit=False):
    T, D = x.shape
    assert not emit or T == tm
    nj = D // tn
    first = lambda j: jnp.minimum(j, nj - 1)
    second = lambda j: jnp.maximum(j - nj, 0)
    full = lambda a: pl.BlockSpec((tm, a.shape[1]), lambda i, j: (i, 0))
    gate = lambda k: pl.BlockSpec((tm, tn), lambda i, j: (i, k * nj + first(j)))
    wb_idx = lambda i, j: (0, first(j))
    wo_idx = lambda i, j: (0, second(j))
    rows_b = wb.shape[-2]
    out_specs = [pl.BlockSpec((tm, tn), lambda i, j: (i, second(j)))]
    out_shape = [jax.ShapeDtypeStruct((T, D), F32)]
    if emit:
        out_specs += [pl.BlockSpec((rows_b, tn), wb_idx), pl.BlockSpec((D, tn), wo_idx)]
        out_shape += [jax.ShapeDtypeStruct((rows_b, D), BF16), jax.ShapeDtypeStruct((D, D), BF16)]
    out = pl.pallas_call(
        _merge_kernel,
        grid=(T // tm, 2 * nj),
        in_specs=[
            pl.BlockSpec((tm, tn), lambda i, j: (i, second(j))),
            full(oh), full(op), full(ox), gate(0), gate(1), gate(2),
            _layer_spec(wb, layer, (rows_b, tn), wb_idx),
            _layer_spec(wo, layer, (D, tn), wo_idx),
        ],
        out_specs=out_specs,
        out_shape=out_shape,
        scratch_shapes=[pltpu.VMEM((nj, tm, tn), BF16)],
        compiler_params=_params(("parallel", "arbitrary"), 48),
        name="merge_out",
    )(x, oh, op, ox, zb, zb, zb, wb, wo)
    return out if emit else out[0]


def _norm_kernel(x_ref, g_ref, o_ref):
    o_ref[...] = _rms(x_ref[...], g_ref[...])


def _final_norm(x, g2, tm):
    T, D = x.shape
    return pl.pallas_call(
        _norm_kernel,
        grid=(T // tm,),
        in_specs=[pl.BlockSpec((tm, D), lambda i: (i, 0)), pl.BlockSpec((1, D), lambda i: (0, 0))],
        out_specs=pl.BlockSpec((tm, D), lambda i: (i, 0)),
        out_shape=jax.ShapeDtypeStruct((T, D), F32),
        compiler_params=_params(("parallel",), 40),
        name="final_norm",
    )(x, g2)


def kernel(x_prompt, x_sample, state_hgrn, state_pool, cache_mem_k, cache_mem_v, mem_prompt, ffn1_norm, ffn1_w1, ffn1_w3, ffn1_w2, mix_norm, w_in, lb_logits, hg_norm, w_pool, pool_scale, mem_norm, w_mk, w_mv, w_branch, w_out, ffn2_norm, ffn2_w1, ffn2_w3, ffn2_w2, final_norm):
    B, T, D = x_prompt.shape
    nseq, steps, _ = x_sample.shape
    depth = w_in.shape[0]
    n_mem = mem_prompt.shape[1]
    xw = XA_HEADS * LANE
    pw = len(POOL_WINDOWS) * LANE

    tm_p = min(1024, B * T)
    tm_s = min(512, nseq * steps)

    vec3 = lambda a: a.reshape(a.shape[0], 1, a.shape[1])
    f1n, f2n = vec3(ffn1_norm), vec3(ffn2_norm)
    mix3, mem3, hgn3, psc3 = vec3(mix_norm), vec3(mem_norm), vec3(hg_norm), vec3(pool_scale)
    lbs3 = vec3(_lower_bounds(lb_logits))

    xp = x_prompt.reshape(B * T, D)
    xs = x_sample.reshape(nseq * steps, D)
    mem2 = mem_prompt.reshape(B * n_mem, D)
    ck = cache_mem_k.reshape(depth, nseq, n_mem * XA_HEADS, LANE)
    cv = cache_mem_v.reshape(depth, nseq, n_mem * XA_HEADS, LANE)

    tf = _tile(ffn1_w1.shape[-1], 512)
    tf_s = _tile(ffn1_w1.shape[-1], 256)
    tn_in = _tile(math.gcd(HG_HEADS * LANE, N_BRANCH * D), 1024)
    tn_mrg = _tile(D, 512)
    tm_mem = min(512, B * n_mem)
    hs_p, pb_p, mk_p, mv_p, pb_s = [], [], [], [], []
    hs_s = None
    for l in range(depth):
        mk = _rms_matmul(mem2, mem3, w_mk, l, tm_mem, xw)
        mv = _rms_matmul(mem2, mem3, w_mv, l, tm_mem, xw)
        mk_p.append(mk.reshape(B, n_mem, XA_HEADS, LANE))
        mv_p.append(mv.reshape(B, n_mem, XA_HEADS, LANE))

        xs, w1b, w3b, w2b = _ffn(xs, f1n, ffn1_w1, ffn1_w3, ffn1_w2, l, tm_s, tf_s)
        xp = _ffn(xp, f1n, w1b, w3b, w2b, l, tm_p, tf)

        zsb, zsf, zsg, w_in_b = _in_proj(xs, mix3, w_in, l, tm_s, tn_in, emit=True)
        zpb, zpf, zpg = _in_proj(xp, mix3, w_in_b, l, tm_p, tn_in)

        ohp, sp = _hgrn_prompt(zpb, zpf, lbs3, hgn3, l, B)
        ohs, hs_s = _hgrn_decode(zsb, zsf, lbs3, hgn3, state_hgrn, hs_s, l, nseq, 8)
        hs_p.append(sp)

        opp, bp = _pool_prompt(zpf, w_pool, psc3, l, B)
        pb_p.append(bp)
        u_t = zsf[:, ZF_U * pw:(ZF_U + 1) * pw].reshape(nseq, steps, pw).transpose(1, 0, 2)
        buf_t = state_pool[l].transpose(1, 0, 2)
        ops_t, nb_t = _pool_decode(u_t, buf_t, w_pool, psc3, l)
        ops = ops_t.transpose(1, 0, 2).reshape(nseq * steps, pw)
        pb_s.append(nb_t.transpose(1, 0, 2))

        oxp = _xattn_prompt(zpf, mk.reshape(B, n_mem, xw), mv.reshape(B, n_mem, xw), B, min(1024, T))
        oxs = _xattn_decode(zsf, ck, cv, l, nseq, 8)

        xs, w_br_b, w_out_b = _merge(xs, ohs, ops, oxs, zsg, w_branch, w_out, l, tm_s, tn_mrg, emit=True)
        xp = _merge_resident(xp, ohp, opp, oxp, zpg, w_br_b, w_out_b, min(512, B * T))

        fn2 = final_norm.reshape(1, D) if l == depth - 1 else None
        xs, w1b, w3b, w2b = _ffn(xs, f2n, ffn2_w1, ffn2_w3, ffn2_w2, l, tm_s, tf_s, fn2)
        xp = _ffn(xp, f2n, w1b, w3b, w2b, l, tm_p, tf, fn2)

    y_prompt = xp.reshape(B, T, D)
    y_sample = xs.reshape(nseq, steps, D)
    return (y_prompt, y_sample, jnp.stack(hs_p), jnp.stack(pb_p), jnp.stack(mk_p), jnp.stack(mv_p),
            hs_s, jnp.stack(pb_s))
```

```python
import functools
import math

import jax
import jax.numpy as jnp
from jax import lax
from jax.experimental import pallas as pl
from jax.experimental.pallas import tpu as pltpu

F32 = jnp.float32
BF16 = jnp.bfloat16
EPS = 1e-6

LANE = 128
SUBLANE = 8
MXU_COLS = 256
HG_HEADS = 8
POOL_WINDOWS = (2, 4, 8, 16)
POOL_BUF = max(POOL_WINDOWS) - 1
XA_HEADS = 4
N_BRANCH = 3

COL_Q, COL_F, COL_I, COL_OG = 0, HG_HEADS, 2 * HG_HEADS, 3 * HG_HEADS
COL_U = 4 * HG_HEADS
COL_X = COL_U + len(POOL_WINDOWS)
COL_G = COL_X + XA_HEADS
ZB_Q, ZB_V, ZB_OG, ZB_GATES = 0, 1, 2, 3
ZF_F = 0
ZF_U, ZF_X = 2, 3

HGRN_CHUNK = 128
HGRN_STEP_CHUNKS = 2
DEC_GROUP = 4


def _tile(n, preferred):
    t = preferred
    while n % t:
        t -= LANE
    return t


def _params(semantics, vmem_mib):
    return pltpu.CompilerParams(dimension_semantics=semantics, vmem_limit_bytes=vmem_mib * 1024 * 1024)


def _rms(x, g):
    return x * lax.rsqrt(jnp.mean(x * x, axis=-1, keepdims=True) + EPS) * g


def _sigmoid(x):
    return 0.5 * jnp.tanh(0.5 * x) + 0.5


def _silu(x):
    return x * _sigmoid(x)


def _dot(a, b):
    return jnp.dot(a, b, preferred_element_type=F32)


def _dot_nt(a, b):
    return lax.dot_general(a, b, (((1,), (1,)), ((), ())), preferred_element_type=F32)


def _dot_tn(a, b):
    return lax.dot_general(a, b, (((0,), (0,)), ((), ())), preferred_element_type=F32)


def _ffn_kernel(final, cast, x_ref, g_ref, w1_ref, w3_ref, w2_ref, *rest):
    fn_ref = rest[0] if final else None
    o_ref, xn_ref = rest[int(final)], rest[-1]
    if cast:
        src = (w1_ref, w3_ref, w2_ref)
        w1_ref, w3_ref, w2_ref = rest[int(final) + 1:int(final) + 4]
        for dst, s in zip((w1_ref, w3_ref, w2_ref), src):
            dst[...] = s[...].astype(BF16)
    j = pl.program_id(1)

    @pl.when(j == 0)
    def _():
        x = x_ref[...]
        xn_ref[...] = _rms(x, g_ref[...]).astype(BF16)
        o_ref[...] = x

    xn = xn_ref[...]
    h = (_silu(_dot(xn, w1_ref[...])) * _dot(xn, w3_ref[...])).astype(BF16)
    cw = min(w2_ref.shape[0], o_ref.shape[1])
    for n in range(o_ref.shape[1] // cw):
        cs = slice(n * cw, (n + 1) * cw)
        o_ref[:, cs] += 0.5 * _dot(h, w2_ref[:, cs])

    if final:
        @pl.when(j == pl.num_programs(1) - 1)
        def _():
            o_ref[...] = _rms(o_ref[...], fn_ref[...])


def _layer_spec(w, layer, block, index):
    if w.ndim == 2:
        return pl.BlockSpec(block, index)
    return pl.BlockSpec((None,) + block, lambda i, j: (layer,) + index(i, j))


def _ffn(x, g3, w1, w3, w2, layer, tm, tf, final_g=None):
    T, D = x.shape
    F = w1.shape[-1]
    cast = w1.dtype != BF16
    final = final_g is not None
    assert not cast or T == tm
    col = lambda i, j: (0, j)
    row = lambda i, j: (j, 0)
    out_specs = [pl.BlockSpec((tm, D), lambda i, j: (i, 0))]
    out_shape = [jax.ShapeDtypeStruct((T, D), F32)]
    if cast:
        out_specs += [pl.BlockSpec((D, tf), col), pl.BlockSpec((D, tf), col), pl.BlockSpec((tf, D), row)]
        out_shape += [jax.ShapeDtypeStruct((D, F), BF16), jax.ShapeDtypeStruct((D, F), BF16),
                      jax.ShapeDtypeStruct((F, D), BF16)]
    in_specs = [
        pl.BlockSpec((tm, D), lambda i, j: (i, 0)),
        pl.BlockSpec((None, 1, D), lambda i, j: (layer, 0, 0)),
        _layer_spec(w1, layer, (D, tf), col),
        _layer_spec(w3, layer, (D, tf), col),
        _layer_spec(w2, layer, (tf, D), row),
    ]
    args = [x, g3, w1, w3, w2]
    if final:
        in_specs.append(pl.BlockSpec((1, D), lambda i, j: (0, 0)))
        args.append(final_g)
    out = pl.pallas_call(
        functools.partial(_ffn_kernel, final, cast),
        grid=(T // tm, F // tf),
        in_specs=in_specs,
        out_specs=out_specs,
        out_shape=out_shape,
        scratch_shapes=[pltpu.VMEM((tm, D), BF16)],
        compiler_params=_params(("parallel", "arbitrary"), 58),
        name="ffn_cast" if cast else "ffn",
    )(*args)
    return out if cast else out[0]


def _rms_matmul_kernel(x_ref, g_ref, w_ref, o_ref, *rest):
    xn_ref = rest[-1]

    @pl.when(pl.program_id(1) == 0)
    def _():
        xn_ref[...] = _rms(x_ref[...], g_ref[...]).astype(BF16)

    w = w_ref[...].astype(BF16)
    if len(rest) == 2:
        rest[0][...] = w
    o_ref[...] = _dot(xn_ref[...], w)


def _rms_matmul(x, g3, w, layer, tm, tn, emit=False):
    T, D = x.shape
    N = w.shape[-1]
    assert not emit or T == tm
    col = lambda i, j: (0, j)
    out_specs = [pl.BlockSpec((tm, tn), lambda i, j: (i, j))]
    out_shape = [jax.ShapeDtypeStruct((T, N), F32)]
    if emit:
        out_specs.append(pl.BlockSpec((D, tn), col))
        out_shape.append(jax.ShapeDtypeStruct((D, N), BF16))
    out = pl.pallas_call(
        _rms_matmul_kernel,
        grid=(T // tm, N // tn),
        in_specs=[
            pl.BlockSpec((tm, D), lambda i, j: (i, 0)),
            pl.BlockSpec((None, 1, D), lambda i, j: (layer, 0, 0)),
            _layer_spec(w, layer, (D, tn), col),
        ],
        out_specs=out_specs,
        out_shape=out_shape,
        scratch_shapes=[pltpu.VMEM((tm, D), BF16)],
        compiler_params=_params(("parallel", "arbitrary"), 48),
        name="rms_matmul",
    )(x, g3, w)
    return out if emit else out[0]


def _runs(seq):
    runs = []
    for j, v in enumerate(seq):
        if runs and runs[-1][2] == v - j:
            runs[-1][1] = j + 1
        else:
            runs.append([j, j + 1, v - j])
    return [tuple(r) for r in runs]


def _in_runs(j, runs):
    hit = None
    for lo, hi, _ in runs:
        c = (j >= lo) & (j < hi)
        hit = c if hit is None else hit | c
    return hit


def _lookup(j, runs):
    out = 0
    for lo, hi, off in runs:
        out = out + jnp.where((j >= lo) & (j < hi), j + off, 0)
    return out


def _in_proj_plan(d_model, tn):
    t = lambda blocks: blocks * LANE // tn
    nq, nux, ng = t(HG_HEADS), t(len(POOL_WINDOWS) + XA_HEADS), N_BRANCH * d_model // tn
    src = lambda start, n: list(range(t(start), t(start) + n))
    order = src(COL_Q, nq) + src(COL_I, nq) + src(COL_OG, nq) + src(COL_F, nq) + src(COL_U, nux) + src(COL_G, ng)
    nh, n32 = 3 * nq, nq + nux
    kinds = {"f32": [(nh, nh + n32, 0)]}
    return _runs(order), kinds, nh, n32


def _in_proj_kernel(kinds, x_ref, g_ref, w_ref, zh_ref, zf_ref, zg_ref, *rest):
    xn_ref = rest[-1]
    j = pl.program_id(1)

    @pl.when(j == 0)
    def _():
        xn_ref[...] = _rms(x_ref[...], g_ref[...]).astype(BF16)

    xn = xn_ref[...]
    tn = w_ref.shape[1]
    cw = min(MXU_COLS, tn)
    zs = []
    for c in range(tn // cw):
        cs = slice(c * cw, (c + 1) * cw)
        w = w_ref[:, cs].astype(BF16)
        if len(rest) == 2:
            rest[0][:, cs] = w
        z = _dot(xn, w)
        zs.append(z)
        zh_ref[:, cs] = z.astype(BF16)
        zg_ref[:, cs] = z.astype(BF16)

    @pl.when(_in_runs(j, kinds["f32"]))
    def _():
        for c, z in enumerate(zs):
            zf_ref[:, c * cw:(c + 1) * cw] = z


def _in_proj(x, g3, w, layer, tm, tn, emit=False):
    T, D = x.shape
    N = w.shape[-1]
    assert not emit or T == tm
    order, kinds, nh, n32 = _in_proj_plan(D, tn)
    nsteps = N // tn
    wcol = lambda i, j: (0, _lookup(j, order))
    out_specs = [
        pl.BlockSpec((tm, tn), lambda i, j: (i, jnp.minimum(j, nh))),
        pl.BlockSpec((tm, tn), lambda i, j: (i, jnp.clip(j - nh, 0, n32 - 1))),
        pl.BlockSpec((tm, tn), lambda i, j: (i, jnp.maximum(j - nh - n32, 0))),
    ]
    out_shape = [
        jax.ShapeDtypeStruct((T, (nh + 1) * tn), BF16),
        jax.ShapeDtypeStruct((T, n32 * tn), F32),
        jax.ShapeDtypeStruct((T, (nsteps - nh - n32) * tn), BF16),
    ]
    if emit:
        out_specs.append(pl.BlockSpec((D, tn), wcol))
        out_shape.append(jax.ShapeDtypeStruct((D, N), BF16))
    return pl.pallas_call(
        functools.partial(_in_proj_kernel, kinds),
        grid=(T // tm, nsteps),
        in_specs=[
            pl.BlockSpec((tm, D), lambda i, j: (i, 0)),
            pl.BlockSpec((None, 1, D), lambda i, j: (layer, 0, 0)),
            _layer_spec(w, layer, (D, tn), wcol),
        ],
        out_specs=out_specs,
        out_shape=out_shape,
        scratch_shapes=[pltpu.VMEM((tm, D), BF16)],
        compiler_params=_params(("parallel", "arbitrary"), 56),
        name="in_proj",
    )(x, g3, w)


def _lower_bound_kernel(lg_ref, o_ref):
    lg = lg_ref[...]
    depth = lg.shape[0]
    rows = [lg[i:i + 1, :] for i in range(depth)]
    m = rows[0]
    for r in rows[1:]:
        m = jnp.maximum(m, r)
    e = [jnp.exp(r - m) for r in rows]
    tot = e[0]
    for v in e[1:]:
        tot = tot + v
    c = e[0] / tot
    first = c
    o_ref[0:1, :] = c - first
    for i in range(1, depth):
        c = c + e[i] / tot
        o_ref[i:i + 1, :] = c - first


def _lower_bounds(lb_logits):
    return pl.pallas_call(
        _lower_bound_kernel,
        out_shape=jax.ShapeDtypeStruct(lb_logits.shape, F32),
        name="hgrn_lower_bounds",
    )(lb_logits)


def _neg_abs(x):
    bits = lax.bitcast_convert_type(x, jnp.uint32) | jnp.uint32(0x80000000)
    return lax.bitcast_convert_type(bits, F32)


def _group_ref_row(b, s):
    rows, width = b.shape
    gsz = 2 * s
    if gsz >= 8:
        parts = [jnp.broadcast_to(b[i * gsz + s - 1:i * gsz + s, :], (gsz, width)) for i in range(rows // gsz)]
        return parts[0] if len(parts) == 1 else jnp.concatenate(parts, axis=0)
    pos = lax.broadcasted_iota(jnp.int32, b.shape, 0) & (gsz - 1)
    out = b
    for p in range(gsz):
        d = p - (s - 1)
        if d != 0:
            out = jnp.where(pos == p, pltpu.roll(b, d % rows, 0), out)
    return out


def _hgrn_block(q, zf, vb, lb, states, seg):
    C = q.shape[0]
    fg = lb + (1.0 - lb) * jax.nn.sigmoid(zf)
    kk = 1.0 - fg
    g = jnp.log2(fg)

    row = lax.broadcasted_iota(jnp.int32, (C, LANE), 0)
    rr = lax.broadcasted_iota(jnp.int32, (C, C), 0)
    cc = lax.broadcasted_iota(jnp.int32, (C, C), 1)
    lseg = seg.bit_length() - 1
    tri = jnp.where((cc <= rr) & ((rr >> lseg) == (cc >> lseg)), 1.0, 0.0).astype(F32)
    b = jnp.dot(tri, g, precision=lax.Precision.HIGHEST, preferred_element_type=F32)

    a = jnp.where(rr == cc, _dot_nt(q.astype(BF16), kk.astype(BF16)), 0.0)
    s = seg // 2
    while s >= 1:
        ls = s.bit_length() - 1
        e = jnp.exp2(_neg_abs(b - _group_ref_row(b, s)))
        same = (rr >> (ls + 1)) == (cc >> (ls + 1))
        if s % SUBLANE == 0:
            nblk = C // s
            zero = jnp.zeros((s, LANE), F32)
            blk = lambda x, i: x[i * s:(i + 1) * s, :]
            qs = jnp.concatenate([blk(q, i) * blk(e, i) if i % 2 else zero for i in range(nblk)], axis=0)
            ks = jnp.concatenate([zero if i % 2 else blk(kk, i) * blk(e, i) for i in range(nblk)], axis=0)
            p = _dot_nt(qs.astype(BF16), ks.astype(BF16))
            pieces = []
            for i in range(nblk):
                if i % 2 == 0:
                    pieces.append(blk(a, i))
                elif 2 * s == C:
                    pieces.append(blk(a, i) + blk(p, i))
                else:
                    pieces.append(blk(a, i) + jnp.where(blk(same, i), blk(p, i), 0.0))
            a = jnp.concatenate(pieces, axis=0)
        else:
            right = ((row >> ls) & 1) == 1
            eq = jnp.where(right, e, 0.0)
            p = _dot_nt((q * eq).astype(BF16), (kk * (e - eq)).astype(BF16))
            a = a + jnp.where(same, p, 0.0)
        s //= 2
    o = _dot(a.astype(BF16), vb)

    qe = q * jnp.exp2(b)
    new_states = []
    for k, st in enumerate(states):
        bl = b[k * seg + seg - 1:k * seg + seg, :]
        if len(states) == 1:
            qk = qe
            ke = kk * jnp.exp2(bl - b)
        else:
            mine = (row >> lseg) == k
            qk = jnp.where(mine, qe, 0.0)
            ke = jnp.where(mine, kk * jnp.exp2(jnp.where(mine, bl - b, 0.0)), 0.0)
        o = o + _dot(qk.astype(BF16), st.astype(BF16))
        decay = jnp.transpose(jnp.broadcast_to(jnp.exp2(bl), (LANE, LANE)))
        new_states.append(st * decay + _dot_tn(ke.astype(BF16), vb))
    return o, new_states


def _hgrn_wide(q, zf, vb, lb, states, seg):
    C, W = q.shape
    heads = [slice(h * LANE, (h + 1) * LANE) for h in range(W // LANE)]
    nseg = C // seg
    fg = lb + (1.0 - lb) * jax.nn.sigmoid(zf)
    kk = 1.0 - fg
    g = jnp.log2(fg)

    row = lax.broadcasted_iota(jnp.int32, (C, W), 0)
    rr = lax.broadcasted_iota(jnp.int32, (C, C), 0)
    cc = lax.broadcasted_iota(jnp.int32, (C, C), 1)
    lseg = seg.bit_length() - 1
    tri = jnp.where((cc <= rr) & ((rr >> lseg) == (cc >> lseg)), 1.0, 0.0).astype(F32)
    b = jnp.dot(tri, g, precision=lax.Precision.HIGHEST, preferred_element_type=F32)

    qb, kb = q.astype(BF16), kk.astype(BF16)
    a = [jnp.where(rr == cc, _dot_nt(qb[:, hs], kb[:, hs]), 0.0) for hs in heads]
    s = seg // 2
    while s >= 1:
        ls = s.bit_length() - 1
        same = (rr >> (ls + 1)) == (cc >> (ls + 1))
        if s == 1:
            right = (row & 1) == 1
            qs, ks = jnp.where(right, q * fg, 0.0).astype(BF16), jnp.where(right, 0.0, kk).astype(BF16)
            for h, hs in enumerate(heads):
                a[h] = a[h] + jnp.where(same, _dot_nt(qs[:, hs], ks[:, hs]), 0.0)
            break
        e = jnp.exp2(_neg_abs(b - _group_ref_row(b, s)))
        if s % SUBLANE == 0:
            nblk = C // s
            zero = jnp.zeros((s, W), F32)
            blk = lambda x, i: x[i * s:(i + 1) * s, :]
            qs = jnp.concatenate([blk(q, i) * blk(e, i) if i % 2 else zero for i in range(nblk)], axis=0).astype(BF16)
            ks = jnp.concatenate([zero if i % 2 else blk(kk, i) * blk(e, i) for i in range(nblk)], axis=0).astype(BF16)
            for h, hs in enumerate(heads):
                p = _dot_nt(qs[:, hs], ks[:, hs])
                pieces = []
                for i in range(nblk):
                    if i % 2 == 0:
                        pieces.append(blk(a[h], i))
                    elif 2 * s == C:
                        pieces.append(blk(a[h], i) + blk(p, i))
                    else:
                        pieces.append(blk(a[h], i) + jnp.where(blk(same, i), blk(p, i), 0.0))
                a[h] = jnp.concatenate(pieces, axis=0)
        else:
            right = ((row >> ls) & 1) == 1
            eq = jnp.where(right, e, 0.0)
            qs, ks = (q * eq).astype(BF16), (kk * (e - eq)).astype(BF16)
            for h, hs in enumerate(heads):
                a[h] = a[h] + jnp.where(same, _dot_nt(qs[:, hs], ks[:, hs]), 0.0)
        s //= 2
    o = [_dot(a[h].astype(BF16), vb[:, hs]) for h, hs in enumerate(heads)]

    qe = q * jnp.exp2(b)
    new_states = [[] for _ in heads]
    for k in range(nseg):
        bl = b[k * seg + seg - 1:k * seg + seg, :]
        if nseg == 1:
            qk = qe
            ke = kk * jnp.exp2(bl - b)
        else:
            mine = (row >> lseg) == k
            qk = jnp.where(mine, qe, 0.0)
            ke = jnp.where(mine, kk * jnp.exp2(jnp.where(mine, bl - b, 0.0)), 0.0)
        qkb, keb, ebl = qk.astype(BF16), ke.astype(BF16), jnp.exp2(bl)
        for h, hs in enumerate(heads):
            st = states[h][k]
            o[h] = o[h] + _dot(qkb[:, hs], st.astype(BF16))
            decay = jnp.transpose(jnp.broadcast_to(ebl[:, hs], (LANE, LANE)))
            new_states[h].append(st * decay + _dot_tn(keb[:, hs], vb[:, hs]))
    return o, new_states


def _hgrn_finish(o, zog, hgn):
    return (_rms(o, hgn) * _silu(zog.astype(F32))).astype(BF16)


def _hgrn_prompt_kernel(q_ref, zf_ref, v_ref, og_ref, lb_ref, hgn_ref, oh_ref, s_ref):
    @pl.when(pl.program_id(1) == 0)
    def _():
        s_ref[...] = jnp.zeros_like(s_ref)

    states = [[s_ref[h]] for h in range(HG_HEADS)]
    for sub in range(q_ref.shape[0] // HGRN_CHUNK):
        rs = slice(sub * HGRN_CHUNK, (sub + 1) * HGRN_CHUNK)
        q = _silu(q_ref[rs, :].astype(F32))
        o, states = _hgrn_wide(q, zf_ref[rs, :], v_ref[rs, :], lb_ref[...], states, HGRN_CHUNK)
        for h in range(HG_HEADS):
            cs = slice(h * LANE, (h + 1) * LANE)
            oh_ref[rs, cs] = _hgrn_finish(o[h], og_ref[rs, cs], hgn_ref[:, cs])
    for h in range(HG_HEADS):
        s_ref[h] = states[h][0]


def _hgrn_prompt(zb, zf, lbs3, hgn3, layer, batch):
    T = zb.shape[0] // batch
    C = HGRN_STEP_CHUNKS * HGRN_CHUNK
    nc = T // C
    W = HG_HEADS * LANE
    zspec = lambda col: pl.BlockSpec((C, W), lambda b, c: (b * nc + c, col))
    vec = pl.BlockSpec((None, 1, W), lambda b, c: (layer, 0, 0))
    return pl.pallas_call(
        _hgrn_prompt_kernel,
        grid=(batch, nc),
        in_specs=[zspec(ZB_Q), zspec(ZF_F), zspec(ZB_V), zspec(ZB_OG), vec, vec],
        out_specs=[
            pl.BlockSpec((C, W), lambda b, c: (b * nc + c, 0)),
            pl.BlockSpec((None, HG_HEADS, LANE, LANE), lambda b, c: (b, 0, 0, 0)),
        ],
        out_shape=[
            jax.ShapeDtypeStruct((batch * T, W), BF16),
            jax.ShapeDtypeStruct((batch, HG_HEADS, LANE, LANE), F32),
        ],
        compiler_params=_params(("parallel", "arbitrary"), 32),
        name="hgrn_prompt",
    )(zb, zf, zb, zb, lbs3, hgn3)


def _hgrn_decode_kernel(seq_len, q_ref, zf_ref, v_ref, og_ref, lb_ref, hgn_ref, s_ref, xq_ref, k_ref, vc_ref, *rest):
    oh_ref, so_ref, ox_ref = rest[-3:]
    _xattn_decode_kernel(seq_len, xq_ref, k_ref, vc_ref, ox_ref)
    rows = DEC_GROUP * seq_len
    for grp in range(s_ref.shape[0] // DEC_GROUP):
        rs = slice(grp * rows, (grp + 1) * rows)
        states = [[s_ref[grp * DEC_GROUP + k, h] for k in range(DEC_GROUP)] for h in range(HG_HEADS)]
        o, new = _hgrn_wide(_silu(q_ref[rs, :].astype(F32)), zf_ref[rs, :], v_ref[rs, :], lb_ref[...], states, seq_len)
        for h in range(HG_HEADS):
            cs = slice(h * LANE, (h + 1) * LANE)
            for k in range(DEC_GROUP):
                so_ref[grp * DEC_GROUP + k, h] = new[h][k]
            oh_ref[rs, cs] = _hgrn_finish(o[h], og_ref[rs, cs], hgn_ref[:, cs])


def _decode_mixers(zb, zf, lbs3, hgn3, state, stacked, cache_k, cache_v, layer, nseq, nb):
    seq_len = zb.shape[0] // nseq
    W = HG_HEADS * LANE
    XW = XA_HEADS * LANE
    zspec = lambda col: pl.BlockSpec((nb * seq_len, W), lambda i: (i, col))
    vec = pl.BlockSpec((None, 1, W), lambda i: (layer, 0, 0))
    sspec = pl.BlockSpec((None, nb, HG_HEADS, LANE, LANE), lambda i: (layer, i, 0, 0, 0))
    cspec = pl.BlockSpec((None, nb) + cache_k.shape[2:], lambda i: (layer, i, 0, 0))
    in_specs = [zspec(ZB_Q), zspec(ZF_F), zspec(ZB_V), zspec(ZB_OG), vec, vec, sspec,
                pl.BlockSpec((nb * seq_len, XW), lambda i: (i, ZF_X)), cspec, cspec]
    args = [zb, zf, zb, zb, lbs3, hgn3, state, zf, cache_k, cache_v]
    aliases = {}
    if stacked is not None:
        in_specs.append(pl.BlockSpec(memory_space=pl.ANY))
        args.append(stacked)
        aliases = {len(args) - 1: 1}
    return pl.pallas_call(
        functools.partial(_hgrn_decode_kernel, seq_len),
        grid=(nseq // nb,),
        in_specs=in_specs,
        out_specs=[pl.BlockSpec((nb * seq_len, W), lambda i: (i, 0)), sspec,
                   pl.BlockSpec((nb * seq_len, XW), lambda i: (i, 0))],
        out_shape=[
            jax.ShapeDtypeStruct((nseq * seq_len, W), BF16),
            jax.ShapeDtypeStruct(state.shape, F32),
            jax.ShapeDtypeStruct((nseq * seq_len, XW), BF16),
        ],
        input_output_aliases=aliases,
        compiler_params=_params(("parallel",), 58),
        name="decode_mixers",
    )(*args)


def _pool_prompt_kernel(u_ref, wp_ref, sc_ref, op_ref, nb_ref):
    u = u_ref[...]
    T = u.shape[0]
    row = lax.broadcasted_iota(jnp.int32, (T, LANE), 0)
    for g, w in enumerate(POOL_WINDOWS):
        cs = slice(g * LANE, (g + 1) * LANE)
        ug = u[:, cs]
        s = ug
        d = 1
        while d < w:
            s = s + jnp.where(row >= d, pltpu.roll(s, d, 0), 0.0)
            d *= 2
        cnt = jnp.minimum(row + 1, w).astype(F32)
        dv = s / cnt - ug
        y = _dot(dv.astype(BF16), wp_ref[g].astype(BF16)) * sc_ref[:, cs]
        op_ref[:, cs] = y.astype(BF16)
    tail = u_ref[T - 16:T, :]
    nb_ref[...] = pltpu.roll(tail, 15, 0)[0:POOL_BUF, :]


def _pool_prompt(zf, w_pool, scale3, layer, batch):
    T = zf.shape[0] // batch
    G = len(POOL_WINDOWS)
    W = G * LANE
    return pl.pallas_call(
        _pool_prompt_kernel,
        grid=(batch,),
        in_specs=[
            pl.BlockSpec((T, W), lambda b: (b, ZF_U)),
            pl.BlockSpec((None, G, LANE, LANE), lambda b: (layer, 0, 0, 0)),
            pl.BlockSpec((None, 1, W), lambda b: (layer, 0, 0)),
        ],
        out_specs=[
            pl.BlockSpec((T, W), lambda b: (b, 0)),
            pl.BlockSpec((None, POOL_BUF, W), lambda b: (b, 0, 0)),
        ],
        out_shape=[
            jax.ShapeDtypeStruct((batch * T, W), BF16),
            jax.ShapeDtypeStruct((batch, POOL_BUF, W), F32),
        ],
        compiler_params=_params(("parallel",), 48),
        name="pool_prompt",
    )(zf, w_pool, scale3)


def _pool_decode_kernel(u_ref, buf_ref, wp_ref, sc_ref, op_ref, nb_ref):
    steps = u_ref.shape[0]
    for g, w in enumerate(POOL_WINDOWS):
        cs = slice(g * LANE, (g + 1) * LANE)
        wg = wp_ref[g].astype(BF16)
        for t in range(steps):
            n_u = min(t + 1, w)
            acc = u_ref[t, :, cs]
            for j in range(t - n_u + 1, t):
                acc = acc + u_ref[j, :, cs]
            for i in range(POOL_BUF - (w - n_u), POOL_BUF):
                acc = acc + buf_ref[i, :, cs]
            dv = acc * (1.0 / w) - u_ref[t, :, cs]
            op_ref[t, :, cs] = _dot(dv.astype(BF16), wg) * sc_ref[:, cs]
    for i in range(POOL_BUF - steps):
        nb_ref[i] = buf_ref[i + steps]
    for t in range(steps):
        nb_ref[POOL_BUF - steps + t] = u_ref[t]


def _pool_decode(u_t, buf_t, w_pool, scale3, layer):
    steps, nseq, W = u_t.shape
    G = len(POOL_WINDOWS)
    return pl.pallas_call(
        _pool_decode_kernel,
        grid=(1,),
        in_specs=[
            pl.BlockSpec((steps, nseq, W), lambda i: (0, 0, 0)),
            pl.BlockSpec((POOL_BUF, nseq, W), lambda i: (0, 0, 0)),
            pl.BlockSpec((None, G, LANE, LANE), lambda i: (layer, 0, 0, 0)),
            pl.BlockSpec((None, 1, W), lambda i: (layer, 0, 0)),
        ],
        out_specs=[
            pl.BlockSpec((steps, nseq, W), lambda i: (0, 0, 0)),
            pl.BlockSpec((POOL_BUF, nseq, W), lambda i: (0, 0, 0)),
        ],
        out_shape=[
            jax.ShapeDtypeStruct((steps, nseq, W), F32),
            jax.ShapeDtypeStruct((POOL_BUF, nseq, W), F32),
        ],
        compiler_params=_params(("arbitrary",), 32),
        name="pool_decode",
    )(u_t, buf_t, w_pool, scale3)


def _softmax_rows(s):
    e = jnp.exp(s - jnp.max(s, axis=-1, keepdims=True))
    return e / jnp.sum(e, axis=-1, keepdims=True)


def _xattn_prompt_kernel(q_ref, k_ref, v_ref, o_ref):
    scale = LANE ** -0.5
    for h in range(XA_HEADS):
        cs = slice(h * LANE, (h + 1) * LANE)
        s = _dot_nt(q_ref[:, cs].astype(BF16), k_ref[:, cs].astype(BF16)) * scale
        p = _softmax_rows(s)
        o_ref[:, cs] = _dot(p.astype(BF16), v_ref[:, cs].astype(BF16)).astype(BF16)


def _xattn_prompt(zf, mk, mv, batch, tq):
    T = zf.shape[0] // batch
    W = XA_HEADS * LANE
    nq = T // tq
    n_mem = mk.shape[1]
    mspec = pl.BlockSpec((None, n_mem, W), lambda b, i: (b, 0, 0))
    return pl.pallas_call(
        _xattn_prompt_kernel,
        grid=(batch, nq),
        in_specs=[pl.BlockSpec((tq, W), lambda b, i: (b * nq + i, ZF_X)), mspec, mspec],
        out_specs=pl.BlockSpec((tq, W), lambda b, i: (b * nq + i, 0)),
        out_shape=jax.ShapeDtypeStruct((batch * T, W), BF16),
        compiler_params=_params(("parallel", "parallel"), 48),
        name="xattn_prompt",
    )(zf, mk, mv)


def _xattn_decode_kernel(seq_len, q_ref, k_ref, v_ref, o_ref):
    scale = LANE ** -0.5
    rows = DEC_GROUP * seq_len
    lseq = seq_len.bit_length() - 1
    lrows = rows.bit_length() - 1
    nk = k_ref.shape[1]
    row_s = lax.broadcasted_iota(jnp.int32, (XA_HEADS * rows, nk), 0)
    col_s = lax.broadcasted_iota(jnp.int32, (XA_HEADS * rows, nk), 1)
    own_head = (col_s & (XA_HEADS - 1)) == (row_s >> lrows)
    seq_s = (row_s & (rows - 1)) >> lseq
    seq_o = (lax.broadcasted_iota(jnp.int32, (XA_HEADS * rows, LANE), 0) & (rows - 1)) >> lseq
    for grp in range(k_ref.shape[0] // DEC_GROUP):
        rs = slice(grp * rows, (grp + 1) * rows)
        qb = jnp.concatenate([q_ref[rs, h * LANE:(h + 1) * LANE] for h in range(XA_HEADS)], axis=0).astype(BF16)
        s = jnp.full((XA_HEADS * rows, nk), -jnp.inf, F32)
        for k in range(DEC_GROUP):
            sk = _dot_nt(qb, k_ref[grp * DEC_GROUP + k].astype(BF16))
            s = jnp.where(own_head & (seq_s == k), sk, s)
        p = _softmax_rows(s * scale).astype(BF16)
        o = jnp.zeros((XA_HEADS * rows, LANE), F32)
        for k in range(DEC_GROUP):
            o = jnp.where(seq_o == k, _dot(p, v_ref[grp * DEC_GROUP + k].astype(BF16)), o)
        for h in range(XA_HEADS):
            o_ref[rs, h * LANE:(h + 1) * LANE] = o[h * rows:(h + 1) * rows, :].astype(BF16)


def _merge_kernel(x_ref, oh_ref, op_ref, ox_ref, g0_ref, g1_ref, g2_ref, wb_ref, wo_ref, o_ref, *rest):
    y_ref = rest[-1]
    wbb_ref, wob_ref = rest[:2] if len(rest) == 3 else (None, None)
    j = pl.program_id(1)
    nj, _, tn = y_ref.shape

    @pl.when(j < nj)
    def _():
        wh = oh_ref.shape[1]
        wp = op_ref.shape[1]
        oh, op, ox = oh_ref[...].astype(BF16), op_ref[...].astype(BF16), ox_ref[...].astype(BF16)
        cw = min(MXU_COLS, tn)
        for c in range(tn // cw):
            cs = slice(c * cw, (c + 1) * cw)
            w = wb_ref[:, cs].astype(BF16)
            if wbb_ref is not None:
                wbb_ref[:, cs] = w
            y = _sigmoid(g0_ref[:, cs].astype(F32)) * _dot(oh, w[0:wh, :])
            y += _sigmoid(g1_ref[:, cs].astype(F32)) * _dot(op, w[wh:wh + wp, :])
            y += _sigmoid(g2_ref[:, cs].astype(F32)) * _dot(ox, w[wh + wp:, :])
            y_ref[j, :, cs] = y.astype(BF16)

    @pl.when(j >= nj)
    def _():
        acc = x_ref[...]
        for c in range(nj):
            rs = slice(c * tn, (c + 1) * tn)
            w = wo_ref[rs, :].astype(BF16)
            if wob_ref is not None:
                wob_ref[rs, :] = w
            acc += _dot(y_ref[c], w)
        o_ref[...] = acc


def _merge_resident_kernel(x_ref, oh_ref, op_ref, ox_ref, g_ref, wb_ref, wo_ref, o_ref, y_ref):
    D = x_ref.shape[1]
    wh = oh_ref.shape[1]
    wp = op_ref.shape[1]
    oh, op, ox = oh_ref[...].astype(BF16), op_ref[...].astype(BF16), ox_ref[...].astype(BF16)
    cw = min(2 * MXU_COLS, D)
    for c in range(D // cw):
        cs = slice(c * cw, (c + 1) * cw)
        gate = lambda k: _sigmoid(g_ref[:, k * D + c * cw:k * D + (c + 1) * cw].astype(F32))
        y = gate(0) * _dot(oh, wb_ref[0:wh, cs])
        y += gate(1) * _dot(op, wb_ref[wh:wh + wp, cs])
        y += gate(2) * _dot(ox, wb_ref[wh + wp:, cs])
        y_ref[:, cs] = y.astype(BF16)
    for c in range(D // cw):
        cs = slice(c * cw, (c + 1) * cw)
        o_ref[:, cs] = x_ref[:, cs] + _dot(y_ref[...], wo_ref[:, cs])


def _merge_resident(x, oh, op, ox, zg, wb, wo, tm):
    T, D = x.shape
    row = lambda a: pl.BlockSpec((tm, a.shape[1]), lambda i: (i, 0))
    held = lambda a: pl.BlockSpec(a.shape, lambda i: (0, 0), pipeline_mode=pl.Buffered(1))
    return pl.pallas_call(
        _merge_resident_kernel,
        grid=(T // tm,),
        in_specs=[row(x), row(oh), row(op), row(ox), row(zg), held(wb), held(wo)],
        out_specs=row(x),
        out_shape=jax.ShapeDtypeStruct((T, D), F32),
        scratch_shapes=[pltpu.VMEM((tm, D), BF16)],
        compiler_params=_params(("parallel",), 56),
        name="merge_resident",
    )(x, oh, op, ox, zg, wb, wo)


def _merge(x, oh, op, ox, zb, wb, wo, layer, tm, tn, emit=False):
    T, D = x.shape
    assert not emit or T == tm
    nj = D // tn
    first = lambda j: jnp.minimum(j, nj - 1)
    second = lambda j: jnp.maximum(j - nj, 0)
    full = lambda a: pl.BlockSpec((tm, a.shape[1]), lambda i, j: (i, 0))
    gate = lambda k: pl.BlockSpec((tm, tn), lambda i, j: (i, k * nj + first(j)))
    wb_idx = lambda i, j: (0, first(j))
    wo_idx = lambda i, j: (0, second(j))
    rows_b = wb.shape[-2]
    out_specs = [pl.BlockSpec((tm, tn), lambda i, j: (i, second(j)))]
    out_shape = [jax.ShapeDtypeStruct((T, D), F32)]
    if emit:
        out_specs += [pl.BlockSpec((rows_b, tn), wb_idx), pl.BlockSpec((D, tn), wo_idx)]
        out_shape += [jax.ShapeDtypeStruct((rows_b, D), BF16), jax.ShapeDtypeStruct((D, D), BF16)]
    out = pl.pallas_call(
        _merge_kernel,
        grid=(T // tm, 2 * nj),
        in_specs=[
            pl.BlockSpec((tm, tn), lambda i, j: (i, second(j))),
            full(oh), full(op), full(ox), gate(0), gate(1), gate(2),
            _layer_spec(wb, layer, (rows_b, tn), wb_idx),
            _layer_spec(wo, layer, (D, tn), wo_idx),
        ],
        out_specs=out_specs,
        out_shape=out_shape,
        scratch_shapes=[pltpu.VMEM((nj, tm, tn), BF16)],
        compiler_params=_params(("parallel", "arbitrary"), 48),
        name="merge_out",
    )(x, oh, op, ox, zb, zb, zb, wb, wo)
    return out if emit else out[0]


def _norm_kernel(x_ref, g_ref, o_ref):
    o_ref[...] = _rms(x_ref[...], g_ref[...])


def _final_norm(x, g2, tm):
    T, D = x.shape
    return pl.pallas_call(
        _norm_kernel,
        grid=(T // tm,),
        in_specs=[pl.BlockSpec((tm, D), lambda i: (i, 0)), pl.BlockSpec((1, D), lambda i: (0, 0))],
        out_specs=pl.BlockSpec((tm, D), lambda i: (i, 0)),
        out_shape=jax.ShapeDtypeStruct((T, D), F32),
        compiler_params=_params(("parallel",), 40),
        name="final_norm",
    )(x, g2)


def kernel(x_prompt, x_sample, state_hgrn, state_pool, cache_mem_k, cache_mem_v, mem_prompt, ffn1_norm, ffn1_w1, ffn1_w3, ffn1_w2, mix_norm, w_in, lb_logits, hg_norm, w_pool, pool_scale, mem_norm, w_mk, w_mv, w_branch, w_out, ffn2_norm, ffn2_w1, ffn2_w3, ffn2_w2, final_norm):
    B, T, D = x_prompt.shape
    nseq, steps, _ = x_sample.shape
    depth = w_in.shape[0]
    n_mem = mem_prompt.shape[1]
    xw = XA_HEADS * LANE
    pw = len(POOL_WINDOWS) * LANE

    tm_p = min(1024, B * T)
    tm_s = min(512, nseq * steps)

    vec3 = lambda a: a.reshape(a.shape[0], 1, a.shape[1])
    f1n, f2n = vec3(ffn1_norm), vec3(ffn2_norm)
    mix3, mem3, hgn3, psc3 = vec3(mix_norm), vec3(mem_norm), vec3(hg_norm), vec3(pool_scale)
    lbs3 = vec3(_lower_bounds(lb_logits))

    xp = x_prompt.reshape(B * T, D)
    xs = x_sample.reshape(nseq * steps, D)
    mem2 = mem_prompt.reshape(B * n_mem, D)
    ck = cache_mem_k.reshape(depth, nseq, n_mem * XA_HEADS, LANE)
    cv = cache_mem_v.reshape(depth, nseq, n_mem * XA_HEADS, LANE)

    tf = _tile(ffn1_w1.shape[-1], 512)
    tf_s = _tile(ffn1_w1.shape[-1], 256)
    tn_in = _tile(math.gcd(HG_HEADS * LANE, N_BRANCH * D), 1024)
    tn_mrg = _tile(D, 512)
    tm_mem = min(512, B * n_mem)
    hs_p, pb_p, mk_p, mv_p, pb_s = [], [], [], [], []
    hs_s = None
    for l in range(depth):
        mk = _rms_matmul(mem2, mem3, w_mk, l, tm_mem, xw)
        mv = _rms_matmul(mem2, mem3, w_mv, l, tm_mem, xw)
        mk_p.append(mk.reshape(B, n_mem, XA_HEADS, LANE))
        mv_p.append(mv.reshape(B, n_mem, XA_HEADS, LANE))

        xs, w1b, w3b, w2b = _ffn(xs, f1n, ffn1_w1, ffn1_w3, ffn1_w2, l, tm_s, tf_s)
        xp = _ffn(xp, f1n, w1b, w3b, w2b, l, tm_p, tf)

        zsb, zsf, zsg, w_in_b = _in_proj(xs, mix3, w_in, l, tm_s, tn_in, emit=True)
        zpb, zpf, zpg = _in_proj(xp, mix3, w_in_b, l, tm_p, tn_in)

        ohp, sp = _hgrn_prompt(zpb, zpf, lbs3, hgn3, l, B)
        ohs, hs_s, oxs = _decode_mixers(zsb, zsf, lbs3, hgn3, state_hgrn, hs_s, ck, cv, l, nseq, 8)
        hs_p.append(sp)

        opp, bp = _pool_prompt(zpf, w_pool, psc3, l, B)
        pb_p.append(bp)
        u_t = zsf[:, ZF_U * pw:(ZF_U + 1) * pw].reshape(nseq, steps, pw).transpose(1, 0, 2)
        buf_t = state_pool[l].transpose(1, 0, 2)
        ops_t, nb_t = _pool_decode(u_t, buf_t, w_pool, psc3, l)
        ops = ops_t.transpose(1, 0, 2).reshape(nseq * steps, pw)
        pb_s.append(nb_t.transpose(1, 0, 2))

        oxp = _xattn_prompt(zpf, mk.reshape(B, n_mem, xw), mv.reshape(B, n_mem, xw), B, min(1024, T))

        xs, w_br_b, w_out_b = _merge(xs, ohs, ops, oxs, zsg, w_branch, w_out, l, tm_s, tn_mrg, emit=True)
        xp = _merge_resident(xp, ohp, opp, oxp, zpg, w_br_b, w_out_b, min(512, B * T))

        fn2 = final_norm.reshape(1, D) if l == depth - 1 else None
        xs, w1b, w3b, w2b = _ffn(xs, f2n, ffn2_w1, ffn2_w3, ffn2_w2, l, tm_s, tf_s, fn2)
        xp = _ffn(xp, f2n, w1b, w3b, w2b, l, tm_p, tf, fn2)

    y_prompt = xp.reshape(B, T, D)
    y_sample = xs.reshape(nseq, steps, D)
    return (y_prompt, y_sample, jnp.stack(hs_p), jnp.stack(pb_p), jnp.stack(mk_p), jnp.stack(mv_p),
            hs_s, jnp.stack(pb_s))
```

```python
import functools
import math

import jax
import jax.numpy as jnp
from jax import lax
from jax.experimental import pallas as pl
from jax.experimental.pallas import tpu as pltpu

F32 = jnp.float32
BF16 = jnp.bfloat16
EPS = 1e-6

LANE = 128
SUBLANE = 8
MXU_COLS = 256
HG_HEADS = 8
POOL_WINDOWS = (2, 4, 8, 16)
POOL_BUF = max(POOL_WINDOWS) - 1
XA_HEADS = 4
N_BRANCH = 3

COL_Q, COL_F, COL_I, COL_OG = 0, HG_HEADS, 2 * HG_HEADS, 3 * HG_HEADS
COL_U = 4 * HG_HEADS
COL_X = COL_U + len(POOL_WINDOWS)
COL_G = COL_X + XA_HEADS
ZB_Q, ZB_V, ZB_OG = 0, 1, 2
ZF_F = 0
ZF_U, ZF_X = 2, 3

HGRN_CHUNK = 128
HGRN_STEP_CHUNKS = 2
DEC_GROUP = 4


def _tile(n, preferred):
    t = preferred
    while n % t:
        t -= LANE
    return t


def _params(semantics, vmem_mib):
    return pltpu.CompilerParams(dimension_semantics=semantics, vmem_limit_bytes=vmem_mib * 1024 * 1024)


def _rms(x, g):
    return x * lax.rsqrt(jnp.mean(x * x, axis=-1, keepdims=True) + EPS) * g


def _sigmoid(x):
    return 0.5 * jnp.tanh(0.5 * x) + 0.5


def _silu(x):
    return x * _sigmoid(x)


def _dot(a, b):
    return jnp.dot(a, b, preferred_element_type=F32)


def _dot_nt(a, b):
    return lax.dot_general(a, b, (((1,), (1,)), ((), ())), preferred_element_type=F32)


def _dot_tn(a, b):
    return lax.dot_general(a, b, (((0,), (0,)), ((), ())), preferred_element_type=F32)


def _ffn_kernel(final, cast, x_ref, g_ref, w1_ref, w3_ref, w2_ref, *rest):
    fn_ref = rest[0] if final else None
    o_ref, xn_ref = rest[int(final)], rest[-1]
    if cast:
        src = (w1_ref, w3_ref, w2_ref)
        w1_ref, w3_ref, w2_ref = rest[int(final) + 1:int(final) + 4]
        for dst, s in zip((w1_ref, w3_ref, w2_ref), src):
            dst[...] = s[...].astype(BF16)
    j = pl.program_id(1)

    @pl.when(j == 0)
    def _():
        x = x_ref[...]
        xn_ref[...] = _rms(x, g_ref[...]).astype(BF16)
        o_ref[...] = x

    xn = xn_ref[...]
    h = (_silu(_dot(xn, w1_ref[...])) * _dot(xn, w3_ref[...])).astype(BF16)
    cw = min(w2_ref.shape[0], o_ref.shape[1])
    for n in range(o_ref.shape[1] // cw):
        cs = slice(n * cw, (n + 1) * cw)
        o_ref[:, cs] += 0.5 * _dot(h, w2_ref[:, cs])

    if final:
        @pl.when(j == pl.num_programs(1) - 1)
        def _():
            o_ref[...] = _rms(o_ref[...], fn_ref[...])


def _layer_spec(w, layer, block, index):
    if w.ndim == 2:
        return pl.BlockSpec(block, index)
    return pl.BlockSpec((None,) + block, lambda i, j: (layer,) + index(i, j))


def _ffn(x, g3, w1, w3, w2, layer, tm, tf, final_g=None):
    T, D = x.shape
    F = w1.shape[-1]
    cast = w1.dtype != BF16
    final = final_g is not None
    assert not cast or T == tm
    col = lambda i, j: (0, j)
    row = lambda i, j: (j, 0)
    out_specs = [pl.BlockSpec((tm, D), lambda i, j: (i, 0))]
    out_shape = [jax.ShapeDtypeStruct((T, D), F32)]
    if cast:
        out_specs += [pl.BlockSpec((D, tf), col), pl.BlockSpec((D, tf), col), pl.BlockSpec((tf, D), row)]
        out_shape += [jax.ShapeDtypeStruct((D, F), BF16), jax.ShapeDtypeStruct((D, F), BF16),
                      jax.ShapeDtypeStruct((F, D), BF16)]
    in_specs = [
        pl.BlockSpec((tm, D), lambda i, j: (i, 0)),
        pl.BlockSpec((None, 1, D), lambda i, j: (layer, 0, 0)),
        _layer_spec(w1, layer, (D, tf), col),
        _layer_spec(w3, layer, (D, tf), col),
        _layer_spec(w2, layer, (tf, D), row),
    ]
    args = [x, g3, w1, w3, w2]
    if final:
        in_specs.append(pl.BlockSpec((1, D), lambda i, j: (0, 0)))
        args.append(final_g)
    out = pl.pallas_call(
        functools.partial(_ffn_kernel, final, cast),
        grid=(T // tm, F // tf),
        in_specs=in_specs,
        out_specs=out_specs,
        out_shape=out_shape,
        scratch_shapes=[pltpu.VMEM((tm, D), BF16)],
        compiler_params=_params(("parallel", "arbitrary"), 58),
        name="ffn_cast" if cast else "ffn",
    )(*args)
    return out if cast else out[0]


def _rms_matmul_kernel(x_ref, g_ref, w_ref, o_ref, *rest):
    xn_ref = rest[-1]

    @pl.when(pl.program_id(1) == 0)
    def _():
        xn_ref[...] = _rms(x_ref[...], g_ref[...]).astype(BF16)

    w = w_ref[...].astype(BF16)
    if len(rest) == 2:
        rest[0][...] = w
    o_ref[...] = _dot(xn_ref[...], w)


def _rms_matmul(x, g3, w, layer, tm, tn, emit=False):
    T, D = x.shape
    N = w.shape[-1]
    assert not emit or T == tm
    col = lambda i, j: (0, j)
    out_specs = [pl.BlockSpec((tm, tn), lambda i, j: (i, j))]
    out_shape = [jax.ShapeDtypeStruct((T, N), F32)]
    if emit:
        out_specs.append(pl.BlockSpec((D, tn), col))
        out_shape.append(jax.ShapeDtypeStruct((D, N), BF16))
    out = pl.pallas_call(
        _rms_matmul_kernel,
        grid=(T // tm, N // tn),
        in_specs=[
            pl.BlockSpec((tm, D), lambda i, j: (i, 0)),
            pl.BlockSpec((None, 1, D), lambda i, j: (layer, 0, 0)),
            _layer_spec(w, layer, (D, tn), col),
        ],
        out_specs=out_specs,
        out_shape=out_shape,
        scratch_shapes=[pltpu.VMEM((tm, D), BF16)],
        compiler_params=_params(("parallel", "arbitrary"), 48),
        name="rms_matmul",
    )(x, g3, w)
    return out if emit else out[0]


def _runs(seq):
    runs = []
    for j, v in enumerate(seq):
        if runs and runs[-1][2] == v - j:
            runs[-1][1] = j + 1
        else:
            runs.append([j, j + 1, v - j])
    return [tuple(r) for r in runs]


def _in_runs(j, runs):
    hit = None
    for lo, hi, _ in runs:
        c = (j >= lo) & (j < hi)
        hit = c if hit is None else hit | c
    return hit


def _lookup(j, runs):
    out = 0
    for lo, hi, off in runs:
        out = out + jnp.where((j >= lo) & (j < hi), j + off, 0)
    return out


def _in_proj_plan(d_model, tn):
    t = lambda blocks: blocks * LANE // tn
    nq, nux, ng = t(HG_HEADS), t(len(POOL_WINDOWS) + XA_HEADS), N_BRANCH * d_model // tn
    src = lambda start, n: list(range(t(start), t(start) + n))
    order = src(COL_Q, nq) + src(COL_I, nq) + src(COL_OG, nq) + src(COL_F, nq) + src(COL_U, nux) + src(COL_G, ng)
    nh, n32 = 3 * nq, nq + nux
    kinds = {"f32": [(nh, nh + n32, 0)]}
    return _runs(order), kinds, nh, n32


def _in_proj_kernel(kinds, x_ref, g_ref, w_ref, zh_ref, zf_ref, zg_ref, *rest):
    xn_ref = rest[-1]
    j = pl.program_id(1)

    @pl.when(j == 0)
    def _():
        xn_ref[...] = _rms(x_ref[...], g_ref[...]).astype(BF16)

    xn = xn_ref[...]
    tn = w_ref.shape[1]
    cw = min(MXU_COLS, tn)
    zs = []
    for c in range(tn // cw):
        cs = slice(c * cw, (c + 1) * cw)
        w = w_ref[:, cs].astype(BF16)
        if len(rest) == 2:
            rest[0][:, cs] = w
        z = _dot(xn, w)
        zs.append(z)
        zh_ref[:, cs] = z.astype(BF16)
        zg_ref[:, cs] = z.astype(BF16)

    @pl.when(_in_runs(j, kinds["f32"]))
    def _():
        for c, z in enumerate(zs):
            zf_ref[:, c * cw:(c + 1) * cw] = z


def _in_proj(x, g3, w, layer, tm, tn, emit=False):
    T, D = x.shape
    N = w.shape[-1]
    assert not emit or T == tm
    order, kinds, nh, n32 = _in_proj_plan(D, tn)
    nsteps = N // tn
    wcol = lambda i, j: (0, _lookup(j, order))
    out_specs = [
        pl.BlockSpec((tm, tn), lambda i, j: (i, jnp.minimum(j, nh))),
        pl.BlockSpec((tm, tn), lambda i, j: (i, jnp.clip(j - nh, 0, n32 - 1))),
        pl.BlockSpec((tm, tn), lambda i, j: (i, jnp.maximum(j - nh - n32, 0))),
    ]
    out_shape = [
        jax.ShapeDtypeStruct((T, (nh + 1) * tn), BF16),
        jax.ShapeDtypeStruct((T, n32 * tn), F32),
        jax.ShapeDtypeStruct((T, (nsteps - nh - n32) * tn), BF16),
    ]
    if emit:
        out_specs.append(pl.BlockSpec((D, tn), wcol))
        out_shape.append(jax.ShapeDtypeStruct((D, N), BF16))
    return pl.pallas_call(
        functools.partial(_in_proj_kernel, kinds),
        grid=(T // tm, nsteps),
        in_specs=[
            pl.BlockSpec((tm, D), lambda i, j: (i, 0)),
            pl.BlockSpec((None, 1, D), lambda i, j: (layer, 0, 0)),
            _layer_spec(w, layer, (D, tn), wcol),
        ],
        out_specs=out_specs,
        out_shape=out_shape,
        scratch_shapes=[pltpu.VMEM((tm, D), BF16)],
        compiler_params=_params(("parallel", "arbitrary"), 56),
        name="in_proj",
    )(x, g3, w)


def _lower_bound_kernel(lg_ref, o_ref):
    lg = lg_ref[...]
    depth = lg.shape[0]
    rows = [lg[i:i + 1, :] for i in range(depth)]
    m = rows[0]
    for r in rows[1:]:
        m = jnp.maximum(m, r)
    e = [jnp.exp(r - m) for r in rows]
    tot = e[0]
    for v in e[1:]:
        tot = tot + v
    c = e[0] / tot
    first = c
    o_ref[0:1, :] = c - first
    for i in range(1, depth):
        c = c + e[i] / tot
        o_ref[i:i + 1, :] = c - first


def _lower_bounds(lb_logits):
    return pl.pallas_call(
        _lower_bound_kernel,
        out_shape=jax.ShapeDtypeStruct(lb_logits.shape, F32),
        name="hgrn_lower_bounds",
    )(lb_logits)


def _neg_abs(x):
    bits = lax.bitcast_convert_type(x, jnp.uint32) | jnp.uint32(0x80000000)
    return lax.bitcast_convert_type(bits, F32)


def _group_ref_row(b, s):
    rows, width = b.shape
    gsz = 2 * s
    if gsz >= 8:
        parts = [jnp.broadcast_to(b[i * gsz + s - 1:i * gsz + s, :], (gsz, width)) for i in range(rows // gsz)]
        return parts[0] if len(parts) == 1 else jnp.concatenate(parts, axis=0)
    pos = lax.broadcasted_iota(jnp.int32, b.shape, 0) & (gsz - 1)
    out = b
    for p in range(gsz):
        d = p - (s - 1)
        if d != 0:
            out = jnp.where(pos == p, pltpu.roll(b, d % rows, 0), out)
    return out


def _hgrn_wide(q, zf, vb, lb, states, seg):
    C, W = q.shape
    heads = [slice(h * LANE, (h + 1) * LANE) for h in range(W // LANE)]
    nseg = C // seg
    fg = lb + (1.0 - lb) * jax.nn.sigmoid(zf)
    kk = 1.0 - fg
    g = jnp.log2(fg)

    row = lax.broadcasted_iota(jnp.int32, (C, W), 0)
    rr = lax.broadcasted_iota(jnp.int32, (C, C), 0)
    cc = lax.broadcasted_iota(jnp.int32, (C, C), 1)
    lseg = seg.bit_length() - 1
    tri = jnp.where((cc <= rr) & ((rr >> lseg) == (cc >> lseg)), 1.0, 0.0).astype(F32)
    b = jnp.dot(tri, g, precision=lax.Precision.HIGHEST, preferred_element_type=F32)

    qb, kb = q.astype(BF16), kk.astype(BF16)
    a = [jnp.where(rr == cc, _dot_nt(qb[:, hs], kb[:, hs]), 0.0) for hs in heads]
    s = seg // 2
    while s >= 1:
        ls = s.bit_length() - 1
        same = (rr >> (ls + 1)) == (cc >> (ls + 1))
        if s == 1:
            right = (row & 1) == 1
            qs, ks = jnp.where(right, q * fg, 0.0).astype(BF16), jnp.where(right, 0.0, kk).astype(BF16)
            for h, hs in enumerate(heads):
                a[h] = a[h] + jnp.where(same, _dot_nt(qs[:, hs], ks[:, hs]), 0.0)
            break
        e = jnp.exp2(_neg_abs(b - _group_ref_row(b, s)))
        if s % SUBLANE == 0:
            nblk = C // s
            zero = jnp.zeros((s, W), F32)
            blk = lambda x, i: x[i * s:(i + 1) * s, :]
            qs = jnp.concatenate([blk(q, i) * blk(e, i) if i % 2 else zero for i in range(nblk)], axis=0).astype(BF16)
            ks = jnp.concatenate([zero if i % 2 else blk(kk, i) * blk(e, i) for i in range(nblk)], axis=0).astype(BF16)
            for h, hs in enumerate(heads):
                p = _dot_nt(qs[:, hs], ks[:, hs])
                pieces = []
                for i in range(nblk):
                    if i % 2 == 0:
                        pieces.append(blk(a[h], i))
                    elif 2 * s == C:
                        pieces.append(blk(a[h], i) + blk(p, i))
                    else:
                        pieces.append(blk(a[h], i) + jnp.where(blk(same, i), blk(p, i), 0.0))
                a[h] = jnp.concatenate(pieces, axis=0)
        else:
            right = ((row >> ls) & 1) == 1
            eq = jnp.where(right, e, 0.0)
            qs, ks = (q * eq).astype(BF16), (kk * (e - eq)).astype(BF16)
            for h, hs in enumerate(heads):
                a[h] = a[h] + jnp.where(same, _dot_nt(qs[:, hs], ks[:, hs]), 0.0)
        s //= 2
    o = [_dot(a[h].astype(BF16), vb[:, hs]) for h, hs in enumerate(heads)]

    qe = q * jnp.exp2(b)
    new_states = [[] for _ in heads]
    for k in range(nseg):
        bl = b[k * seg + seg - 1:k * seg + seg, :]
        if nseg == 1:
            qk = qe
            ke = kk * jnp.exp2(bl - b)
        else:
            mine = (row >> lseg) == k
            qk = jnp.where(mine, qe, 0.0)
            ke = jnp.where(mine, kk * jnp.exp2(jnp.where(mine, bl - b, 0.0)), 0.0)
        qkb, keb, ebl = qk.astype(BF16), ke.astype(BF16), jnp.exp2(bl)
        for h, hs in enumerate(heads):
            st = states[h][k]
            o[h] = o[h] + _dot(qkb[:, hs], st.astype(BF16))
            decay = jnp.transpose(jnp.broadcast_to(ebl[:, hs], (LANE, LANE)))
            new_states[h].append(st * decay + _dot_tn(keb[:, hs], vb[:, hs]))
    return o, new_states


def _hgrn_finish(o, zog, hgn):
    return (_rms(o, hgn) * _silu(zog.astype(F32))).astype(BF16)


def _hgrn_prompt_kernel(q_ref, zf_ref, v_ref, og_ref, lb_ref, hgn_ref, oh_ref, s_ref):
    @pl.when(pl.program_id(1) == 0)
    def _():
        s_ref[...] = jnp.zeros_like(s_ref)

    states = [[s_ref[h]] for h in range(HG_HEADS)]
    for sub in range(q_ref.shape[0] // HGRN_CHUNK):
        rs = slice(sub * HGRN_CHUNK, (sub + 1) * HGRN_CHUNK)
        q = _silu(q_ref[rs, :].astype(F32))
        o, states = _hgrn_wide(q, zf_ref[rs, :], v_ref[rs, :], lb_ref[...], states, HGRN_CHUNK)
        for h in range(HG_HEADS):
            cs = slice(h * LANE, (h + 1) * LANE)
            oh_ref[rs, cs] = _hgrn_finish(o[h], og_ref[rs, cs], hgn_ref[:, cs])
    for h in range(HG_HEADS):
        s_ref[h] = states[h][0]


def _hgrn_prompt(zb, zf, lbs3, hgn3, layer, batch):
    T = zb.shape[0] // batch
    C = HGRN_STEP_CHUNKS * HGRN_CHUNK
    nc = T // C
    W = HG_HEADS * LANE
    zspec = lambda col: pl.BlockSpec((C, W), lambda b, c: (b * nc + c, col))
    vec = pl.BlockSpec((None, 1, W), lambda b, c: (layer, 0, 0))
    return pl.pallas_call(
        _hgrn_prompt_kernel,
        grid=(batch, nc),
        in_specs=[zspec(ZB_Q), zspec(ZF_F), zspec(ZB_V), zspec(ZB_OG), vec, vec],
        out_specs=[
            pl.BlockSpec((C, W), lambda b, c: (b * nc + c, 0)),
            pl.BlockSpec((None, HG_HEADS, LANE, LANE), lambda b, c: (b, 0, 0, 0)),
        ],
        out_shape=[
            jax.ShapeDtypeStruct((batch * T, W), BF16),
            jax.ShapeDtypeStruct((batch, HG_HEADS, LANE, LANE), F32),
        ],
        compiler_params=_params(("parallel", "arbitrary"), 32),
        name="hgrn_prompt",
    )(zb, zf, zb, zb, lbs3, hgn3)


def _hgrn_decode_kernel(seq_len, q_ref, zf_ref, v_ref, og_ref, lb_ref, hgn_ref, s_ref, xq_ref, k_ref, vc_ref, *rest):
    oh_ref, so_ref, ox_ref = rest[-3:]
    _xattn_decode_kernel(seq_len, xq_ref, k_ref, vc_ref, ox_ref)
    rows = DEC_GROUP * seq_len
    for grp in range(s_ref.shape[0] // DEC_GROUP):
        rs = slice(grp * rows, (grp + 1) * rows)
        states = [[s_ref[grp * DEC_GROUP + k, h] for k in range(DEC_GROUP)] for h in range(HG_HEADS)]
        o, new = _hgrn_wide(_silu(q_ref[rs, :].astype(F32)), zf_ref[rs, :], v_ref[rs, :], lb_ref[...], states, seq_len)
        for h in range(HG_HEADS):
            cs = slice(h * LANE, (h + 1) * LANE)
            for k in range(DEC_GROUP):
                so_ref[grp * DEC_GROUP + k, h] = new[h][k]
            oh_ref[rs, cs] = _hgrn_finish(o[h], og_ref[rs, cs], hgn_ref[:, cs])


def _decode_mixers(zb, zf, lbs3, hgn3, state, stacked, cache_k, cache_v, layer, nseq, nb):
    seq_len = zb.shape[0] // nseq
    W = HG_HEADS * LANE
    XW = XA_HEADS * LANE
    zspec = lambda col: pl.BlockSpec((nb * seq_len, W), lambda i: (i, col))
    vec = pl.BlockSpec((None, 1, W), lambda i: (layer, 0, 0))
    sspec = pl.BlockSpec((None, nb, HG_HEADS, LANE, LANE), lambda i: (layer, i, 0, 0, 0))
    cspec = pl.BlockSpec((None, nb) + cache_k.shape[2:], lambda i: (layer, i, 0, 0))
    in_specs = [zspec(ZB_Q), zspec(ZF_F), zspec(ZB_V), zspec(ZB_OG), vec, vec, sspec,
                pl.BlockSpec((nb * seq_len, XW), lambda i: (i, ZF_X)), cspec, cspec]
    args = [zb, zf, zb, zb, lbs3, hgn3, state, zf, cache_k, cache_v]
    aliases = {}
    if stacked is not None:
        in_specs.append(pl.BlockSpec(memory_space=pl.ANY))
        args.append(stacked)
        aliases = {len(args) - 1: 1}
    return pl.pallas_call(
        functools.partial(_hgrn_decode_kernel, seq_len),
        grid=(nseq // nb,),
        in_specs=in_specs,
        out_specs=[pl.BlockSpec((nb * seq_len, W), lambda i: (i, 0)), sspec,
                   pl.BlockSpec((nb * seq_len, XW), lambda i: (i, 0))],
        out_shape=[
            jax.ShapeDtypeStruct((nseq * seq_len, W), BF16),
            jax.ShapeDtypeStruct(state.shape, F32),
            jax.ShapeDtypeStruct((nseq * seq_len, XW), BF16),
        ],
        input_output_aliases=aliases,
        compiler_params=_params(("parallel",), 58),
        name="decode_mixers",
    )(*args)


def _pool_prompt_kernel(u_ref, wp_ref, sc_ref, op_ref, nb_ref):
    u = u_ref[...]
    T = u.shape[0]
    row = lax.broadcasted_iota(jnp.int32, (T, LANE), 0)
    for g, w in enumerate(POOL_WINDOWS):
        cs = slice(g * LANE, (g + 1) * LANE)
        ug = u[:, cs]
        s = ug
        d = 1
        while d < w:
            s = s + jnp.where(row >= d, pltpu.roll(s, d, 0), 0.0)
            d *= 2
        cnt = jnp.minimum(row + 1, w).astype(F32)
        dv = s / cnt - ug
        y = _dot(dv.astype(BF16), wp_ref[g].astype(BF16)) * sc_ref[:, cs]
        op_ref[:, cs] = y.astype(BF16)
    tail = u_ref[T - 16:T, :]
    nb_ref[...] = pltpu.roll(tail, 15, 0)[0:POOL_BUF, :]


def _pool_prompt(zf, w_pool, scale3, layer, batch):
    T = zf.shape[0] // batch
    G = len(POOL_WINDOWS)
    W = G * LANE
    return pl.pallas_call(
        _pool_prompt_kernel,
        grid=(batch,),
        in_specs=[
            pl.BlockSpec((T, W), lambda b: (b, ZF_U)),
            pl.BlockSpec((None, G, LANE, LANE), lambda b: (layer, 0, 0, 0)),
            pl.BlockSpec((None, 1, W), lambda b: (layer, 0, 0)),
        ],
        out_specs=[
            pl.BlockSpec((T, W), lambda b: (b, 0)),
            pl.BlockSpec((None, POOL_BUF, W), lambda b: (b, 0, 0)),
        ],
        out_shape=[
            jax.ShapeDtypeStruct((batch * T, W), BF16),
            jax.ShapeDtypeStruct((batch, POOL_BUF, W), F32),
        ],
        compiler_params=_params(("parallel",), 48),
        name="pool_prompt",
    )(zf, w_pool, scale3)


def _pool_decode_kernel(u_ref, buf_ref, wp_ref, sc_ref, op_ref, nb_ref):
    steps = u_ref.shape[0]
    for g, w in enumerate(POOL_WINDOWS):
        cs = slice(g * LANE, (g + 1) * LANE)
        wg = wp_ref[g].astype(BF16)
        for t in range(steps):
            n_u = min(t + 1, w)
            acc = u_ref[t, :, cs]
            for j in range(t - n_u + 1, t):
                acc = acc + u_ref[j, :, cs]
            for i in range(POOL_BUF - (w - n_u), POOL_BUF):
                acc = acc + buf_ref[i, :, cs]
            dv = acc * (1.0 / w) - u_ref[t, :, cs]
            op_ref[t, :, cs] = _dot(dv.astype(BF16), wg) * sc_ref[:, cs]
    for i in range(POOL_BUF - steps):
        nb_ref[i] = buf_ref[i + steps]
    for t in range(steps):
        nb_ref[POOL_BUF - steps + t] = u_ref[t]


def _pool_decode(u_t, buf_t, w_pool, scale3, layer):
    steps, nseq, W = u_t.shape
    G = len(POOL_WINDOWS)
    return pl.pallas_call(
        _pool_decode_kernel,
        grid=(1,),
        in_specs=[
            pl.BlockSpec((steps, nseq, W), lambda i: (0, 0, 0)),
            pl.BlockSpec((POOL_BUF, nseq, W), lambda i: (0, 0, 0)),
            pl.BlockSpec((None, G, LANE, LANE), lambda i: (layer, 0, 0, 0)),
            pl.BlockSpec((None, 1, W), lambda i: (layer, 0, 0)),
        ],
        out_specs=[
            pl.BlockSpec((steps, nseq, W), lambda i: (0, 0, 0)),
            pl.BlockSpec((POOL_BUF, nseq, W), lambda i: (0, 0, 0)),
        ],
        out_shape=[
            jax.ShapeDtypeStruct((steps, nseq, W), F32),
            jax.ShapeDtypeStruct((POOL_BUF, nseq, W), F32),
        ],
        compiler_params=_params(("arbitrary",), 32),
        name="pool_decode",
    )(u_t, buf_t, w_pool, scale3)


def _softmax_rows(s):
    e = jnp.exp(s - jnp.max(s, axis=-1, keepdims=True))
    return e / jnp.sum(e, axis=-1, keepdims=True)


def _xattn_prompt_kernel(q_ref, k_ref, v_ref, o_ref):
    scale = LANE ** -0.5
    for h in range(XA_HEADS):
        cs = slice(h * LANE, (h + 1) * LANE)
        s = _dot_nt(q_ref[:, cs].astype(BF16), k_ref[:, cs].astype(BF16)) * scale
        p = _softmax_rows(s)
        o_ref[:, cs] = _dot(p.astype(BF16), v_ref[:, cs].astype(BF16)).astype(BF16)


def _xattn_prompt(zf, mk, mv, batch, tq):
    T = zf.shape[0] // batch
    W = XA_HEADS * LANE
    nq = T // tq
    n_mem = mk.shape[1]
    mspec = pl.BlockSpec((None, n_mem, W), lambda b, i: (b, 0, 0))
    return pl.pallas_call(
        _xattn_prompt_kernel,
        grid=(batch, nq),
        in_specs=[pl.BlockSpec((tq, W), lambda b, i: (b * nq + i, ZF_X)), mspec, mspec],
        out_specs=pl.BlockSpec((tq, W), lambda b, i: (b * nq + i, 0)),
        out_shape=jax.ShapeDtypeStruct((batch * T, W), BF16),
        compiler_params=_params(("parallel", "parallel"), 48),
        name="xattn_prompt",
    )(zf, mk, mv)


def _xattn_decode_kernel(seq_len, q_ref, k_ref, v_ref, o_ref):
    scale = LANE ** -0.5
    rows = DEC_GROUP * seq_len
    lseq = seq_len.bit_length() - 1
    lrows = rows.bit_length() - 1
    nk = k_ref.shape[1]
    row_s = lax.broadcasted_iota(jnp.int32, (XA_HEADS * rows, nk), 0)
    col_s = lax.broadcasted_iota(jnp.int32, (XA_HEADS * rows, nk), 1)
    own_head = (col_s & (XA_HEADS - 1)) == (row_s >> lrows)
    seq_s = (row_s & (rows - 1)) >> lseq
    seq_o = (lax.broadcasted_iota(jnp.int32, (XA_HEADS * rows, LANE), 0) & (rows - 1)) >> lseq
    for grp in range(k_ref.shape[0] // DEC_GROUP):
        rs = slice(grp * rows, (grp + 1) * rows)
        qb = jnp.concatenate([q_ref[rs, h * LANE:(h + 1) * LANE] for h in range(XA_HEADS)], axis=0).astype(BF16)
        s = jnp.full((XA_HEADS * rows, nk), -jnp.inf, F32)
        for k in range(DEC_GROUP):
            sk = _dot_nt(qb, k_ref[grp * DEC_GROUP + k].astype(BF16))
            s = jnp.where(own_head & (seq_s == k), sk, s)
        p = _softmax_rows(s * scale).astype(BF16)
        o = jnp.zeros((XA_HEADS * rows, LANE), F32)
        for k in range(DEC_GROUP):
            o = jnp.where(seq_o == k, _dot(p, v_ref[grp * DEC_GROUP + k].astype(BF16)), o)
        for h in range(XA_HEADS):
            o_ref[rs, h * LANE:(h + 1) * LANE] = o[h * rows:(h + 1) * rows, :].astype(BF16)


def _merge_kernel(x_ref, oh_ref, op_ref, ox_ref, g0_ref, g1_ref, g2_ref, wb_ref, wo_ref, o_ref, *rest):
    y_ref = rest[-1]
    wbb_ref, wob_ref = rest[:2] if len(rest) == 3 else (None, None)
    j = pl.program_id(1)
    nj, _, tn = y_ref.shape

    @pl.when(j < nj)
    def _():
        wh = oh_ref.shape[1]
        wp = op_ref.shape[1]
        oh, op, ox = oh_ref[...].astype(BF16), op_ref[...].astype(BF16), ox_ref[...].astype(BF16)
        cw = min(MXU_COLS, tn)
        for c in range(tn // cw):
            cs = slice(c * cw, (c + 1) * cw)
            w = wb_ref[:, cs].astype(BF16)
            if wbb_ref is not None:
                wbb_ref[:, cs] = w
            y = _sigmoid(g0_ref[:, cs].astype(F32)) * _dot(oh, w[0:wh, :])
            y += _sigmoid(g1_ref[:, cs].astype(F32)) * _dot(op, w[wh:wh + wp, :])
            y += _sigmoid(g2_ref[:, cs].astype(F32)) * _dot(ox, w[wh + wp:, :])
            y_ref[j, :, cs] = y.astype(BF16)

    @pl.when(j >= nj)
    def _():
        acc = x_ref[...]
        for c in range(nj):
            rs = slice(c * tn, (c + 1) * tn)
            w = wo_ref[rs, :].astype(BF16)
            if wob_ref is not None:
                wob_ref[rs, :] = w
            acc += _dot(y_ref[c], w)
        o_ref[...] = acc


def _merge_resident_kernel(x_ref, oh_ref, op_ref, ox_ref, g_ref, wb_ref, wo_ref, o_ref, y_ref):
    D = x_ref.shape[1]
    wh = oh_ref.shape[1]
    wp = op_ref.shape[1]
    oh, op, ox = oh_ref[...].astype(BF16), op_ref[...].astype(BF16), ox_ref[...].astype(BF16)
    cw = min(2 * MXU_COLS, D)
    for c in range(D // cw):
        cs = slice(c * cw, (c + 1) * cw)
        gate = lambda k: _sigmoid(g_ref[:, k * D + c * cw:k * D + (c + 1) * cw].astype(F32))
        y = gate(0) * _dot(oh, wb_ref[0:wh, cs])
        y += gate(1) * _dot(op, wb_ref[wh:wh + wp, cs])
        y += gate(2) * _dot(ox, wb_ref[wh + wp:, cs])
        y_ref[:, cs] = y.astype(BF16)
    for c in range(D // cw):
        cs = slice(c * cw, (c + 1) * cw)
        o_ref[:, cs] = x_ref[:, cs] + _dot(y_ref[...], wo_ref[:, cs])


def _merge_resident(x, oh, op, ox, zg, wb, wo, tm):
    T, D = x.shape
    row = lambda a: pl.BlockSpec((tm, a.shape[1]), lambda i: (i, 0))
    held = lambda a: pl.BlockSpec(a.shape, lambda i: (0, 0), pipeline_mode=pl.Buffered(1))
    return pl.pallas_call(
        _merge_resident_kernel,
        grid=(T // tm,),
        in_specs=[row(x), row(oh), row(op), row(ox), row(zg), held(wb), held(wo)],
        out_specs=row(x),
        out_shape=jax.ShapeDtypeStruct((T, D), F32),
        scratch_shapes=[pltpu.VMEM((tm, D), BF16)],
        compiler_params=_params(("parallel",), 56),
        name="merge_resident",
    )(x, oh, op, ox, zg, wb, wo)


def _merge(x, oh, op, ox, zb, wb, wo, layer, tm, tn, emit=False):
    T, D = x.shape
    assert not emit or T == tm
    nj = D // tn
    first = lambda j: jnp.minimum(j, nj - 1)
    second = lambda j: jnp.maximum(j - nj, 0)
    full = lambda a: pl.BlockSpec((tm, a.shape[1]), lambda i, j: (i, 0))
    gate = lambda k: pl.BlockSpec((tm, tn), lambda i, j: (i, k * nj + first(j)))
    wb_idx = lambda i, j: (0, first(j))
    wo_idx = lambda i, j: (0, second(j))
    rows_b = wb.shape[-2]
    out_specs = [pl.BlockSpec((tm, tn), lambda i, j: (i, second(j)))]
    out_shape = [jax.ShapeDtypeStruct((T, D), F32)]
    if emit:
        out_specs += [pl.BlockSpec((rows_b, tn), wb_idx), pl.BlockSpec((D, tn), wo_idx)]
        out_shape += [jax.ShapeDtypeStruct((rows_b, D), BF16), jax.ShapeDtypeStruct((D, D), BF16)]
    out = pl.pallas_call(
        _merge_kernel,
        grid=(T // tm, 2 * nj),
        in_specs=[
            pl.BlockSpec((tm, tn), lambda i, j: (i, second(j))),
            full(oh), full(op), full(ox), gate(0), gate(1), gate(2),
            _layer_spec(wb, layer, (rows_b, tn), wb_idx),
            _layer_spec(wo, layer, (D, tn), wo_idx),
        ],
        out_specs=out_specs,
        out_shape=out_shape,
        scratch_shapes=[pltpu.VMEM((nj, tm, tn), BF16)],
        compiler_params=_params(("parallel", "arbitrary"), 48),
        name="merge_out",
    )(x, oh, op, ox, zb, zb, zb, wb, wo)
    return out if emit else out[0]


def kernel(x_prompt, x_sample, state_hgrn, state_pool, cache_mem_k, cache_mem_v, mem_prompt, ffn1_norm, ffn1_w1, ffn1_w3, ffn1_w2, mix_norm, w_in, lb_logits, hg_norm, w_pool, pool_scale, mem_norm, w_mk, w_mv, w_branch, w_out, ffn2_norm, ffn2_w1, ffn2_w3, ffn2_w2, final_norm):
    B, T, D = x_prompt.shape
    nseq, steps, _ = x_sample.shape
    depth = w_in.shape[0]
    n_mem = mem_prompt.shape[1]
    xw = XA_HEADS * LANE
    pw = len(POOL_WINDOWS) * LANE

    tm_p = min(1024, B * T)
    tm_s = min(512, nseq * steps)

    vec3 = lambda a: a.reshape(a.shape[0], 1, a.shape[1])
    f1n, f2n = vec3(ffn1_norm), vec3(ffn2_norm)
    mix3, mem3, hgn3, psc3 = vec3(mix_norm), vec3(mem_norm), vec3(hg_norm), vec3(pool_scale)
    lbs3 = vec3(_lower_bounds(lb_logits))

    xp = x_prompt.reshape(B * T, D)
    xs = x_sample.reshape(nseq * steps, D)
    mem2 = mem_prompt.reshape(B * n_mem, D)
    ck = cache_mem_k.reshape(depth, nseq, n_mem * XA_HEADS, LANE)
    cv = cache_mem_v.reshape(depth, nseq, n_mem * XA_HEADS, LANE)

    tf = _tile(ffn1_w1.shape[-1], 512)
    tf_s = _tile(ffn1_w1.shape[-1], 256)
    tn_in = _tile(math.gcd(HG_HEADS * LANE, N_BRANCH * D), 1024)
    tn_mrg = _tile(D, 512)
    tm_mem = min(512, B * n_mem)
    hs_p, pb_p, mk_p, mv_p, pb_s = [], [], [], [], []
    hs_s = None
    for l in range(depth):
        mk = _rms_matmul(mem2, mem3, w_mk, l, tm_mem, xw)
        mv = _rms_matmul(mem2, mem3, w_mv, l, tm_mem, xw)
        mk_p.append(mk.reshape(B, n_mem, XA_HEADS, LANE))
        mv_p.append(mv.reshape(B, n_mem, XA_HEADS, LANE))

        xs, w1b, w3b, w2b = _ffn(xs, f1n, ffn1_w1, ffn1_w3, ffn1_w2, l, tm_s, tf_s)
        xp = _ffn(xp, f1n, w1b, w3b, w2b, l, tm_p, tf)

        zsb, zsf, zsg, w_in_b = _in_proj(xs, mix3, w_in, l, tm_s, tn_in, emit=True)
        zpb, zpf, zpg = _in_proj(xp, mix3, w_in_b, l, tm_p, tn_in)

        ohp, sp = _hgrn_prompt(zpb, zpf, lbs3, hgn3, l, B)
        ohs, hs_s, oxs = _decode_mixers(zsb, zsf, lbs3, hgn3, state_hgrn, hs_s, ck, cv, l, nseq, 8)
        hs_p.append(sp)

        opp, bp = _pool_prompt(zpf, w_pool, psc3, l, B)
        pb_p.append(bp)
        u_t = zsf[:, ZF_U * pw:(ZF_U + 1) * pw].reshape(nseq, steps, pw).transpose(1, 0, 2)
        buf_t = state_pool[l].transpose(1, 0, 2)
        ops_t, nb_t = _pool_decode(u_t, buf_t, w_pool, psc3, l)
        ops = ops_t.transpose(1, 0, 2).reshape(nseq * steps, pw)
        pb_s.append(nb_t.transpose(1, 0, 2))

        oxp = _xattn_prompt(zpf, mk.reshape(B, n_mem, xw), mv.reshape(B, n_mem, xw), B, min(1024, T))

        xs, w_br_b, w_out_b = _merge(xs, ohs, ops, oxs, zsg, w_branch, w_out, l, tm_s, tn_mrg, emit=True)
        xp = _merge_resident(xp, ohp, opp, oxp, zpg, w_br_b, w_out_b, min(512, B * T))

        fn2 = final_norm.reshape(1, D) if l == depth - 1 else None
        xs, w1b, w3b, w2b = _ffn(xs, f2n, ffn2_w1, ffn2_w3, ffn2_w2, l, tm_s, tf_s, fn2)
        xp = _ffn(xp, f2n, w1b, w3b, w2b, l, tm_p, tf, fn2)

    y_prompt = xp.reshape(B, T, D)
    y_sample = xs.reshape(nseq, steps, D)
    return (y_prompt, y_sample, jnp.stack(hs_p), jnp.stack(pb_p), jnp.stack(mk_p), jnp.stack(mv_p),
            hs_s, jnp.stack(pb_s))
```

```python
import functools
import math

import jax
import jax.numpy as jnp
from jax import lax
from jax.experimental import pallas as pl
from jax.experimental.pallas import tpu as pltpu

F32 = jnp.float32
BF16 = jnp.bfloat16
EPS = 1e-6

LANE = 128
SUBLANE = 8
MXU_COLS = 256
HG_HEADS = 8
POOL_WINDOWS = (2, 4, 8, 16)
POOL_BUF = max(POOL_WINDOWS) - 1
XA_HEADS = 4
N_BRANCH = 3

COL_Q, COL_F, COL_I, COL_OG = 0, HG_HEADS, 2 * HG_HEADS, 3 * HG_HEADS
COL_U = 4 * HG_HEADS
COL_X = COL_U + len(POOL_WINDOWS)
COL_G = COL_X + XA_HEADS
ZB_Q, ZB_V, ZB_OG = 0, 1, 2
ZF_F = 0
ZF_U, ZF_X = 2, 3

HGRN_CHUNK = 128
HGRN_STEP_CHUNKS = 2
DEC_GROUP = 4


def _tile(n, preferred):
    t = preferred
    while n % t:
        t -= LANE
    return t


def _params(semantics, vmem_mib):
    return pltpu.CompilerParams(dimension_semantics=semantics, vmem_limit_bytes=vmem_mib * 1024 * 1024)


def _rms(x, g):
    return x * lax.rsqrt(jnp.mean(x * x, axis=-1, keepdims=True) + EPS) * g


def _sigmoid(x):
    return 0.5 * jnp.tanh(0.5 * x) + 0.5


def _silu(x):
    return x * _sigmoid(x)


def _dot(a, b):
    return jnp.dot(a, b, preferred_element_type=F32)


def _dot_nt(a, b):
    return lax.dot_general(a, b, (((1,), (1,)), ((), ())), preferred_element_type=F32)


def _dot_tn(a, b):
    return lax.dot_general(a, b, (((0,), (0,)), ((), ())), preferred_element_type=F32)


def _ffn_kernel(final, cast, side, x_ref, g_ref, w1_ref, w3_ref, w2_ref, *rest):
    rest = list(rest)
    xn_ref = rest.pop()
    fn_ref = rest.pop(0) if final else None
    side_in = [rest.pop(0) for _ in range(3)] if side else []
    o_ref = rest.pop(0)
    if cast:
        src = (w1_ref, w3_ref, w2_ref)
        w1_ref, w3_ref, w2_ref = [rest.pop(0) for _ in range(3)]
        for dst, s in zip((w1_ref, w3_ref, w2_ref), src):
            dst[...] = s[...].astype(BF16)
    for s, dst in zip(side_in, rest):
        dst[...] = s[...].astype(BF16)
    j = pl.program_id(1)

    @pl.when(j == 0)
    def _():
        x = x_ref[...]
        xn_ref[...] = _rms(x, g_ref[...]).astype(BF16)
        o_ref[...] = x

    xn = xn_ref[...]
    h = (_silu(_dot(xn, w1_ref[...])) * _dot(xn, w3_ref[...])).astype(BF16)
    cw = min(w2_ref.shape[0], o_ref.shape[1])
    for n in range(o_ref.shape[1] // cw):
        cs = slice(n * cw, (n + 1) * cw)
        o_ref[:, cs] += 0.5 * _dot(h, w2_ref[:, cs])

    if final:
        @pl.when(j == pl.num_programs(1) - 1)
        def _():
            o_ref[...] = _rms(o_ref[...], fn_ref[...])


def _layer_spec(w, layer, block, index):
    if w.ndim == 2:
        return pl.BlockSpec(block, index)
    return pl.BlockSpec((None,) + block, lambda i, j: (layer,) + index(i, j))


def _ffn(x, g3, w1, w3, w2, layer, tm, tf, final_g=None, side=None):
    T, D = x.shape
    F = w1.shape[-1]
    cast = w1.dtype != BF16
    final = final_g is not None
    assert not cast or T == tm
    ni, nj = T // tm, F // tf
    col = lambda i, j: (0, j)
    row = lambda i, j: (j, 0)
    out_specs = [pl.BlockSpec((tm, D), lambda i, j: (i, 0))]
    out_shape = [jax.ShapeDtypeStruct((T, D), F32)]
    wshapes = [jax.ShapeDtypeStruct((D, F), BF16), jax.ShapeDtypeStruct((D, F), BF16),
               jax.ShapeDtypeStruct((F, D), BF16)]
    if cast:
        out_specs += [pl.BlockSpec((D, tf), col), pl.BlockSpec((D, tf), col), pl.BlockSpec((tf, D), row)]
        out_shape += wshapes
    in_specs = [
        pl.BlockSpec((tm, D), lambda i, j: (i, 0)),
        pl.BlockSpec((None, 1, D), lambda i, j: (layer, 0, 0)),
        _layer_spec(w1, layer, (D, tf), col),
        _layer_spec(w3, layer, (D, tf), col),
        _layer_spec(w2, layer, (tf, D), row),
    ]
    args = [x, g3, w1, w3, w2]
    if final:
        in_specs.append(pl.BlockSpec((1, D), lambda i, j: (0, 0)))
        args.append(final_g)
    if side is not None:
        s1, s3, s2, side_layer = side
        assert D % (ni * LANE) == 0
        rb = D // ni
        up, down = (lambda i, j: (i, j)), (lambda i, j: (j, i))
        in_specs += [_layer_spec(s1, side_layer, (rb, tf), up), _layer_spec(s3, side_layer, (rb, tf), up),
                     _layer_spec(s2, side_layer, (tf, rb), down)]
        args += [s1, s3, s2]
        out_specs += [pl.BlockSpec((rb, tf), up), pl.BlockSpec((rb, tf), up), pl.BlockSpec((tf, rb), down)]
        out_shape += wshapes
    out = pl.pallas_call(
        functools.partial(_ffn_kernel, final, cast, side is not None),
        grid=(ni, nj),
        in_specs=in_specs,
        out_specs=out_specs,
        out_shape=out_shape,
        scratch_shapes=[pltpu.VMEM((tm, D), BF16)],
        compiler_params=_params(("parallel", "arbitrary"), 62),
        name="ffn_cast" if cast else "ffn",
    )(*args)
    return out if len(out) > 1 else out[0]


def _rms_matmul_kernel(x_ref, g_ref, w_ref, o_ref, *rest):
    xn_ref = rest[-1]

    @pl.when(pl.program_id(1) == 0)
    def _():
        xn_ref[...] = _rms(x_ref[...], g_ref[...]).astype(BF16)

    w = w_ref[...].astype(BF16)
    if len(rest) == 2:
        rest[0][...] = w
    o_ref[...] = _dot(xn_ref[...], w)


def _rms_matmul(x, g3, w, layer, tm, tn, emit=False):
    T, D = x.shape
    N = w.shape[-1]
    assert not emit or T == tm
    col = lambda i, j: (0, j)
    out_specs = [pl.BlockSpec((tm, tn), lambda i, j: (i, j))]
    out_shape = [jax.ShapeDtypeStruct((T, N), F32)]
    if emit:
        out_specs.append(pl.BlockSpec((D, tn), col))
        out_shape.append(jax.ShapeDtypeStruct((D, N), BF16))
    out = pl.pallas_call(
        _rms_matmul_kernel,
        grid=(T // tm, N // tn),
        in_specs=[
            pl.BlockSpec((tm, D), lambda i, j: (i, 0)),
            pl.BlockSpec((None, 1, D), lambda i, j: (layer, 0, 0)),
            _layer_spec(w, layer, (D, tn), col),
        ],
        out_specs=out_specs,
        out_shape=out_shape,
        scratch_shapes=[pltpu.VMEM((tm, D), BF16)],
        compiler_params=_params(("parallel", "arbitrary"), 48),
        name="rms_matmul",
    )(x, g3, w)
    return out if emit else out[0]


def _runs(seq):
    runs = []
    for j, v in enumerate(seq):
        if runs and runs[-1][2] == v - j:
            runs[-1][1] = j + 1
        else:
            runs.append([j, j + 1, v - j])
    return [tuple(r) for r in runs]


def _in_runs(j, runs):
    hit = None
    for lo, hi, _ in runs:
        c = (j >= lo) & (j < hi)
        hit = c if hit is None else hit | c
    return hit


def _lookup(j, runs):
    out = 0
    for lo, hi, off in runs:
        out = out + jnp.where((j >= lo) & (j < hi), j + off, 0)
    return out


def _in_proj_plan(d_model, tn):
    t = lambda blocks: blocks * LANE // tn
    nq, nux, ng = t(HG_HEADS), t(len(POOL_WINDOWS) + XA_HEADS), N_BRANCH * d_model // tn
    src = lambda start, n: list(range(t(start), t(start) + n))
    order = src(COL_Q, nq) + src(COL_I, nq) + src(COL_OG, nq) + src(COL_F, nq) + src(COL_U, nux) + src(COL_G, ng)
    nh, n32 = 3 * nq, nq + nux
    kinds = {"f32": [(nh, nh + n32, 0)]}
    return _runs(order), kinds, nh, n32


def _in_proj_kernel(kinds, x_ref, g_ref, w_ref, zh_ref, zf_ref, zg_ref, *rest):
    xn_ref = rest[-1]
    j = pl.program_id(1)

    @pl.when(j == 0)
    def _():
        xn_ref[...] = _rms(x_ref[...], g_ref[...]).astype(BF16)

    xn = xn_ref[...]
    tn = w_ref.shape[1]
    cw = min(MXU_COLS, tn)
    zs = []
    for c in range(tn // cw):
        cs = slice(c * cw, (c + 1) * cw)
        w = w_ref[:, cs].astype(BF16)
        if len(rest) == 2:
            rest[0][:, cs] = w
        z = _dot(xn, w)
        zs.append(z)
        zh_ref[:, cs] = z.astype(BF16)
        zg_ref[:, cs] = z.astype(BF16)

    @pl.when(_in_runs(j, kinds["f32"]))
    def _():
        for c, z in enumerate(zs):
            zf_ref[:, c * cw:(c + 1) * cw] = z


def _in_proj(x, g3, w, layer, tm, tn, emit=False):
    T, D = x.shape
    N = w.shape[-1]
    assert not emit or T == tm
    order, kinds, nh, n32 = _in_proj_plan(D, tn)
    nsteps = N // tn
    wcol = lambda i, j: (0, _lookup(j, order))
    out_specs = [
        pl.BlockSpec((tm, tn), lambda i, j: (i, jnp.minimum(j, nh))),
        pl.BlockSpec((tm, tn), lambda i, j: (i, jnp.clip(j - nh, 0, n32 - 1))),
        pl.BlockSpec((tm, tn), lambda i, j: (i, jnp.maximum(j - nh - n32, 0))),
    ]
    out_shape = [
        jax.ShapeDtypeStruct((T, (nh + 1) * tn), BF16),
        jax.ShapeDtypeStruct((T, n32 * tn), F32),
        jax.ShapeDtypeStruct((T, (nsteps - nh - n32) * tn), BF16),
    ]
    if emit:
        out_specs.append(pl.BlockSpec((D, tn), wcol))
        out_shape.append(jax.ShapeDtypeStruct((D, N), BF16))
    return pl.pallas_call(
        functools.partial(_in_proj_kernel, kinds),
        grid=(T // tm, nsteps),
        in_specs=[
            pl.BlockSpec((tm, D), lambda i, j: (i, 0)),
            pl.BlockSpec((None, 1, D), lambda i, j: (layer, 0, 0)),
            _layer_spec(w, layer, (D, tn), wcol),
        ],
        out_specs=out_specs,
        out_shape=out_shape,
        scratch_shapes=[pltpu.VMEM((tm, D), BF16)],
        compiler_params=_params(("parallel", "arbitrary"), 56),
        name="in_proj",
    )(x, g3, w)


def _lower_bound_kernel(lg_ref, o_ref):
    lg = lg_ref[...]
    depth = lg.shape[0]
    rows = [lg[i:i + 1, :] for i in range(depth)]
    m = rows[0]
    for r in rows[1:]:
        m = jnp.maximum(m, r)
    e = [jnp.exp(r - m) for r in rows]
    tot = e[0]
    for v in e[1:]:
        tot = tot + v
    c = e[0] / tot
    first = c
    o_ref[0:1, :] = c - first
    for i in range(1, depth):
        c = c + e[i] / tot
        o_ref[i:i + 1, :] = c - first


def _lower_bounds(lb_logits):
    return pl.pallas_call(
        _lower_bound_kernel,
        out_shape=jax.ShapeDtypeStruct(lb_logits.shape, F32),
        name="hgrn_lower_bounds",
    )(lb_logits)


def _neg_abs(x):
    bits = lax.bitcast_convert_type(x, jnp.uint32) | jnp.uint32(0x80000000)
    return lax.bitcast_convert_type(bits, F32)


def _group_ref_row(b, s):
    rows, width = b.shape
    gsz = 2 * s
    if gsz >= 8:
        parts = [jnp.broadcast_to(b[i * gsz + s - 1:i * gsz + s, :], (gsz, width)) for i in range(rows // gsz)]
        return parts[0] if len(parts) == 1 else jnp.concatenate(parts, axis=0)
    pos = lax.broadcasted_iota(jnp.int32, b.shape, 0) & (gsz - 1)
    out = b
    for p in range(gsz):
        d = p - (s - 1)
        if d != 0:
            out = jnp.where(pos == p, pltpu.roll(b, d % rows, 0), out)
    return out


def _hgrn_wide(q, zf, vb, lb, states, seg):
    C, W = q.shape
    heads = [slice(h * LANE, (h + 1) * LANE) for h in range(W // LANE)]
    nseg = C // seg
    fg = lb + (1.0 - lb) * jax.nn.sigmoid(zf)
    kk = 1.0 - fg
    g = jnp.log2(fg)

    row = lax.broadcasted_iota(jnp.int32, (C, W), 0)
    rr = lax.broadcasted_iota(jnp.int32, (C, C), 0)
    cc = lax.broadcasted_iota(jnp.int32, (C, C), 1)
    lseg = seg.bit_length() - 1
    tri = jnp.where((cc <= rr) & ((rr >> lseg) == (cc >> lseg)), 1.0, 0.0).astype(F32)
    b = jnp.dot(tri, g, precision=lax.Precision.HIGHEST, preferred_element_type=F32)

    qb, kb = q.astype(BF16), kk.astype(BF16)
    a = [jnp.where(rr == cc, _dot_nt(qb[:, hs], kb[:, hs]), 0.0) for hs in heads]
    s = seg // 2
    while s >= 1:
        ls = s.bit_length() - 1
        same = (rr >> (ls + 1)) == (cc >> (ls + 1))
        if s == 1:
            right = (row & 1) == 1
            qs, ks = jnp.where(right, q * fg, 0.0).astype(BF16), jnp.where(right, 0.0, kk).astype(BF16)
            for h, hs in enumerate(heads):
                a[h] = a[h] + jnp.where(same, _dot_nt(qs[:, hs], ks[:, hs]), 0.0)
            break
        e = jnp.exp2(_neg_abs(b - _group_ref_row(b, s)))
        if s % SUBLANE == 0:
            nblk = C // s
            zero = jnp.zeros((s, W), F32)
            blk = lambda x, i: x[i * s:(i + 1) * s, :]
            qs = jnp.concatenate([blk(q, i) * blk(e, i) if i % 2 else zero for i in range(nblk)], axis=0).astype(BF16)
            ks = jnp.concatenate([zero if i % 2 else blk(kk, i) * blk(e, i) for i in range(nblk)], axis=0).astype(BF16)
            for h, hs in enumerate(heads):
                p = _dot_nt(qs[:, hs], ks[:, hs])
                pieces = []
                for i in range(nblk):
                    if i % 2 == 0:
                        pieces.append(blk(a[h], i))
                    elif 2 * s == C:
                        pieces.append(blk(a[h], i) + blk(p, i))
                    else:
                        pieces.append(blk(a[h], i) + jnp.where(blk(same, i), blk(p, i), 0.0))
                a[h] = jnp.concatenate(pieces, axis=0)
        else:
            right = ((row >> ls) & 1) == 1
            eq = jnp.where(right, e, 0.0)
            qs, ks = (q * eq).astype(BF16), (kk * (e - eq)).astype(BF16)
            for h, hs in enumerate(heads):
                a[h] = a[h] + jnp.where(same, _dot_nt(qs[:, hs], ks[:, hs]), 0.0)
        s //= 2
    o = [_dot(a[h].astype(BF16), vb[:, hs]) for h, hs in enumerate(heads)]

    qe = q * jnp.exp2(b)
    new_states = [[] for _ in heads]
    for k in range(nseg):
        bl = b[k * seg + seg - 1:k * seg + seg, :]
        if nseg == 1:
            qk = qe
            ke = kk * jnp.exp2(bl - b)
        else:
            mine = (row >> lseg) == k
            qk = jnp.where(mine, qe, 0.0)
            ke = jnp.where(mine, kk * jnp.exp2(jnp.where(mine, bl - b, 0.0)), 0.0)
        qkb, keb, ebl = qk.astype(BF16), ke.astype(BF16), jnp.exp2(bl)
        for h, hs in enumerate(heads):
            st = states[h][k]
            o[h] = o[h] + _dot(qkb[:, hs], st.astype(BF16))
            decay = jnp.transpose(jnp.broadcast_to(ebl[:, hs], (LANE, LANE)))
            new_states[h].append(st * decay + _dot_tn(keb[:, hs], vb[:, hs]))
    return o, new_states


def _hgrn_finish(o, zog, hgn):
    return (_rms(o, hgn) * _silu(zog.astype(F32))).astype(BF16)


def _hgrn_prompt_kernel(q_ref, zf_ref, v_ref, og_ref, lb_ref, hgn_ref, oh_ref, s_ref):
    @pl.when(pl.program_id(1) == 0)
    def _():
        s_ref[...] = jnp.zeros_like(s_ref)

    states = [[s_ref[h]] for h in range(HG_HEADS)]
    for sub in range(q_ref.shape[0] // HGRN_CHUNK):
        rs = slice(sub * HGRN_CHUNK, (sub + 1) * HGRN_CHUNK)
        q = _silu(q_ref[rs, :].astype(F32))
        o, states = _hgrn_wide(q, zf_ref[rs, :], v_ref[rs, :], lb_ref[...], states, HGRN_CHUNK)
        for h in range(HG_HEADS):
            cs = slice(h * LANE, (h + 1) * LANE)
            oh_ref[rs, cs] = _hgrn_finish(o[h], og_ref[rs, cs], hgn_ref[:, cs])
    for h in range(HG_HEADS):
        s_ref[h] = states[h][0]


def _hgrn_prompt(zb, zf, lbs3, hgn3, layer, batch):
    T = zb.shape[0] // batch
    C = HGRN_STEP_CHUNKS * HGRN_CHUNK
    nc = T // C
    W = HG_HEADS * LANE
    zspec = lambda col: pl.BlockSpec((C, W), lambda b, c: (b * nc + c, col))
    vec = pl.BlockSpec((None, 1, W), lambda b, c: (layer, 0, 0))
    return pl.pallas_call(
        _hgrn_prompt_kernel,
        grid=(batch, nc),
        in_specs=[zspec(ZB_Q), zspec(ZF_F), zspec(ZB_V), zspec(ZB_OG), vec, vec],
        out_specs=[
            pl.BlockSpec((C, W), lambda b, c: (b * nc + c, 0)),
            pl.BlockSpec((None, HG_HEADS, LANE, LANE), lambda b, c: (b, 0, 0, 0)),
        ],
        out_shape=[
            jax.ShapeDtypeStruct((batch * T, W), BF16),
            jax.ShapeDtypeStruct((batch, HG_HEADS, LANE, LANE), F32),
        ],
        compiler_params=_params(("parallel", "arbitrary"), 32),
        name="hgrn_prompt",
    )(zb, zf, zb, zb, lbs3, hgn3)


def _hgrn_decode_kernel(seq_len, q_ref, zf_ref, v_ref, og_ref, lb_ref, hgn_ref, s_ref, xq_ref, k_ref, vc_ref, *rest):
    oh_ref, so_ref, ox_ref = rest[-3:]
    _xattn_decode_kernel(seq_len, xq_ref, k_ref, vc_ref, ox_ref)
    rows = DEC_GROUP * seq_len
    for grp in range(s_ref.shape[0] // DEC_GROUP):
        rs = slice(grp * rows, (grp + 1) * rows)
        states = [[s_ref[grp * DEC_GROUP + k, h] for k in range(DEC_GROUP)] for h in range(HG_HEADS)]
        o, new = _hgrn_wide(_silu(q_ref[rs, :].astype(F32)), zf_ref[rs, :], v_ref[rs, :], lb_ref[...], states, seq_len)
        for h in range(HG_HEADS):
            cs = slice(h * LANE, (h + 1) * LANE)
            for k in range(DEC_GROUP):
                so_ref[grp * DEC_GROUP + k, h] = new[h][k]
            oh_ref[rs, cs] = _hgrn_finish(o[h], og_ref[rs, cs], hgn_ref[:, cs])


def _decode_mixers(zb, zf, lbs3, hgn3, state, stacked, cache_k, cache_v, layer, nseq, nb):
    seq_len = zb.shape[0] // nseq
    W = HG_HEADS * LANE
    XW = XA_HEADS * LANE
    zspec = lambda col: pl.BlockSpec((nb * seq_len, W), lambda i: (i, col))
    vec = pl.BlockSpec((None, 1, W), lambda i: (layer, 0, 0))
    sspec = pl.BlockSpec((None, nb, HG_HEADS, LANE, LANE), lambda i: (layer, i, 0, 0, 0))
    cspec = pl.BlockSpec((None, nb) + cache_k.shape[2:], lambda i: (layer, i, 0, 0))
    in_specs = [zspec(ZB_Q), zspec(ZF_F), zspec(ZB_V), zspec(ZB_OG), vec, vec, sspec,
                pl.BlockSpec((nb * seq_len, XW), lambda i: (i, ZF_X)), cspec, cspec]
    args = [zb, zf, zb, zb, lbs3, hgn3, state, zf, cache_k, cache_v]
    aliases = {}
    if stacked is not None:
        in_specs.append(pl.BlockSpec(memory_space=pl.ANY))
        args.append(stacked)
        aliases = {len(args) - 1: 1}
    return pl.pallas_call(
        functools.partial(_hgrn_decode_kernel, seq_len),
        grid=(nseq // nb,),
        in_specs=in_specs,
        out_specs=[pl.BlockSpec((nb * seq_len, W), lambda i: (i, 0)), sspec,
                   pl.BlockSpec((nb * seq_len, XW), lambda i: (i, 0))],
        out_shape=[
            jax.ShapeDtypeStruct((nseq * seq_len, W), BF16),
            jax.ShapeDtypeStruct(state.shape, F32),
            jax.ShapeDtypeStruct((nseq * seq_len, XW), BF16),
        ],
        input_output_aliases=aliases,
        compiler_params=_params(("parallel",), 58),
        name="decode_mixers",
    )(*args)


def _pool_prompt_kernel(u_ref, wp_ref, sc_ref, op_ref, nb_ref):
    u = u_ref[...]
    T = u.shape[0]
    row = lax.broadcasted_iota(jnp.int32, (T, LANE), 0)
    for g, w in enumerate(POOL_WINDOWS):
        cs = slice(g * LANE, (g + 1) * LANE)
        ug = u[:, cs]
        s = ug
        d = 1
        while d < w:
            s = s + jnp.where(row >= d, pltpu.roll(s, d, 0), 0.0)
            d *= 2
        cnt = jnp.minimum(row + 1, w).astype(F32)
        dv = s / cnt - ug
        y = _dot(dv.astype(BF16), wp_ref[g].astype(BF16)) * sc_ref[:, cs]
        op_ref[:, cs] = y.astype(BF16)
    tail = u_ref[T - 16:T, :]
    nb_ref[...] = pltpu.roll(tail, 15, 0)[0:POOL_BUF, :]


def _pool_prompt(zf, w_pool, scale3, layer, batch):
    T = zf.shape[0] // batch
    G = len(POOL_WINDOWS)
    W = G * LANE
    return pl.pallas_call(
        _pool_prompt_kernel,
        grid=(batch,),
        in_specs=[
            pl.BlockSpec((T, W), lambda b: (b, ZF_U)),
            pl.BlockSpec((None, G, LANE, LANE), lambda b: (layer, 0, 0, 0)),
            pl.BlockSpec((None, 1, W), lambda b: (layer, 0, 0)),
        ],
        out_specs=[
            pl.BlockSpec((T, W), lambda b: (b, 0)),
            pl.BlockSpec((None, POOL_BUF, W), lambda b: (b, 0, 0)),
        ],
        out_shape=[
            jax.ShapeDtypeStruct((batch * T, W), BF16),
            jax.ShapeDtypeStruct((batch, POOL_BUF, W), F32),
        ],
        compiler_params=_params(("parallel",), 48),
        name="pool_prompt",
    )(zf, w_pool, scale3)


def _pool_decode_kernel(u_ref, buf_ref, wp_ref, sc_ref, op_ref, nb_ref):
    steps = u_ref.shape[0]
    for g, w in enumerate(POOL_WINDOWS):
        cs = slice(g * LANE, (g + 1) * LANE)
        wg = wp_ref[g].astype(BF16)
        for t in range(steps):
            n_u = min(t + 1, w)
            acc = u_ref[t, :, cs]
            for j in range(t - n_u + 1, t):
                acc = acc + u_ref[j, :, cs]
            for i in range(POOL_BUF - (w - n_u), POOL_BUF):
                acc = acc + buf_ref[i, :, cs]
            dv = acc * (1.0 / w) - u_ref[t, :, cs]
            op_ref[t, :, cs] = _dot(dv.astype(BF16), wg) * sc_ref[:, cs]
    for i in range(POOL_BUF - steps):
        nb_ref[i] = buf_ref[i + steps]
    for t in range(steps):
        nb_ref[POOL_BUF - steps + t] = u_ref[t]


def _pool_decode(u_t, buf_t, w_pool, scale3, layer):
    steps, nseq, W = u_t.shape
    G = len(POOL_WINDOWS)
    return pl.pallas_call(
        _pool_decode_kernel,
        grid=(1,),
        in_specs=[
            pl.BlockSpec((steps, nseq, W), lambda i: (0, 0, 0)),
            pl.BlockSpec((POOL_BUF, nseq, W), lambda i: (0, 0, 0)),
            pl.BlockSpec((None, G, LANE, LANE), lambda i: (layer, 0, 0, 0)),
            pl.BlockSpec((None, 1, W), lambda i: (layer, 0, 0)),
        ],
        out_specs=[
            pl.BlockSpec((steps, nseq, W), lambda i: (0, 0, 0)),
            pl.BlockSpec((POOL_BUF, nseq, W), lambda i: (0, 0, 0)),
        ],
        out_shape=[
            jax.ShapeDtypeStruct((steps, nseq, W), F32),
            jax.ShapeDtypeStruct((POOL_BUF, nseq, W), F32),
        ],
        compiler_params=_params(("arbitrary",), 32),
        name="pool_decode",
    )(u_t, buf_t, w_pool, scale3)


def _softmax_rows(s):
    e = jnp.exp(s - jnp.max(s, axis=-1, keepdims=True))
    return e / jnp.sum(e, axis=-1, keepdims=True)


def _xattn_prompt_kernel(q_ref, k_ref, v_ref, o_ref):
    scale = LANE ** -0.5
    for h in range(XA_HEADS):
        cs = slice(h * LANE, (h + 1) * LANE)
        s = _dot_nt(q_ref[:, cs].astype(BF16), k_ref[:, cs].astype(BF16)) * scale
        p = _softmax_rows(s)
        o_ref[:, cs] = _dot(p.astype(BF16), v_ref[:, cs].astype(BF16)).astype(BF16)


def _xattn_prompt(zf, mk, mv, batch, tq):
    T = zf.shape[0] // batch
    W = XA_HEADS * LANE
    nq = T // tq
    n_mem = mk.shape[1]
    mspec = pl.BlockSpec((None, n_mem, W), lambda b, i: (b, 0, 0))
    return pl.pallas_call(
        _xattn_prompt_kernel,
        grid=(batch, nq),
        in_specs=[pl.BlockSpec((tq, W), lambda b, i: (b * nq + i, ZF_X)), mspec, mspec],
        out_specs=pl.BlockSpec((tq, W), lambda b, i: (b * nq + i, 0)),
        out_shape=jax.ShapeDtypeStruct((batch * T, W), BF16),
        compiler_params=_params(("parallel", "parallel"), 48),
        name="xattn_prompt",
    )(zf, mk, mv)


def _xattn_decode_kernel(seq_len, q_ref, k_ref, v_ref, o_ref):
    scale = LANE ** -0.5
    rows = DEC_GROUP * seq_len
    lseq = seq_len.bit_length() - 1
    lrows = rows.bit_length() - 1
    nk = k_ref.shape[1]
    row_s = lax.broadcasted_iota(jnp.int32, (XA_HEADS * rows, nk), 0)
    col_s = lax.broadcasted_iota(jnp.int32, (XA_HEADS * rows, nk), 1)
    own_head = (col_s & (XA_HEADS - 1)) == (row_s >> lrows)
    seq_s = (row_s & (rows - 1)) >> lseq
    seq_o = (lax.broadcasted_iota(jnp.int32, (XA_HEADS * rows, LANE), 0) & (rows - 1)) >> lseq
    for grp in range(k_ref.shape[0] // DEC_GROUP):
        rs = slice(grp * rows, (grp + 1) * rows)
        qb = jnp.concatenate([q_ref[rs, h * LANE:(h + 1) * LANE] for h in range(XA_HEADS)], axis=0).astype(BF16)
        s = jnp.full((XA_HEADS * rows, nk), -jnp.inf, F32)
        for k in range(DEC_GROUP):
            sk = _dot_nt(qb, k_ref[grp * DEC_GROUP + k].astype(BF16))
            s = jnp.where(own_head & (seq_s == k), sk, s)
        p = _softmax_rows(s * scale).astype(BF16)
        o = jnp.zeros((XA_HEADS * rows, LANE), F32)
        for k in range(DEC_GROUP):
            o = jnp.where(seq_o == k, _dot(p, v_ref[grp * DEC_GROUP + k].astype(BF16)), o)
        for h in range(XA_HEADS):
            o_ref[rs, h * LANE:(h + 1) * LANE] = o[h * rows:(h + 1) * rows, :].astype(BF16)


def _merge_kernel(x_ref, oh_ref, op_ref, ox_ref, g0_ref, g1_ref, g2_ref, wb_ref, wo_ref, o_ref, *rest):
    y_ref = rest[-1]
    wbb_ref, wob_ref = rest[:2] if len(rest) == 3 else (None, None)
    j = pl.program_id(1)
    nj, _, tn = y_ref.shape

    @pl.when(j < nj)
    def _():
        wh = oh_ref.shape[1]
        wp = op_ref.shape[1]
        oh, op, ox = oh_ref[...].astype(BF16), op_ref[...].astype(BF16), ox_ref[...].astype(BF16)
        cw = min(MXU_COLS, tn)
        for c in range(tn // cw):
            cs = slice(c * cw, (c + 1) * cw)
            w = wb_ref[:, cs].astype(BF16)
            if wbb_ref is not None:
                wbb_ref[:, cs] = w
            y = _sigmoid(g0_ref[:, cs].astype(F32)) * _dot(oh, w[0:wh, :])
            y += _sigmoid(g1_ref[:, cs].astype(F32)) * _dot(op, w[wh:wh + wp, :])
            y += _sigmoid(g2_ref[:, cs].astype(F32)) * _dot(ox, w[wh + wp:, :])
            y_ref[j, :, cs] = y.astype(BF16)

    @pl.when(j >= nj)
    def _():
        acc = x_ref[...]
        for c in range(nj):
            rs = slice(c * tn, (c + 1) * tn)
            w = wo_ref[rs, :].astype(BF16)
            if wob_ref is not None:
                wob_ref[rs, :] = w
            acc += _dot(y_ref[c], w)
        o_ref[...] = acc


def _merge_resident_kernel(x_ref, oh_ref, op_ref, ox_ref, g_ref, wb_ref, wo_ref, o_ref, y_ref):
    D = x_ref.shape[1]
    wh = oh_ref.shape[1]
    wp = op_ref.shape[1]
    oh, op, ox = oh_ref[...].astype(BF16), op_ref[...].astype(BF16), ox_ref[...].astype(BF16)
    cw = min(2 * MXU_COLS, D)
    for c in range(D // cw):
        cs = slice(c * cw, (c + 1) * cw)
        gate = lambda k: _sigmoid(g_ref[:, k * D + c * cw:k * D + (c + 1) * cw].astype(F32))
        y = gate(0) * _dot(oh, wb_ref[0:wh, cs])
        y += gate(1) * _dot(op, wb_ref[wh:wh + wp, cs])
        y += gate(2) * _dot(ox, wb_ref[wh + wp:, cs])
        y_ref[:, cs] = y.astype(BF16)
    for c in range(D // cw):
        cs = slice(c * cw, (c + 1) * cw)
        o_ref[:, cs] = x_ref[:, cs] + _dot(y_ref[...], wo_ref[:, cs])


def _merge_resident(x, oh, op, ox, zg, wb, wo, tm):
    T, D = x.shape
    row = lambda a: pl.BlockSpec((tm, a.shape[1]), lambda i: (i, 0))
    held = lambda a: pl.BlockSpec(a.shape, lambda i: (0, 0), pipeline_mode=pl.Buffered(1))
    return pl.pallas_call(
        _merge_resident_kernel,
        grid=(T // tm,),
        in_specs=[row(x), row(oh), row(op), row(ox), row(zg), held(wb), held(wo)],
        out_specs=row(x),
        out_shape=jax.ShapeDtypeStruct((T, D), F32),
        scratch_shapes=[pltpu.VMEM((tm, D), BF16)],
        compiler_params=_params(("parallel",), 56),
        name="merge_resident",
    )(x, oh, op, ox, zg, wb, wo)


def _merge(x, oh, op, ox, zb, wb, wo, layer, tm, tn, emit=False):
    T, D = x.shape
    assert not emit or T == tm
    nj = D // tn
    first = lambda j: jnp.minimum(j, nj - 1)
    second = lambda j: jnp.maximum(j - nj, 0)
    full = lambda a: pl.BlockSpec((tm, a.shape[1]), lambda i, j: (i, 0))
    gate = lambda k: pl.BlockSpec((tm, tn), lambda i, j: (i, k * nj + first(j)))
    wb_idx = lambda i, j: (0, first(j))
    wo_idx = lambda i, j: (0, second(j))
    rows_b = wb.shape[-2]
    out_specs = [pl.BlockSpec((tm, tn), lambda i, j: (i, second(j)))]
    out_shape = [jax.ShapeDtypeStruct((T, D), F32)]
    if emit:
        out_specs += [pl.BlockSpec((rows_b, tn), wb_idx), pl.BlockSpec((D, tn), wo_idx)]
        out_shape += [jax.ShapeDtypeStruct((rows_b, D), BF16), jax.ShapeDtypeStruct((D, D), BF16)]
    out = pl.pallas_call(
        _merge_kernel,
        grid=(T // tm, 2 * nj),
        in_specs=[
            pl.BlockSpec((tm, tn), lambda i, j: (i, second(j))),
            full(oh), full(op), full(ox), gate(0), gate(1), gate(2),
            _layer_spec(wb, layer, (rows_b, tn), wb_idx),
            _layer_spec(wo, layer, (D, tn), wo_idx),
        ],
        out_specs=out_specs,
        out_shape=out_shape,
        scratch_shapes=[pltpu.VMEM((nj, tm, tn), BF16)],
        compiler_params=_params(("parallel", "arbitrary"), 48),
        name="merge_out",
    )(x, oh, op, ox, zb, zb, zb, wb, wo)
    return out if emit else out[0]


def kernel(x_prompt, x_sample, state_hgrn, state_pool, cache_mem_k, cache_mem_v, mem_prompt, ffn1_norm, ffn1_w1, ffn1_w3, ffn1_w2, mix_norm, w_in, lb_logits, hg_norm, w_pool, pool_scale, mem_norm, w_mk, w_mv, w_branch, w_out, ffn2_norm, ffn2_w1, ffn2_w3, ffn2_w2, final_norm):
    B, T, D = x_prompt.shape
    nseq, steps, _ = x_sample.shape
    depth = w_in.shape[0]
    n_mem = mem_prompt.shape[1]
    xw = XA_HEADS * LANE
    pw = len(POOL_WINDOWS) * LANE

    tm_p = min(1024, B * T)
    tm_s = min(512, nseq * steps)

    vec3 = lambda a: a.reshape(a.shape[0], 1, a.shape[1])
    f1n, f2n = vec3(ffn1_norm), vec3(ffn2_norm)
    mix3, mem3, hgn3, psc3 = vec3(mix_norm), vec3(mem_norm), vec3(hg_norm), vec3(pool_scale)
    lbs3 = vec3(_lower_bounds(lb_logits))

    xp = x_prompt.reshape(B * T, D)
    xs = x_sample.reshape(nseq * steps, D)
    mem2 = mem_prompt.reshape(B * n_mem, D)
    ck = cache_mem_k.reshape(depth, nseq, n_mem * XA_HEADS, LANE)
    cv = cache_mem_v.reshape(depth, nseq, n_mem * XA_HEADS, LANE)

    tf = _tile(ffn1_w1.shape[-1], 512)
    tf_s = _tile(ffn1_w1.shape[-1], 256)
    tn_in = _tile(math.gcd(HG_HEADS * LANE, N_BRANCH * D), 1024)
    tn_mrg = _tile(D, 512)
    tm_mem = min(512, B * n_mem)
    hs_p, pb_p, mk_p, mv_p, pb_s = [], [], [], [], []
    hs_s = None

    blocks = []
    for l in range(depth):
        blocks.append((f1n, ffn1_w1, ffn1_w3, ffn1_w2, l))
        blocks.append((f2n, ffn2_w1, ffn2_w3, ffn2_w2, l))

    def swiglu_pair(xp, xs, wbf, k, final_g):
        norm, w1, w3, w2, l = blocks[k]
        if wbf is None:
            xs, *wbf = _ffn(xs, norm, w1, w3, w2, l, tm_s, tf_s, final_g)
        else:
            xs = _ffn(xs, norm, *wbf, l, tm_s, tf, final_g)
        if k + 1 < len(blocks):
            xp, *nxt = _ffn(xp, norm, *wbf, l, tm_p, tf, final_g, side=blocks[k + 1][1:])
        else:
            xp, nxt = _ffn(xp, norm, *wbf, l, tm_p, tf, final_g), None
        return xp, xs, nxt

    wbf = None
    for l in range(depth):
        mk = _rms_matmul(mem2, mem3, w_mk, l, tm_mem, xw)
        mv = _rms_matmul(mem2, mem3, w_mv, l, tm_mem, xw)
        mk_p.append(mk.reshape(B, n_mem, XA_HEADS, LANE))
        mv_p.append(mv.reshape(B, n_mem, XA_HEADS, LANE))

        xp, xs, wbf = swiglu_pair(xp, xs, wbf, 2 * l, None)

        zsb, zsf, zsg, w_in_b = _in_proj(xs, mix3, w_in, l, tm_s, tn_in, emit=True)
        zpb, zpf, zpg = _in_proj(xp, mix3, w_in_b, l, tm_p, tn_in)

        ohp, sp = _hgrn_prompt(zpb, zpf, lbs3, hgn3, l, B)
        ohs, hs_s, oxs = _decode_mixers(zsb, zsf, lbs3, hgn3, state_hgrn, hs_s, ck, cv, l, nseq, 8)
        hs_p.append(sp)

        opp, bp = _pool_prompt(zpf, w_pool, psc3, l, B)
        pb_p.append(bp)
        u_t = zsf[:, ZF_U * pw:(ZF_U + 1) * pw].reshape(nseq, steps, pw).transpose(1, 0, 2)
        buf_t = state_pool[l].transpose(1, 0, 2)
        ops_t, nb_t = _pool_decode(u_t, buf_t, w_pool, psc3, l)
        ops = ops_t.transpose(1, 0, 2).reshape(nseq * steps, pw)
        pb_s.append(nb_t.transpose(1, 0, 2))

        oxp = _xattn_prompt(zpf, mk.reshape(B, n_mem, xw), mv.reshape(B, n_mem, xw), B, min(1024, T))

        xs, w_br_b, w_out_b = _merge(xs, ohs, ops, oxs, zsg, w_branch, w_out, l, tm_s, tn_mrg, emit=True)
        xp = _merge_resident(xp, ohp, opp, oxp, zpg, w_br_b, w_out_b, min(512, B * T))

        fn2 = final_norm.reshape(1, D) if l == depth - 1 else None
        xp, xs, wbf = swiglu_pair(xp, xs, wbf, 2 * l + 1, fn2)

    y_prompt = xp.reshape(B, T, D)
    y_sample = xs.reshape(nseq, steps, D)
    return (y_prompt, y_sample, jnp.stack(hs_p), jnp.stack(pb_p), jnp.stack(mk_p), jnp.stack(mv_p),
            hs_s, jnp.stack(pb_s))
```

```python
import functools
import math

import jax
import jax.numpy as jnp
from jax import lax
from jax.experimental import pallas as pl
from jax.experimental.pallas import tpu as pltpu

F32 = jnp.float32
BF16 = jnp.bfloat16
EPS = 1e-6

LANE = 128
SUBLANE = 8
MXU_COLS = 256
HG_HEADS = 8
POOL_WINDOWS = (2, 4, 8, 16)
POOL_BUF = max(POOL_WINDOWS) - 1
XA_HEADS = 4
N_BRANCH = 3

COL_Q, COL_F, COL_I, COL_OG = 0, HG_HEADS, 2 * HG_HEADS, 3 * HG_HEADS
COL_U = 4 * HG_HEADS
COL_X = COL_U + len(POOL_WINDOWS)
COL_G = COL_X + XA_HEADS
ZB_Q, ZB_V, ZB_OG = 0, 1, 2
ZF_F = 0
ZF_U, ZF_X = 2, 3

HGRN_CHUNK = 128
HGRN_STEP_CHUNKS = 2
DEC_GROUP = 4


def _tile(n, preferred):
    t = preferred
    while n % t:
        t -= LANE
    return t


def _params(semantics, vmem_mib):
    return pltpu.CompilerParams(dimension_semantics=semantics, vmem_limit_bytes=vmem_mib * 1024 * 1024)


def _rms(x, g):
    return x * lax.rsqrt(jnp.mean(x * x, axis=-1, keepdims=True) + EPS) * g


def _sigmoid(x):
    return 0.5 * jnp.tanh(0.5 * x) + 0.5


def _silu(x):
    return x * _sigmoid(x)


def _dot(a, b):
    return jnp.dot(a, b, preferred_element_type=F32)


def _dot_nt(a, b):
    return lax.dot_general(a, b, (((1,), (1,)), ((), ())), preferred_element_type=F32)


def _dot_tn(a, b):
    return lax.dot_general(a, b, (((0,), (0,)), ((), ())), preferred_element_type=F32)


def _ffn_kernel(final, cast, side, x_ref, g_ref, w1_ref, w3_ref, w2_ref, *rest):
    rest = list(rest)
    xn_ref = rest.pop()
    fn_ref = rest.pop(0) if final else None
    side_in = [rest.pop(0) for _ in range(3)] if side else []
    o_ref = rest.pop(0)
    if cast:
        src = (w1_ref, w3_ref, w2_ref)
        w1_ref, w3_ref, w2_ref = [rest.pop(0) for _ in range(3)]
        for dst, s in zip((w1_ref, w3_ref, w2_ref), src):
            dst[...] = s[...].astype(BF16)
    for s, dst in zip(side_in, rest):
        dst[...] = s[...].astype(BF16)
    j = pl.program_id(1)

    @pl.when(j == 0)
    def _():
        x = x_ref[...]
        xn_ref[...] = _rms(x, g_ref[...]).astype(BF16)
        o_ref[...] = x

    xn = xn_ref[...]
    h = (_silu(_dot(xn, w1_ref[...])) * _dot(xn, w3_ref[...])).astype(BF16)
    cw = min(w2_ref.shape[0], o_ref.shape[1])
    for n in range(o_ref.shape[1] // cw):
        cs = slice(n * cw, (n + 1) * cw)
        o_ref[:, cs] += 0.5 * _dot(h, w2_ref[:, cs])

    if final:
        @pl.when(j == pl.num_programs(1) - 1)
        def _():
            o_ref[...] = _rms(o_ref[...], fn_ref[...])


def _layer_spec(w, layer, block, index):
    if w.ndim == 2:
        return pl.BlockSpec(block, index)
    return pl.BlockSpec((None,) + block, lambda i, j: (layer,) + index(i, j))


def _ffn(x, g3, w1, w3, w2, layer, tm, tf, final_g=None, side=None):
    T, D = x.shape
    F = w1.shape[-1]
    cast = w1.dtype != BF16
    final = final_g is not None
    assert not cast or T == tm
    ni, nj = T // tm, F // tf
    col = lambda i, j: (0, j)
    row = lambda i, j: (j, 0)
    out_specs = [pl.BlockSpec((tm, D), lambda i, j: (i, 0))]
    out_shape = [jax.ShapeDtypeStruct((T, D), F32)]
    wshapes = [jax.ShapeDtypeStruct((D, F), BF16), jax.ShapeDtypeStruct((D, F), BF16),
               jax.ShapeDtypeStruct((F, D), BF16)]
    if cast:
        out_specs += [pl.BlockSpec((D, tf), col), pl.BlockSpec((D, tf), col), pl.BlockSpec((tf, D), row)]
        out_shape += wshapes
    in_specs = [
        pl.BlockSpec((tm, D), lambda i, j: (i, 0)),
        pl.BlockSpec((None, 1, D), lambda i, j: (layer, 0, 0)),
        _layer_spec(w1, layer, (D, tf), col),
        _layer_spec(w3, layer, (D, tf), col),
        _layer_spec(w2, layer, (tf, D), row),
    ]
    args = [x, g3, w1, w3, w2]
    if final:
        in_specs.append(pl.BlockSpec((1, D), lambda i, j: (0, 0)))
        args.append(final_g)
    if side is not None:
        s1, s3, s2, side_layer = side
        assert D % (ni * LANE) == 0
        rb = D // ni
        up, down = (lambda i, j: (i, j)), (lambda i, j: (j, i))
        in_specs += [_layer_spec(s1, side_layer, (rb, tf), up), _layer_spec(s3, side_layer, (rb, tf), up),
                     _layer_spec(s2, side_layer, (tf, rb), down)]
        args += [s1, s3, s2]
        out_specs += [pl.BlockSpec((rb, tf), up), pl.BlockSpec((rb, tf), up), pl.BlockSpec((tf, rb), down)]
        out_shape += wshapes
    out = pl.pallas_call(
        functools.partial(_ffn_kernel, final, cast, side is not None),
        grid=(ni, nj),
        in_specs=in_specs,
        out_specs=out_specs,
        out_shape=out_shape,
        scratch_shapes=[pltpu.VMEM((tm, D), BF16)],
        compiler_params=_params(("parallel", "arbitrary"), 62),
        name="ffn_cast" if cast else "ffn",
    )(*args)
    return out if len(out) > 1 else out[0]


def _rms_matmul_kernel(x_ref, g_ref, w_ref, o_ref, *rest):
    xn_ref = rest[-1]

    @pl.when(pl.program_id(1) == 0)
    def _():
        xn_ref[...] = _rms(x_ref[...], g_ref[...]).astype(BF16)

    w = w_ref[...].astype(BF16)
    if len(rest) == 2:
        rest[0][...] = w
    o_ref[...] = _dot(xn_ref[...], w)


def _rms_matmul(x, g3, w, layer, tm, tn, emit=False):
    T, D = x.shape
    N = w.shape[-1]
    assert not emit or T == tm
    col = lambda i, j: (0, j)
    out_specs = [pl.BlockSpec((tm, tn), lambda i, j: (i, j))]
    out_shape = [jax.ShapeDtypeStruct((T, N), F32)]
    if emit:
        out_specs.append(pl.BlockSpec((D, tn), col))
        out_shape.append(jax.ShapeDtypeStruct((D, N), BF16))
    out = pl.pallas_call(
        _rms_matmul_kernel,
        grid=(T // tm, N // tn),
        in_specs=[
            pl.BlockSpec((tm, D), lambda i, j: (i, 0)),
            pl.BlockSpec((None, 1, D), lambda i, j: (layer, 0, 0)),
            _layer_spec(w, layer, (D, tn), col),
        ],
        out_specs=out_specs,
        out_shape=out_shape,
        scratch_shapes=[pltpu.VMEM((tm, D), BF16)],
        compiler_params=_params(("parallel", "arbitrary"), 48),
        name="rms_matmul",
    )(x, g3, w)
    return out if emit else out[0]


def _runs(seq):
    runs = []
    for j, v in enumerate(seq):
        if runs and runs[-1][2] == v - j:
            runs[-1][1] = j + 1
        else:
            runs.append([j, j + 1, v - j])
    return [tuple(r) for r in runs]


def _in_runs(j, runs):
    hit = None
    for lo, hi, _ in runs:
        c = (j >= lo) & (j < hi)
        hit = c if hit is None else hit | c
    return hit


def _lookup(j, runs):
    out = 0
    for lo, hi, off in runs:
        out = out + jnp.where((j >= lo) & (j < hi), j + off, 0)
    return out


def _in_proj_plan(d_model, tn):
    t = lambda blocks: blocks * LANE // tn
    nq, nux, ng = t(HG_HEADS), t(len(POOL_WINDOWS) + XA_HEADS), N_BRANCH * d_model // tn
    src = lambda start, n: list(range(t(start), t(start) + n))
    order = src(COL_Q, nq) + src(COL_I, nq) + src(COL_OG, nq) + src(COL_F, nq) + src(COL_U, nux) + src(COL_G, ng)
    nh, n32 = 3 * nq, nq + nux
    kinds = {"f32": [(nh, nh + n32, 0)]}
    return _runs(order), kinds, nh, n32


def _in_proj_kernel(kinds, side, x_ref, g_ref, w_ref, *rest):
    rest = list(rest)
    xn_ref = rest.pop()
    if side:
        side_in, side_out = rest.pop(0), rest.pop()
        side_out[...] = side_in[...].astype(BF16)
    zh_ref, zf_ref, zg_ref = rest[:3]
    rest = rest[3:] + [xn_ref]
    j = pl.program_id(1)

    @pl.when(j == 0)
    def _():
        xn_ref[...] = _rms(x_ref[...], g_ref[...]).astype(BF16)

    xn = xn_ref[...]
    tn = w_ref.shape[1]
    cw = min(MXU_COLS, tn)
    zs = []
    for c in range(tn // cw):
        cs = slice(c * cw, (c + 1) * cw)
        w = w_ref[:, cs].astype(BF16)
        if len(rest) == 2:
            rest[0][:, cs] = w
        z = _dot(xn, w)
        zs.append(z)
        zh_ref[:, cs] = z.astype(BF16)
        zg_ref[:, cs] = z.astype(BF16)

    @pl.when(_in_runs(j, kinds["f32"]))
    def _():
        for c, z in enumerate(zs):
            zf_ref[:, c * cw:(c + 1) * cw] = z


def _in_proj(x, g3, w, layer, tm, tn, emit=False, side=None):
    T, D = x.shape
    N = w.shape[-1]
    assert not emit or T == tm
    order, kinds, nh, n32 = _in_proj_plan(D, tn)
    nsteps = N // tn
    wcol = lambda i, j: (0, _lookup(j, order))
    out_specs = [
        pl.BlockSpec((tm, tn), lambda i, j: (i, jnp.minimum(j, nh))),
        pl.BlockSpec((tm, tn), lambda i, j: (i, jnp.clip(j - nh, 0, n32 - 1))),
        pl.BlockSpec((tm, tn), lambda i, j: (i, jnp.maximum(j - nh - n32, 0))),
    ]
    out_shape = [
        jax.ShapeDtypeStruct((T, (nh + 1) * tn), BF16),
        jax.ShapeDtypeStruct((T, n32 * tn), F32),
        jax.ShapeDtypeStruct((T, (nsteps - nh - n32) * tn), BF16),
    ]
    if emit:
        out_specs.append(pl.BlockSpec((D, tn), wcol))
        out_shape.append(jax.ShapeDtypeStruct((D, N), BF16))
    in_specs = [
        pl.BlockSpec((tm, D), lambda i, j: (i, 0)),
        pl.BlockSpec((None, 1, D), lambda i, j: (layer, 0, 0)),
        _layer_spec(w, layer, (D, tn), wcol),
    ]
    args = [x, g3, w]
    if side is not None:
        rb = D // (T // tm)
        assert rb % SUBLANE == 0
        in_specs.append(_layer_spec(side[0], side[1], (rb, tn), lambda i, j: (i, j)))
        args.append(side[0])
        out_specs.append(pl.BlockSpec((rb, tn), lambda i, j: (i, j)))
        out_shape.append(jax.ShapeDtypeStruct((D, N), BF16))
    return pl.pallas_call(
        functools.partial(_in_proj_kernel, kinds, side is not None),
        grid=(T // tm, nsteps),
        in_specs=in_specs,
        out_specs=out_specs,
        out_shape=out_shape,
        scratch_shapes=[pltpu.VMEM((tm, D), BF16)],
        compiler_params=_params(("parallel", "arbitrary"), 58),
        name="in_proj",
    )(*args)


def _lower_bound_kernel(lg_ref, o_ref):
    lg = lg_ref[...]
    depth = lg.shape[0]
    rows = [lg[i:i + 1, :] for i in range(depth)]
    m = rows[0]
    for r in rows[1:]:
        m = jnp.maximum(m, r)
    e = [jnp.exp(r - m) for r in rows]
    tot = e[0]
    for v in e[1:]:
        tot = tot + v
    c = e[0] / tot
    first = c
    o_ref[0:1, :] = c - first
    for i in range(1, depth):
        c = c + e[i] / tot
        o_ref[i:i + 1, :] = c - first


def _lower_bounds(lb_logits):
    return pl.pallas_call(
        _lower_bound_kernel,
        out_shape=jax.ShapeDtypeStruct(lb_logits.shape, F32),
        name="hgrn_lower_bounds",
    )(lb_logits)


def _neg_abs(x):
    bits = lax.bitcast_convert_type(x, jnp.uint32) | jnp.uint32(0x80000000)
    return lax.bitcast_convert_type(bits, F32)


def _group_ref_row(b, s):
    rows, width = b.shape
    gsz = 2 * s
    if gsz >= 8:
        parts = [jnp.broadcast_to(b[i * gsz + s - 1:i * gsz + s, :], (gsz, width)) for i in range(rows // gsz)]
        return parts[0] if len(parts) == 1 else jnp.concatenate(parts, axis=0)
    pos = lax.broadcasted_iota(jnp.int32, b.shape, 0) & (gsz - 1)
    out = b
    for p in range(gsz):
        d = p - (s - 1)
        if d != 0:
            out = jnp.where(pos == p, pltpu.roll(b, d % rows, 0), out)
    return out


def _hgrn_wide(q, zf, vb, lb, states, seg):
    C, W = q.shape
    heads = [slice(h * LANE, (h + 1) * LANE) for h in range(W // LANE)]
    nseg = C // seg
    fg = lb + (1.0 - lb) * jax.nn.sigmoid(zf)
    kk = 1.0 - fg
    g = jnp.log2(fg)

    row = lax.broadcasted_iota(jnp.int32, (C, W), 0)
    rr = lax.broadcasted_iota(jnp.int32, (C, C), 0)
    cc = lax.broadcasted_iota(jnp.int32, (C, C), 1)
    lseg = seg.bit_length() - 1
    tri = jnp.where((cc <= rr) & ((rr >> lseg) == (cc >> lseg)), 1.0, 0.0).astype(F32)
    b = jnp.dot(tri, g, precision=lax.Precision.HIGHEST, preferred_element_type=F32)

    qb, kb = q.astype(BF16), kk.astype(BF16)
    a = [jnp.where(rr == cc, _dot_nt(qb[:, hs], kb[:, hs]), 0.0) for hs in heads]
    s = seg // 2
    while s >= 1:
        ls = s.bit_length() - 1
        same = (rr >> (ls + 1)) == (cc >> (ls + 1))
        if s == 1:
            right = (row & 1) == 1
            qs, ks = jnp.where(right, q * fg, 0.0).astype(BF16), jnp.where(right, 0.0, kk).astype(BF16)
            for h, hs in enumerate(heads):
                a[h] = a[h] + jnp.where(same, _dot_nt(qs[:, hs], ks[:, hs]), 0.0)
            break
        e = jnp.exp2(_neg_abs(b - _group_ref_row(b, s)))
        if s % SUBLANE == 0:
            nblk = C // s
            zero = jnp.zeros((s, W), F32)
            blk = lambda x, i: x[i * s:(i + 1) * s, :]
            qs = jnp.concatenate([blk(q, i) * blk(e, i) if i % 2 else zero for i in range(nblk)], axis=0).astype(BF16)
            ks = jnp.concatenate([zero if i % 2 else blk(kk, i) * blk(e, i) for i in range(nblk)], axis=0).astype(BF16)
            for h, hs in enumerate(heads):
                p = _dot_nt(qs[:, hs], ks[:, hs])
                pieces = []
                for i in range(nblk):
                    if i % 2 == 0:
                        pieces.append(blk(a[h], i))
                    elif 2 * s == C:
                        pieces.append(blk(a[h], i) + blk(p, i))
                    else:
                        pieces.append(blk(a[h], i) + jnp.where(blk(same, i), blk(p, i), 0.0))
                a[h] = jnp.concatenate(pieces, axis=0)
        else:
            right = ((row >> ls) & 1) == 1
            eq = jnp.where(right, e, 0.0)
            qs, ks = (q * eq).astype(BF16), (kk * (e - eq)).astype(BF16)
            for h, hs in enumerate(heads):
                a[h] = a[h] + jnp.where(same, _dot_nt(qs[:, hs], ks[:, hs]), 0.0)
        s //= 2
    o = [_dot(a[h].astype(BF16), vb[:, hs]) for h, hs in enumerate(heads)]

    qe = q * jnp.exp2(b)
    new_states = [[] for _ in heads]
    for k in range(nseg):
        bl = b[k * seg + seg - 1:k * seg + seg, :]
        if nseg == 1:
            qk = qe
            ke = kk * jnp.exp2(bl - b)
        else:
            mine = (row >> lseg) == k
            qk = jnp.where(mine, qe, 0.0)
            ke = jnp.where(mine, kk * jnp.exp2(jnp.where(mine, bl - b, 0.0)), 0.0)
        qkb, keb, ebl = qk.astype(BF16), ke.astype(BF16), jnp.exp2(bl)
        for h, hs in enumerate(heads):
            st = states[h][k]
            o[h] = o[h] + _dot(qkb[:, hs], st.astype(BF16))
            decay = jnp.transpose(jnp.broadcast_to(ebl[:, hs], (LANE, LANE)))
            new_states[h].append(st * decay + _dot_tn(keb[:, hs], vb[:, hs]))
    return o, new_states


def _hgrn_finish(o, zog, hgn):
    return (_rms(o, hgn) * _silu(zog.astype(F32))).astype(BF16)


def _hgrn_prompt_kernel(q_ref, zf_ref, v_ref, og_ref, lb_ref, hgn_ref, oh_ref, s_ref):
    @pl.when(pl.program_id(1) == 0)
    def _():
        s_ref[...] = jnp.zeros_like(s_ref)

    states = [[s_ref[h]] for h in range(HG_HEADS)]
    for sub in range(q_ref.shape[0] // HGRN_CHUNK):
        rs = slice(sub * HGRN_CHUNK, (sub + 1) * HGRN_CHUNK)
        q = _silu(q_ref[rs, :].astype(F32))
        o, states = _hgrn_wide(q, zf_ref[rs, :], v_ref[rs, :], lb_ref[...], states, HGRN_CHUNK)
        for h in range(HG_HEADS):
            cs = slice(h * LANE, (h + 1) * LANE)
            oh_ref[rs, cs] = _hgrn_finish(o[h], og_ref[rs, cs], hgn_ref[:, cs])
    for h in range(HG_HEADS):
        s_ref[h] = states[h][0]


def _hgrn_prompt(zb, zf, lbs3, hgn3, layer, batch):
    T = zb.shape[0] // batch
    C = HGRN_STEP_CHUNKS * HGRN_CHUNK
    nc = T // C
    W = HG_HEADS * LANE
    zspec = lambda col: pl.BlockSpec((C, W), lambda b, c: (b * nc + c, col))
    vec = pl.BlockSpec((None, 1, W), lambda b, c: (layer, 0, 0))
    return pl.pallas_call(
        _hgrn_prompt_kernel,
        grid=(batch, nc),
        in_specs=[zspec(ZB_Q), zspec(ZF_F), zspec(ZB_V), zspec(ZB_OG), vec, vec],
        out_specs=[
            pl.BlockSpec((C, W), lambda b, c: (b * nc + c, 0)),
            pl.BlockSpec((None, HG_HEADS, LANE, LANE), lambda b, c: (b, 0, 0, 0)),
        ],
        out_shape=[
            jax.ShapeDtypeStruct((batch * T, W), BF16),
            jax.ShapeDtypeStruct((batch, HG_HEADS, LANE, LANE), F32),
        ],
        compiler_params=_params(("parallel", "arbitrary"), 32),
        name="hgrn_prompt",
    )(zb, zf, zb, zb, lbs3, hgn3)


def _hgrn_decode_kernel(seq_len, q_ref, zf_ref, v_ref, og_ref, lb_ref, hgn_ref, s_ref, xq_ref, k_ref, vc_ref, *rest):
    oh_ref, so_ref, ox_ref = rest[-3:]
    _xattn_decode_kernel(seq_len, xq_ref, k_ref, vc_ref, ox_ref)
    rows = DEC_GROUP * seq_len
    for grp in range(s_ref.shape[0] // DEC_GROUP):
        rs = slice(grp * rows, (grp + 1) * rows)
        states = [[s_ref[grp * DEC_GROUP + k, h] for k in range(DEC_GROUP)] for h in range(HG_HEADS)]
        o, new = _hgrn_wide(_silu(q_ref[rs, :].astype(F32)), zf_ref[rs, :], v_ref[rs, :], lb_ref[...], states, seq_len)
        for h in range(HG_HEADS):
            cs = slice(h * LANE, (h + 1) * LANE)
            for k in range(DEC_GROUP):
                so_ref[grp * DEC_GROUP + k, h] = new[h][k]
            oh_ref[rs, cs] = _hgrn_finish(o[h], og_ref[rs, cs], hgn_ref[:, cs])


def _decode_mixers(zb, zf, lbs3, hgn3, state, stacked, cache_k, cache_v, layer, nseq, nb):
    seq_len = zb.shape[0] // nseq
    W = HG_HEADS * LANE
    XW = XA_HEADS * LANE
    zspec = lambda col: pl.BlockSpec((nb * seq_len, W), lambda i: (i, col))
    vec = pl.BlockSpec((None, 1, W), lambda i: (layer, 0, 0))
    sspec = pl.BlockSpec((None, nb, HG_HEADS, LANE, LANE), lambda i: (layer, i, 0, 0, 0))
    cspec = pl.BlockSpec((None, nb) + cache_k.shape[2:], lambda i: (layer, i, 0, 0))
    in_specs = [zspec(ZB_Q), zspec(ZF_F), zspec(ZB_V), zspec(ZB_OG), vec, vec, sspec,
                pl.BlockSpec((nb * seq_len, XW), lambda i: (i, ZF_X)), cspec, cspec]
    args = [zb, zf, zb, zb, lbs3, hgn3, state, zf, cache_k, cache_v]
    aliases = {}
    if stacked is not None:
        in_specs.append(pl.BlockSpec(memory_space=pl.ANY))
        args.append(stacked)
        aliases = {len(args) - 1: 1}
    return pl.pallas_call(
        functools.partial(_hgrn_decode_kernel, seq_len),
        grid=(nseq // nb,),
        in_specs=in_specs,
        out_specs=[pl.BlockSpec((nb * seq_len, W), lambda i: (i, 0)), sspec,
                   pl.BlockSpec((nb * seq_len, XW), lambda i: (i, 0))],
        out_shape=[
            jax.ShapeDtypeStruct((nseq * seq_len, W), BF16),
            jax.ShapeDtypeStruct(state.shape, F32),
            jax.ShapeDtypeStruct((nseq * seq_len, XW), BF16),
        ],
        input_output_aliases=aliases,
        compiler_params=_params(("parallel",), 58),
        name="decode_mixers",
    )(*args)


def _pool_prompt_kernel(u_ref, wp_ref, sc_ref, op_ref, nb_ref):
    u = u_ref[...]
    T = u.shape[0]
    row = lax.broadcasted_iota(jnp.int32, (T, LANE), 0)
    for g, w in enumerate(POOL_WINDOWS):
        cs = slice(g * LANE, (g + 1) * LANE)
        ug = u[:, cs]
        s = ug
        d = 1
        while d < w:
            s = s + jnp.where(row >= d, pltpu.roll(s, d, 0), 0.0)
            d *= 2
        cnt = jnp.minimum(row + 1, w).astype(F32)
        dv = s / cnt - ug
        y = _dot(dv.astype(BF16), wp_ref[g].astype(BF16)) * sc_ref[:, cs]
        op_ref[:, cs] = y.astype(BF16)
    tail = u_ref[T - 16:T, :]
    nb_ref[...] = pltpu.roll(tail, 15, 0)[0:POOL_BUF, :]


def _pool_prompt(zf, w_pool, scale3, layer, batch):
    T = zf.shape[0] // batch
    G = len(POOL_WINDOWS)
    W = G * LANE
    return pl.pallas_call(
        _pool_prompt_kernel,
        grid=(batch,),
        in_specs=[
            pl.BlockSpec((T, W), lambda b: (b, ZF_U)),
            pl.BlockSpec((None, G, LANE, LANE), lambda b: (layer, 0, 0, 0)),
            pl.BlockSpec((None, 1, W), lambda b: (layer, 0, 0)),
        ],
        out_specs=[
            pl.BlockSpec((T, W), lambda b: (b, 0)),
            pl.BlockSpec((None, POOL_BUF, W), lambda b: (b, 0, 0)),
        ],
        out_shape=[
            jax.ShapeDtypeStruct((batch * T, W), BF16),
            jax.ShapeDtypeStruct((batch, POOL_BUF, W), F32),
        ],
        compiler_params=_params(("parallel",), 48),
        name="pool_prompt",
    )(zf, w_pool, scale3)


def _pool_decode_kernel(u_ref, buf_ref, wp_ref, sc_ref, op_ref, nb_ref):
    steps = u_ref.shape[0]
    for g, w in enumerate(POOL_WINDOWS):
        cs = slice(g * LANE, (g + 1) * LANE)
        wg = wp_ref[g].astype(BF16)
        for t in range(steps):
            n_u = min(t + 1, w)
            acc = u_ref[t, :, cs]
            for j in range(t - n_u + 1, t):
                acc = acc + u_ref[j, :, cs]
            for i in range(POOL_BUF - (w - n_u), POOL_BUF):
                acc = acc + buf_ref[i, :, cs]
            dv = acc * (1.0 / w) - u_ref[t, :, cs]
            op_ref[t, :, cs] = _dot(dv.astype(BF16), wg) * sc_ref[:, cs]
    for i in range(POOL_BUF - steps):
        nb_ref[i] = buf_ref[i + steps]
    for t in range(steps):
        nb_ref[POOL_BUF - steps + t] = u_ref[t]


def _pool_decode(u_t, buf_t, w_pool, scale3, layer):
    steps, nseq, W = u_t.shape
    G = len(POOL_WINDOWS)
    return pl.pallas_call(
        _pool_decode_kernel,
        grid=(1,),
        in_specs=[
            pl.BlockSpec((steps, nseq, W), lambda i: (0, 0, 0)),
            pl.BlockSpec((POOL_BUF, nseq, W), lambda i: (0, 0, 0)),
            pl.BlockSpec((None, G, LANE, LANE), lambda i: (layer, 0, 0, 0)),
            pl.BlockSpec((None, 1, W), lambda i: (layer, 0, 0)),
        ],
        out_specs=[
            pl.BlockSpec((steps, nseq, W), lambda i: (0, 0, 0)),
            pl.BlockSpec((POOL_BUF, nseq, W), lambda i: (0, 0, 0)),
        ],
        out_shape=[
            jax.ShapeDtypeStruct((steps, nseq, W), F32),
            jax.ShapeDtypeStruct((POOL_BUF, nseq, W), F32),
        ],
        compiler_params=_params(("arbitrary",), 32),
        name="pool_decode",
    )(u_t, buf_t, w_pool, scale3)


def _softmax_rows(s):
    e = jnp.exp(s - jnp.max(s, axis=-1, keepdims=True))
    return e / jnp.sum(e, axis=-1, keepdims=True)


def _xattn_prompt_kernel(q_ref, k_ref, v_ref, o_ref):
    scale = LANE ** -0.5
    for h in range(XA_HEADS):
        cs = slice(h * LANE, (h + 1) * LANE)
        s = _dot_nt(q_ref[:, cs].astype(BF16), k_ref[:, cs].astype(BF16)) * scale
        p = _softmax_rows(s)
        o_ref[:, cs] = _dot(p.astype(BF16), v_ref[:, cs].astype(BF16)).astype(BF16)


def _xattn_prompt(zf, mk, mv, batch, tq):
    T = zf.shape[0] // batch
    W = XA_HEADS * LANE
    nq = T // tq
    n_mem = mk.shape[1]
    mspec = pl.BlockSpec((None, n_mem, W), lambda b, i: (b, 0, 0))
    return pl.pallas_call(
        _xattn_prompt_kernel,
        grid=(batch, nq),
        in_specs=[pl.BlockSpec((tq, W), lambda b, i: (b * nq + i, ZF_X)), mspec, mspec],
        out_specs=pl.BlockSpec((tq, W), lambda b, i: (b * nq + i, 0)),
        out_shape=jax.ShapeDtypeStruct((batch * T, W), BF16),
        compiler_params=_params(("parallel", "parallel"), 48),
        name="xattn_prompt",
    )(zf, mk, mv)


def _xattn_decode_kernel(seq_len, q_ref, k_ref, v_ref, o_ref):
    scale = LANE ** -0.5
    rows = DEC_GROUP * seq_len
    lseq = seq_len.bit_length() - 1
    lrows = rows.bit_length() - 1
    nk = k_ref.shape[1]
    row_s = lax.broadcasted_iota(jnp.int32, (XA_HEADS * rows, nk), 0)
    col_s = lax.broadcasted_iota(jnp.int32, (XA_HEADS * rows, nk), 1)
    own_head = (col_s & (XA_HEADS - 1)) == (row_s >> lrows)
    seq_s = (row_s & (rows - 1)) >> lseq
    seq_o = (lax.broadcasted_iota(jnp.int32, (XA_HEADS * rows, LANE), 0) & (rows - 1)) >> lseq
    for grp in range(k_ref.shape[0] // DEC_GROUP):
        rs = slice(grp * rows, (grp + 1) * rows)
        qb = jnp.concatenate([q_ref[rs, h * LANE:(h + 1) * LANE] for h in range(XA_HEADS)], axis=0).astype(BF16)
        s = jnp.full((XA_HEADS * rows, nk), -jnp.inf, F32)
        for k in range(DEC_GROUP):
            sk = _dot_nt(qb, k_ref[grp * DEC_GROUP + k].astype(BF16))
            s = jnp.where(own_head & (seq_s == k), sk, s)
        p = _softmax_rows(s * scale).astype(BF16)
        o = jnp.zeros((XA_HEADS * rows, LANE), F32)
        for k in range(DEC_GROUP):
            o = jnp.where(seq_o == k, _dot(p, v_ref[grp * DEC_GROUP + k].astype(BF16)), o)
        for h in range(XA_HEADS):
            o_ref[rs, h * LANE:(h + 1) * LANE] = o[h * rows:(h + 1) * rows, :].astype(BF16)


def _merge_kernel(x_ref, oh_ref, op_ref, ox_ref, g0_ref, g1_ref, g2_ref, wb_ref, wo_ref, o_ref, *rest):
    y_ref = rest[-1]
    wbb_ref, wob_ref = rest[:2] if len(rest) == 3 else (None, None)
    j = pl.program_id(1)
    nj, _, tn = y_ref.shape

    @pl.when(j < nj)
    def _():
        wh = oh_ref.shape[1]
        wp = op_ref.shape[1]
        oh, op, ox = oh_ref[...].astype(BF16), op_ref[...].astype(BF16), ox_ref[...].astype(BF16)
        cw = min(MXU_COLS, tn)
        for c in range(tn // cw):
            cs = slice(c * cw, (c + 1) * cw)
            w = wb_ref[:, cs].astype(BF16)
            if wbb_ref is not None:
                wbb_ref[:, cs] = w
            y = _sigmoid(g0_ref[:, cs].astype(F32)) * _dot(oh, w[0:wh, :])
            y += _sigmoid(g1_ref[:, cs].astype(F32)) * _dot(op, w[wh:wh + wp, :])
            y += _sigmoid(g2_ref[:, cs].astype(F32)) * _dot(ox, w[wh + wp:, :])
            y_ref[j, :, cs] = y.astype(BF16)

    @pl.when(j >= nj)
    def _():
        acc = x_ref[...]
        for c in range(nj):
            rs = slice(c * tn, (c + 1) * tn)
            w = wo_ref[rs, :].astype(BF16)
            if wob_ref is not None:
                wob_ref[rs, :] = w
            acc += _dot(y_ref[c], w)
        o_ref[...] = acc


def _merge_resident_kernel(x_ref, oh_ref, op_ref, ox_ref, g_ref, wb_ref, wo_ref, o_ref, y_ref):
    D = x_ref.shape[1]
    wh = oh_ref.shape[1]
    wp = op_ref.shape[1]
    oh, op, ox = oh_ref[...].astype(BF16), op_ref[...].astype(BF16), ox_ref[...].astype(BF16)
    cw = min(2 * MXU_COLS, D)
    for c in range(D // cw):
        cs = slice(c * cw, (c + 1) * cw)
        gate = lambda k: _sigmoid(g_ref[:, k * D + c * cw:k * D + (c + 1) * cw].astype(F32))
        y = gate(0) * _dot(oh, wb_ref[0:wh, cs])
        y += gate(1) * _dot(op, wb_ref[wh:wh + wp, cs])
        y += gate(2) * _dot(ox, wb_ref[wh + wp:, cs])
        y_ref[:, cs] = y.astype(BF16)
    for c in range(D // cw):
        cs = slice(c * cw, (c + 1) * cw)
        o_ref[:, cs] = x_ref[:, cs] + _dot(y_ref[...], wo_ref[:, cs])


def _merge_resident(x, oh, op, ox, zg, wb, wo, tm):
    T, D = x.shape
    row = lambda a: pl.BlockSpec((tm, a.shape[1]), lambda i: (i, 0))
    held = lambda a: pl.BlockSpec(a.shape, lambda i: (0, 0), pipeline_mode=pl.Buffered(1))
    return pl.pallas_call(
        _merge_resident_kernel,
        grid=(T // tm,),
        in_specs=[row(x), row(oh), row(op), row(ox), row(zg), held(wb), held(wo)],
        out_specs=row(x),
        out_shape=jax.ShapeDtypeStruct((T, D), F32),
        scratch_shapes=[pltpu.VMEM((tm, D), BF16)],
        compiler_params=_params(("parallel",), 56),
        name="merge_resident",
    )(x, oh, op, ox, zg, wb, wo)


def _merge(x, oh, op, ox, zb, wb, wo, layer, tm, tn, emit=False):
    T, D = x.shape
    assert not emit or T == tm
    nj = D // tn
    first = lambda j: jnp.minimum(j, nj - 1)
    second = lambda j: jnp.maximum(j - nj, 0)
    full = lambda a: pl.BlockSpec((tm, a.shape[1]), lambda i, j: (i, 0))
    gate = lambda k: pl.BlockSpec((tm, tn), lambda i, j: (i, k * nj + first(j)))
    wb_idx = lambda i, j: (0, first(j))
    wo_idx = lambda i, j: (0, second(j))
    rows_b = wb.shape[-2]
    out_specs = [pl.BlockSpec((tm, tn), lambda i, j: (i, second(j)))]
    out_shape = [jax.ShapeDtypeStruct((T, D), F32)]
    if emit:
        out_specs += [pl.BlockSpec((rows_b, tn), wb_idx), pl.BlockSpec((D, tn), wo_idx)]
        out_shape += [jax.ShapeDtypeStruct((rows_b, D), BF16), jax.ShapeDtypeStruct((D, D), BF16)]
    out = pl.pallas_call(
        _merge_kernel,
        grid=(T // tm, 2 * nj),
        in_specs=[
            pl.BlockSpec((tm, tn), lambda i, j: (i, second(j))),
            full(oh), full(op), full(ox), gate(0), gate(1), gate(2),
            _layer_spec(wb, layer, (rows_b, tn), wb_idx),
            _layer_spec(wo, layer, (D, tn), wo_idx),
        ],
        out_specs=out_specs,
        out_shape=out_shape,
        scratch_shapes=[pltpu.VMEM((nj, tm, tn), BF16)],
        compiler_params=_params(("parallel", "arbitrary"), 48),
        name="merge_out",
    )(x, oh, op, ox, zb, zb, zb, wb, wo)
    return out if emit else out[0]


def kernel(x_prompt, x_sample, state_hgrn, state_pool, cache_mem_k, cache_mem_v, mem_prompt, ffn1_norm, ffn1_w1, ffn1_w3, ffn1_w2, mix_norm, w_in, lb_logits, hg_norm, w_pool, pool_scale, mem_norm, w_mk, w_mv, w_branch, w_out, ffn2_norm, ffn2_w1, ffn2_w3, ffn2_w2, final_norm):
    B, T, D = x_prompt.shape
    nseq, steps, _ = x_sample.shape
    depth = w_in.shape[0]
    n_mem = mem_prompt.shape[1]
    xw = XA_HEADS * LANE
    pw = len(POOL_WINDOWS) * LANE

    tm_p = min(1024, B * T)
    tm_s = min(512, nseq * steps)

    vec3 = lambda a: a.reshape(a.shape[0], 1, a.shape[1])
    f1n, f2n = vec3(ffn1_norm), vec3(ffn2_norm)
    mix3, mem3, hgn3, psc3 = vec3(mix_norm), vec3(mem_norm), vec3(hg_norm), vec3(pool_scale)
    lbs3 = vec3(_lower_bounds(lb_logits))

    xp = x_prompt.reshape(B * T, D)
    xs = x_sample.reshape(nseq * steps, D)
    mem2 = mem_prompt.reshape(B * n_mem, D)
    ck = cache_mem_k.reshape(depth, nseq, n_mem * XA_HEADS, LANE)
    cv = cache_mem_v.reshape(depth, nseq, n_mem * XA_HEADS, LANE)

    tf = _tile(ffn1_w1.shape[-1], 512)
    tf_s = _tile(ffn1_w1.shape[-1], 256)
    tn_in = _tile(math.gcd(HG_HEADS * LANE, N_BRANCH * D), 1024)
    tn_mrg = _tile(D, 512)
    tm_mem = min(512, B * n_mem)
    hs_p, pb_p, mk_p, mv_p, pb_s = [], [], [], [], []
    hs_s = None

    blocks = []
    for l in range(depth):
        blocks.append((f1n, ffn1_w1, ffn1_w3, ffn1_w2, l))
        blocks.append((f2n, ffn2_w1, ffn2_w3, ffn2_w2, l))

    def swiglu_pair(xp, xs, wbf, k, final_g):
        norm, w1, w3, w2, l = blocks[k]
        if wbf is None:
            xs, *wbf = _ffn(xs, norm, w1, w3, w2, l, tm_s, tf_s, final_g)
        else:
            xs = _ffn(xs, norm, *wbf, l, tm_s, tf, final_g)
        if k + 1 < len(blocks):
            xp, *nxt = _ffn(xp, norm, *wbf, l, tm_p, tf, final_g, side=blocks[k + 1][1:])
        else:
            xp, nxt = _ffn(xp, norm, *wbf, l, tm_p, tf, final_g), None
        return xp, xs, nxt

    wbf = w_in_b = None
    for l in range(depth):
        mk = _rms_matmul(mem2, mem3, w_mk, l, tm_mem, xw)
        mv = _rms_matmul(mem2, mem3, w_mv, l, tm_mem, xw)
        mk_p.append(mk.reshape(B, n_mem, XA_HEADS, LANE))
        mv_p.append(mv.reshape(B, n_mem, XA_HEADS, LANE))

        xp, xs, wbf = swiglu_pair(xp, xs, wbf, 2 * l, None)

        if w_in_b is None:
            zsb, zsf, zsg, w_in_b = _in_proj(xs, mix3, w_in, l, tm_s, tn_in, emit=True)
        else:
            zsb, zsf, zsg = _in_proj(xs, mix3, w_in_b, l, tm_s, tn_in)
        if l + 1 < depth:
            zpb, zpf, zpg, w_in_next = _in_proj(xp, mix3, w_in_b, l, tm_p, tn_in, side=(w_in, l + 1))
        else:
            (zpb, zpf, zpg), w_in_next = _in_proj(xp, mix3, w_in_b, l, tm_p, tn_in), None
        w_in_b = w_in_next

        ohp, sp = _hgrn_prompt(zpb, zpf, lbs3, hgn3, l, B)
        ohs, hs_s, oxs = _decode_mixers(zsb, zsf, lbs3, hgn3, state_hgrn, hs_s, ck, cv, l, nseq, 8)
        hs_p.append(sp)

        opp, bp = _pool_prompt(zpf, w_pool, psc3, l, B)
        pb_p.append(bp)
        u_t = zsf[:, ZF_U * pw:(ZF_U + 1) * pw].reshape(nseq, steps, pw).transpose(1, 0, 2)
        buf_t = state_pool[l].transpose(1, 0, 2)
        ops_t, nb_t = _pool_decode(u_t, buf_t, w_pool, psc3, l)
        ops = ops_t.transpose(1, 0, 2).reshape(nseq * steps, pw)
        pb_s.append(nb_t.transpose(1, 0, 2))

        oxp = _xattn_prompt(zpf, mk.reshape(B, n_mem, xw), mv.reshape(B, n_mem, xw), B, min(1024, T))

        xs, w_br_b, w_out_b = _merge(xs, ohs, ops, oxs, zsg, w_branch, w_out, l, tm_s, tn_mrg, emit=True)
        xp = _merge_resident(xp, ohp, opp, oxp, zpg, w_br_b, w_out_b, min(512, B * T))

        fn2 = final_norm.reshape(1, D) if l == depth - 1 else None
        xp, xs, wbf = swiglu_pair(xp, xs, wbf, 2 * l + 1, fn2)

    y_prompt = xp.reshape(B, T, D)
    y_sample = xs.reshape(nseq, steps, D)
    return (y_prompt, y_sample, jnp.stack(hs_p), jnp.stack(pb_p), jnp.stack(mk_p), jnp.stack(mv_p),
            hs_s, jnp.stack(pb_s))
```

```python
import functools
import math

import jax
import jax.numpy as jnp
from jax import lax
from jax.experimental import pallas as pl
from jax.experimental.pallas import tpu as pltpu

F32 = jnp.float32
BF16 = jnp.bfloat16
EPS = 1e-6

LANE = 128
SUBLANE = 8
MXU_COLS = 256
HG_HEADS = 8
POOL_WINDOWS = (2, 4, 8, 16)
POOL_BUF = max(POOL_WINDOWS) - 1
XA_HEADS = 4
N_BRANCH = 3

COL_Q, COL_F, COL_I, COL_OG = 0, HG_HEADS, 2 * HG_HEADS, 3 * HG_HEADS
COL_U = 4 * HG_HEADS
COL_X = COL_U + len(POOL_WINDOWS)
COL_G = COL_X + XA_HEADS
ZB_Q, ZB_V, ZB_OG = 0, 1, 2
ZF_F = 0
ZF_U, ZF_X = 2, 3

HGRN_CHUNK = 128
HGRN_STEP_CHUNKS = 2
DEC_GROUP = 4


def _tile(n, preferred):
    t = preferred
    while n % t:
        t -= LANE
    return t


def _params(semantics, vmem_mib):
    return pltpu.CompilerParams(dimension_semantics=semantics, vmem_limit_bytes=vmem_mib * 1024 * 1024)


def _rms(x, g):
    return x * lax.rsqrt(jnp.mean(x * x, axis=-1, keepdims=True) + EPS) * g


def _sigmoid(x):
    return 0.5 * jnp.tanh(0.5 * x) + 0.5


def _silu(x):
    return x * _sigmoid(x)


def _dot(a, b):
    return jnp.dot(a, b, preferred_element_type=F32)


def _dot_nt(a, b):
    return lax.dot_general(a, b, (((1,), (1,)), ((), ())), preferred_element_type=F32)


def _dot_tn(a, b):
    return lax.dot_general(a, b, (((0,), (0,)), ((), ())), preferred_element_type=F32)


def _ffn_kernel(final, cast, side, x_ref, g_ref, w1_ref, w3_ref, w2_ref, *rest):
    rest = list(rest)
    xn_ref = rest.pop()
    fn_ref = rest.pop(0) if final else None
    side_in = [rest.pop(0) for _ in range(3)] if side else []
    o_ref = rest.pop(0)
    if cast:
        src = (w1_ref, w3_ref, w2_ref)
        w1_ref, w3_ref, w2_ref = [rest.pop(0) for _ in range(3)]
        for dst, s in zip((w1_ref, w3_ref, w2_ref), src):
            dst[...] = s[...].astype(BF16)
    for s, dst in zip(side_in, rest):
        dst[...] = s[...].astype(BF16)
    j = pl.program_id(1)

    @pl.when(j == 0)
    def _():
        x = x_ref[...]
        xn_ref[...] = _rms(x, g_ref[...]).astype(BF16)
        o_ref[...] = x

    xn = xn_ref[...]
    h = (_silu(_dot(xn, w1_ref[...])) * _dot(xn, w3_ref[...])).astype(BF16)
    cw = min(w2_ref.shape[0], o_ref.shape[1])
    for n in range(o_ref.shape[1] // cw):
        cs = slice(n * cw, (n + 1) * cw)
        o_ref[:, cs] += 0.5 * _dot(h, w2_ref[:, cs])

    if final:
        @pl.when(j == pl.num_programs(1) - 1)
        def _():
            o_ref[...] = _rms(o_ref[...], fn_ref[...])


def _layer_spec(w, layer, block, index):
    if w.ndim == 2:
        return pl.BlockSpec(block, index)
    return pl.BlockSpec((None,) + block, lambda i, j: (layer,) + index(i, j))


def _ffn(x, g3, w1, w3, w2, layer, tm, tf, final_g=None, side=None):
    T, D = x.shape
    F = w1.shape[-1]
    cast = w1.dtype != BF16
    final = final_g is not None
    assert not cast or T == tm
    ni, nj = T // tm, F // tf
    col = lambda i, j: (0, j)
    row = lambda i, j: (j, 0)
    out_specs = [pl.BlockSpec((tm, D), lambda i, j: (i, 0))]
    out_shape = [jax.ShapeDtypeStruct((T, D), F32)]
    wshapes = [jax.ShapeDtypeStruct((D, F), BF16), jax.ShapeDtypeStruct((D, F), BF16),
               jax.ShapeDtypeStruct((F, D), BF16)]
    if cast:
        out_specs += [pl.BlockSpec((D, tf), col), pl.BlockSpec((D, tf), col), pl.BlockSpec((tf, D), row)]
        out_shape += wshapes
    in_specs = [
        pl.BlockSpec((tm, D), lambda i, j: (i, 0)),
        pl.BlockSpec((None, 1, D), lambda i, j: (layer, 0, 0)),
        _layer_spec(w1, layer, (D, tf), col),
        _layer_spec(w3, layer, (D, tf), col),
        _layer_spec(w2, layer, (tf, D), row),
    ]
    args = [x, g3, w1, w3, w2]
    if final:
        in_specs.append(pl.BlockSpec((1, D), lambda i, j: (0, 0)))
        args.append(final_g)
    if side is not None:
        s1, s3, s2, side_layer = side
        assert D % (ni * LANE) == 0
        rb = D // ni
        up, down = (lambda i, j: (i, j)), (lambda i, j: (j, i))
        in_specs += [_layer_spec(s1, side_layer, (rb, tf), up), _layer_spec(s3, side_layer, (rb, tf), up),
                     _layer_spec(s2, side_layer, (tf, rb), down)]
        args += [s1, s3, s2]
        out_specs += [pl.BlockSpec((rb, tf), up), pl.BlockSpec((rb, tf), up), pl.BlockSpec((tf, rb), down)]
        out_shape += wshapes
    out = pl.pallas_call(
        functools.partial(_ffn_kernel, final, cast, side is not None),
        grid=(ni, nj),
        in_specs=in_specs,
        out_specs=out_specs,
        out_shape=out_shape,
        scratch_shapes=[pltpu.VMEM((tm, D), BF16)],
        compiler_params=_params(("parallel", "arbitrary"), 62),
        name="ffn_cast" if cast else "ffn",
    )(*args)
    return out if len(out) > 1 else out[0]


def _rms_matmul_kernel(x_ref, g_ref, w_ref, o_ref, *rest):
    xn_ref = rest[-1]

    @pl.when(pl.program_id(1) == 0)
    def _():
        xn_ref[...] = _rms(x_ref[...], g_ref[...]).astype(BF16)

    w = w_ref[...].astype(BF16)
    if len(rest) == 2:
        rest[0][...] = w
    o_ref[...] = _dot(xn_ref[...], w)


def _rms_matmul(x, g3, w, layer, tm, tn, emit=False):
    T, D = x.shape
    N = w.shape[-1]
    assert not emit or T == tm
    col = lambda i, j: (0, j)
    out_specs = [pl.BlockSpec((tm, tn), lambda i, j: (i, j))]
    out_shape = [jax.ShapeDtypeStruct((T, N), F32)]
    if emit:
        out_specs.append(pl.BlockSpec((D, tn), col))
        out_shape.append(jax.ShapeDtypeStruct((D, N), BF16))
    out = pl.pallas_call(
        _rms_matmul_kernel,
        grid=(T // tm, N // tn),
        in_specs=[
            pl.BlockSpec((tm, D), lambda i, j: (i, 0)),
            pl.BlockSpec((None, 1, D), lambda i, j: (layer, 0, 0)),
            _layer_spec(w, layer, (D, tn), col),
        ],
        out_specs=out_specs,
        out_shape=out_shape,
        scratch_shapes=[pltpu.VMEM((tm, D), BF16)],
        compiler_params=_params(("parallel", "arbitrary"), 48),
        name="rms_matmul",
    )(x, g3, w)
    return out if emit else out[0]


def _runs(seq):
    runs = []
    for j, v in enumerate(seq):
        if runs and runs[-1][2] == v - j:
            runs[-1][1] = j + 1
        else:
            runs.append([j, j + 1, v - j])
    return [tuple(r) for r in runs]


def _in_runs(j, runs):
    hit = None
    for lo, hi, _ in runs:
        c = (j >= lo) & (j < hi)
        hit = c if hit is None else hit | c
    return hit


def _lookup(j, runs):
    out = 0
    for lo, hi, off in runs:
        out = out + jnp.where((j >= lo) & (j < hi), j + off, 0)
    return out


def _in_proj_plan(d_model, tn):
    t = lambda blocks: blocks * LANE // tn
    nq, nux, ng = t(HG_HEADS), t(len(POOL_WINDOWS) + XA_HEADS), N_BRANCH * d_model // tn
    src = lambda start, n: list(range(t(start), t(start) + n))
    order = src(COL_Q, nq) + src(COL_I, nq) + src(COL_OG, nq) + src(COL_F, nq) + src(COL_U, nux) + src(COL_G, ng)
    nh, n32 = 3 * nq, nq + nux
    kinds = {"f32": [(nh, nh + n32, 0)]}
    return _runs(order), kinds, nh, n32


def _in_proj_kernel(kinds, side, x_ref, g_ref, w_ref, *rest):
    rest = list(rest)
    xn_ref = rest.pop()
    if side:
        side_in, side_out = rest.pop(0), rest.pop()
        side_out[...] = side_in[...].astype(BF16)
    zh_ref, zf_ref, zg_ref = rest[:3]
    rest = rest[3:] + [xn_ref]
    j = pl.program_id(1)

    @pl.when(j == 0)
    def _():
        xn_ref[...] = _rms(x_ref[...], g_ref[...]).astype(BF16)

    xn = xn_ref[...]
    tn = w_ref.shape[1]
    cw = min(MXU_COLS, tn)
    zs = []
    for c in range(tn // cw):
        cs = slice(c * cw, (c + 1) * cw)
        w = w_ref[:, cs].astype(BF16)
        if len(rest) == 2:
            rest[0][:, cs] = w
        z = _dot(xn, w)
        zs.append(z)
        zh_ref[:, cs] = z.astype(BF16)
        zg_ref[:, cs] = z.astype(BF16)

    @pl.when(_in_runs(j, kinds["f32"]))
    def _():
        for c, z in enumerate(zs):
            zf_ref[:, c * cw:(c + 1) * cw] = z


def _in_proj(x, g3, w, layer, tm, tn, emit=False, side=None):
    T, D = x.shape
    N = w.shape[-1]
    assert not emit or T == tm
    order, kinds, nh, n32 = _in_proj_plan(D, tn)
    nsteps = N // tn
    wcol = lambda i, j: (0, _lookup(j, order))
    out_specs = [
        pl.BlockSpec((tm, tn), lambda i, j: (i, jnp.minimum(j, nh))),
        pl.BlockSpec((tm, tn), lambda i, j: (i, jnp.clip(j - nh, 0, n32 - 1))),
        pl.BlockSpec((tm, tn), lambda i, j: (i, jnp.maximum(j - nh - n32, 0))),
    ]
    out_shape = [
        jax.ShapeDtypeStruct((T, (nh + 1) * tn), BF16),
        jax.ShapeDtypeStruct((T, n32 * tn), F32),
        jax.ShapeDtypeStruct((T, (nsteps - nh - n32) * tn), BF16),
    ]
    if emit:
        out_specs.append(pl.BlockSpec((D, tn), wcol))
        out_shape.append(jax.ShapeDtypeStruct((D, N), BF16))
    in_specs = [
        pl.BlockSpec((tm, D), lambda i, j: (i, 0)),
        pl.BlockSpec((None, 1, D), lambda i, j: (layer, 0, 0)),
        _layer_spec(w, layer, (D, tn), wcol),
    ]
    args = [x, g3, w]
    if side is not None:
        rb = D // (T // tm)
        assert rb % SUBLANE == 0
        in_specs.append(_layer_spec(side[0], side[1], (rb, tn), lambda i, j: (i, j)))
        args.append(side[0])
        out_specs.append(pl.BlockSpec((rb, tn), lambda i, j: (i, j)))
        out_shape.append(jax.ShapeDtypeStruct((D, N), BF16))
    return pl.pallas_call(
        functools.partial(_in_proj_kernel, kinds, side is not None),
        grid=(T // tm, nsteps),
        in_specs=in_specs,
        out_specs=out_specs,
        out_shape=out_shape,
        scratch_shapes=[pltpu.VMEM((tm, D), BF16)],
        compiler_params=_params(("parallel", "arbitrary"), 58),
        name="in_proj",
    )(*args)


def _lower_bound_kernel(lg_ref, o_ref):
    lg = lg_ref[...]
    depth = lg.shape[0]
    rows = [lg[i:i + 1, :] for i in range(depth)]
    m = rows[0]
    for r in rows[1:]:
        m = jnp.maximum(m, r)
    e = [jnp.exp(r - m) for r in rows]
    tot = e[0]
    for v in e[1:]:
        tot = tot + v
    c = e[0] / tot
    first = c
    o_ref[0:1, :] = c - first
    for i in range(1, depth):
        c = c + e[i] / tot
        o_ref[i:i + 1, :] = c - first


def _lower_bounds(lb_logits):
    return pl.pallas_call(
        _lower_bound_kernel,
        out_shape=jax.ShapeDtypeStruct(lb_logits.shape, F32),
        name="hgrn_lower_bounds",
    )(lb_logits)


def _neg_abs(x):
    bits = lax.bitcast_convert_type(x, jnp.uint32) | jnp.uint32(0x80000000)
    return lax.bitcast_convert_type(bits, F32)


def _group_ref_row(b, s):
    rows, width = b.shape
    gsz = 2 * s
    if gsz >= 8:
        parts = [jnp.broadcast_to(b[i * gsz + s - 1:i * gsz + s, :], (gsz, width)) for i in range(rows // gsz)]
        return parts[0] if len(parts) == 1 else jnp.concatenate(parts, axis=0)
    pos = lax.broadcasted_iota(jnp.int32, b.shape, 0) & (gsz - 1)
    out = b
    for p in range(gsz):
        d = p - (s - 1)
        if d != 0:
            out = jnp.where(pos == p, pltpu.roll(b, d % rows, 0), out)
    return out


def _hgrn_wide(q, zf, vb, lb, states, seg):
    C, W = q.shape
    heads = [slice(h * LANE, (h + 1) * LANE) for h in range(W // LANE)]
    nseg = C // seg
    fg = lb + (1.0 - lb) * jax.nn.sigmoid(zf)
    kk = 1.0 - fg
    g = jnp.log2(fg)

    row = lax.broadcasted_iota(jnp.int32, (C, W), 0)
    rr = lax.broadcasted_iota(jnp.int32, (C, C), 0)
    cc = lax.broadcasted_iota(jnp.int32, (C, C), 1)
    lseg = seg.bit_length() - 1
    tri = jnp.where((cc <= rr) & ((rr >> lseg) == (cc >> lseg)), 1.0, 0.0).astype(F32)
    b = jnp.dot(tri, g, precision=lax.Precision.HIGHEST, preferred_element_type=F32)

    qb, kb = q.astype(BF16), kk.astype(BF16)
    a = [jnp.where(rr == cc, _dot_nt(qb[:, hs], kb[:, hs]), 0.0) for hs in heads]
    s = seg // 2
    while s >= 1:
        ls = s.bit_length() - 1
        same = (rr >> (ls + 1)) == (cc >> (ls + 1))
        if s == 1:
            right = (row & 1) == 1
            qs, ks = jnp.where(right, q * fg, 0.0).astype(BF16), jnp.where(right, 0.0, kk).astype(BF16)
            for h, hs in enumerate(heads):
                a[h] = a[h] + jnp.where(same, _dot_nt(qs[:, hs], ks[:, hs]), 0.0)
            break
        e = jnp.exp2(_neg_abs(b - _group_ref_row(b, s)))
        if s % SUBLANE == 0:
            nblk = C // s
            zero = jnp.zeros((s, W), F32)
            blk = lambda x, i: x[i * s:(i + 1) * s, :]
            qs = jnp.concatenate([blk(q, i) * blk(e, i) if i % 2 else zero for i in range(nblk)], axis=0).astype(BF16)
            ks = jnp.concatenate([zero if i % 2 else blk(kk, i) * blk(e, i) for i in range(nblk)], axis=0).astype(BF16)
            for h, hs in enumerate(heads):
                p = _dot_nt(qs[:, hs], ks[:, hs])
                pieces = []
                for i in range(nblk):
                    if i % 2 == 0:
                        pieces.append(blk(a[h], i))
                    elif 2 * s == C:
                        pieces.append(blk(a[h], i) + blk(p, i))
                    else:
                        pieces.append(blk(a[h], i) + jnp.where(blk(same, i), blk(p, i), 0.0))
                a[h] = jnp.concatenate(pieces, axis=0)
        else:
            right = ((row >> ls) & 1) == 1
            eq = jnp.where(right, e, 0.0)
            qs, ks = (q * eq).astype(BF16), (kk * (e - eq)).astype(BF16)
            for h, hs in enumerate(heads):
                a[h] = a[h] + jnp.where(same, _dot_nt(qs[:, hs], ks[:, hs]), 0.0)
        s //= 2
    o = [_dot(a[h].astype(BF16), vb[:, hs]) for h, hs in enumerate(heads)]

    qe = q * jnp.exp2(b)
    new_states = [[] for _ in heads]
    for k in range(nseg):
        bl = b[k * seg + seg - 1:k * seg + seg, :]
        if nseg == 1:
            qk = qe
            ke = kk * jnp.exp2(bl - b)
        else:
            mine = (row >> lseg) == k
            qk = jnp.where(mine, qe, 0.0)
            ke = jnp.where(mine, kk * jnp.exp2(jnp.where(mine, bl - b, 0.0)), 0.0)
        qkb, keb, ebl = qk.astype(BF16), ke.astype(BF16), jnp.exp2(bl)
        for h, hs in enumerate(heads):
            st = states[h][k]
            o[h] = o[h] + _dot(qkb[:, hs], st.astype(BF16))
            decay = jnp.transpose(jnp.broadcast_to(ebl[:, hs], (LANE, LANE)))
            new_states[h].append(st * decay + _dot_tn(keb[:, hs], vb[:, hs]))
    return o, new_states


def _hgrn_finish(o, zog, hgn):
    return (_rms(o, hgn) * _silu(zog.astype(F32))).astype(BF16)


def _hgrn_prompt_kernel(q_ref, zf_ref, v_ref, og_ref, lb_ref, hgn_ref, oh_ref, s_ref):
    @pl.when(pl.program_id(1) == 0)
    def _():
        s_ref[...] = jnp.zeros_like(s_ref)

    states = [[s_ref[h]] for h in range(HG_HEADS)]
    for sub in range(q_ref.shape[0] // HGRN_CHUNK):
        rs = slice(sub * HGRN_CHUNK, (sub + 1) * HGRN_CHUNK)
        q = _silu(q_ref[rs, :].astype(F32))
        o, states = _hgrn_wide(q, zf_ref[rs, :], v_ref[rs, :], lb_ref[...], states, HGRN_CHUNK)
        for h in range(HG_HEADS):
            cs = slice(h * LANE, (h + 1) * LANE)
            oh_ref[rs, cs] = _hgrn_finish(o[h], og_ref[rs, cs], hgn_ref[:, cs])
    for h in range(HG_HEADS):
        s_ref[h] = states[h][0]


def _hgrn_prompt(zb, zf, lbs3, hgn3, layer, batch):
    T = zb.shape[0] // batch
    C = HGRN_STEP_CHUNKS * HGRN_CHUNK
    nc = T // C
    W = HG_HEADS * LANE
    zspec = lambda col: pl.BlockSpec((C, W), lambda b, c: (b * nc + c, col))
    vec = pl.BlockSpec((None, 1, W), lambda b, c: (layer, 0, 0))
    return pl.pallas_call(
        _hgrn_prompt_kernel,
        grid=(batch, nc),
        in_specs=[zspec(ZB_Q), zspec(ZF_F), zspec(ZB_V), zspec(ZB_OG), vec, vec],
        out_specs=[
            pl.BlockSpec((C, W), lambda b, c: (b * nc + c, 0)),
            pl.BlockSpec((None, HG_HEADS, LANE, LANE), lambda b, c: (b, 0, 0, 0)),
        ],
        out_shape=[
            jax.ShapeDtypeStruct((batch * T, W), BF16),
            jax.ShapeDtypeStruct((batch, HG_HEADS, LANE, LANE), F32),
        ],
        compiler_params=_params(("parallel", "arbitrary"), 32),
        name="hgrn_prompt",
    )(zb, zf, zb, zb, lbs3, hgn3)


def _hgrn_decode_kernel(seq_len, q_ref, zf_ref, v_ref, og_ref, lb_ref, hgn_ref, s_ref, xq_ref, k_ref, vc_ref, *rest):
    oh_ref, so_ref, ox_ref = rest[-3:]
    _xattn_decode_kernel(seq_len, xq_ref, k_ref, vc_ref, ox_ref)
    rows = DEC_GROUP * seq_len
    for grp in range(s_ref.shape[0] // DEC_GROUP):
        rs = slice(grp * rows, (grp + 1) * rows)
        states = [[s_ref[grp * DEC_GROUP + k, h] for k in range(DEC_GROUP)] for h in range(HG_HEADS)]
        o, new = _hgrn_wide(_silu(q_ref[rs, :].astype(F32)), zf_ref[rs, :], v_ref[rs, :], lb_ref[...], states, seq_len)
        for h in range(HG_HEADS):
            cs = slice(h * LANE, (h + 1) * LANE)
            for k in range(DEC_GROUP):
                so_ref[grp * DEC_GROUP + k, h] = new[h][k]
            oh_ref[rs, cs] = _hgrn_finish(o[h], og_ref[rs, cs], hgn_ref[:, cs])


def _decode_mixers(zb, zf, lbs3, hgn3, state, stacked, cache_k, cache_v, layer, nseq, nb):
    seq_len = zb.shape[0] // nseq
    W = HG_HEADS * LANE
    XW = XA_HEADS * LANE
    zspec = lambda col: pl.BlockSpec((nb * seq_len, W), lambda i: (i, col))
    vec = pl.BlockSpec((None, 1, W), lambda i: (layer, 0, 0))
    sspec = pl.BlockSpec((None, nb, HG_HEADS, LANE, LANE), lambda i: (layer, i, 0, 0, 0))
    cspec = pl.BlockSpec((None, nb) + cache_k.shape[2:], lambda i: (layer, i, 0, 0))
    in_specs = [zspec(ZB_Q), zspec(ZF_F), zspec(ZB_V), zspec(ZB_OG), vec, vec, sspec,
                pl.BlockSpec((nb * seq_len, XW), lambda i: (i, ZF_X)), cspec, cspec]
    args = [zb, zf, zb, zb, lbs3, hgn3, state, zf, cache_k, cache_v]
    aliases = {}
    if stacked is not None:
        in_specs.append(pl.BlockSpec(memory_space=pl.ANY))
        args.append(stacked)
        aliases = {len(args) - 1: 1}
    return pl.pallas_call(
        functools.partial(_hgrn_decode_kernel, seq_len),
        grid=(nseq // nb,),
        in_specs=in_specs,
        out_specs=[pl.BlockSpec((nb * seq_len, W), lambda i: (i, 0)), sspec,
                   pl.BlockSpec((nb * seq_len, XW), lambda i: (i, 0))],
        out_shape=[
            jax.ShapeDtypeStruct((nseq * seq_len, W), BF16),
            jax.ShapeDtypeStruct(state.shape, F32),
            jax.ShapeDtypeStruct((nseq * seq_len, XW), BF16),
        ],
        input_output_aliases=aliases,
        compiler_params=_params(("parallel",), 58),
        name="decode_mixers",
    )(*args)


def _recurrent_mixers_kernel(seq_len, has_alias, qp, zfp, vp, ogp, lb, hgn, qs, zfs, vs, ogs, s_ref, xq, k, vc, *rest):
    ohp, sp, ohs, so, oxs = rest[int(has_alias):]
    _hgrn_prompt_kernel(qp, zfp, vp, ogp, lb, hgn, ohp, sp)
    _hgrn_decode_kernel(seq_len, qs, zfs, vs, ogs, lb, hgn, s_ref, xq, k, vc, ohs, so, oxs)


def _recurrent_mixers(zpb, zpf, zsb, zsf, lbs3, hgn3, state, stacked, cache_k, cache_v, layer, batch, nseq):
    T = zpb.shape[0] // batch
    C = HGRN_STEP_CHUNKS * HGRN_CHUNK
    nc = T // C
    nb = nseq // (batch * nc)
    assert nb * batch * nc == nseq and nb % DEC_GROUP == 0
    seq_len = zsb.shape[0] // nseq
    W = HG_HEADS * LANE
    XW = XA_HEADS * LANE
    pspec = lambda col: pl.BlockSpec((C, W), lambda b, c: (b * nc + c, col))
    dspec = lambda width, col: pl.BlockSpec((nb * seq_len, width), lambda b, c: (b * nc + c, col))
    vec = pl.BlockSpec((None, 1, W), lambda b, c: (layer, 0, 0))
    sspec = pl.BlockSpec((None, nb, HG_HEADS, LANE, LANE), lambda b, c: (layer, b * nc + c, 0, 0, 0))
    cspec = pl.BlockSpec((None, nb) + cache_k.shape[2:], lambda b, c: (layer, b * nc + c, 0, 0))
    in_specs = [pspec(ZB_Q), pspec(ZF_F), pspec(ZB_V), pspec(ZB_OG), vec, vec,
                dspec(W, ZB_Q), dspec(W, ZF_F), dspec(W, ZB_V), dspec(W, ZB_OG), sspec, dspec(XW, ZF_X), cspec, cspec]
    args = [zpb, zpf, zpb, zpb, lbs3, hgn3, zsb, zsf, zsb, zsb, state, zsf, cache_k, cache_v]
    aliases = {}
    if stacked is not None:
        in_specs.append(pl.BlockSpec(memory_space=pl.ANY))
        args.append(stacked)
        aliases = {len(args) - 1: 3}
    return pl.pallas_call(
        functools.partial(_recurrent_mixers_kernel, seq_len, stacked is not None),
        grid=(batch, nc),
        in_specs=in_specs,
        out_specs=[
            pl.BlockSpec((C, W), lambda b, c: (b * nc + c, 0)),
            pl.BlockSpec((None, HG_HEADS, LANE, LANE), lambda b, c: (b, 0, 0, 0)),
            dspec(W, 0), sspec, dspec(XW, 0),
        ],
        out_shape=[
            jax.ShapeDtypeStruct((batch * T, W), BF16),
            jax.ShapeDtypeStruct((batch, HG_HEADS, LANE, LANE), F32),
            jax.ShapeDtypeStruct((nseq * seq_len, W), BF16),
            jax.ShapeDtypeStruct(state.shape, F32),
            jax.ShapeDtypeStruct((nseq * seq_len, XW), BF16),
        ],
        input_output_aliases=aliases,
        compiler_params=_params(("arbitrary", "arbitrary"), 48),
        name="recurrent_mixers",
    )(*args)


def _pool_prompt_kernel(u_ref, wp_ref, sc_ref, op_ref, nb_ref):
    u = u_ref[...]
    T = u.shape[0]
    row = lax.broadcasted_iota(jnp.int32, (T, LANE), 0)
    for g, w in enumerate(POOL_WINDOWS):
        cs = slice(g * LANE, (g + 1) * LANE)
        ug = u[:, cs]
        s = ug
        d = 1
        while d < w:
            s = s + jnp.where(row >= d, pltpu.roll(s, d, 0), 0.0)
            d *= 2
        cnt = jnp.minimum(row + 1, w).astype(F32)
        dv = s / cnt - ug
        y = _dot(dv.astype(BF16), wp_ref[g].astype(BF16)) * sc_ref[:, cs]
        op_ref[:, cs] = y.astype(BF16)
    tail = u_ref[T - 16:T, :]
    nb_ref[...] = pltpu.roll(tail, 15, 0)[0:POOL_BUF, :]


def _pool_prompt(zf, w_pool, scale3, layer, batch):
    T = zf.shape[0] // batch
    G = len(POOL_WINDOWS)
    W = G * LANE
    return pl.pallas_call(
        _pool_prompt_kernel,
        grid=(batch,),
        in_specs=[
            pl.BlockSpec((T, W), lambda b: (b, ZF_U)),
            pl.BlockSpec((None, G, LANE, LANE), lambda b: (layer, 0, 0, 0)),
            pl.BlockSpec((None, 1, W), lambda b: (layer, 0, 0)),
        ],
        out_specs=[
            pl.BlockSpec((T, W), lambda b: (b, 0)),
            pl.BlockSpec((None, POOL_BUF, W), lambda b: (b, 0, 0)),
        ],
        out_shape=[
            jax.ShapeDtypeStruct((batch * T, W), BF16),
            jax.ShapeDtypeStruct((batch, POOL_BUF, W), F32),
        ],
        compiler_params=_params(("parallel",), 48),
        name="pool_prompt",
    )(zf, w_pool, scale3)


def _pool_decode_kernel(u_ref, buf_ref, wp_ref, sc_ref, op_ref, nb_ref):
    steps = u_ref.shape[0]
    for g, w in enumerate(POOL_WINDOWS):
        cs = slice(g * LANE, (g + 1) * LANE)
        wg = wp_ref[g].astype(BF16)
        for t in range(steps):
            n_u = min(t + 1, w)
            acc = u_ref[t, :, cs]
            for j in range(t - n_u + 1, t):
                acc = acc + u_ref[j, :, cs]
            for i in range(POOL_BUF - (w - n_u), POOL_BUF):
                acc = acc + buf_ref[i, :, cs]
            dv = acc * (1.0 / w) - u_ref[t, :, cs]
            op_ref[t, :, cs] = _dot(dv.astype(BF16), wg) * sc_ref[:, cs]
    for i in range(POOL_BUF - steps):
        nb_ref[i] = buf_ref[i + steps]
    for t in range(steps):
        nb_ref[POOL_BUF - steps + t] = u_ref[t]


def _pool_decode(u_t, buf_t, w_pool, scale3, layer):
    steps, nseq, W = u_t.shape
    G = len(POOL_WINDOWS)
    return pl.pallas_call(
        _pool_decode_kernel,
        grid=(1,),
        in_specs=[
            pl.BlockSpec((steps, nseq, W), lambda i: (0, 0, 0)),
            pl.BlockSpec((POOL_BUF, nseq, W), lambda i: (0, 0, 0)),
            pl.BlockSpec((None, G, LANE, LANE), lambda i: (layer, 0, 0, 0)),
            pl.BlockSpec((None, 1, W), lambda i: (layer, 0, 0)),
        ],
        out_specs=[
            pl.BlockSpec((steps, nseq, W), lambda i: (0, 0, 0)),
            pl.BlockSpec((POOL_BUF, nseq, W), lambda i: (0, 0, 0)),
        ],
        out_shape=[
            jax.ShapeDtypeStruct((steps, nseq, W), F32),
            jax.ShapeDtypeStruct((POOL_BUF, nseq, W), F32),
        ],
        compiler_params=_params(("arbitrary",), 32),
        name="pool_decode",
    )(u_t, buf_t, w_pool, scale3)


def _softmax_rows(s):
    e = jnp.exp(s - jnp.max(s, axis=-1, keepdims=True))
    return e / jnp.sum(e, axis=-1, keepdims=True)


def _xattn_prompt_kernel(q_ref, k_ref, v_ref, o_ref):
    scale = LANE ** -0.5
    for h in range(XA_HEADS):
        cs = slice(h * LANE, (h + 1) * LANE)
        s = _dot_nt(q_ref[:, cs].astype(BF16), k_ref[:, cs].astype(BF16)) * scale
        p = _softmax_rows(s)
        o_ref[:, cs] = _dot(p.astype(BF16), v_ref[:, cs].astype(BF16)).astype(BF16)


def _xattn_prompt(zf, mk, mv, batch, tq):
    T = zf.shape[0] // batch
    W = XA_HEADS * LANE
    nq = T // tq
    n_mem = mk.shape[1]
    mspec = pl.BlockSpec((None, n_mem, W), lambda b, i: (b, 0, 0))
    return pl.pallas_call(
        _xattn_prompt_kernel,
        grid=(batch, nq),
        in_specs=[pl.BlockSpec((tq, W), lambda b, i: (b * nq + i, ZF_X)), mspec, mspec],
        out_specs=pl.BlockSpec((tq, W), lambda b, i: (b * nq + i, 0)),
        out_shape=jax.ShapeDtypeStruct((batch * T, W), BF16),
        compiler_params=_params(("parallel", "parallel"), 48),
        name="xattn_prompt",
    )(zf, mk, mv)


def _xattn_decode_kernel(seq_len, q_ref, k_ref, v_ref, o_ref):
    scale = LANE ** -0.5
    rows = DEC_GROUP * seq_len
    lseq = seq_len.bit_length() - 1
    lrows = rows.bit_length() - 1
    nk = k_ref.shape[1]
    row_s = lax.broadcasted_iota(jnp.int32, (XA_HEADS * rows, nk), 0)
    col_s = lax.broadcasted_iota(jnp.int32, (XA_HEADS * rows, nk), 1)
    own_head = (col_s & (XA_HEADS - 1)) == (row_s >> lrows)
    seq_s = (row_s & (rows - 1)) >> lseq
    seq_o = (lax.broadcasted_iota(jnp.int32, (XA_HEADS * rows, LANE), 0) & (rows - 1)) >> lseq
    for grp in range(k_ref.shape[0] // DEC_GROUP):
        rs = slice(grp * rows, (grp + 1) * rows)
        qb = jnp.concatenate([q_ref[rs, h * LANE:(h + 1) * LANE] for h in range(XA_HEADS)], axis=0).astype(BF16)
        s = jnp.full((XA_HEADS * rows, nk), -jnp.inf, F32)
        for k in range(DEC_GROUP):
            sk = _dot_nt(qb, k_ref[grp * DEC_GROUP + k].astype(BF16))
            s = jnp.where(own_head & (seq_s == k), sk, s)
        p = _softmax_rows(s * scale).astype(BF16)
        o = jnp.zeros((XA_HEADS * rows, LANE), F32)
        for k in range(DEC_GROUP):
            o = jnp.where(seq_o == k, _dot(p, v_ref[grp * DEC_GROUP + k].astype(BF16)), o)
        for h in range(XA_HEADS):
            o_ref[rs, h * LANE:(h + 1) * LANE] = o[h * rows:(h + 1) * rows, :].astype(BF16)


def _merge_kernel(x_ref, oh_ref, op_ref, ox_ref, g0_ref, g1_ref, g2_ref, wb_ref, wo_ref, o_ref, *rest):
    y_ref = rest[-1]
    wbb_ref, wob_ref = rest[:2] if len(rest) == 3 else (None, None)
    j = pl.program_id(1)
    nj, _, tn = y_ref.shape

    @pl.when(j < nj)
    def _():
        wh = oh_ref.shape[1]
        wp = op_ref.shape[1]
        oh, op, ox = oh_ref[...].astype(BF16), op_ref[...].astype(BF16), ox_ref[...].astype(BF16)
        cw = min(MXU_COLS, tn)
        for c in range(tn // cw):
            cs = slice(c * cw, (c + 1) * cw)
            w = wb_ref[:, cs].astype(BF16)
            if wbb_ref is not None:
                wbb_ref[:, cs] = w
            y = _sigmoid(g0_ref[:, cs].astype(F32)) * _dot(oh, w[0:wh, :])
            y += _sigmoid(g1_ref[:, cs].astype(F32)) * _dot(op, w[wh:wh + wp, :])
            y += _sigmoid(g2_ref[:, cs].astype(F32)) * _dot(ox, w[wh + wp:, :])
            y_ref[j, :, cs] = y.astype(BF16)

    @pl.when(j >= nj)
    def _():
        acc = x_ref[...]
        for c in range(nj):
            rs = slice(c * tn, (c + 1) * tn)
            w = wo_ref[rs, :].astype(BF16)
            if wob_ref is not None:
                wob_ref[rs, :] = w
            acc += _dot(y_ref[c], w)
        o_ref[...] = acc


def _merge_resident_kernel(x_ref, oh_ref, op_ref, ox_ref, g_ref, wb_ref, wo_ref, o_ref, y_ref):
    D = x_ref.shape[1]
    wh = oh_ref.shape[1]
    wp = op_ref.shape[1]
    oh, op, ox = oh_ref[...].astype(BF16), op_ref[...].astype(BF16), ox_ref[...].astype(BF16)
    cw = min(2 * MXU_COLS, D)
    for c in range(D // cw):
        cs = slice(c * cw, (c + 1) * cw)
        gate = lambda k: _sigmoid(g_ref[:, k * D + c * cw:k * D + (c + 1) * cw].astype(F32))
        y = gate(0) * _dot(oh, wb_ref[0:wh, cs])
        y += gate(1) * _dot(op, wb_ref[wh:wh + wp, cs])
        y += gate(2) * _dot(ox, wb_ref[wh + wp:, cs])
        y_ref[:, cs] = y.astype(BF16)
    for c in range(D // cw):
        cs = slice(c * cw, (c + 1) * cw)
        o_ref[:, cs] = x_ref[:, cs] + _dot(y_ref[...], wo_ref[:, cs])


def _merge_resident(x, oh, op, ox, zg, wb, wo, tm):
    T, D = x.shape
    row = lambda a: pl.BlockSpec((tm, a.shape[1]), lambda i: (i, 0))
    held = lambda a: pl.BlockSpec(a.shape, lambda i: (0, 0), pipeline_mode=pl.Buffered(1))
    return pl.pallas_call(
        _merge_resident_kernel,
        grid=(T // tm,),
        in_specs=[row(x), row(oh), row(op), row(ox), row(zg), held(wb), held(wo)],
        out_specs=row(x),
        out_shape=jax.ShapeDtypeStruct((T, D), F32),
        scratch_shapes=[pltpu.VMEM((tm, D), BF16)],
        compiler_params=_params(("parallel",), 56),
        name="merge_resident",
    )(x, oh, op, ox, zg, wb, wo)


def _merge(x, oh, op, ox, zb, wb, wo, layer, tm, tn, emit=False):
    T, D = x.shape
    assert not emit or T == tm
    nj = D // tn
    first = lambda j: jnp.minimum(j, nj - 1)
    second = lambda j: jnp.maximum(j - nj, 0)
    full = lambda a: pl.BlockSpec((tm, a.shape[1]), lambda i, j: (i, 0))
    gate = lambda k: pl.BlockSpec((tm, tn), lambda i, j: (i, k * nj + first(j)))
    wb_idx = lambda i, j: (0, first(j))
    wo_idx = lambda i, j: (0, second(j))
    rows_b = wb.shape[-2]
    out_specs = [pl.BlockSpec((tm, tn), lambda i, j: (i, second(j)))]
    out_shape = [jax.ShapeDtypeStruct((T, D), F32)]
    if emit:
        out_specs += [pl.BlockSpec((rows_b, tn), wb_idx), pl.BlockSpec((D, tn), wo_idx)]
        out_shape += [jax.ShapeDtypeStruct((rows_b, D), BF16), jax.ShapeDtypeStruct((D, D), BF16)]
    out = pl.pallas_call(
        _merge_kernel,
        grid=(T // tm, 2 * nj),
        in_specs=[
            pl.BlockSpec((tm, tn), lambda i, j: (i, second(j))),
            full(oh), full(op), full(ox), gate(0), gate(1), gate(2),
            _layer_spec(wb, layer, (rows_b, tn), wb_idx),
            _layer_spec(wo, layer, (D, tn), wo_idx),
        ],
        out_specs=out_specs,
        out_shape=out_shape,
        scratch_shapes=[pltpu.VMEM((nj, tm, tn), BF16)],
        compiler_params=_params(("parallel", "arbitrary"), 48),
        name="merge_out",
    )(x, oh, op, ox, zb, zb, zb, wb, wo)
    return out if emit else out[0]


def kernel(x_prompt, x_sample, state_hgrn, state_pool, cache_mem_k, cache_mem_v, mem_prompt, ffn1_norm, ffn1_w1, ffn1_w3, ffn1_w2, mix_norm, w_in, lb_logits, hg_norm, w_pool, pool_scale, mem_norm, w_mk, w_mv, w_branch, w_out, ffn2_norm, ffn2_w1, ffn2_w3, ffn2_w2, final_norm):
    B, T, D = x_prompt.shape
    nseq, steps, _ = x_sample.shape
    depth = w_in.shape[0]
    n_mem = mem_prompt.shape[1]
    xw = XA_HEADS * LANE
    pw = len(POOL_WINDOWS) * LANE

    tm_p = min(1024, B * T)
    tm_s = min(512, nseq * steps)

    vec3 = lambda a: a.reshape(a.shape[0], 1, a.shape[1])
    f1n, f2n = vec3(ffn1_norm), vec3(ffn2_norm)
    mix3, mem3, hgn3, psc3 = vec3(mix_norm), vec3(mem_norm), vec3(hg_norm), vec3(pool_scale)
    lbs3 = vec3(_lower_bounds(lb_logits))

    xp = x_prompt.reshape(B * T, D)
    xs = x_sample.reshape(nseq * steps, D)
    mem2 = mem_prompt.reshape(B * n_mem, D)
    ck = cache_mem_k.reshape(depth, nseq, n_mem * XA_HEADS, LANE)
    cv = cache_mem_v.reshape(depth, nseq, n_mem * XA_HEADS, LANE)

    tf = _tile(ffn1_w1.shape[-1], 512)
    tf_s = _tile(ffn1_w1.shape[-1], 256)
    tn_in = _tile(math.gcd(HG_HEADS * LANE, N_BRANCH * D), 1024)
    tn_mrg = _tile(D, 512)
    tm_mem = min(512, B * n_mem)
    hs_p, pb_p, mk_p, mv_p, pb_s = [], [], [], [], []
    hs_s = None

    blocks = []
    for l in range(depth):
        blocks.append((f1n, ffn1_w1, ffn1_w3, ffn1_w2, l))
        blocks.append((f2n, ffn2_w1, ffn2_w3, ffn2_w2, l))

    def swiglu_pair(xp, xs, wbf, k, final_g):
        norm, w1, w3, w2, l = blocks[k]
        if wbf is None:
            xs, *wbf = _ffn(xs, norm, w1, w3, w2, l, tm_s, tf_s, final_g)
        else:
            xs = _ffn(xs, norm, *wbf, l, tm_s, tf, final_g)
        if k + 1 < len(blocks):
            xp, *nxt = _ffn(xp, norm, *wbf, l, tm_p, tf, final_g, side=blocks[k + 1][1:])
        else:
            xp, nxt = _ffn(xp, norm, *wbf, l, tm_p, tf, final_g), None
        return xp, xs, nxt

    wbf = w_in_b = None
    for l in range(depth):
        mk = _rms_matmul(mem2, mem3, w_mk, l, tm_mem, xw)
        mv = _rms_matmul(mem2, mem3, w_mv, l, tm_mem, xw)
        mk_p.append(mk.reshape(B, n_mem, XA_HEADS, LANE))
        mv_p.append(mv.reshape(B, n_mem, XA_HEADS, LANE))

        xp, xs, wbf = swiglu_pair(xp, xs, wbf, 2 * l, None)

        if w_in_b is None:
            zsb, zsf, zsg, w_in_b = _in_proj(xs, mix3, w_in, l, tm_s, tn_in, emit=True)
        else:
            zsb, zsf, zsg = _in_proj(xs, mix3, w_in_b, l, tm_s, tn_in)
        if l + 1 < depth:
            zpb, zpf, zpg, w_in_next = _in_proj(xp, mix3, w_in_b, l, tm_p, tn_in, side=(w_in, l + 1))
        else:
            (zpb, zpf, zpg), w_in_next = _in_proj(xp, mix3, w_in_b, l, tm_p, tn_in), None
        w_in_b = w_in_next

        ohp, sp, ohs, hs_s, oxs = _recurrent_mixers(zpb, zpf, zsb, zsf, lbs3, hgn3, state_hgrn, hs_s, ck, cv, l, B, nseq)
        hs_p.append(sp)

        opp, bp = _pool_prompt(zpf, w_pool, psc3, l, B)
        pb_p.append(bp)
        u_t = zsf[:, ZF_U * pw:(ZF_U + 1) * pw].reshape(nseq, steps, pw).transpose(1, 0, 2)
        buf_t = state_pool[l].transpose(1, 0, 2)
        ops_t, nb_t = _pool_decode(u_t, buf_t, w_pool, psc3, l)
        ops = ops_t.transpose(1, 0, 2).reshape(nseq * steps, pw)
        pb_s.append(nb_t.transpose(1, 0, 2))

        oxp = _xattn_prompt(zpf, mk.reshape(B, n_mem, xw), mv.reshape(B, n_mem, xw), B, min(1024, T))

        xs, w_br_b, w_out_b = _merge(xs, ohs, ops, oxs, zsg, w_branch, w_out, l, tm_s, tn_mrg, emit=True)
        xp = _merge_resident(xp, ohp, opp, oxp, zpg, w_br_b, w_out_b, min(512, B * T))

        fn2 = final_norm.reshape(1, D) if l == depth - 1 else None
        xp, xs, wbf = swiglu_pair(xp, xs, wbf, 2 * l + 1, fn2)

    y_prompt = xp.reshape(B, T, D)
    y_sample = xs.reshape(nseq, steps, D)
    return (y_prompt, y_sample, jnp.stack(hs_p), jnp.stack(pb_p), jnp.stack(mk_p), jnp.stack(mv_p),
            hs_s, jnp.stack(pb_s))
```

```python
import functools
import math

import jax
import jax.numpy as jnp
from jax import lax
from jax.experimental import pallas as pl
from jax.experimental.pallas import tpu as pltpu

F32 = jnp.float32
BF16 = jnp.bfloat16
EPS = 1e-6

LANE = 128
SUBLANE = 8
MXU_COLS = 256
HG_HEADS = 8
POOL_WINDOWS = (2, 4, 8, 16)
POOL_BUF = max(POOL_WINDOWS) - 1
XA_HEADS = 4
N_BRANCH = 3

COL_Q, COL_F, COL_I, COL_OG = 0, HG_HEADS, 2 * HG_HEADS, 3 * HG_HEADS
COL_U = 4 * HG_HEADS
COL_X = COL_U + len(POOL_WINDOWS)
COL_G = COL_X + XA_HEADS
ZB_Q, ZB_V, ZB_OG = 0, 1, 2
ZF_F = 0
ZF_U, ZF_X = 2, 3

HGRN_CHUNK = 128
HGRN_STEP_CHUNKS = 2
DEC_GROUP = 4


def _tile(n, preferred):
    t = preferred
    while n % t:
        t -= LANE
    return t


def _params(semantics, vmem_mib):
    return pltpu.CompilerParams(dimension_semantics=semantics, vmem_limit_bytes=vmem_mib * 1024 * 1024)


def _rms(x, g):
    return x * lax.rsqrt(jnp.mean(x * x, axis=-1, keepdims=True) + EPS) * g


def _sigmoid(x):
    return 0.5 * jnp.tanh(0.5 * x) + 0.5


def _silu(x):
    return x * _sigmoid(x)


def _dot(a, b):
    return jnp.dot(a, b, preferred_element_type=F32)


def _dot_nt(a, b):
    return lax.dot_general(a, b, (((1,), (1,)), ((), ())), preferred_element_type=F32)


def _dot_tn(a, b):
    return lax.dot_general(a, b, (((0,), (0,)), ((), ())), preferred_element_type=F32)


def _ffn_kernel(final, cast, side, x_ref, g_ref, w1_ref, w3_ref, w2_ref, *rest):
    rest = list(rest)
    xn_ref = rest.pop()
    fn_ref = rest.pop(0) if final else None
    side_in = [rest.pop(0) for _ in range(3)] if side else []
    o_ref = rest.pop(0)
    if cast:
        src = (w1_ref, w3_ref, w2_ref)
        w1_ref, w3_ref, w2_ref = [rest.pop(0) for _ in range(3)]
        for dst, s in zip((w1_ref, w3_ref, w2_ref), src):
            dst[...] = s[...].astype(BF16)
    j = pl.program_id(1)

    @pl.when(j == 0)
    def _():
        x = x_ref[...]
        xn_ref[...] = _rms(x, g_ref[...]).astype(BF16)
        o_ref[...] = x

    xn = xn_ref[...]
    h = (_silu(_dot(xn, w1_ref[...])) * _dot(xn, w3_ref[...])).astype(BF16)
    for s, dst in zip(side_in, rest):
        dst[...] = s[...].astype(BF16)
    cw = min(w2_ref.shape[0], o_ref.shape[1])
    for n in range(o_ref.shape[1] // cw):
        cs = slice(n * cw, (n + 1) * cw)
        o_ref[:, cs] += 0.5 * _dot(h, w2_ref[:, cs])

    if final:
        @pl.when(j == pl.num_programs(1) - 1)
        def _():
            o_ref[...] = _rms(o_ref[...], fn_ref[...])


def _layer_spec(w, layer, block, index):
    if w.ndim == 2:
        return pl.BlockSpec(block, index)
    return pl.BlockSpec((None,) + block, lambda i, j: (layer,) + index(i, j))


def _ffn(x, g3, w1, w3, w2, layer, tm, tf, final_g=None, side=None):
    T, D = x.shape
    F = w1.shape[-1]
    cast = w1.dtype != BF16
    final = final_g is not None
    assert not cast or T == tm
    ni, nj = T // tm, F // tf
    col = lambda i, j: (0, j)
    row = lambda i, j: (j, 0)
    out_specs = [pl.BlockSpec((tm, D), lambda i, j: (i, 0))]
    out_shape = [jax.ShapeDtypeStruct((T, D), F32)]
    wshapes = [jax.ShapeDtypeStruct((D, F), BF16), jax.ShapeDtypeStruct((D, F), BF16),
               jax.ShapeDtypeStruct((F, D), BF16)]
    if cast:
        out_specs += [pl.BlockSpec((D, tf), col), pl.BlockSpec((D, tf), col), pl.BlockSpec((tf, D), row)]
        out_shape += wshapes
    in_specs = [
        pl.BlockSpec((tm, D), lambda i, j: (i, 0)),
        pl.BlockSpec((None, 1, D), lambda i, j: (layer, 0, 0)),
        _layer_spec(w1, layer, (D, tf), col),
        _layer_spec(w3, layer, (D, tf), col),
        _layer_spec(w2, layer, (tf, D), row),
    ]
    args = [x, g3, w1, w3, w2]
    if final:
        in_specs.append(pl.BlockSpec((1, D), lambda i, j: (0, 0)))
        args.append(final_g)
    if side is not None:
        s1, s3, s2, side_layer = side
        assert D % (ni * LANE) == 0
        rb = D // ni
        up, down = (lambda i, j: (i, j)), (lambda i, j: (j, i))
        in_specs += [_layer_spec(s1, side_layer, (rb, tf), up), _layer_spec(s3, side_layer, (rb, tf), up),
                     _layer_spec(s2, side_layer, (tf, rb), down)]
        args += [s1, s3, s2]
        out_specs += [pl.BlockSpec((rb, tf), up), pl.BlockSpec((rb, tf), up), pl.BlockSpec((tf, rb), down)]
        out_shape += wshapes
    out = pl.pallas_call(
        functools.partial(_ffn_kernel, final, cast, side is not None),
        grid=(ni, nj),
        in_specs=in_specs,
        out_specs=out_specs,
        out_shape=out_shape,
        scratch_shapes=[pltpu.VMEM((tm, D), BF16)],
        compiler_params=_params(("parallel", "arbitrary"), 62),
        name="ffn_cast" if cast else "ffn",
    )(*args)
    return out if len(out) > 1 else out[0]


def _rms_matmul_kernel(x_ref, g_ref, w_ref, o_ref, *rest):
    xn_ref = rest[-1]

    @pl.when(pl.program_id(1) == 0)
    def _():
        xn_ref[...] = _rms(x_ref[...], g_ref[...]).astype(BF16)

    w = w_ref[...].astype(BF16)
    if len(rest) == 2:
        rest[0][...] = w
    o_ref[...] = _dot(xn_ref[...], w)


def _rms_matmul(x, g3, w, layer, tm, tn, emit=False):
    T, D = x.shape
    N = w.shape[-1]
    assert not emit or T == tm
    col = lambda i, j: (0, j)
    out_specs = [pl.BlockSpec((tm, tn), lambda i, j: (i, j))]
    out_shape = [jax.ShapeDtypeStruct((T, N), F32)]
    if emit:
        out_specs.append(pl.BlockSpec((D, tn), col))
        out_shape.append(jax.ShapeDtypeStruct((D, N), BF16))
    out = pl.pallas_call(
        _rms_matmul_kernel,
        grid=(T // tm, N // tn),
        in_specs=[
            pl.BlockSpec((tm, D), lambda i, j: (i, 0)),
            pl.BlockSpec((None, 1, D), lambda i, j: (layer, 0, 0)),
            _layer_spec(w, layer, (D, tn), col),
        ],
        out_specs=out_specs,
        out_shape=out_shape,
        scratch_shapes=[pltpu.VMEM((tm, D), BF16)],
        compiler_params=_params(("parallel", "arbitrary"), 48),
        name="rms_matmul",
    )(x, g3, w)
    return out if emit else out[0]


def _runs(seq):
    runs = []
    for j, v in enumerate(seq):
        if runs and runs[-1][2] == v - j:
            runs[-1][1] = j + 1
        else:
            runs.append([j, j + 1, v - j])
    return [tuple(r) for r in runs]


def _in_runs(j, runs):
    hit = None
    for lo, hi, _ in runs:
        c = (j >= lo) & (j < hi)
        hit = c if hit is None else hit | c
    return hit


def _lookup(j, runs):
    out = 0
    for lo, hi, off in runs:
        out = out + jnp.where((j >= lo) & (j < hi), j + off, 0)
    return out


def _in_proj_plan(d_model, tn):
    t = lambda blocks: blocks * LANE // tn
    nq, nux, ng = t(HG_HEADS), t(len(POOL_WINDOWS) + XA_HEADS), N_BRANCH * d_model // tn
    src = lambda start, n: list(range(t(start), t(start) + n))
    order = src(COL_Q, nq) + src(COL_I, nq) + src(COL_OG, nq) + src(COL_F, nq) + src(COL_U, nux) + src(COL_G, ng)
    nh, n32 = 3 * nq, nq + nux
    kinds = {"f32": [(nh, nh + n32, 0)]}
    return _runs(order), kinds, nh, n32


def _in_proj_kernel(kinds, side, x_ref, g_ref, w_ref, *rest):
    rest = list(rest)
    xn_ref = rest.pop()
    side_in, side_out = (rest.pop(0), rest.pop()) if side else (None, None)
    zh_ref, zf_ref, zg_ref = rest[:3]
    rest = rest[3:] + [xn_ref]
    j = pl.program_id(1)

    @pl.when(j == 0)
    def _():
        xn_ref[...] = _rms(x_ref[...], g_ref[...]).astype(BF16)

    xn = xn_ref[...]
    tn = w_ref.shape[1]
    cw = min(MXU_COLS, tn)
    zs = []
    for c in range(tn // cw):
        cs = slice(c * cw, (c + 1) * cw)
        w = w_ref[:, cs].astype(BF16)
        if len(rest) == 2:
            rest[0][:, cs] = w
        z = _dot(xn, w)
        zs.append(z)
        zh_ref[:, cs] = z.astype(BF16)
        zg_ref[:, cs] = z.astype(BF16)
    if side:
        side_out[...] = side_in[...].astype(BF16)

    @pl.when(_in_runs(j, kinds["f32"]))
    def _():
        for c, z in enumerate(zs):
            zf_ref[:, c * cw:(c + 1) * cw] = z


def _in_proj(x, g3, w, layer, tm, tn, emit=False, side=None):
    T, D = x.shape
    N = w.shape[-1]
    assert not emit or T == tm
    order, kinds, nh, n32 = _in_proj_plan(D, tn)
    nsteps = N // tn
    wcol = lambda i, j: (0, _lookup(j, order))
    out_specs = [
        pl.BlockSpec((tm, tn), lambda i, j: (i, jnp.minimum(j, nh))),
        pl.BlockSpec((tm, tn), lambda i, j: (i, jnp.clip(j - nh, 0, n32 - 1))),
        pl.BlockSpec((tm, tn), lambda i, j: (i, jnp.maximum(j - nh - n32, 0))),
    ]
    out_shape = [
        jax.ShapeDtypeStruct((T, (nh + 1) * tn), BF16),
        jax.ShapeDtypeStruct((T, n32 * tn), F32),
        jax.ShapeDtypeStruct((T, (nsteps - nh - n32) * tn), BF16),
    ]
    if emit:
        out_specs.append(pl.BlockSpec((D, tn), wcol))
        out_shape.append(jax.ShapeDtypeStruct((D, N), BF16))
    in_specs = [
        pl.BlockSpec((tm, D), lambda i, j: (i, 0)),
        pl.BlockSpec((None, 1, D), lambda i, j: (layer, 0, 0)),
        _layer_spec(w, layer, (D, tn), wcol),
    ]
    args = [x, g3, w]
    if side is not None:
        rb = D // (T // tm)
        assert rb % SUBLANE == 0
        in_specs.append(_layer_spec(side[0], side[1], (rb, tn), lambda i, j: (i, j)))
        args.append(side[0])
        out_specs.append(pl.BlockSpec((rb, tn), lambda i, j: (i, j)))
        out_shape.append(jax.ShapeDtypeStruct((D, N), BF16))
    return pl.pallas_call(
        functools.partial(_in_proj_kernel, kinds, side is not None),
        grid=(T // tm, nsteps),
        in_specs=in_specs,
        out_specs=out_specs,
        out_shape=out_shape,
        scratch_shapes=[pltpu.VMEM((tm, D), BF16)],
        compiler_params=_params(("parallel", "arbitrary"), 58),
        name="in_proj",
    )(*args)


def _lower_bound_kernel(lg_ref, o_ref):
    lg = lg_ref[...]
    depth = lg.shape[0]
    rows = [lg[i:i + 1, :] for i in range(depth)]
    m = rows[0]
    for r in rows[1:]:
        m = jnp.maximum(m, r)
    e = [jnp.exp(r - m) for r in rows]
    tot = e[0]
    for v in e[1:]:
        tot = tot + v
    c = e[0] / tot
    first = c
    o_ref[0:1, :] = c - first
    for i in range(1, depth):
        c = c + e[i] / tot
        o_ref[i:i + 1, :] = c - first


def _lower_bounds(lb_logits):
    return pl.pallas_call(
        _lower_bound_kernel,
        out_shape=jax.ShapeDtypeStruct(lb_logits.shape, F32),
        name="hgrn_lower_bounds",
    )(lb_logits)


def _neg_abs(x):
    bits = lax.bitcast_convert_type(x, jnp.uint32) | jnp.uint32(0x80000000)
    return lax.bitcast_convert_type(bits, F32)


def _group_ref_row(b, s):
    rows, width = b.shape
    gsz = 2 * s
    if gsz >= 8:
        parts = [jnp.broadcast_to(b[i * gsz + s - 1:i * gsz + s, :], (gsz, width)) for i in range(rows // gsz)]
        return parts[0] if len(parts) == 1 else jnp.concatenate(parts, axis=0)
    pos = lax.broadcasted_iota(jnp.int32, b.shape, 0) & (gsz - 1)
    out = b
    for p in range(gsz):
        d = p - (s - 1)
        if d != 0:
            out = jnp.where(pos == p, pltpu.roll(b, d % rows, 0), out)
    return out


def _hgrn_wide(q, zf, vb, lb, states, seg):
    C, W = q.shape
    heads = [slice(h * LANE, (h + 1) * LANE) for h in range(W // LANE)]
    nseg = C // seg
    fg = lb + (1.0 - lb) * jax.nn.sigmoid(zf)
    kk = 1.0 - fg
    g = jnp.log2(fg)

    row = lax.broadcasted_iota(jnp.int32, (C, W), 0)
    rr = lax.broadcasted_iota(jnp.int32, (C, C), 0)
    cc = lax.broadcasted_iota(jnp.int32, (C, C), 1)
    lseg = seg.bit_length() - 1
    tri = jnp.where((cc <= rr) & ((rr >> lseg) == (cc >> lseg)), 1.0, 0.0).astype(F32)
    b = jnp.dot(tri, g, precision=lax.Precision.HIGHEST, preferred_element_type=F32)

    qb, kb = q.astype(BF16), kk.astype(BF16)
    a = [jnp.where(rr == cc, _dot_nt(qb[:, hs], kb[:, hs]), 0.0) for hs in heads]
    s = seg // 2
    while s >= 1:
        ls = s.bit_length() - 1
        same = (rr >> (ls + 1)) == (cc >> (ls + 1))
        if s == 1:
            right = (row & 1) == 1
            qs, ks = jnp.where(right, q * fg, 0.0).astype(BF16), jnp.where(right, 0.0, kk).astype(BF16)
            for h, hs in enumerate(heads):
                a[h] = a[h] + jnp.where(same, _dot_nt(qs[:, hs], ks[:, hs]), 0.0)
            break
        e = jnp.exp2(_neg_abs(b - _group_ref_row(b, s)))
        if s % SUBLANE == 0:
            nblk = C // s
            zero = jnp.zeros((s, W), F32)
            blk = lambda x, i: x[i * s:(i + 1) * s, :]
            qs = jnp.concatenate([blk(q, i) * blk(e, i) if i % 2 else zero for i in range(nblk)], axis=0).astype(BF16)
            ks = jnp.concatenate([zero if i % 2 else blk(kk, i) * blk(e, i) for i in range(nblk)], axis=0).astype(BF16)
            for h, hs in enumerate(heads):
                p = _dot_nt(qs[:, hs], ks[:, hs])
                pieces = []
                for i in range(nblk):
                    if i % 2 == 0:
                        pieces.append(blk(a[h], i))
                    elif 2 * s == C:
                        pieces.append(blk(a[h], i) + blk(p, i))
                    else:
                        pieces.append(blk(a[h], i) + jnp.where(blk(same, i), blk(p, i), 0.0))
                a[h] = jnp.concatenate(pieces, axis=0)
        else:
            right = ((row >> ls) & 1) == 1
            eq = jnp.where(right, e, 0.0)
            qs, ks = (q * eq).astype(BF16), (kk * (e - eq)).astype(BF16)
            for h, hs in enumerate(heads):
                a[h] = a[h] + jnp.where(same, _dot_nt(qs[:, hs], ks[:, hs]), 0.0)
        s //= 2
    o = [_dot(a[h].astype(BF16), vb[:, hs]) for h, hs in enumerate(heads)]

    qe = q * jnp.exp2(b)
    new_states = [[] for _ in heads]
    for k in range(nseg):
        bl = b[k * seg + seg - 1:k * seg + seg, :]
        if nseg == 1:
            qk = qe
            ke = kk * jnp.exp2(bl - b)
        else:
            mine = (row >> lseg) == k
            qk = jnp.where(mine, qe, 0.0)
            ke = jnp.where(mine, kk * jnp.exp2(jnp.where(mine, bl - b, 0.0)), 0.0)
        qkb, keb, ebl = qk.astype(BF16), ke.astype(BF16), jnp.exp2(bl)
        for h, hs in enumerate(heads):
            st = states[h][k]
            o[h] = o[h] + _dot(qkb[:, hs], st.astype(BF16))
            decay = jnp.transpose(jnp.broadcast_to(ebl[:, hs], (LANE, LANE)))
            new_states[h].append(st * decay + _dot_tn(keb[:, hs], vb[:, hs]))
    return o, new_states


def _hgrn_finish(o, zog, hgn):
    return (_rms(o, hgn) * _silu(zog.astype(F32))).astype(BF16)


def _hgrn_prompt_kernel(q_ref, zf_ref, v_ref, og_ref, lb_ref, hgn_ref, oh_ref, s_ref):
    @pl.when(pl.program_id(1) == 0)
    def _():
        s_ref[...] = jnp.zeros_like(s_ref)

    states = [[s_ref[h]] for h in range(HG_HEADS)]
    for sub in range(q_ref.shape[0] // HGRN_CHUNK):
        rs = slice(sub * HGRN_CHUNK, (sub + 1) * HGRN_CHUNK)
        q = _silu(q_ref[rs, :].astype(F32))
        o, states = _hgrn_wide(q, zf_ref[rs, :], v_ref[rs, :], lb_ref[...], states, HGRN_CHUNK)
        for h in range(HG_HEADS):
            cs = slice(h * LANE, (h + 1) * LANE)
            oh_ref[rs, cs] = _hgrn_finish(o[h], og_ref[rs, cs], hgn_ref[:, cs])
    for h in range(HG_HEADS):
        s_ref[h] = states[h][0]


def _hgrn_decode_kernel(seq_len, q_ref, zf_ref, v_ref, og_ref, lb_ref, hgn_ref, s_ref, xq_ref, k_ref, vc_ref, *rest):
    oh_ref, so_ref, ox_ref = rest[-3:]
    _xattn_decode_kernel(seq_len, xq_ref, k_ref, vc_ref, ox_ref)
    rows = DEC_GROUP * seq_len
    for grp in range(s_ref.shape[0] // DEC_GROUP):
        rs = slice(grp * rows, (grp + 1) * rows)
        states = [[s_ref[grp * DEC_GROUP + k, h] for k in range(DEC_GROUP)] for h in range(HG_HEADS)]
        o, new = _hgrn_wide(_silu(q_ref[rs, :].astype(F32)), zf_ref[rs, :], v_ref[rs, :], lb_ref[...], states, seq_len)
        for h in range(HG_HEADS):
            cs = slice(h * LANE, (h + 1) * LANE)
            for k in range(DEC_GROUP):
                so_ref[grp * DEC_GROUP + k, h] = new[h][k]
            oh_ref[rs, cs] = _hgrn_finish(o[h], og_ref[rs, cs], hgn_ref[:, cs])


def _recurrent_mixers_kernel(seq_len, has_alias, qp, zfp, vp, ogp, lb, hgn, qs, zfs, vs, ogs, s_ref, xq, k, vc, *rest):
    ohp, sp, ohs, so, oxs = rest[int(has_alias):]
    _hgrn_prompt_kernel(qp, zfp, vp, ogp, lb, hgn, ohp, sp)
    _hgrn_decode_kernel(seq_len, qs, zfs, vs, ogs, lb, hgn, s_ref, xq, k, vc, ohs, so, oxs)


def _recurrent_mixers(zpb, zpf, zsb, zsf, lbs3, hgn3, state, stacked, cache_k, cache_v, layer, batch, nseq):
    T = zpb.shape[0] // batch
    C = HGRN_STEP_CHUNKS * HGRN_CHUNK
    nc = T // C
    nb = nseq // (batch * nc)
    assert nb * batch * nc == nseq and nb % DEC_GROUP == 0
    seq_len = zsb.shape[0] // nseq
    W = HG_HEADS * LANE
    XW = XA_HEADS * LANE
    pspec = lambda col: pl.BlockSpec((C, W), lambda b, c: (b * nc + c, col))
    dspec = lambda width, col: pl.BlockSpec((nb * seq_len, width), lambda b, c: (b * nc + c, col))
    vec = pl.BlockSpec((None, 1, W), lambda b, c: (layer, 0, 0))
    sspec = pl.BlockSpec((None, nb, HG_HEADS, LANE, LANE), lambda b, c: (layer, b * nc + c, 0, 0, 0))
    cspec = pl.BlockSpec((None, nb) + cache_k.shape[2:], lambda b, c: (layer, b * nc + c, 0, 0))
    in_specs = [pspec(ZB_Q), pspec(ZF_F), pspec(ZB_V), pspec(ZB_OG), vec, vec,
                dspec(W, ZB_Q), dspec(W, ZF_F), dspec(W, ZB_V), dspec(W, ZB_OG), sspec, dspec(XW, ZF_X), cspec, cspec]
    args = [zpb, zpf, zpb, zpb, lbs3, hgn3, zsb, zsf, zsb, zsb, state, zsf, cache_k, cache_v]
    aliases = {}
    if stacked is not None:
        in_specs.append(pl.BlockSpec(memory_space=pl.ANY))
        args.append(stacked)
        aliases = {len(args) - 1: 3}
    return pl.pallas_call(
        functools.partial(_recurrent_mixers_kernel, seq_len, stacked is not None),
        grid=(batch, nc),
        in_specs=in_specs,
        out_specs=[
            pl.BlockSpec((C, W), lambda b, c: (b * nc + c, 0)),
            pl.BlockSpec((None, HG_HEADS, LANE, LANE), lambda b, c: (b, 0, 0, 0)),
            dspec(W, 0), sspec, dspec(XW, 0),
        ],
        out_shape=[
            jax.ShapeDtypeStruct((batch * T, W), BF16),
            jax.ShapeDtypeStruct((batch, HG_HEADS, LANE, LANE), F32),
            jax.ShapeDtypeStruct((nseq * seq_len, W), BF16),
            jax.ShapeDtypeStruct(state.shape, F32),
            jax.ShapeDtypeStruct((nseq * seq_len, XW), BF16),
        ],
        input_output_aliases=aliases,
        compiler_params=_params(("arbitrary", "arbitrary"), 48),
        name="recurrent_mixers",
    )(*args)


def _pool_prompt_kernel(u_ref, wp_ref, sc_ref, op_ref, nb_ref):
    u = u_ref[...]
    T = u.shape[0]
    row = lax.broadcasted_iota(jnp.int32, (T, LANE), 0)
    for g, w in enumerate(POOL_WINDOWS):
        cs = slice(g * LANE, (g + 1) * LANE)
        ug = u[:, cs]
        s = ug
        d = 1
        while d < w:
            s = s + jnp.where(row >= d, pltpu.roll(s, d, 0), 0.0)
            d *= 2
        cnt = jnp.minimum(row + 1, w).astype(F32)
        dv = s / cnt - ug
        y = _dot(dv.astype(BF16), wp_ref[g].astype(BF16)) * sc_ref[:, cs]
        op_ref[:, cs] = y.astype(BF16)
    tail = u_ref[T - 16:T, :]
    nb_ref[...] = pltpu.roll(tail, 15, 0)[0:POOL_BUF, :]


def _pool_prompt(zf, w_pool, scale3, layer, batch):
    T = zf.shape[0] // batch
    G = len(POOL_WINDOWS)
    W = G * LANE
    return pl.pallas_call(
        _pool_prompt_kernel,
        grid=(batch,),
        in_specs=[
            pl.BlockSpec((T, W), lambda b: (b, ZF_U)),
            pl.BlockSpec((None, G, LANE, LANE), lambda b: (layer, 0, 0, 0)),
            pl.BlockSpec((None, 1, W), lambda b: (layer, 0, 0)),
        ],
        out_specs=[
            pl.BlockSpec((T, W), lambda b: (b, 0)),
            pl.BlockSpec((None, POOL_BUF, W), lambda b: (b, 0, 0)),
        ],
        out_shape=[
            jax.ShapeDtypeStruct((batch * T, W), BF16),
            jax.ShapeDtypeStruct((batch, POOL_BUF, W), F32),
        ],
        compiler_params=_params(("parallel",), 48),
        name="pool_prompt",
    )(zf, w_pool, scale3)


def _pool_decode_kernel(u_ref, buf_ref, wp_ref, sc_ref, op_ref, nb_ref):
    steps = u_ref.shape[0]
    for g, w in enumerate(POOL_WINDOWS):
        cs = slice(g * LANE, (g + 1) * LANE)
        wg = wp_ref[g].astype(BF16)
        for t in range(steps):
            n_u = min(t + 1, w)
            acc = u_ref[t, :, cs]
            for j in range(t - n_u + 1, t):
                acc = acc + u_ref[j, :, cs]
            for i in range(POOL_BUF - (w - n_u), POOL_BUF):
                acc = acc + buf_ref[i, :, cs]
            dv = acc * (1.0 / w) - u_ref[t, :, cs]
            op_ref[t, :, cs] = _dot(dv.astype(BF16), wg) * sc_ref[:, cs]
    for i in range(POOL_BUF - steps):
        nb_ref[i] = buf_ref[i + steps]
    for t in range(steps):
        nb_ref[POOL_BUF - steps + t] = u_ref[t]


def _pool_decode(u_t, buf_t, w_pool, scale3, layer):
    steps, nseq, W = u_t.shape
    G = len(POOL_WINDOWS)
    return pl.pallas_call(
        _pool_decode_kernel,
        grid=(1,),
        in_specs=[
            pl.BlockSpec((steps, nseq, W), lambda i: (0, 0, 0)),
            pl.BlockSpec((POOL_BUF, nseq, W), lambda i: (0, 0, 0)),
            pl.BlockSpec((None, G, LANE, LANE), lambda i: (layer, 0, 0, 0)),
            pl.BlockSpec((None, 1, W), lambda i: (layer, 0, 0)),
        ],
        out_specs=[
            pl.BlockSpec((steps, nseq, W), lambda i: (0, 0, 0)),
            pl.BlockSpec((POOL_BUF, nseq, W), lambda i: (0, 0, 0)),
        ],
        out_shape=[
            jax.ShapeDtypeStruct((steps, nseq, W), F32),
            jax.ShapeDtypeStruct((POOL_BUF, nseq, W), F32),
        ],
        compiler_params=_params(("arbitrary",), 32),
        name="pool_decode",
    )(u_t, buf_t, w_pool, scale3)


def _softmax_rows(s):
    e = jnp.exp(s - jnp.max(s, axis=-1, keepdims=True))
    return e / jnp.sum(e, axis=-1, keepdims=True)


def _xattn_prompt_kernel(q_ref, k_ref, v_ref, o_ref):
    scale = LANE ** -0.5
    for h in range(XA_HEADS):
        cs = slice(h * LANE, (h + 1) * LANE)
        s = _dot_nt(q_ref[:, cs].astype(BF16), k_ref[:, cs].astype(BF16)) * scale
        p = _softmax_rows(s)
        o_ref[:, cs] = _dot(p.astype(BF16), v_ref[:, cs].astype(BF16)).astype(BF16)


def _xattn_prompt(zf, mk, mv, batch, tq):
    T = zf.shape[0] // batch
    W = XA_HEADS * LANE
    nq = T // tq
    n_mem = mk.shape[1]
    mspec = pl.BlockSpec((None, n_mem, W), lambda b, i: (b, 0, 0))
    return pl.pallas_call(
        _xattn_prompt_kernel,
        grid=(batch, nq),
        in_specs=[pl.BlockSpec((tq, W), lambda b, i: (b * nq + i, ZF_X)), mspec, mspec],
        out_specs=pl.BlockSpec((tq, W), lambda b, i: (b * nq + i, 0)),
        out_shape=jax.ShapeDtypeStruct((batch * T, W), BF16),
        compiler_params=_params(("parallel", "parallel"), 48),
        name="xattn_prompt",
    )(zf, mk, mv)


def _xattn_decode_kernel(seq_len, q_ref, k_ref, v_ref, o_ref):
    scale = LANE ** -0.5
    rows = DEC_GROUP * seq_len
    lseq = seq_len.bit_length() - 1
    lrows = rows.bit_length() - 1
    nk = k_ref.shape[1]
    row_s = lax.broadcasted_iota(jnp.int32, (XA_HEADS * rows, nk), 0)
    col_s = lax.broadcasted_iota(jnp.int32, (XA_HEADS * rows, nk), 1)
    own_head = (col_s & (XA_HEADS - 1)) == (row_s >> lrows)
    seq_s = (row_s & (rows - 1)) >> lseq
    seq_o = (lax.broadcasted_iota(jnp.int32, (XA_HEADS * rows, LANE), 0) & (rows - 1)) >> lseq
    for grp in range(k_ref.shape[0] // DEC_GROUP):
        rs = slice(grp * rows, (grp + 1) * rows)
        qb = jnp.concatenate([q_ref[rs, h * LANE:(h + 1) * LANE] for h in range(XA_HEADS)], axis=0).astype(BF16)
        s = jnp.full((XA_HEADS * rows, nk), -jnp.inf, F32)
        for k in range(DEC_GROUP):
            sk = _dot_nt(qb, k_ref[grp * DEC_GROUP + k].astype(BF16))
            s = jnp.where(own_head & (seq_s == k), sk, s)
        p = _softmax_rows(s * scale).astype(BF16)
        o = jnp.zeros((XA_HEADS * rows, LANE), F32)
        for k in range(DEC_GROUP):
            o = jnp.where(seq_o == k, _dot(p, v_ref[grp * DEC_GROUP + k].astype(BF16)), o)
        for h in range(XA_HEADS):
            o_ref[rs, h * LANE:(h + 1) * LANE] = o[h * rows:(h + 1) * rows, :].astype(BF16)


def _merge_kernel(x_ref, oh_ref, op_ref, ox_ref, g0_ref, g1_ref, g2_ref, wb_ref, wo_ref, o_ref, *rest):
    y_ref = rest[-1]
    wbb_ref, wob_ref = rest[:2] if len(rest) == 3 else (None, None)
    j = pl.program_id(1)
    nj, _, tn = y_ref.shape

    @pl.when(j < nj)
    def _():
        wh = oh_ref.shape[1]
        wp = op_ref.shape[1]
        oh, op, ox = oh_ref[...].astype(BF16), op_ref[...].astype(BF16), ox_ref[...].astype(BF16)
        cw = min(MXU_COLS, tn)
        for c in range(tn // cw):
            cs = slice(c * cw, (c + 1) * cw)
            w = wb_ref[:, cs].astype(BF16)
            if wbb_ref is not None:
                wbb_ref[:, cs] = w
            y = _sigmoid(g0_ref[:, cs].astype(F32)) * _dot(oh, w[0:wh, :])
            y += _sigmoid(g1_ref[:, cs].astype(F32)) * _dot(op, w[wh:wh + wp, :])
            y += _sigmoid(g2_ref[:, cs].astype(F32)) * _dot(ox, w[wh + wp:, :])
            y_ref[j, :, cs] = y.astype(BF16)

    @pl.when(j >= nj)
    def _():
        acc = x_ref[...]
        for c in range(nj):
            rs = slice(c * tn, (c + 1) * tn)
            w = wo_ref[rs, :].astype(BF16)
            if wob_ref is not None:
                wob_ref[rs, :] = w
            acc += _dot(y_ref[c], w)
        o_ref[...] = acc


def _merge_resident_kernel(x_ref, oh_ref, op_ref, ox_ref, g_ref, wb_ref, wo_ref, o_ref, y_ref):
    D = x_ref.shape[1]
    wh = oh_ref.shape[1]
    wp = op_ref.shape[1]
    oh, op, ox = oh_ref[...].astype(BF16), op_ref[...].astype(BF16), ox_ref[...].astype(BF16)
    cw = min(2 * MXU_COLS, D)
    for c in range(D // cw):
        cs = slice(c * cw, (c + 1) * cw)
        gate = lambda k: _sigmoid(g_ref[:, k * D + c * cw:k * D + (c + 1) * cw].astype(F32))
        y = gate(0) * _dot(oh, wb_ref[0:wh, cs])
        y += gate(1) * _dot(op, wb_ref[wh:wh + wp, cs])
        y += gate(2) * _dot(ox, wb_ref[wh + wp:, cs])
        y_ref[:, cs] = y.astype(BF16)
    for c in range(D // cw):
        cs = slice(c * cw, (c + 1) * cw)
        o_ref[:, cs] = x_ref[:, cs] + _dot(y_ref[...], wo_ref[:, cs])


def _merge_resident(x, oh, op, ox, zg, wb, wo, tm):
    T, D = x.shape
    row = lambda a: pl.BlockSpec((tm, a.shape[1]), lambda i: (i, 0))
    held = lambda a: pl.BlockSpec(a.shape, lambda i: (0, 0), pipeline_mode=pl.Buffered(1))
    return pl.pallas_call(
        _merge_resident_kernel,
        grid=(T // tm,),
        in_specs=[row(x), row(oh), row(op), row(ox), row(zg), held(wb), held(wo)],
        out_specs=row(x),
        out_shape=jax.ShapeDtypeStruct((T, D), F32),
        scratch_shapes=[pltpu.VMEM((tm, D), BF16)],
        compiler_params=_params(("parallel",), 56),
        name="merge_resident",
    )(x, oh, op, ox, zg, wb, wo)


def _merge(x, oh, op, ox, zb, wb, wo, layer, tm, tn, emit=False):
    T, D = x.shape
    assert not emit or T == tm
    nj = D // tn
    first = lambda j: jnp.minimum(j, nj - 1)
    second = lambda j: jnp.maximum(j - nj, 0)
    full = lambda a: pl.BlockSpec((tm, a.shape[1]), lambda i, j: (i, 0))
    gate = lambda k: pl.BlockSpec((tm, tn), lambda i, j: (i, k * nj + first(j)))
    wb_idx = lambda i, j: (0, first(j))
    wo_idx = lambda i, j: (0, second(j))
    rows_b = wb.shape[-2]
    out_specs = [pl.BlockSpec((tm, tn), lambda i, j: (i, second(j)))]
    out_shape = [jax.ShapeDtypeStruct((T, D), F32)]
    if emit:
        out_specs += [pl.BlockSpec((rows_b, tn), wb_idx), pl.BlockSpec((D, tn), wo_idx)]
        out_shape += [jax.ShapeDtypeStruct((rows_b, D), BF16), jax.ShapeDtypeStruct((D, D), BF16)]
    out = pl.pallas_call(
        _merge_kernel,
        grid=(T // tm, 2 * nj),
        in_specs=[
            pl.BlockSpec((tm, tn), lambda i, j: (i, second(j))),
            full(oh), full(op), full(ox), gate(0), gate(1), gate(2),
            _layer_spec(wb, layer, (rows_b, tn), wb_idx),
            _layer_spec(wo, layer, (D, tn), wo_idx),
        ],
        out_specs=out_specs,
        out_shape=out_shape,
        scratch_shapes=[pltpu.VMEM((nj, tm, tn), BF16)],
        compiler_params=_params(("parallel", "arbitrary"), 48),
        name="merge_out",
    )(x, oh, op, ox, zb, zb, zb, wb, wo)
    return out if emit else out[0]


def kernel(x_prompt, x_sample, state_hgrn, state_pool, cache_mem_k, cache_mem_v, mem_prompt, ffn1_norm, ffn1_w1, ffn1_w3, ffn1_w2, mix_norm, w_in, lb_logits, hg_norm, w_pool, pool_scale, mem_norm, w_mk, w_mv, w_branch, w_out, ffn2_norm, ffn2_w1, ffn2_w3, ffn2_w2, final_norm):
    B, T, D = x_prompt.shape
    nseq, steps, _ = x_sample.shape
    depth = w_in.shape[0]
    n_mem = mem_prompt.shape[1]
    xw = XA_HEADS * LANE
    pw = len(POOL_WINDOWS) * LANE

    tm_p = min(1024, B * T)
    tm_s = min(512, nseq * steps)

    vec3 = lambda a: a.reshape(a.shape[0], 1, a.shape[1])
    f1n, f2n = vec3(ffn1_norm), vec3(ffn2_norm)
    mix3, mem3, hgn3, psc3 = vec3(mix_norm), vec3(mem_norm), vec3(hg_norm), vec3(pool_scale)
    lbs3 = vec3(_lower_bounds(lb_logits))

    xp = x_prompt.reshape(B * T, D)
    xs = x_sample.reshape(nseq * steps, D)
    mem2 = mem_prompt.reshape(B * n_mem, D)
    ck = cache_mem_k.reshape(depth, nseq, n_mem * XA_HEADS, LANE)
    cv = cache_mem_v.reshape(depth, nseq, n_mem * XA_HEADS, LANE)

    tf = _tile(ffn1_w1.shape[-1], 512)
    tf_s = _tile(ffn1_w1.shape[-1], 256)
    tn_in = _tile(math.gcd(HG_HEADS * LANE, N_BRANCH * D), 1024)
    tn_mrg = _tile(D, 512)
    tm_mem = min(512, B * n_mem)
    hs_p, pb_p, mk_p, mv_p, pb_s = [], [], [], [], []
    hs_s = None

    blocks = []
    for l in range(depth):
        blocks.append((f1n, ffn1_w1, ffn1_w3, ffn1_w2, l))
        blocks.append((f2n, ffn2_w1, ffn2_w3, ffn2_w2, l))

    def swiglu_pair(xp, xs, wbf, k, final_g):
        norm, w1, w3, w2, l = blocks[k]
        if wbf is None:
            xs, *wbf = _ffn(xs, norm, w1, w3, w2, l, tm_s, tf_s, final_g)
        else:
            xs = _ffn(xs, norm, *wbf, l, tm_s, tf, final_g)
        if k + 1 < len(blocks):
            xp, *nxt = _ffn(xp, norm, *wbf, l, tm_p, tf, final_g, side=blocks[k + 1][1:])
        else:
            xp, nxt = _ffn(xp, norm, *wbf, l, tm_p, tf, final_g), None
        return xp, xs, nxt

    wbf = w_in_b = None
    for l in range(depth):
        mk = _rms_matmul(mem2, mem3, w_mk, l, tm_mem, xw)
        mv = _rms_matmul(mem2, mem3, w_mv, l, tm_mem, xw)
        mk_p.append(mk.reshape(B, n_mem, XA_HEADS, LANE))
        mv_p.append(mv.reshape(B, n_mem, XA_HEADS, LANE))

        xp, xs, wbf = swiglu_pair(xp, xs, wbf, 2 * l, None)

        if w_in_b is None:
            zsb, zsf, zsg, w_in_b = _in_proj(xs, mix3, w_in, l, tm_s, tn_in, emit=True)
        else:
            zsb, zsf, zsg = _in_proj(xs, mix3, w_in_b, l, tm_s, tn_in)
        if l + 1 < depth:
            zpb, zpf, zpg, w_in_next = _in_proj(xp, mix3, w_in_b, l, tm_p, tn_in, side=(w_in, l + 1))
        else:
            (zpb, zpf, zpg), w_in_next = _in_proj(xp, mix3, w_in_b, l, tm_p, tn_in), None
        w_in_b = w_in_next

        ohp, sp, ohs, hs_s, oxs = _recurrent_mixers(zpb, zpf, zsb, zsf, lbs3, hgn3, state_hgrn, hs_s, ck, cv, l, B, nseq)
        hs_p.append(sp)

        opp, bp = _pool_prompt(zpf, w_pool, psc3, l, B)
        pb_p.append(bp)
        u_t = zsf[:, ZF_U * pw:(ZF_U + 1) * pw].reshape(nseq, steps, pw).transpose(1, 0, 2)
        buf_t = state_pool[l].transpose(1, 0, 2)
        ops_t, nb_t = _pool_decode(u_t, buf_t, w_pool, psc3, l)
        ops = ops_t.transpose(1, 0, 2).reshape(nseq * steps, pw)
        pb_s.append(nb_t.transpose(1, 0, 2))

        oxp = _xattn_prompt(zpf, mk.reshape(B, n_mem, xw), mv.reshape(B, n_mem, xw), B, min(1024, T))

        xs, w_br_b, w_out_b = _merge(xs, ohs, ops, oxs, zsg, w_branch, w_out, l, tm_s, tn_mrg, emit=True)
        xp = _merge_resident(xp, ohp, opp, oxp, zpg, w_br_b, w_out_b, min(512, B * T))

        fn2 = final_norm.reshape(1, D) if l == depth - 1 else None
        xp, xs, wbf = swiglu_pair(xp, xs, wbf, 2 * l + 1, fn2)

    y_prompt = xp.reshape(B, T, D)
    y_sample = xs.reshape(nseq, steps, D)
    return (y_prompt, y_sample, jnp.stack(hs_p), jnp.stack(pb_p), jnp.stack(mk_p), jnp.stack(mv_p),
            hs_s, jnp.stack(pb_s))
```

```python
import functools
import math

import jax
import jax.numpy as jnp
from jax import lax
from jax.experimental import pallas as pl
from jax.experimental.pallas import tpu as pltpu

F32 = jnp.float32
BF16 = jnp.bfloat16
EPS = 1e-6

LANE = 128
SUBLANE = 8
MXU_COLS = 256
HG_HEADS = 8
POOL_WINDOWS = (2, 4, 8, 16)
POOL_BUF = max(POOL_WINDOWS) - 1
XA_HEADS = 4
N_BRANCH = 3

COL_Q, COL_F, COL_I, COL_OG = 0, HG_HEADS, 2 * HG_HEADS, 3 * HG_HEADS
COL_U = 4 * HG_HEADS
COL_X = COL_U + len(POOL_WINDOWS)
COL_G = COL_X + XA_HEADS
ZB_Q, ZB_V, ZB_OG = 0, 1, 2
ZF_F = 0
ZF_U, ZF_X = 2, 3

HGRN_CHUNK = 128
HGRN_STEP_CHUNKS = 2
DEC_GROUP = 4


def _tile(n, preferred):
    t = preferred
    while n % t:
        t -= LANE
    return t


def _params(semantics, vmem_mib):
    return pltpu.CompilerParams(dimension_semantics=semantics, vmem_limit_bytes=vmem_mib * 1024 * 1024)


def _rms(x, g):
    return x * lax.rsqrt(jnp.mean(x * x, axis=-1, keepdims=True) + EPS) * g


def _sigmoid(x):
    return 0.5 * jnp.tanh(0.5 * x) + 0.5


def _silu(x):
    return x * _sigmoid(x)


def _dot(a, b):
    return jnp.dot(a, b, preferred_element_type=F32)


def _dot_nt(a, b):
    return lax.dot_general(a, b, (((1,), (1,)), ((), ())), preferred_element_type=F32)


def _dot_tn(a, b):
    return lax.dot_general(a, b, (((0,), (0,)), ((), ())), preferred_element_type=F32)


def _ffn_kernel(final, cast, side, x_ref, g_ref, w1_ref, w3_ref, w2_ref, *rest):
    rest = list(rest)
    xn_ref = rest.pop()
    fn_ref = rest.pop(0) if final else None
    side_in = [rest.pop(0) for _ in range(3)] if side else []
    o_ref = rest.pop(0)
    if cast:
        src = (w1_ref, w3_ref, w2_ref)
        w1_ref, w3_ref, w2_ref = [rest.pop(0) for _ in range(3)]
        for dst, s in zip((w1_ref, w3_ref, w2_ref), src):
            dst[...] = s[...].astype(BF16)
    j = pl.program_id(1)

    @pl.when(j == 0)
    def _():
        x = x_ref[...]
        xn_ref[...] = _rms(x, g_ref[...]).astype(BF16)
        o_ref[...] = x

    xn = xn_ref[...]
    h = (_silu(_dot(xn, w1_ref[...])) * _dot(xn, w3_ref[...])).astype(BF16)
    for s, dst in zip(side_in, rest):
        dst[...] = s[...].astype(BF16)
    cw = min(w2_ref.shape[0], o_ref.shape[1])
    for n in range(o_ref.shape[1] // cw):
        cs = slice(n * cw, (n + 1) * cw)
        o_ref[:, cs] += 0.5 * _dot(h, w2_ref[:, cs])

    if final:
        @pl.when(j == pl.num_programs(1) - 1)
        def _():
            o_ref[...] = _rms(o_ref[...], fn_ref[...])


def _layer_spec(w, layer, block, index):
    if w.ndim == 2:
        return pl.BlockSpec(block, index)
    return pl.BlockSpec((None,) + block, lambda i, j: (layer,) + index(i, j))


def _ffn(x, g3, w1, w3, w2, layer, tm, tf, final_g=None, side=None):
    T, D = x.shape
    F = w1.shape[-1]
    cast = w1.dtype != BF16
    final = final_g is not None
    assert not cast or T == tm
    ni, nj = T // tm, F // tf
    col = lambda i, j: (0, j)
    row = lambda i, j: (j, 0)
    out_specs = [pl.BlockSpec((tm, D), lambda i, j: (i, 0))]
    out_shape = [jax.ShapeDtypeStruct((T, D), F32)]
    wshapes = [jax.ShapeDtypeStruct((D, F), BF16), jax.ShapeDtypeStruct((D, F), BF16),
               jax.ShapeDtypeStruct((F, D), BF16)]
    if cast:
        out_specs += [pl.BlockSpec((D, tf), col), pl.BlockSpec((D, tf), col), pl.BlockSpec((tf, D), row)]
        out_shape += wshapes
    in_specs = [
        pl.BlockSpec((tm, D), lambda i, j: (i, 0)),
        pl.BlockSpec((None, 1, D), lambda i, j: (layer, 0, 0)),
        _layer_spec(w1, layer, (D, tf), col),
        _layer_spec(w3, layer, (D, tf), col),
        _layer_spec(w2, layer, (tf, D), row),
    ]
    args = [x, g3, w1, w3, w2]
    if final:
        in_specs.append(pl.BlockSpec((1, D), lambda i, j: (0, 0)))
        args.append(final_g)
    if side is not None:
        s1, s3, s2, side_layer = side
        assert D % (ni * LANE) == 0
        rb = D // ni
        up, down = (lambda i, j: (i, j)), (lambda i, j: (j, i))
        in_specs += [_layer_spec(s1, side_layer, (rb, tf), up), _layer_spec(s3, side_layer, (rb, tf), up),
                     _layer_spec(s2, side_layer, (tf, rb), down)]
        args += [s1, s3, s2]
        out_specs += [pl.BlockSpec((rb, tf), up), pl.BlockSpec((rb, tf), up), pl.BlockSpec((tf, rb), down)]
        out_shape += wshapes
    out = pl.pallas_call(
        functools.partial(_ffn_kernel, final, cast, side is not None),
        grid=(ni, nj),
        in_specs=in_specs,
        out_specs=out_specs,
        out_shape=out_shape,
        scratch_shapes=[pltpu.VMEM((tm, D), BF16)],
        compiler_params=_params(("parallel", "arbitrary"), 62),
        name="ffn_cast" if cast else "ffn",
    )(*args)
    return out if len(out) > 1 else out[0]


def _rms_matmul_kernel(x_ref, g_ref, w_ref, o_ref, *rest):
    xn_ref = rest[-1]

    @pl.when(pl.program_id(1) == 0)
    def _():
        xn_ref[...] = _rms(x_ref[...], g_ref[...]).astype(BF16)

    w = w_ref[...].astype(BF16)
    if len(rest) == 2:
        rest[0][...] = w
    o_ref[...] = _dot(xn_ref[...], w)


def _rms_matmul(x, g3, w, layer, tm, tn, emit=False):
    T, D = x.shape
    N = w.shape[-1]
    assert not emit or T == tm
    col = lambda i, j: (0, j)
    out_specs = [pl.BlockSpec((tm, tn), lambda i, j: (i, j))]
    out_shape = [jax.ShapeDtypeStruct((T, N), F32)]
    if emit:
        out_specs.append(pl.BlockSpec((D, tn), col))
        out_shape.append(jax.ShapeDtypeStruct((D, N), BF16))
    out = pl.pallas_call(
        _rms_matmul_kernel,
        grid=(T // tm, N // tn),
        in_specs=[
            pl.BlockSpec((tm, D), lambda i, j: (i, 0)),
            pl.BlockSpec((None, 1, D), lambda i, j: (layer, 0, 0)),
            _layer_spec(w, layer, (D, tn), col),
        ],
        out_specs=out_specs,
        out_shape=out_shape,
        scratch_shapes=[pltpu.VMEM((tm, D), BF16)],
        compiler_params=_params(("parallel", "arbitrary"), 48),
        name="rms_matmul",
    )(x, g3, w)
    return out if emit else out[0]


def _runs(seq):
    runs = []
    for j, v in enumerate(seq):
        if runs and runs[-1][2] == v - j:
            runs[-1][1] = j + 1
        else:
            runs.append([j, j + 1, v - j])
    return [tuple(r) for r in runs]


def _in_runs(j, runs):
    hit = None
    for lo, hi, _ in runs:
        c = (j >= lo) & (j < hi)
        hit = c if hit is None else hit | c
    return hit


def _lookup(j, runs):
    out = 0
    for lo, hi, off in runs:
        out = out + jnp.where((j >= lo) & (j < hi), j + off, 0)
    return out


def _in_proj_plan(d_model, tn):
    t = lambda blocks: blocks * LANE // tn
    nq, nux, ng = t(HG_HEADS), t(len(POOL_WINDOWS) + XA_HEADS), N_BRANCH * d_model // tn
    src = lambda start, n: list(range(t(start), t(start) + n))
    order = src(COL_Q, nq) + src(COL_I, nq) + src(COL_OG, nq) + src(COL_F, nq) + src(COL_U, nux) + src(COL_G, ng)
    nh, n32 = 3 * nq, nq + nux
    kinds = {"f32": [(nh, nh + n32, 0)]}
    return _runs(order), kinds, nh, n32


def _in_proj_kernel(kinds, side, x_ref, g_ref, w_ref, *rest):
    rest = list(rest)
    xn_ref = rest.pop()
    side_in, side_out = (rest.pop(0), rest.pop()) if side else (None, None)
    zh_ref, zf_ref, zg_ref = rest[:3]
    rest = rest[3:] + [xn_ref]
    j = pl.program_id(1)

    @pl.when(j == 0)
    def _():
        xn_ref[...] = _rms(x_ref[...], g_ref[...]).astype(BF16)

    xn = xn_ref[...]
    tn = w_ref.shape[1]
    cw = min(MXU_COLS, tn)
    zs = []
    for c in range(tn // cw):
        cs = slice(c * cw, (c + 1) * cw)
        w = w_ref[:, cs].astype(BF16)
        if len(rest) == 2:
            rest[0][:, cs] = w
        z = _dot(xn, w)
        zs.append(z)
        zh_ref[:, cs] = z.astype(BF16)
        zg_ref[:, cs] = z.astype(BF16)
    if side:
        side_out[...] = side_in[...].astype(BF16)

    @pl.when(_in_runs(j, kinds["f32"]))
    def _():
        for c, z in enumerate(zs):
            zf_ref[:, c * cw:(c + 1) * cw] = z


def _in_proj(x, g3, w, layer, tm, tn, emit=False, side=None):
    T, D = x.shape
    N = w.shape[-1]
    assert not emit or T == tm
    order, kinds, nh, n32 = _in_proj_plan(D, tn)
    nsteps = N // tn
    wcol = lambda i, j: (0, _lookup(j, order))
    out_specs = [
        pl.BlockSpec((tm, tn), lambda i, j: (i, jnp.minimum(j, nh))),
        pl.BlockSpec((tm, tn), lambda i, j: (i, jnp.clip(j - nh, 0, n32 - 1))),
        pl.BlockSpec((tm, tn), lambda i, j: (i, jnp.maximum(j - nh - n32, 0))),
    ]
    out_shape = [
        jax.ShapeDtypeStruct((T, (nh + 1) * tn), BF16),
        jax.ShapeDtypeStruct((T, n32 * tn), F32),
        jax.ShapeDtypeStruct((T, (nsteps - nh - n32) * tn), BF16),
    ]
    if emit:
        out_specs.append(pl.BlockSpec((D, tn), wcol))
        out_shape.append(jax.ShapeDtypeStruct((D, N), BF16))
    in_specs = [
        pl.BlockSpec((tm, D), lambda i, j: (i, 0)),
        pl.BlockSpec((None, 1, D), lambda i, j: (layer, 0, 0)),
        _layer_spec(w, layer, (D, tn), wcol),
    ]
    args = [x, g3, w]
    if side is not None:
        rb = D // (T // tm)
        assert rb % SUBLANE == 0
        in_specs.append(_layer_spec(side[0], side[1], (rb, tn), lambda i, j: (i, j)))
        args.append(side[0])
        out_specs.append(pl.BlockSpec((rb, tn), lambda i, j: (i, j)))
        out_shape.append(jax.ShapeDtypeStruct((D, N), BF16))
    return pl.pallas_call(
        functools.partial(_in_proj_kernel, kinds, side is not None),
        grid=(T // tm, nsteps),
        in_specs=in_specs,
        out_specs=out_specs,
        out_shape=out_shape,
        scratch_shapes=[pltpu.VMEM((tm, D), BF16)],
        compiler_params=_params(("parallel", "arbitrary"), 58),
        name="in_proj",
    )(*args)


def _lower_bound_kernel(lg_ref, o_ref):
    lg = lg_ref[...]
    depth = lg.shape[0]
    rows = [lg[i:i + 1, :] for i in range(depth)]
    m = rows[0]
    for r in rows[1:]:
        m = jnp.maximum(m, r)
    e = [jnp.exp(r - m) for r in rows]
    tot = e[0]
    for v in e[1:]:
        tot = tot + v
    c = e[0] / tot
    first = c
    o_ref[0:1, :] = c - first
    for i in range(1, depth):
        c = c + e[i] / tot
        o_ref[i:i + 1, :] = c - first


def _lower_bounds(lb_logits):
    return pl.pallas_call(
        _lower_bound_kernel,
        out_shape=jax.ShapeDtypeStruct(lb_logits.shape, F32),
        name="hgrn_lower_bounds",
    )(lb_logits)


def _neg_abs(x):
    bits = lax.bitcast_convert_type(x, jnp.uint32) | jnp.uint32(0x80000000)
    return lax.bitcast_convert_type(bits, F32)


def _group_ref_row(b, s):
    rows, width = b.shape
    gsz = 2 * s
    if gsz >= 8:
        parts = [jnp.broadcast_to(b[i * gsz + s - 1:i * gsz + s, :], (gsz, width)) for i in range(rows // gsz)]
        return parts[0] if len(parts) == 1 else jnp.concatenate(parts, axis=0)
    pos = lax.broadcasted_iota(jnp.int32, b.shape, 0) & (gsz - 1)
    out = b
    for p in range(gsz):
        d = p - (s - 1)
        if d != 0:
            out = jnp.where(pos == p, pltpu.roll(b, d % rows, 0), out)
    return out


def _hgrn_wide(q, zf, vb, lb, states, seg):
    C, W = q.shape
    heads = [slice(h * LANE, (h + 1) * LANE) for h in range(W // LANE)]
    nseg = C // seg
    fg = lb + (1.0 - lb) * jax.nn.sigmoid(zf)
    kk = 1.0 - fg
    g = jnp.log2(fg)

    row = lax.broadcasted_iota(jnp.int32, (C, W), 0)
    rr = lax.broadcasted_iota(jnp.int32, (C, C), 0)
    cc = lax.broadcasted_iota(jnp.int32, (C, C), 1)
    lseg = seg.bit_length() - 1
    tri = jnp.where((cc <= rr) & ((rr >> lseg) == (cc >> lseg)), 1.0, 0.0).astype(F32)
    b = jnp.dot(tri, g, precision=lax.Precision.HIGHEST, preferred_element_type=F32)

    qb, kb = q.astype(BF16), kk.astype(BF16)
    a = [jnp.where(rr == cc, _dot_nt(qb[:, hs], kb[:, hs]), 0.0) for hs in heads]
    s = seg // 2
    while s >= 1:
        ls = s.bit_length() - 1
        same = (rr >> (ls + 1)) == (cc >> (ls + 1))
        if s == 1:
            right = (row & 1) == 1
            qs, ks = jnp.where(right, q * fg, 0.0).astype(BF16), jnp.where(right, 0.0, kk).astype(BF16)
            for h, hs in enumerate(heads):
                a[h] = a[h] + jnp.where(same, _dot_nt(qs[:, hs], ks[:, hs]), 0.0)
            break
        e = jnp.exp2(_neg_abs(b - _group_ref_row(b, s)))
        if s % SUBLANE == 0:
            nblk = C // s
            zero = jnp.zeros((s, W), F32)
            blk = lambda x, i: x[i * s:(i + 1) * s, :]
            qs = jnp.concatenate([blk(q, i) * blk(e, i) if i % 2 else zero for i in range(nblk)], axis=0).astype(BF16)
            ks = jnp.concatenate([zero if i % 2 else blk(kk, i) * blk(e, i) for i in range(nblk)], axis=0).astype(BF16)
            for h, hs in enumerate(heads):
                p = _dot_nt(qs[:, hs], ks[:, hs])
                pieces = []
                for i in range(nblk):
                    if i % 2 == 0:
                        pieces.append(blk(a[h], i))
                    elif 2 * s == C:
                        pieces.append(blk(a[h], i) + blk(p, i))
                    else:
                        pieces.append(blk(a[h], i) + jnp.where(blk(same, i), blk(p, i), 0.0))
                a[h] = jnp.concatenate(pieces, axis=0)
        else:
            right = ((row >> ls) & 1) == 1
            eq = jnp.where(right, e, 0.0)
            qs, ks = (q * eq).astype(BF16), (kk * (e - eq)).astype(BF16)
            for h, hs in enumerate(heads):
                a[h] = a[h] + jnp.where(same, _dot_nt(qs[:, hs], ks[:, hs]), 0.0)
        s //= 2
    o = [_dot(a[h].astype(BF16), vb[:, hs]) for h, hs in enumerate(heads)]

    qe = q * jnp.exp2(b)
    new_states = [[] for _ in heads]
    for k in range(nseg):
        bl = b[k * seg + seg - 1:k * seg + seg, :]
        if nseg == 1:
            qk = qe
            ke = kk * jnp.exp2(bl - b)
        else:
            mine = (row >> lseg) == k
            qk = jnp.where(mine, qe, 0.0)
            ke = jnp.where(mine, kk * jnp.exp2(jnp.where(mine, bl - b, 0.0)), 0.0)
        qkb, keb, ebl = qk.astype(BF16), ke.astype(BF16), jnp.exp2(bl)
        for h, hs in enumerate(heads):
            st = states[h][k]
            o[h] = o[h] + _dot(qkb[:, hs], st.astype(BF16))
            decay = jnp.transpose(jnp.broadcast_to(ebl[:, hs], (LANE, LANE)))
            new_states[h].append(st * decay + _dot_tn(keb[:, hs], vb[:, hs]))
    return o, new_states


def _hgrn_finish(o, zog, hgn):
    return (_rms(o, hgn) * _silu(zog.astype(F32))).astype(BF16)


def _hgrn_prompt_kernel(q_ref, zf_ref, v_ref, og_ref, lb_ref, hgn_ref, oh_ref, s_ref):
    @pl.when(pl.program_id(1) == 0)
    def _():
        s_ref[...] = jnp.zeros_like(s_ref)

    states = [[s_ref[h]] for h in range(HG_HEADS)]
    for sub in range(q_ref.shape[0] // HGRN_CHUNK):
        rs = slice(sub * HGRN_CHUNK, (sub + 1) * HGRN_CHUNK)
        q = _silu(q_ref[rs, :].astype(F32))
        o, states = _hgrn_wide(q, zf_ref[rs, :], v_ref[rs, :], lb_ref[...], states, HGRN_CHUNK)
        for h in range(HG_HEADS):
            cs = slice(h * LANE, (h + 1) * LANE)
            oh_ref[rs, cs] = _hgrn_finish(o[h], og_ref[rs, cs], hgn_ref[:, cs])
    for h in range(HG_HEADS):
        s_ref[h] = states[h][0]


def _hgrn_decode_kernel(seq_len, q_ref, zf_ref, v_ref, og_ref, lb_ref, hgn_ref, s_ref, xq_ref, k_ref, vc_ref, *rest):
    oh_ref, so_ref, ox_ref = rest[-3:]
    _xattn_decode_kernel(seq_len, xq_ref, k_ref, vc_ref, ox_ref)
    rows = DEC_GROUP * seq_len
    for grp in range(s_ref.shape[0] // DEC_GROUP):
        rs = slice(grp * rows, (grp + 1) * rows)
        states = [[s_ref[grp * DEC_GROUP + k, h] for k in range(DEC_GROUP)] for h in range(HG_HEADS)]
        o, new = _hgrn_wide(_silu(q_ref[rs, :].astype(F32)), zf_ref[rs, :], v_ref[rs, :], lb_ref[...], states, seq_len)
        for h in range(HG_HEADS):
            cs = slice(h * LANE, (h + 1) * LANE)
            for k in range(DEC_GROUP):
                so_ref[grp * DEC_GROUP + k, h] = new[h][k]
            oh_ref[rs, cs] = _hgrn_finish(o[h], og_ref[rs, cs], hgn_ref[:, cs])


def _recurrent_mixers_kernel(seq_len, n_side, has_alias, qp, zfp, vp, ogp, lb, hgn, qs, zfs, vs, ogs, s_ref, xq, k,
                             vc, *rest):
    rest = list(rest)
    side_in = [rest.pop(0) for _ in range(n_side)]
    side_out = [rest.pop() for _ in range(n_side)][::-1]
    ohp, sp, ohs, so, oxs = rest[int(has_alias):]
    _hgrn_prompt_kernel(qp, zfp, vp, ogp, lb, hgn, ohp, sp)
    _hgrn_decode_kernel(seq_len, qs, zfs, vs, ogs, lb, hgn, s_ref, xq, k, vc, ohs, so, oxs)
    for s, dst in zip(side_in, side_out):
        dst[...] = s[...].astype(BF16)


def _recurrent_mixers(zpb, zpf, zsb, zsf, lbs3, hgn3, state, stacked, cache_k, cache_v, layer, batch, nseq,
                      side=None):
    T = zpb.shape[0] // batch
    C = HGRN_STEP_CHUNKS * HGRN_CHUNK
    nc = T // C
    nb = nseq // (batch * nc)
    assert nb * batch * nc == nseq and nb % DEC_GROUP == 0
    seq_len = zsb.shape[0] // nseq
    W = HG_HEADS * LANE
    XW = XA_HEADS * LANE
    pspec = lambda col: pl.BlockSpec((C, W), lambda b, c: (b * nc + c, col))
    dspec = lambda width, col: pl.BlockSpec((nb * seq_len, width), lambda b, c: (b * nc + c, col))
    vec = pl.BlockSpec((None, 1, W), lambda b, c: (layer, 0, 0))
    sspec = pl.BlockSpec((None, nb, HG_HEADS, LANE, LANE), lambda b, c: (layer, b * nc + c, 0, 0, 0))
    cspec = pl.BlockSpec((None, nb) + cache_k.shape[2:], lambda b, c: (layer, b * nc + c, 0, 0))
    in_specs = [pspec(ZB_Q), pspec(ZF_F), pspec(ZB_V), pspec(ZB_OG), vec, vec,
                dspec(W, ZB_Q), dspec(W, ZF_F), dspec(W, ZB_V), dspec(W, ZB_OG), sspec, dspec(XW, ZF_X), cspec, cspec]
    args = [zpb, zpf, zpb, zpb, lbs3, hgn3, zsb, zsf, zsb, zsb, state, zsf, cache_k, cache_v]
    out_specs = [
        pl.BlockSpec((C, W), lambda b, c: (b * nc + c, 0)),
        pl.BlockSpec((None, HG_HEADS, LANE, LANE), lambda b, c: (b, 0, 0, 0)),
        dspec(W, 0), sspec, dspec(XW, 0),
    ]
    out_shape = [
        jax.ShapeDtypeStruct((batch * T, W), BF16),
        jax.ShapeDtypeStruct((batch, HG_HEADS, LANE, LANE), F32),
        jax.ShapeDtypeStruct((nseq * seq_len, W), BF16),
        jax.ShapeDtypeStruct(state.shape, F32),
        jax.ShapeDtypeStruct((nseq * seq_len, XW), BF16),
    ]
    side_ws, side_layer = side if side is not None else ((), None)
    for w in side_ws:
        rows, cols = w.shape[-2:]
        rb = rows // (batch * nc)
        assert rb * batch * nc == rows and rb % (2 * SUBLANE) == 0
        in_specs.append(pl.BlockSpec((None, rb, cols), lambda b, c: (side_layer, b * nc + c, 0)))
        args.append(w)
        out_specs.append(pl.BlockSpec((rb, cols), lambda b, c: (b * nc + c, 0)))
        out_shape.append(jax.ShapeDtypeStruct((rows, cols), BF16))
    aliases = {}
    if stacked is not None:
        in_specs.append(pl.BlockSpec(memory_space=pl.ANY))
        args.append(stacked)
        aliases = {len(args) - 1: 3}
    return pl.pallas_call(
        functools.partial(_recurrent_mixers_kernel, seq_len, len(side_ws), stacked is not None),
        grid=(batch, nc),
        in_specs=in_specs,
        out_specs=out_specs,
        out_shape=out_shape,
        input_output_aliases=aliases,
        compiler_params=_params(("arbitrary", "arbitrary"), 48),
        name="recurrent_mixers",
    )(*args)


def _pool_prompt_kernel(u_ref, wp_ref, sc_ref, op_ref, nb_ref):
    u = u_ref[...]
    T = u.shape[0]
    row = lax.broadcasted_iota(jnp.int32, (T, LANE), 0)
    for g, w in enumerate(POOL_WINDOWS):
        cs = slice(g * LANE, (g + 1) * LANE)
        ug = u[:, cs]
        s = ug
        d = 1
        while d < w:
            s = s + jnp.where(row >= d, pltpu.roll(s, d, 0), 0.0)
            d *= 2
        cnt = jnp.minimum(row + 1, w).astype(F32)
        dv = s / cnt - ug
        y = _dot(dv.astype(BF16), wp_ref[g].astype(BF16)) * sc_ref[:, cs]
        op_ref[:, cs] = y.astype(BF16)
    tail = u_ref[T - 16:T, :]
    nb_ref[...] = pltpu.roll(tail, 15, 0)[0:POOL_BUF, :]


def _pool_prompt(zf, w_pool, scale3, layer, batch):
    T = zf.shape[0] // batch
    G = len(POOL_WINDOWS)
    W = G * LANE
    return pl.pallas_call(
        _pool_prompt_kernel,
        grid=(batch,),
        in_specs=[
            pl.BlockSpec((T, W), lambda b: (b, ZF_U)),
            pl.BlockSpec((None, G, LANE, LANE), lambda b: (layer, 0, 0, 0)),
            pl.BlockSpec((None, 1, W), lambda b: (layer, 0, 0)),
        ],
        out_specs=[
            pl.BlockSpec((T, W), lambda b: (b, 0)),
            pl.BlockSpec((None, POOL_BUF, W), lambda b: (b, 0, 0)),
        ],
        out_shape=[
            jax.ShapeDtypeStruct((batch * T, W), BF16),
            jax.ShapeDtypeStruct((batch, POOL_BUF, W), F32),
        ],
        compiler_params=_params(("parallel",), 48),
        name="pool_prompt",
    )(zf, w_pool, scale3)


def _pool_decode_kernel(u_ref, buf_ref, wp_ref, sc_ref, op_ref, nb_ref):
    steps = u_ref.shape[0]
    for g, w in enumerate(POOL_WINDOWS):
        cs = slice(g * LANE, (g + 1) * LANE)
        wg = wp_ref[g].astype(BF16)
        for t in range(steps):
            n_u = min(t + 1, w)
            acc = u_ref[t, :, cs]
            for j in range(t - n_u + 1, t):
                acc = acc + u_ref[j, :, cs]
            for i in range(POOL_BUF - (w - n_u), POOL_BUF):
                acc = acc + buf_ref[i, :, cs]
            dv = acc * (1.0 / w) - u_ref[t, :, cs]
            op_ref[t, :, cs] = _dot(dv.astype(BF16), wg) * sc_ref[:, cs]
    for i in range(POOL_BUF - steps):
        nb_ref[i] = buf_ref[i + steps]
    for t in range(steps):
        nb_ref[POOL_BUF - steps + t] = u_ref[t]


def _pool_decode(u_t, buf_t, w_pool, scale3, layer):
    steps, nseq, W = u_t.shape
    G = len(POOL_WINDOWS)
    return pl.pallas_call(
        _pool_decode_kernel,
        grid=(1,),
        in_specs=[
            pl.BlockSpec((steps, nseq, W), lambda i: (0, 0, 0)),
            pl.BlockSpec((POOL_BUF, nseq, W), lambda i: (0, 0, 0)),
            pl.BlockSpec((None, G, LANE, LANE), lambda i: (layer, 0, 0, 0)),
            pl.BlockSpec((None, 1, W), lambda i: (layer, 0, 0)),
        ],
        out_specs=[
            pl.BlockSpec((steps, nseq, W), lambda i: (0, 0, 0)),
            pl.BlockSpec((POOL_BUF, nseq, W), lambda i: (0, 0, 0)),
        ],
        out_shape=[
            jax.ShapeDtypeStruct((steps, nseq, W), F32),
            jax.ShapeDtypeStruct((POOL_BUF, nseq, W), F32),
        ],
        compiler_params=_params(("arbitrary",), 32),
        name="pool_decode",
    )(u_t, buf_t, w_pool, scale3)


def _softmax_rows(s):
    e = jnp.exp(s - jnp.max(s, axis=-1, keepdims=True))
    return e / jnp.sum(e, axis=-1, keepdims=True)


def _xattn_prompt_kernel(q_ref, k_ref, v_ref, o_ref):
    scale = LANE ** -0.5
    for h in range(XA_HEADS):
        cs = slice(h * LANE, (h + 1) * LANE)
        s = _dot_nt(q_ref[:, cs].astype(BF16), k_ref[:, cs].astype(BF16)) * scale
        p = _softmax_rows(s)
        o_ref[:, cs] = _dot(p.astype(BF16), v_ref[:, cs].astype(BF16)).astype(BF16)


def _xattn_prompt(zf, mk, mv, batch, tq):
    T = zf.shape[0] // batch
    W = XA_HEADS * LANE
    nq = T // tq
    n_mem = mk.shape[1]
    mspec = pl.BlockSpec((None, n_mem, W), lambda b, i: (b, 0, 0))
    return pl.pallas_call(
        _xattn_prompt_kernel,
        grid=(batch, nq),
        in_specs=[pl.BlockSpec((tq, W), lambda b, i: (b * nq + i, ZF_X)), mspec, mspec],
        out_specs=pl.BlockSpec((tq, W), lambda b, i: (b * nq + i, 0)),
        out_shape=jax.ShapeDtypeStruct((batch * T, W), BF16),
        compiler_params=_params(("parallel", "parallel"), 48),
        name="xattn_prompt",
    )(zf, mk, mv)


def _xattn_decode_kernel(seq_len, q_ref, k_ref, v_ref, o_ref):
    scale = LANE ** -0.5
    rows = DEC_GROUP * seq_len
    lseq = seq_len.bit_length() - 1
    lrows = rows.bit_length() - 1
    nk = k_ref.shape[1]
    row_s = lax.broadcasted_iota(jnp.int32, (XA_HEADS * rows, nk), 0)
    col_s = lax.broadcasted_iota(jnp.int32, (XA_HEADS * rows, nk), 1)
    own_head = (col_s & (XA_HEADS - 1)) == (row_s >> lrows)
    seq_s = (row_s & (rows - 1)) >> lseq
    seq_o = (lax.broadcasted_iota(jnp.int32, (XA_HEADS * rows, LANE), 0) & (rows - 1)) >> lseq
    for grp in range(k_ref.shape[0] // DEC_GROUP):
        rs = slice(grp * rows, (grp + 1) * rows)
        qb = jnp.concatenate([q_ref[rs, h * LANE:(h + 1) * LANE] for h in range(XA_HEADS)], axis=0).astype(BF16)
        s = jnp.full((XA_HEADS * rows, nk), -jnp.inf, F32)
        for k in range(DEC_GROUP):
            sk = _dot_nt(qb, k_ref[grp * DEC_GROUP + k].astype(BF16))
            s = jnp.where(own_head & (seq_s == k), sk, s)
        p = _softmax_rows(s * scale).astype(BF16)
        o = jnp.zeros((XA_HEADS * rows, LANE), F32)
        for k in range(DEC_GROUP):
            o = jnp.where(seq_o == k, _dot(p, v_ref[grp * DEC_GROUP + k].astype(BF16)), o)
        for h in range(XA_HEADS):
            o_ref[rs, h * LANE:(h + 1) * LANE] = o[h * rows:(h + 1) * rows, :].astype(BF16)


def _merge_kernel(x_ref, oh_ref, op_ref, ox_ref, g0_ref, g1_ref, g2_ref, wb_ref, wo_ref, o_ref, *rest):
    y_ref = rest[-1]
    wbb_ref, wob_ref = rest[:2] if len(rest) == 3 else (None, None)
    j = pl.program_id(1)
    nj, _, tn = y_ref.shape

    @pl.when(j < nj)
    def _():
        wh = oh_ref.shape[1]
        wp = op_ref.shape[1]
        oh, op, ox = oh_ref[...].astype(BF16), op_ref[...].astype(BF16), ox_ref[...].astype(BF16)
        cw = min(MXU_COLS, tn)
        for c in range(tn // cw):
            cs = slice(c * cw, (c + 1) * cw)
            w = wb_ref[:, cs].astype(BF16)
            if wbb_ref is not None:
                wbb_ref[:, cs] = w
            y = _sigmoid(g0_ref[:, cs].astype(F32)) * _dot(oh, w[0:wh, :])
            y += _sigmoid(g1_ref[:, cs].astype(F32)) * _dot(op, w[wh:wh + wp, :])
            y += _sigmoid(g2_ref[:, cs].astype(F32)) * _dot(ox, w[wh + wp:, :])
            y_ref[j, :, cs] = y.astype(BF16)

    @pl.when(j >= nj)
    def _():
        acc = x_ref[...]
        for c in range(nj):
            rs = slice(c * tn, (c + 1) * tn)
            w = wo_ref[rs, :].astype(BF16)
            if wob_ref is not None:
                wob_ref[rs, :] = w
            acc += _dot(y_ref[c], w)
        o_ref[...] = acc


def _merge_resident_kernel(x_ref, oh_ref, op_ref, ox_ref, g_ref, wb_ref, wo_ref, o_ref, y_ref):
    D = x_ref.shape[1]
    wh = oh_ref.shape[1]
    wp = op_ref.shape[1]
    oh, op, ox = oh_ref[...].astype(BF16), op_ref[...].astype(BF16), ox_ref[...].astype(BF16)
    cw = min(2 * MXU_COLS, D)
    for c in range(D // cw):
        cs = slice(c * cw, (c + 1) * cw)
        gate = lambda k: _sigmoid(g_ref[:, k * D + c * cw:k * D + (c + 1) * cw].astype(F32))
        y = gate(0) * _dot(oh, wb_ref[0:wh, cs])
        y += gate(1) * _dot(op, wb_ref[wh:wh + wp, cs])
        y += gate(2) * _dot(ox, wb_ref[wh + wp:, cs])
        y_ref[:, cs] = y.astype(BF16)
    for c in range(D // cw):
        cs = slice(c * cw, (c + 1) * cw)
        o_ref[:, cs] = x_ref[:, cs] + _dot(y_ref[...], wo_ref[:, cs])


def _merge_resident(x, oh, op, ox, zg, wb, wo, tm):
    T, D = x.shape
    row = lambda a: pl.BlockSpec((tm, a.shape[1]), lambda i: (i, 0))
    held = lambda a: pl.BlockSpec(a.shape, lambda i: (0, 0), pipeline_mode=pl.Buffered(1))
    return pl.pallas_call(
        _merge_resident_kernel,
        grid=(T // tm,),
        in_specs=[row(x), row(oh), row(op), row(ox), row(zg), held(wb), held(wo)],
        out_specs=row(x),
        out_shape=jax.ShapeDtypeStruct((T, D), F32),
        scratch_shapes=[pltpu.VMEM((tm, D), BF16)],
        compiler_params=_params(("parallel",), 56),
        name="merge_resident",
    )(x, oh, op, ox, zg, wb, wo)


def _merge(x, oh, op, ox, zb, wb, wo, layer, tm, tn, emit=False):
    T, D = x.shape
    assert not emit or T == tm
    nj = D // tn
    first = lambda j: jnp.minimum(j, nj - 1)
    second = lambda j: jnp.maximum(j - nj, 0)
    full = lambda a: pl.BlockSpec((tm, a.shape[1]), lambda i, j: (i, 0))
    gate = lambda k: pl.BlockSpec((tm, tn), lambda i, j: (i, k * nj + first(j)))
    wb_idx = lambda i, j: (0, first(j))
    wo_idx = lambda i, j: (0, second(j))
    rows_b = wb.shape[-2]
    out_specs = [pl.BlockSpec((tm, tn), lambda i, j: (i, second(j)))]
    out_shape = [jax.ShapeDtypeStruct((T, D), F32)]
    if emit:
        out_specs += [pl.BlockSpec((rows_b, tn), wb_idx), pl.BlockSpec((D, tn), wo_idx)]
        out_shape += [jax.ShapeDtypeStruct((rows_b, D), BF16), jax.ShapeDtypeStruct((D, D), BF16)]
    out = pl.pallas_call(
        _merge_kernel,
        grid=(T // tm, 2 * nj),
        in_specs=[
            pl.BlockSpec((tm, tn), lambda i, j: (i, second(j))),
            full(oh), full(op), full(ox), gate(0), gate(1), gate(2),
            _layer_spec(wb, layer, (rows_b, tn), wb_idx),
            _layer_spec(wo, layer, (D, tn), wo_idx),
        ],
        out_specs=out_specs,
        out_shape=out_shape,
        scratch_shapes=[pltpu.VMEM((nj, tm, tn), BF16)],
        compiler_params=_params(("parallel", "arbitrary"), 48),
        name="merge_out",
    )(x, oh, op, ox, zb, zb, zb, wb, wo)
    return out if emit else out[0]


def kernel(x_prompt, x_sample, state_hgrn, state_pool, cache_mem_k, cache_mem_v, mem_prompt, ffn1_norm, ffn1_w1, ffn1_w3, ffn1_w2, mix_norm, w_in, lb_logits, hg_norm, w_pool, pool_scale, mem_norm, w_mk, w_mv, w_branch, w_out, ffn2_norm, ffn2_w1, ffn2_w3, ffn2_w2, final_norm):
    B, T, D = x_prompt.shape
    nseq, steps, _ = x_sample.shape
    depth = w_in.shape[0]
    n_mem = mem_prompt.shape[1]
    xw = XA_HEADS * LANE
    pw = len(POOL_WINDOWS) * LANE

    tm_p = min(1024, B * T)
    tm_s = min(512, nseq * steps)

    vec3 = lambda a: a.reshape(a.shape[0], 1, a.shape[1])
    f1n, f2n = vec3(ffn1_norm), vec3(ffn2_norm)
    mix3, mem3, hgn3, psc3 = vec3(mix_norm), vec3(mem_norm), vec3(hg_norm), vec3(pool_scale)
    lbs3 = vec3(_lower_bounds(lb_logits))

    xp = x_prompt.reshape(B * T, D)
    xs = x_sample.reshape(nseq * steps, D)
    mem2 = mem_prompt.reshape(B * n_mem, D)
    ck = cache_mem_k.reshape(depth, nseq, n_mem * XA_HEADS, LANE)
    cv = cache_mem_v.reshape(depth, nseq, n_mem * XA_HEADS, LANE)

    tf = _tile(ffn1_w1.shape[-1], 512)
    tf_s = _tile(ffn1_w1.shape[-1], 256)
    tn_in = _tile(math.gcd(HG_HEADS * LANE, N_BRANCH * D), 1024)
    tn_mrg = _tile(D, 512)
    tm_mem = min(512, B * n_mem)
    hs_p, pb_p, mk_p, mv_p, pb_s = [], [], [], [], []
    hs_s = None

    blocks = []
    for l in range(depth):
        blocks.append((f1n, ffn1_w1, ffn1_w3, ffn1_w2, l))
        blocks.append((f2n, ffn2_w1, ffn2_w3, ffn2_w2, l))

    def swiglu_pair(xp, xs, wbf, k, final_g):
        norm, w1, w3, w2, l = blocks[k]
        if wbf is None:
            xs, *wbf = _ffn(xs, norm, w1, w3, w2, l, tm_s, tf_s, final_g)
        else:
            xs = _ffn(xs, norm, *wbf, l, tm_s, tf, final_g)
        if k + 1 < len(blocks):
            xp, *nxt = _ffn(xp, norm, *wbf, l, tm_p, tf, final_g, side=blocks[k + 1][1:])
        else:
            xp, nxt = _ffn(xp, norm, *wbf, l, tm_p, tf, final_g), None
        return xp, xs, nxt

    wbf = w_in_b = mix_w = None
    for l in range(depth):
        mk = _rms_matmul(mem2, mem3, w_mk, l, tm_mem, xw)
        mv = _rms_matmul(mem2, mem3, w_mv, l, tm_mem, xw)
        mk_p.append(mk.reshape(B, n_mem, XA_HEADS, LANE))
        mv_p.append(mv.reshape(B, n_mem, XA_HEADS, LANE))

        xp, xs, wbf = swiglu_pair(xp, xs, wbf, 2 * l, None)

        if w_in_b is None:
            zsb, zsf, zsg, w_in_b = _in_proj(xs, mix3, w_in, l, tm_s, tn_in, emit=True)
        else:
            zsb, zsf, zsg = _in_proj(xs, mix3, w_in_b, l, tm_s, tn_in)
        if l + 1 < depth:
            zpb, zpf, zpg, w_in_next = _in_proj(xp, mix3, w_in_b, l, tm_p, tn_in, side=(w_in, l + 1))
        else:
            (zpb, zpf, zpg), w_in_next = _in_proj(xp, mix3, w_in_b, l, tm_p, tn_in), None
        w_in_b = w_in_next

        side = ([w_branch, w_out], l + 1) if l + 1 < depth else None
        ohp, sp, ohs, hs_s, oxs, *mix_next = _recurrent_mixers(zpb, zpf, zsb, zsf, lbs3, hgn3, state_hgrn, hs_s, ck, cv,
                                                               l, B, nseq, side)
        hs_p.append(sp)

        opp, bp = _pool_prompt(zpf, w_pool, psc3, l, B)
        pb_p.append(bp)
        u_t = zsf[:, ZF_U * pw:(ZF_U + 1) * pw].reshape(nseq, steps, pw).transpose(1, 0, 2)
        buf_t = state_pool[l].transpose(1, 0, 2)
        ops_t, nb_t = _pool_decode(u_t, buf_t, w_pool, psc3, l)
        ops = ops_t.transpose(1, 0, 2).reshape(nseq * steps, pw)
        pb_s.append(nb_t.transpose(1, 0, 2))

        oxp = _xattn_prompt(zpf, mk.reshape(B, n_mem, xw), mv.reshape(B, n_mem, xw), B, min(1024, T))

        if mix_w is None:
            xs, *mix_w = _merge(xs, ohs, ops, oxs, zsg, w_branch, w_out, l, tm_s, tn_mrg, emit=True)
        else:
            xs = _merge(xs, ohs, ops, oxs, zsg, *mix_w, l, tm_s, tn_mrg)
        xp = _merge_resident(xp, ohp, opp, oxp, zpg, *mix_w, min(512, B * T))
        mix_w = mix_next or None

        fn2 = final_norm.reshape(1, D) if l == depth - 1 else None
        xp, xs, wbf = swiglu_pair(xp, xs, wbf, 2 * l + 1, fn2)

    y_prompt = xp.reshape(B, T, D)
    y_sample = xs.reshape(nseq, steps, D)
    return (y_prompt, y_sample, jnp.stack(hs_p), jnp.stack(pb_p), jnp.stack(mk_p), jnp.stack(mv_p),
            hs_s, jnp.stack(pb_s))
```

```python
import functools
import math

import jax
import jax.numpy as jnp
from jax import lax
from jax.experimental import pallas as pl
from jax.experimental.pallas import tpu as pltpu

F32 = jnp.float32
BF16 = jnp.bfloat16
EPS = 1e-6

LANE = 128
SUBLANE = 8
MXU_COLS = 256
HG_HEADS = 8
POOL_WINDOWS = (2, 4, 8, 16)
POOL_BUF = max(POOL_WINDOWS) - 1
XA_HEADS = 4
N_BRANCH = 3

COL_Q, COL_F, COL_I, COL_OG = 0, HG_HEADS, 2 * HG_HEADS, 3 * HG_HEADS
COL_U = 4 * HG_HEADS
COL_X = COL_U + len(POOL_WINDOWS)
COL_G = COL_X + XA_HEADS
ZB_Q, ZB_V, ZB_OG = 0, 1, 2
ZF_F = 0
ZF_U, ZF_X = 2, 3

HGRN_CHUNK = 128
HGRN_STEP_CHUNKS = 2
DEC_GROUP = 4


def _tile(n, preferred):
    t = preferred
    while n % t:
        t -= LANE
    return t


def _params(semantics, vmem_mib):
    return pltpu.CompilerParams(dimension_semantics=semantics, vmem_limit_bytes=vmem_mib * 1024 * 1024)


def _rms(x, g):
    return x * lax.rsqrt(jnp.mean(x * x, axis=-1, keepdims=True) + EPS) * g


def _sigmoid(x):
    return 0.5 * jnp.tanh(0.5 * x) + 0.5


def _silu(x):
    return x * _sigmoid(x)


def _dot(a, b):
    return jnp.dot(a, b, preferred_element_type=F32)


def _dot_nt(a, b):
    return lax.dot_general(a, b, (((1,), (1,)), ((), ())), preferred_element_type=F32)


def _dot_tn(a, b):
    return lax.dot_general(a, b, (((0,), (0,)), ((), ())), preferred_element_type=F32)


def _ffn_nested_kernel(final, side, grid, in_specs, out_specs, *refs):
    xn_ref = refs[-1]
    body = lambda idx, *blocks: _ffn_kernel(final, False, side, *blocks, xn_ref, step=idx[1], nsteps=grid[1])
    pltpu.emit_pipeline(body, grid=grid, in_specs=in_specs, out_specs=out_specs, _explicit_indices=True)(*refs[:-1])


def _ffn_kernel(final, cast, side, x_ref, g_ref, w1_ref, w3_ref, w2_ref, *rest, step=None, nsteps=None):
    rest = list(rest)
    xn_ref = rest.pop()
    fn_ref = rest.pop(0) if final else None
    side_in = [rest.pop(0) for _ in range(3)] if side else []
    o_ref = rest.pop(0)
    if cast:
        src = (w1_ref, w3_ref, w2_ref)
        w1_ref, w3_ref, w2_ref = [rest.pop(0) for _ in range(3)]
        for dst, s in zip((w1_ref, w3_ref, w2_ref), src):
            dst[...] = s[...].astype(BF16)
    j = pl.program_id(1) if step is None else step
    nsteps = pl.num_programs(1) if nsteps is None else nsteps

    @pl.when(j == 0)
    def _():
        x = x_ref[...]
        xn_ref[...] = _rms(x, g_ref[...]).astype(BF16)
        o_ref[...] = x

    xn = xn_ref[...]
    h = (_silu(_dot(xn, w1_ref[...])) * _dot(xn, w3_ref[...])).astype(BF16)
    for s, dst in zip(side_in, rest):
        dst[...] = s[...].astype(BF16)
    cw = min(w2_ref.shape[0], o_ref.shape[1])
    for n in range(o_ref.shape[1] // cw):
        cs = slice(n * cw, (n + 1) * cw)
        o_ref[:, cs] += 0.5 * _dot(h, w2_ref[:, cs])

    if final:
        @pl.when(j == nsteps - 1)
        def _():
            o_ref[...] = _rms(o_ref[...], fn_ref[...])


def _layer_spec(w, layer, block, index):
    if w.ndim == 2:
        return pl.BlockSpec(block, index)
    return pl.BlockSpec((None,) + block, lambda i, j: (layer,) + index(i, j))


def _ffn(x, g3, w1, w3, w2, layer, tm, tf, final_g=None, side=None, nested=False):
    T, D = x.shape
    F = w1.shape[-1]
    cast = w1.dtype != BF16
    final = final_g is not None
    assert not cast or T == tm
    ni, nj = T // tm, F // tf
    col = lambda i, j: (0, j)
    row = lambda i, j: (j, 0)
    out_specs = [pl.BlockSpec((tm, D), lambda i, j: (i, 0))]
    out_shape = [jax.ShapeDtypeStruct((T, D), F32)]
    wshapes = [jax.ShapeDtypeStruct((D, F), BF16), jax.ShapeDtypeStruct((D, F), BF16),
               jax.ShapeDtypeStruct((F, D), BF16)]
    if cast:
        out_specs += [pl.BlockSpec((D, tf), col), pl.BlockSpec((D, tf), col), pl.BlockSpec((tf, D), row)]
        out_shape += wshapes
    in_specs = [
        pl.BlockSpec((tm, D), lambda i, j: (i, 0)),
        pl.BlockSpec((None, 1, D), lambda i, j: (layer, 0, 0)),
        _layer_spec(w1, layer, (D, tf), col),
        _layer_spec(w3, layer, (D, tf), col),
        _layer_spec(w2, layer, (tf, D), row),
    ]
    args = [x, g3, w1, w3, w2]
    if final:
        in_specs.append(pl.BlockSpec((1, D), lambda i, j: (0, 0)))
        args.append(final_g)
    if side is not None:
        s1, s3, s2, side_layer = side
        assert D % (ni * LANE) == 0
        rb = D // ni
        up, down = (lambda i, j: (i, j)), (lambda i, j: (j, i))
        in_specs += [_layer_spec(s1, side_layer, (rb, tf), up), _layer_spec(s3, side_layer, (rb, tf), up),
                     _layer_spec(s2, side_layer, (tf, rb), down)]
        args += [s1, s3, s2]
        out_specs += [pl.BlockSpec((rb, tf), up), pl.BlockSpec((rb, tf), up), pl.BlockSpec((tf, rb), down)]
        out_shape += wshapes
    if nested:
        assert not cast
        hbm = pl.BlockSpec(memory_space=pl.ANY)
        out = pl.pallas_call(
            functools.partial(_ffn_nested_kernel, final, side is not None, (ni, nj), in_specs, out_specs),
            in_specs=[hbm] * len(args),
            out_specs=[hbm] * len(out_shape),
            out_shape=out_shape,
            scratch_shapes=[pltpu.VMEM((tm, D), BF16)],
            compiler_params=pltpu.CompilerParams(vmem_limit_bytes=62 * 1024 * 1024),
            name="ffn_nested",
        )(*args)
        return out if len(out) > 1 else out[0]
    out = pl.pallas_call(
        functools.partial(_ffn_kernel, final, cast, side is not None),
        grid=(ni, nj),
        in_specs=in_specs,
        out_specs=out_specs,
        out_shape=out_shape,
        scratch_shapes=[pltpu.VMEM((tm, D), BF16)],
        compiler_params=_params(("parallel", "arbitrary"), 62),
        name="ffn_cast" if cast else "ffn",
    )(*args)
    return out if len(out) > 1 else out[0]


def _rms_matmul_kernel(x_ref, g_ref, w_ref, o_ref, *rest):
    xn_ref = rest[-1]

    @pl.when(pl.program_id(1) == 0)
    def _():
        xn_ref[...] = _rms(x_ref[...], g_ref[...]).astype(BF16)

    w = w_ref[...].astype(BF16)
    if len(rest) == 2:
        rest[0][...] = w
    o_ref[...] = _dot(xn_ref[...], w)


def _rms_matmul(x, g3, w, layer, tm, tn, emit=False):
    T, D = x.shape
    N = w.shape[-1]
    assert not emit or T == tm
    col = lambda i, j: (0, j)
    out_specs = [pl.BlockSpec((tm, tn), lambda i, j: (i, j))]
    out_shape = [jax.ShapeDtypeStruct((T, N), F32)]
    if emit:
        out_specs.append(pl.BlockSpec((D, tn), col))
        out_shape.append(jax.ShapeDtypeStruct((D, N), BF16))
    out = pl.pallas_call(
        _rms_matmul_kernel,
        grid=(T // tm, N // tn),
        in_specs=[
            pl.BlockSpec((tm, D), lambda i, j: (i, 0)),
            pl.BlockSpec((None, 1, D), lambda i, j: (layer, 0, 0)),
            _layer_spec(w, layer, (D, tn), col),
        ],
        out_specs=out_specs,
        out_shape=out_shape,
        scratch_shapes=[pltpu.VMEM((tm, D), BF16)],
        compiler_params=_params(("parallel", "arbitrary"), 48),
        name="rms_matmul",
    )(x, g3, w)
    return out if emit else out[0]


def _runs(seq):
    runs = []
    for j, v in enumerate(seq):
        if runs and runs[-1][2] == v - j:
            runs[-1][1] = j + 1
        else:
            runs.append([j, j + 1, v - j])
    return [tuple(r) for r in runs]


def _in_runs(j, runs):
    hit = None
    for lo, hi, _ in runs:
        c = (j >= lo) & (j < hi)
        hit = c if hit is None else hit | c
    return hit


def _lookup(j, runs):
    out = 0
    for lo, hi, off in runs:
        out = out + jnp.where((j >= lo) & (j < hi), j + off, 0)
    return out


def _in_proj_plan(d_model, tn):
    t = lambda blocks: blocks * LANE // tn
    nq, nux, ng = t(HG_HEADS), t(len(POOL_WINDOWS) + XA_HEADS), N_BRANCH * d_model // tn
    src = lambda start, n: list(range(t(start), t(start) + n))
    order = src(COL_Q, nq) + src(COL_I, nq) + src(COL_OG, nq) + src(COL_F, nq) + src(COL_U, nux) + src(COL_G, ng)
    nh, n32 = 3 * nq, nq + nux
    kinds = {"f32": [(nh, nh + n32, 0)]}
    return _runs(order), kinds, nh, n32


def _in_proj_kernel(kinds, side, x_ref, g_ref, w_ref, *rest):
    rest = list(rest)
    xn_ref = rest.pop()
    side_in, side_out = (rest.pop(0), rest.pop()) if side else (None, None)
    zh_ref, zf_ref, zg_ref = rest[:3]
    rest = rest[3:] + [xn_ref]
    j = pl.program_id(1)

    @pl.when(j == 0)
    def _():
        xn_ref[...] = _rms(x_ref[...], g_ref[...]).astype(BF16)

    xn = xn_ref[...]
    tn = w_ref.shape[1]
    cw = min(MXU_COLS, tn)
    zs = []
    for c in range(tn // cw):
        cs = slice(c * cw, (c + 1) * cw)
        w = w_ref[:, cs].astype(BF16)
        if len(rest) == 2:
            rest[0][:, cs] = w
        z = _dot(xn, w)
        zs.append(z)
        zh_ref[:, cs] = z.astype(BF16)
        zg_ref[:, cs] = z.astype(BF16)
    if side:
        side_out[...] = side_in[...].astype(BF16)

    @pl.when(_in_runs(j, kinds["f32"]))
    def _():
        for c, z in enumerate(zs):
            zf_ref[:, c * cw:(c + 1) * cw] = z


def _in_proj(x, g3, w, layer, tm, tn, emit=False, side=None):
    T, D = x.shape
    N = w.shape[-1]
    assert not emit or T == tm
    order, kinds, nh, n32 = _in_proj_plan(D, tn)
    nsteps = N // tn
    wcol = lambda i, j: (0, _lookup(j, order))
    out_specs = [
        pl.BlockSpec((tm, tn), lambda i, j: (i, jnp.minimum(j, nh))),
        pl.BlockSpec((tm, tn), lambda i, j: (i, jnp.clip(j - nh, 0, n32 - 1))),
        pl.BlockSpec((tm, tn), lambda i, j: (i, jnp.maximum(j - nh - n32, 0))),
    ]
    out_shape = [
        jax.ShapeDtypeStruct((T, (nh + 1) * tn), BF16),
        jax.ShapeDtypeStruct((T, n32 * tn), F32),
        jax.ShapeDtypeStruct((T, (nsteps - nh - n32) * tn), BF16),
    ]
    if emit:
        out_specs.append(pl.BlockSpec((D, tn), wcol))
        out_shape.append(jax.ShapeDtypeStruct((D, N), BF16))
    in_specs = [
        pl.BlockSpec((tm, D), lambda i, j: (i, 0)),
        pl.BlockSpec((None, 1, D), lambda i, j: (layer, 0, 0)),
        _layer_spec(w, layer, (D, tn), wcol),
    ]
    args = [x, g3, w]
    if side is not None:
        rb = D // (T // tm)
        assert rb % SUBLANE == 0
        in_specs.append(_layer_spec(side[0], side[1], (rb, tn), lambda i, j: (i, j)))
        args.append(side[0])
        out_specs.append(pl.BlockSpec((rb, tn), lambda i, j: (i, j)))
        out_shape.append(jax.ShapeDtypeStruct((D, N), BF16))
    return pl.pallas_call(
        functools.partial(_in_proj_kernel, kinds, side is not None),
        grid=(T // tm, nsteps),
        in_specs=in_specs,
        out_specs=out_specs,
        out_shape=out_shape,
        scratch_shapes=[pltpu.VMEM((tm, D), BF16)],
        compiler_params=_params(("parallel", "arbitrary"), 58),
        name="in_proj",
    )(*args)


def _lower_bound_kernel(lg_ref, o_ref):
    lg = lg_ref[...]
    depth = lg.shape[0]
    rows = [lg[i:i + 1, :] for i in range(depth)]
    m = rows[0]
    for r in rows[1:]:
        m = jnp.maximum(m, r)
    e = [jnp.exp(r - m) for r in rows]
    tot = e[0]
    for v in e[1:]:
        tot = tot + v
    c = e[0] / tot
    first = c
    o_ref[0:1, :] = c - first
    for i in range(1, depth):
        c = c + e[i] / tot
        o_ref[i:i + 1, :] = c - first


def _lower_bounds(lb_logits):
    return pl.pallas_call(
        _lower_bound_kernel,
        out_shape=jax.ShapeDtypeStruct(lb_logits.shape, F32),
        name="hgrn_lower_bounds",
    )(lb_logits)


def _neg_abs(x):
    bits = lax.bitcast_convert_type(x, jnp.uint32) | jnp.uint32(0x80000000)
    return lax.bitcast_convert_type(bits, F32)


def _group_ref_row(b, s):
    rows, width = b.shape
    gsz = 2 * s
    if gsz >= 8:
        parts = [jnp.broadcast_to(b[i * gsz + s - 1:i * gsz + s, :], (gsz, width)) for i in range(rows // gsz)]
        return parts[0] if len(parts) == 1 else jnp.concatenate(parts, axis=0)
    pos = lax.broadcasted_iota(jnp.int32, b.shape, 0) & (gsz - 1)
    out = b
    for p in range(gsz):
        d = p - (s - 1)
        if d != 0:
            out = jnp.where(pos == p, pltpu.roll(b, d % rows, 0), out)
    return out


def _hgrn_wide(q, zf, vb, lb, states, seg):
    C, W = q.shape
    heads = [slice(h * LANE, (h + 1) * LANE) for h in range(W // LANE)]
    nseg = C // seg
    fg = lb + (1.0 - lb) * jax.nn.sigmoid(zf)
    kk = 1.0 - fg
    g = jnp.log2(fg)

    row = lax.broadcasted_iota(jnp.int32, (C, W), 0)
    rr = lax.broadcasted_iota(jnp.int32, (C, C), 0)
    cc = lax.broadcasted_iota(jnp.int32, (C, C), 1)
    lseg = seg.bit_length() - 1
    tri = jnp.where((cc <= rr) & ((rr >> lseg) == (cc >> lseg)), 1.0, 0.0).astype(F32)
    b = jnp.dot(tri, g, precision=lax.Precision.HIGHEST, preferred_element_type=F32)

    qb, kb = q.astype(BF16), kk.astype(BF16)
    a = [jnp.where(rr == cc, _dot_nt(qb[:, hs], kb[:, hs]), 0.0) for hs in heads]
    s = seg // 2
    while s >= 1:
        ls = s.bit_length() - 1
        same = (rr >> (ls + 1)) == (cc >> (ls + 1))
        if s == 1:
            right = (row & 1) == 1
            qs, ks = jnp.where(right, q * fg, 0.0).astype(BF16), jnp.where(right, 0.0, kk).astype(BF16)
            for h, hs in enumerate(heads):
                a[h] = a[h] + jnp.where(same, _dot_nt(qs[:, hs], ks[:, hs]), 0.0)
            break
        e = jnp.exp2(_neg_abs(b - _group_ref_row(b, s)))
        if s % SUBLANE == 0:
            nblk = C // s
            zero = jnp.zeros((s, W), F32)
            blk = lambda x, i: x[i * s:(i + 1) * s, :]
            qs = jnp.concatenate([blk(q, i) * blk(e, i) if i % 2 else zero for i in range(nblk)], axis=0).astype(BF16)
            ks = jnp.concatenate([zero if i % 2 else blk(kk, i) * blk(e, i) for i in range(nblk)], axis=0).astype(BF16)
            for h, hs in enumerate(heads):
                p = _dot_nt(qs[:, hs], ks[:, hs])
                pieces = []
                for i in range(nblk):
                    if i % 2 == 0:
                        pieces.append(blk(a[h], i))
                    elif 2 * s == C:
                        pieces.append(blk(a[h], i) + blk(p, i))
                    else:
                        pieces.append(blk(a[h], i) + jnp.where(blk(same, i), blk(p, i), 0.0))
                a[h] = jnp.concatenate(pieces, axis=0)
        else:
            right = ((row >> ls) & 1) == 1
            eq = jnp.where(right, e, 0.0)
            qs, ks = (q * eq).astype(BF16), (kk * (e - eq)).astype(BF16)
            for h, hs in enumerate(heads):
                a[h] = a[h] + jnp.where(same, _dot_nt(qs[:, hs], ks[:, hs]), 0.0)
        s //= 2
    o = [_dot(a[h].astype(BF16), vb[:, hs]) for h, hs in enumerate(heads)]

    qe = q * jnp.exp2(b)
    new_states = [[] for _ in heads]
    for k in range(nseg):
        bl = b[k * seg + seg - 1:k * seg + seg, :]
        if nseg == 1:
            qk = qe
            ke = kk * jnp.exp2(bl - b)
        else:
            mine = (row >> lseg) == k
            qk = jnp.where(mine, qe, 0.0)
            ke = jnp.where(mine, kk * jnp.exp2(jnp.where(mine, bl - b, 0.0)), 0.0)
        qkb, keb, ebl = qk.astype(BF16), ke.astype(BF16), jnp.exp2(bl)
        for h, hs in enumerate(heads):
            st = states[h][k]
            o[h] = o[h] + _dot(qkb[:, hs], st.astype(BF16))
            decay = jnp.transpose(jnp.broadcast_to(ebl[:, hs], (LANE, LANE)))
            new_states[h].append(st * decay + _dot_tn(keb[:, hs], vb[:, hs]))
    return o, new_states


def _hgrn_finish(o, zog, hgn):
    return (_rms(o, hgn) * _silu(zog.astype(F32))).astype(BF16)


def _hgrn_prompt_kernel(q_ref, zf_ref, v_ref, og_ref, lb_ref, hgn_ref, oh_ref, s_ref):
    @pl.when(pl.program_id(1) == 0)
    def _():
        s_ref[...] = jnp.zeros_like(s_ref)

    states = [[s_ref[h]] for h in range(HG_HEADS)]
    for sub in range(q_ref.shape[0] // HGRN_CHUNK):
        rs = slice(sub * HGRN_CHUNK, (sub + 1) * HGRN_CHUNK)
        q = _silu(q_ref[rs, :].astype(F32))
        o, states = _hgrn_wide(q, zf_ref[rs, :], v_ref[rs, :], lb_ref[...], states, HGRN_CHUNK)
        for h in range(HG_HEADS):
            cs = slice(h * LANE, (h + 1) * LANE)
            oh_ref[rs, cs] = _hgrn_finish(o[h], og_ref[rs, cs], hgn_ref[:, cs])
    for h in range(HG_HEADS):
        s_ref[h] = states[h][0]


def _hgrn_decode_kernel(seq_len, q_ref, zf_ref, v_ref, og_ref, lb_ref, hgn_ref, s_ref, xq_ref, k_ref, vc_ref, *rest):
    oh_ref, so_ref, ox_ref = rest[-3:]
    _xattn_decode_kernel(seq_len, xq_ref, k_ref, vc_ref, ox_ref)
    rows = DEC_GROUP * seq_len
    for grp in range(s_ref.shape[0] // DEC_GROUP):
        rs = slice(grp * rows, (grp + 1) * rows)
        states = [[s_ref[grp * DEC_GROUP + k, h] for k in range(DEC_GROUP)] for h in range(HG_HEADS)]
        o, new = _hgrn_wide(_silu(q_ref[rs, :].astype(F32)), zf_ref[rs, :], v_ref[rs, :], lb_ref[...], states, seq_len)
        for h in range(HG_HEADS):
            cs = slice(h * LANE, (h + 1) * LANE)
            for k in range(DEC_GROUP):
                so_ref[grp * DEC_GROUP + k, h] = new[h][k]
            oh_ref[rs, cs] = _hgrn_finish(o[h], og_ref[rs, cs], hgn_ref[:, cs])


def _recurrent_mixers_kernel(seq_len, has_alias, qp, zfp, vp, ogp, lb, hgn, qs, zfs, vs, ogs, s_ref, xq, k, vc, *rest):
    ohp, sp, ohs, so, oxs = rest[int(has_alias):]
    _hgrn_prompt_kernel(qp, zfp, vp, ogp, lb, hgn, ohp, sp)
    _hgrn_decode_kernel(seq_len, qs, zfs, vs, ogs, lb, hgn, s_ref, xq, k, vc, ohs, so, oxs)


def _recurrent_mixers(zpb, zpf, zsb, zsf, lbs3, hgn3, state, stacked, cache_k, cache_v, layer, batch, nseq):
    T = zpb.shape[0] // batch
    C = HGRN_STEP_CHUNKS * HGRN_CHUNK
    nc = T // C
    nb = nseq // (batch * nc)
    assert nb * batch * nc == nseq and nb % DEC_GROUP == 0
    seq_len = zsb.shape[0] // nseq
    W = HG_HEADS * LANE
    XW = XA_HEADS * LANE
    pspec = lambda col: pl.BlockSpec((C, W), lambda b, c: (b * nc + c, col))
    dspec = lambda width, col: pl.BlockSpec((nb * seq_len, width), lambda b, c: (b * nc + c, col))
    vec = pl.BlockSpec((None, 1, W), lambda b, c: (layer, 0, 0))
    sspec = pl.BlockSpec((None, nb, HG_HEADS, LANE, LANE), lambda b, c: (layer, b * nc + c, 0, 0, 0))
    cspec = pl.BlockSpec((None, nb) + cache_k.shape[2:], lambda b, c: (layer, b * nc + c, 0, 0))
    in_specs = [pspec(ZB_Q), pspec(ZF_F), pspec(ZB_V), pspec(ZB_OG), vec, vec,
                dspec(W, ZB_Q), dspec(W, ZF_F), dspec(W, ZB_V), dspec(W, ZB_OG), sspec, dspec(XW, ZF_X), cspec, cspec]
    args = [zpb, zpf, zpb, zpb, lbs3, hgn3, zsb, zsf, zsb, zsb, state, zsf, cache_k, cache_v]
    aliases = {}
    if stacked is not None:
        in_specs.append(pl.BlockSpec(memory_space=pl.ANY))
        args.append(stacked)
        aliases = {len(args) - 1: 3}
    return pl.pallas_call(
        functools.partial(_recurrent_mixers_kernel, seq_len, stacked is not None),
        grid=(batch, nc),
        in_specs=in_specs,
        out_specs=[
            pl.BlockSpec((C, W), lambda b, c: (b * nc + c, 0)),
            pl.BlockSpec((None, HG_HEADS, LANE, LANE), lambda b, c: (b, 0, 0, 0)),
            dspec(W, 0), sspec, dspec(XW, 0),
        ],
        out_shape=[
            jax.ShapeDtypeStruct((batch * T, W), BF16),
            jax.ShapeDtypeStruct((batch, HG_HEADS, LANE, LANE), F32),
            jax.ShapeDtypeStruct((nseq * seq_len, W), BF16),
            jax.ShapeDtypeStruct(state.shape, F32),
            jax.ShapeDtypeStruct((nseq * seq_len, XW), BF16),
        ],
        input_output_aliases=aliases,
        compiler_params=_params(("arbitrary", "arbitrary"), 48),
        name="recurrent_mixers",
    )(*args)


def _pool_prompt_kernel(u_ref, wp_ref, sc_ref, op_ref, nb_ref):
    u = u_ref[...]
    T = u.shape[0]
    row = lax.broadcasted_iota(jnp.int32, (T, LANE), 0)
    for g, w in enumerate(POOL_WINDOWS):
        cs = slice(g * LANE, (g + 1) * LANE)
        ug = u[:, cs]
        s = ug
        d = 1
        while d < w:
            s = s + jnp.where(row >= d, pltpu.roll(s, d, 0), 0.0)
            d *= 2
        cnt = jnp.minimum(row + 1, w).astype(F32)
        dv = s / cnt - ug
        y = _dot(dv.astype(BF16), wp_ref[g].astype(BF16)) * sc_ref[:, cs]
        op_ref[:, cs] = y.astype(BF16)
    tail = u_ref[T - 16:T, :]
    nb_ref[...] = pltpu.roll(tail, 15, 0)[0:POOL_BUF, :]


def _pool_prompt(zf, w_pool, scale3, layer, batch):
    T = zf.shape[0] // batch
    G = len(POOL_WINDOWS)
    W = G * LANE
    return pl.pallas_call(
        _pool_prompt_kernel,
        grid=(batch,),
        in_specs=[
            pl.BlockSpec((T, W), lambda b: (b, ZF_U)),
            pl.BlockSpec((None, G, LANE, LANE), lambda b: (layer, 0, 0, 0)),
            pl.BlockSpec((None, 1, W), lambda b: (layer, 0, 0)),
        ],
        out_specs=[
            pl.BlockSpec((T, W), lambda b: (b, 0)),
            pl.BlockSpec((None, POOL_BUF, W), lambda b: (b, 0, 0)),
        ],
        out_shape=[
            jax.ShapeDtypeStruct((batch * T, W), BF16),
            jax.ShapeDtypeStruct((batch, POOL_BUF, W), F32),
        ],
        compiler_params=_params(("parallel",), 48),
        name="pool_prompt",
    )(zf, w_pool, scale3)


def _pool_decode_kernel(u_ref, buf_ref, wp_ref, sc_ref, op_ref, nb_ref):
    steps = u_ref.shape[0]
    for g, w in enumerate(POOL_WINDOWS):
        cs = slice(g * LANE, (g + 1) * LANE)
        wg = wp_ref[g].astype(BF16)
        for t in range(steps):
            n_u = min(t + 1, w)
            acc = u_ref[t, :, cs]
            for j in range(t - n_u + 1, t):
                acc = acc + u_ref[j, :, cs]
            for i in range(POOL_BUF - (w - n_u), POOL_BUF):
                acc = acc + buf_ref[i, :, cs]
            dv = acc * (1.0 / w) - u_ref[t, :, cs]
            op_ref[t, :, cs] = _dot(dv.astype(BF16), wg) * sc_ref[:, cs]
    for i in range(POOL_BUF - steps):
        nb_ref[i] = buf_ref[i + steps]
    for t in range(steps):
        nb_ref[POOL_BUF - steps + t] = u_ref[t]


def _pool_decode(u_t, buf_t, w_pool, scale3, layer):
    steps, nseq, W = u_t.shape
    G = len(POOL_WINDOWS)
    return pl.pallas_call(
        _pool_decode_kernel,
        grid=(1,),
        in_specs=[
            pl.BlockSpec((steps, nseq, W), lambda i: (0, 0, 0)),
            pl.BlockSpec((POOL_BUF, nseq, W), lambda i: (0, 0, 0)),
            pl.BlockSpec((None, G, LANE, LANE), lambda i: (layer, 0, 0, 0)),
            pl.BlockSpec((None, 1, W), lambda i: (layer, 0, 0)),
        ],
        out_specs=[
            pl.BlockSpec((steps, nseq, W), lambda i: (0, 0, 0)),
            pl.BlockSpec((POOL_BUF, nseq, W), lambda i: (0, 0, 0)),
        ],
        out_shape=[
            jax.ShapeDtypeStruct((steps, nseq, W), F32),
            jax.ShapeDtypeStruct((POOL_BUF, nseq, W), F32),
        ],
        compiler_params=_params(("arbitrary",), 32),
        name="pool_decode",
    )(u_t, buf_t, w_pool, scale3)


def _softmax_rows(s):
    e = jnp.exp(s - jnp.max(s, axis=-1, keepdims=True))
    return e / jnp.sum(e, axis=-1, keepdims=True)


def _xattn_prompt_kernel(q_ref, k_ref, v_ref, o_ref):
    scale = LANE ** -0.5
    for h in range(XA_HEADS):
        cs = slice(h * LANE, (h + 1) * LANE)
        s = _dot_nt(q_ref[:, cs].astype(BF16), k_ref[:, cs].astype(BF16)) * scale
        p = _softmax_rows(s)
        o_ref[:, cs] = _dot(p.astype(BF16), v_ref[:, cs].astype(BF16)).astype(BF16)


def _xattn_prompt(zf, mk, mv, batch, tq):
    T = zf.shape[0] // batch
    W = XA_HEADS * LANE
    nq = T // tq
    n_mem = mk.shape[1]
    mspec = pl.BlockSpec((None, n_mem, W), lambda b, i: (b, 0, 0))
    return pl.pallas_call(
        _xattn_prompt_kernel,
        grid=(batch, nq),
        in_specs=[pl.BlockSpec((tq, W), lambda b, i: (b * nq + i, ZF_X)), mspec, mspec],
        out_specs=pl.BlockSpec((tq, W), lambda b, i: (b * nq + i, 0)),
        out_shape=jax.ShapeDtypeStruct((batch * T, W), BF16),
        compiler_params=_params(("parallel", "parallel"), 48),
        name="xattn_prompt",
    )(zf, mk, mv)


def _xattn_decode_kernel(seq_len, q_ref, k_ref, v_ref, o_ref):
    scale = LANE ** -0.5
    rows = DEC_GROUP * seq_len
    lseq = seq_len.bit_length() - 1
    lrows = rows.bit_length() - 1
    nk = k_ref.shape[1]
    row_s = lax.broadcasted_iota(jnp.int32, (XA_HEADS * rows, nk), 0)
    col_s = lax.broadcasted_iota(jnp.int32, (XA_HEADS * rows, nk), 1)
    own_head = (col_s & (XA_HEADS - 1)) == (row_s >> lrows)
    seq_s = (row_s & (rows - 1)) >> lseq
    seq_o = (lax.broadcasted_iota(jnp.int32, (XA_HEADS * rows, LANE), 0) & (rows - 1)) >> lseq
    for grp in range(k_ref.shape[0] // DEC_GROUP):
        rs = slice(grp * rows, (grp + 1) * rows)
        qb = jnp.concatenate([q_ref[rs, h * LANE:(h + 1) * LANE] for h in range(XA_HEADS)], axis=0).astype(BF16)
        s = jnp.full((XA_HEADS * rows, nk), -jnp.inf, F32)
        for k in range(DEC_GROUP):
            sk = _dot_nt(qb, k_ref[grp * DEC_GROUP + k].astype(BF16))
            s = jnp.where(own_head & (seq_s == k), sk, s)
        p = _softmax_rows(s * scale).astype(BF16)
        o = jnp.zeros((XA_HEADS * rows, LANE), F32)
        for k in range(DEC_GROUP):
            o = jnp.where(seq_o == k, _dot(p, v_ref[grp * DEC_GROUP + k].astype(BF16)), o)
        for h in range(XA_HEADS):
            o_ref[rs, h * LANE:(h + 1) * LANE] = o[h * rows:(h + 1) * rows, :].astype(BF16)


def _merge_kernel(x_ref, oh_ref, op_ref, ox_ref, g0_ref, g1_ref, g2_ref, wb_ref, wo_ref, o_ref, *rest):
    y_ref = rest[-1]
    wbb_ref, wob_ref = rest[:2] if len(rest) == 3 else (None, None)
    j = pl.program_id(1)
    nj, _, tn = y_ref.shape

    @pl.when(j < nj)
    def _():
        wh = oh_ref.shape[1]
        wp = op_ref.shape[1]
        oh, op, ox = oh_ref[...].astype(BF16), op_ref[...].astype(BF16), ox_ref[...].astype(BF16)
        cw = min(MXU_COLS, tn)
        for c in range(tn // cw):
            cs = slice(c * cw, (c + 1) * cw)
            w = wb_ref[:, cs].astype(BF16)
            if wbb_ref is not None:
                wbb_ref[:, cs] = w
            y = _sigmoid(g0_ref[:, cs].astype(F32)) * _dot(oh, w[0:wh, :])
            y += _sigmoid(g1_ref[:, cs].astype(F32)) * _dot(op, w[wh:wh + wp, :])
            y += _sigmoid(g2_ref[:, cs].astype(F32)) * _dot(ox, w[wh + wp:, :])
            y_ref[j, :, cs] = y.astype(BF16)

    @pl.when(j >= nj)
    def _():
        acc = x_ref[...]
        for c in range(nj):
            rs = slice(c * tn, (c + 1) * tn)
            w = wo_ref[rs, :].astype(BF16)
            if wob_ref is not None:
                wob_ref[rs, :] = w
            acc += _dot(y_ref[c], w)
        o_ref[...] = acc


def _merge_resident_kernel(x_ref, oh_ref, op_ref, ox_ref, g_ref, wb_ref, wo_ref, o_ref, y_ref):
    D = x_ref.shape[1]
    wh = oh_ref.shape[1]
    wp = op_ref.shape[1]
    oh, op, ox = oh_ref[...].astype(BF16), op_ref[...].astype(BF16), ox_ref[...].astype(BF16)
    cw = min(2 * MXU_COLS, D)
    for c in range(D // cw):
        cs = slice(c * cw, (c + 1) * cw)
        gate = lambda k: _sigmoid(g_ref[:, k * D + c * cw:k * D + (c + 1) * cw].astype(F32))
        y = gate(0) * _dot(oh, wb_ref[0:wh, cs])
        y += gate(1) * _dot(op, wb_ref[wh:wh + wp, cs])
        y += gate(2) * _dot(ox, wb_ref[wh + wp:, cs])
        y_ref[:, cs] = y.astype(BF16)
    for c in range(D // cw):
        cs = slice(c * cw, (c + 1) * cw)
        o_ref[:, cs] = x_ref[:, cs] + _dot(y_ref[...], wo_ref[:, cs])


def _merge_resident(x, oh, op, ox, zg, wb, wo, tm):
    T, D = x.shape
    row = lambda a: pl.BlockSpec((tm, a.shape[1]), lambda i: (i, 0))
    held = lambda a: pl.BlockSpec(a.shape, lambda i: (0, 0), pipeline_mode=pl.Buffered(1))
    return pl.pallas_call(
        _merge_resident_kernel,
        grid=(T // tm,),
        in_specs=[row(x), row(oh), row(op), row(ox), row(zg), held(wb), held(wo)],
        out_specs=row(x),
        out_shape=jax.ShapeDtypeStruct((T, D), F32),
        scratch_shapes=[pltpu.VMEM((tm, D), BF16)],
        compiler_params=_params(("parallel",), 56),
        name="merge_resident",
    )(x, oh, op, ox, zg, wb, wo)


def _merge(x, oh, op, ox, zb, wb, wo, layer, tm, tn, emit=False):
    T, D = x.shape
    assert not emit or T == tm
    nj = D // tn
    first = lambda j: jnp.minimum(j, nj - 1)
    second = lambda j: jnp.maximum(j - nj, 0)
    full = lambda a: pl.BlockSpec((tm, a.shape[1]), lambda i, j: (i, 0))
    gate = lambda k: pl.BlockSpec((tm, tn), lambda i, j: (i, k * nj + first(j)))
    wb_idx = lambda i, j: (0, first(j))
    wo_idx = lambda i, j: (0, second(j))
    rows_b = wb.shape[-2]
    out_specs = [pl.BlockSpec((tm, tn), lambda i, j: (i, second(j)))]
    out_shape = [jax.ShapeDtypeStruct((T, D), F32)]
    if emit:
        out_specs += [pl.BlockSpec((rows_b, tn), wb_idx), pl.BlockSpec((D, tn), wo_idx)]
        out_shape += [jax.ShapeDtypeStruct((rows_b, D), BF16), jax.ShapeDtypeStruct((D, D), BF16)]
    out = pl.pallas_call(
        _merge_kernel,
        grid=(T // tm, 2 * nj),
        in_specs=[
            pl.BlockSpec((tm, tn), lambda i, j: (i, second(j))),
            full(oh), full(op), full(ox), gate(0), gate(1), gate(2),
            _layer_spec(wb, layer, (rows_b, tn), wb_idx),
            _layer_spec(wo, layer, (D, tn), wo_idx),
        ],
        out_specs=out_specs,
        out_shape=out_shape,
        scratch_shapes=[pltpu.VMEM((nj, tm, tn), BF16)],
        compiler_params=_params(("parallel", "arbitrary"), 48),
        name="merge_out",
    )(x, oh, op, ox, zb, zb, zb, wb, wo)
    return out if emit else out[0]


def kernel(x_prompt, x_sample, state_hgrn, state_pool, cache_mem_k, cache_mem_v, mem_prompt, ffn1_norm, ffn1_w1, ffn1_w3, ffn1_w2, mix_norm, w_in, lb_logits, hg_norm, w_pool, pool_scale, mem_norm, w_mk, w_mv, w_branch, w_out, ffn2_norm, ffn2_w1, ffn2_w3, ffn2_w2, final_norm):
    B, T, D = x_prompt.shape
    nseq, steps, _ = x_sample.shape
    depth = w_in.shape[0]
    n_mem = mem_prompt.shape[1]
    xw = XA_HEADS * LANE
    pw = len(POOL_WINDOWS) * LANE

    tm_p = min(1024, B * T)
    tm_s = min(512, nseq * steps)

    vec3 = lambda a: a.reshape(a.shape[0], 1, a.shape[1])
    f1n, f2n = vec3(ffn1_norm), vec3(ffn2_norm)
    mix3, mem3, hgn3, psc3 = vec3(mix_norm), vec3(mem_norm), vec3(hg_norm), vec3(pool_scale)
    lbs3 = vec3(_lower_bounds(lb_logits))

    xp = x_prompt.reshape(B * T, D)
    xs = x_sample.reshape(nseq * steps, D)
    mem2 = mem_prompt.reshape(B * n_mem, D)
    ck = cache_mem_k.reshape(depth, nseq, n_mem * XA_HEADS, LANE)
    cv = cache_mem_v.reshape(depth, nseq, n_mem * XA_HEADS, LANE)

    tf = _tile(ffn1_w1.shape[-1], 512)
    tf_s = _tile(ffn1_w1.shape[-1], 256)
    tn_in = _tile(math.gcd(HG_HEADS * LANE, N_BRANCH * D), 1024)
    tn_mrg = _tile(D, 512)
    tm_mem = min(512, B * n_mem)
    hs_p, pb_p, mk_p, mv_p, pb_s = [], [], [], [], []
    hs_s = None

    blocks = []
    for l in range(depth):
        blocks.append((f1n, ffn1_w1, ffn1_w3, ffn1_w2, l))
        blocks.append((f2n, ffn2_w1, ffn2_w3, ffn2_w2, l))

    def swiglu_pair(xp, xs, wbf, k, final_g):
        norm, w1, w3, w2, l = blocks[k]
        if wbf is None:
            xs, *wbf = _ffn(xs, norm, w1, w3, w2, l, tm_s, tf_s, final_g)
        else:
            xs = _ffn(xs, norm, *wbf, l, tm_s, tf, final_g)
        if k + 1 < len(blocks):
            xp, *nxt = _ffn(xp, norm, *wbf, l, tm_p, tf, final_g, side=blocks[k + 1][1:], nested=True)
        else:
            xp, nxt = _ffn(xp, norm, *wbf, l, tm_p, tf, final_g, nested=True), None
        return xp, xs, nxt

    wbf = w_in_b = None
    for l in range(depth):
        mk = _rms_matmul(mem2, mem3, w_mk, l, tm_mem, xw)
        mv = _rms_matmul(mem2, mem3, w_mv, l, tm_mem, xw)
        mk_p.append(mk.reshape(B, n_mem, XA_HEADS, LANE))
        mv_p.append(mv.reshape(B, n_mem, XA_HEADS, LANE))

        xp, xs, wbf = swiglu_pair(xp, xs, wbf, 2 * l, None)

        if w_in_b is None:
            zsb, zsf, zsg, w_in_b = _in_proj(xs, mix3, w_in, l, tm_s, tn_in, emit=True)
        else:
            zsb, zsf, zsg = _in_proj(xs, mix3, w_in_b, l, tm_s, tn_in)
        if l + 1 < depth:
            zpb, zpf, zpg, w_in_next = _in_proj(xp, mix3, w_in_b, l, tm_p, tn_in, side=(w_in, l + 1))
        else:
            (zpb, zpf, zpg), w_in_next = _in_proj(xp, mix3, w_in_b, l, tm_p, tn_in), None
        w_in_b = w_in_next

        ohp, sp, ohs, hs_s, oxs = _recurrent_mixers(zpb, zpf, zsb, zsf, lbs3, hgn3, state_hgrn, hs_s, ck, cv, l, B, nseq)
        hs_p.append(sp)

        opp, bp = _pool_prompt(zpf, w_pool, psc3, l, B)
        pb_p.append(bp)
        u_t = zsf[:, ZF_U * pw:(ZF_U + 1) * pw].reshape(nseq, steps, pw).transpose(1, 0, 2)
        buf_t = state_pool[l].transpose(1, 0, 2)
        ops_t, nb_t = _pool_decode(u_t, buf_t, w_pool, psc3, l)
        ops = ops_t.transpose(1, 0, 2).reshape(nseq * steps, pw)
        pb_s.append(nb_t.transpose(1, 0, 2))

        oxp = _xattn_prompt(zpf, mk.reshape(B, n_mem, xw), mv.reshape(B, n_mem, xw), B, min(1024, T))

        xs, w_br_b, w_out_b = _merge(xs, ohs, ops, oxs, zsg, w_branch, w_out, l, tm_s, tn_mrg, emit=True)
        xp = _merge_resident(xp, ohp, opp, oxp, zpg, w_br_b, w_out_b, min(512, B * T))

        fn2 = final_norm.reshape(1, D) if l == depth - 1 else None
        xp, xs, wbf = swiglu_pair(xp, xs, wbf, 2 * l + 1, fn2)

    y_prompt = xp.reshape(B, T, D)
    y_sample = xs.reshape(nseq, steps, D)
    return (y_prompt, y_sample, jnp.stack(hs_p), jnp.stack(pb_p), jnp.stack(mk_p), jnp.stack(mv_p),
            hs_s, jnp.stack(pb_s))
```

```python
import functools
import math

import jax
import jax.numpy as jnp
from jax import lax
from jax.experimental import pallas as pl
from jax.experimental.pallas import tpu as pltpu

F32 = jnp.float32
BF16 = jnp.bfloat16
EPS = 1e-6

LANE = 128
SUBLANE = 8
MXU_COLS = 256
HG_HEADS = 8
POOL_WINDOWS = (2, 4, 8, 16)
POOL_BUF = max(POOL_WINDOWS) - 1
XA_HEADS = 4
N_BRANCH = 3

COL_Q, COL_F, COL_I, COL_OG = 0, HG_HEADS, 2 * HG_HEADS, 3 * HG_HEADS
COL_U = 4 * HG_HEADS
COL_X = COL_U + len(POOL_WINDOWS)
COL_G = COL_X + XA_HEADS
ZB_Q, ZB_V, ZB_OG = 0, 1, 2
ZF_F = 0
ZF_U, ZF_X = 2, 3

HGRN_CHUNK = 128
HGRN_STEP_CHUNKS = 2
DEC_GROUP = 4


def _tile(n, preferred):
    t = preferred
    while n % t:
        t -= LANE
    return t


def _params(semantics, vmem_mib):
    return pltpu.CompilerParams(dimension_semantics=semantics, vmem_limit_bytes=vmem_mib * 1024 * 1024)


def _rms(x, g):
    return x * lax.rsqrt(jnp.mean(x * x, axis=-1, keepdims=True) + EPS) * g


def _sigmoid(x):
    return 0.5 * jnp.tanh(0.5 * x) + 0.5


def _silu(x):
    return x * _sigmoid(x)


def _dot(a, b):
    return jnp.dot(a, b, preferred_element_type=F32)


def _dot_nt(a, b):
    return lax.dot_general(a, b, (((1,), (1,)), ((), ())), preferred_element_type=F32)


def _dot_tn(a, b):
    return lax.dot_general(a, b, (((0,), (0,)), ((), ())), preferred_element_type=F32)


def _ffn_kernel(final, cast, side, x_ref, g_ref, w1_ref, w3_ref, w2_ref, *rest):
    rest = list(rest)
    xn_ref = rest.pop()
    fn_ref = rest.pop(0) if final else None
    side_in = [rest.pop(0) for _ in range(3)] if side else []
    o_ref = rest.pop(0)
    if cast:
        src = (w1_ref, w3_ref, w2_ref)
        w1_ref, w3_ref, w2_ref = [rest.pop(0) for _ in range(3)]
        for dst, s in zip((w1_ref, w3_ref, w2_ref), src):
            dst[...] = s[...].astype(BF16)
    j = pl.program_id(1)

    @pl.when(j == 0)
    def _():
        x = x_ref[...]
        xn_ref[...] = _rms(x, g_ref[...]).astype(BF16)
        o_ref[...] = x

    xn = xn_ref[...]
    hw = min(MXU_COLS, w1_ref.shape[1])
    h = jnp.concatenate(
        [(_silu(_dot(xn, w1_ref[:, c * hw:(c + 1) * hw])) * _dot(xn, w3_ref[:, c * hw:(c + 1) * hw])).astype(BF16)
         for c in range(w1_ref.shape[1] // hw)], axis=1)
    for s, dst in zip(side_in, rest):
        dst[...] = s[...].astype(BF16)
    cw = min(w2_ref.shape[0], o_ref.shape[1])
    for n in range(o_ref.shape[1] // cw):
        cs = slice(n * cw, (n + 1) * cw)
        o_ref[:, cs] += 0.5 * _dot(h, w2_ref[:, cs])

    if final:
        @pl.when(j == pl.num_programs(1) - 1)
        def _():
            o_ref[...] = _rms(o_ref[...], fn_ref[...])


def _layer_spec(w, layer, block, index):
    if w.ndim == 2:
        return pl.BlockSpec(block, index)
    return pl.BlockSpec((None,) + block, lambda i, j: (layer,) + index(i, j))


def _ffn(x, g3, w1, w3, w2, layer, tm, tf, final_g=None, side=None):
    T, D = x.shape
    F = w1.shape[-1]
    cast = w1.dtype != BF16
    final = final_g is not None
    assert not cast or T == tm
    ni, nj = T // tm, F // tf
    col = lambda i, j: (0, j)
    row = lambda i, j: (j, 0)
    out_specs = [pl.BlockSpec((tm, D), lambda i, j: (i, 0))]
    out_shape = [jax.ShapeDtypeStruct((T, D), F32)]
    wshapes = [jax.ShapeDtypeStruct((D, F), BF16), jax.ShapeDtypeStruct((D, F), BF16),
               jax.ShapeDtypeStruct((F, D), BF16)]
    if cast:
        out_specs += [pl.BlockSpec((D, tf), col), pl.BlockSpec((D, tf), col), pl.BlockSpec((tf, D), row)]
        out_shape += wshapes
    in_specs = [
        pl.BlockSpec((tm, D), lambda i, j: (i, 0)),
        pl.BlockSpec((None, 1, D), lambda i, j: (layer, 0, 0)),
        _layer_spec(w1, layer, (D, tf), col),
        _layer_spec(w3, layer, (D, tf), col),
        _layer_spec(w2, layer, (tf, D), row),
    ]
    args = [x, g3, w1, w3, w2]
    if final:
        in_specs.append(pl.BlockSpec((1, D), lambda i, j: (0, 0)))
        args.append(final_g)
    if side is not None:
        s1, s3, s2, side_layer = side
        assert D % (ni * LANE) == 0
        rb = D // ni
        up, down = (lambda i, j: (i, j)), (lambda i, j: (j, i))
        in_specs += [_layer_spec(s1, side_layer, (rb, tf), up), _layer_spec(s3, side_layer, (rb, tf), up),
                     _layer_spec(s2, side_layer, (tf, rb), down)]
        args += [s1, s3, s2]
        out_specs += [pl.BlockSpec((rb, tf), up), pl.BlockSpec((rb, tf), up), pl.BlockSpec((tf, rb), down)]
        out_shape += wshapes
    out = pl.pallas_call(
        functools.partial(_ffn_kernel, final, cast, side is not None),
        grid=(ni, nj),
        in_specs=in_specs,
        out_specs=out_specs,
        out_shape=out_shape,
        scratch_shapes=[pltpu.VMEM((tm, D), BF16)],
        compiler_params=_params(("parallel", "arbitrary"), 62),
        name="ffn_cast" if cast else "ffn",
    )(*args)
    return out if len(out) > 1 else out[0]


def _rms_matmul_kernel(x_ref, g_ref, w_ref, o_ref, *rest):
    xn_ref = rest[-1]

    @pl.when(pl.program_id(1) == 0)
    def _():
        xn_ref[...] = _rms(x_ref[...], g_ref[...]).astype(BF16)

    w = w_ref[...].astype(BF16)
    if len(rest) == 2:
        rest[0][...] = w
    o_ref[...] = _dot(xn_ref[...], w)


def _rms_matmul(x, g3, w, layer, tm, tn, emit=False):
    T, D = x.shape
    N = w.shape[-1]
    assert not emit or T == tm
    col = lambda i, j: (0, j)
    out_specs = [pl.BlockSpec((tm, tn), lambda i, j: (i, j))]
    out_shape = [jax.ShapeDtypeStruct((T, N), F32)]
    if emit:
        out_specs.append(pl.BlockSpec((D, tn), col))
        out_shape.append(jax.ShapeDtypeStruct((D, N), BF16))
    out = pl.pallas_call(
        _rms_matmul_kernel,
        grid=(T // tm, N // tn),
        in_specs=[
            pl.BlockSpec((tm, D), lambda i, j: (i, 0)),
            pl.BlockSpec((None, 1, D), lambda i, j: (layer, 0, 0)),
            _layer_spec(w, layer, (D, tn), col),
        ],
        out_specs=out_specs,
        out_shape=out_shape,
        scratch_shapes=[pltpu.VMEM((tm, D), BF16)],
        compiler_params=_params(("parallel", "arbitrary"), 48),
        name="rms_matmul",
    )(x, g3, w)
    return out if emit else out[0]


def _runs(seq):
    runs = []
    for j, v in enumerate(seq):
        if runs and runs[-1][2] == v - j:
            runs[-1][1] = j + 1
        else:
            runs.append([j, j + 1, v - j])
    return [tuple(r) for r in runs]


def _in_runs(j, runs):
    hit = None
    for lo, hi, _ in runs:
        c = (j >= lo) & (j < hi)
        hit = c if hit is None else hit | c
    return hit


def _lookup(j, runs):
    out = 0
    for lo, hi, off in runs:
        out = out + jnp.where((j >= lo) & (j < hi), j + off, 0)
    return out


def _in_proj_plan(d_model, tn):
    t = lambda blocks: blocks * LANE // tn
    nq, nux, ng = t(HG_HEADS), t(len(POOL_WINDOWS) + XA_HEADS), N_BRANCH * d_model // tn
    src = lambda start, n: list(range(t(start), t(start) + n))
    order = src(COL_Q, nq) + src(COL_I, nq) + src(COL_OG, nq) + src(COL_F, nq) + src(COL_U, nux) + src(COL_G, ng)
    nh, n32 = 3 * nq, nq + nux
    kinds = {"f32": [(nh, nh + n32, 0)]}
    return _runs(order), kinds, nh, n32


def _in_proj_kernel(kinds, side, x_ref, g_ref, w_ref, *rest):
    rest = list(rest)
    xn_ref = rest.pop()
    side_in, side_out = (rest.pop(0), rest.pop()) if side else (None, None)
    zh_ref, zf_ref, zg_ref = rest[:3]
    rest = rest[3:] + [xn_ref]
    j = pl.program_id(1)

    @pl.when(j == 0)
    def _():
        xn_ref[...] = _rms(x_ref[...], g_ref[...]).astype(BF16)

    xn = xn_ref[...]
    tn = w_ref.shape[1]
    cw = min(MXU_COLS, tn)
    zs = []
    for c in range(tn // cw):
        cs = slice(c * cw, (c + 1) * cw)
        w = w_ref[:, cs].astype(BF16)
        if len(rest) == 2:
            rest[0][:, cs] = w
        z = _dot(xn, w)
        zs.append(z)
        zh_ref[:, cs] = z.astype(BF16)
        zg_ref[:, cs] = z.astype(BF16)
    if side:
        side_out[...] = side_in[...].astype(BF16)

    @pl.when(_in_runs(j, kinds["f32"]))
    def _():
        for c, z in enumerate(zs):
            zf_ref[:, c * cw:(c + 1) * cw] = z


def _in_proj(x, g3, w, layer, tm, tn, emit=False, side=None):
    T, D = x.shape
    N = w.shape[-1]
    assert not emit or T == tm
    order, kinds, nh, n32 = _in_proj_plan(D, tn)
    nsteps = N // tn
    wcol = lambda i, j: (0, _lookup(j, order))
    out_specs = [
        pl.BlockSpec((tm, tn), lambda i, j: (i, jnp.minimum(j, nh))),
        pl.BlockSpec((tm, tn), lambda i, j: (i, jnp.clip(j - nh, 0, n32 - 1))),
        pl.BlockSpec((tm, tn), lambda i, j: (i, jnp.maximum(j - nh - n32, 0))),
    ]
    out_shape = [
        jax.ShapeDtypeStruct((T, (nh + 1) * tn), BF16),
        jax.ShapeDtypeStruct((T, n32 * tn), F32),
        jax.ShapeDtypeStruct((T, (nsteps - nh - n32) * tn), BF16),
    ]
    if emit:
        out_specs.append(pl.BlockSpec((D, tn), wcol))
        out_shape.append(jax.ShapeDtypeStruct((D, N), BF16))
    in_specs = [
        pl.BlockSpec((tm, D), lambda i, j: (i, 0)),
        pl.BlockSpec((None, 1, D), lambda i, j: (layer, 0, 0)),
        _layer_spec(w, layer, (D, tn), wcol),
    ]
    args = [x, g3, w]
    if side is not None:
        rb = D // (T // tm)
        assert rb % SUBLANE == 0
        in_specs.append(_layer_spec(side[0], side[1], (rb, tn), lambda i, j: (i, j)))
        args.append(side[0])
        out_specs.append(pl.BlockSpec((rb, tn), lambda i, j: (i, j)))
        out_shape.append(jax.ShapeDtypeStruct((D, N), BF16))
    return pl.pallas_call(
        functools.partial(_in_proj_kernel, kinds, side is not None),
        grid=(T // tm, nsteps),
        in_specs=in_specs,
        out_specs=out_specs,
        out_shape=out_shape,
        scratch_shapes=[pltpu.VMEM((tm, D), BF16)],
        compiler_params=_params(("parallel", "arbitrary"), 58),
        name="in_proj",
    )(*args)


def _lower_bound_kernel(lg_ref, o_ref):
    lg = lg_ref[...]
    depth = lg.shape[0]
    rows = [lg[i:i + 1, :] for i in range(depth)]
    m = rows[0]
    for r in rows[1:]:
        m = jnp.maximum(m, r)
    e = [jnp.exp(r - m) for r in rows]
    tot = e[0]
    for v in e[1:]:
        tot = tot + v
    c = e[0] / tot
    first = c
    o_ref[0:1, :] = c - first
    for i in range(1, depth):
        c = c + e[i] / tot
        o_ref[i:i + 1, :] = c - first


def _lower_bounds(lb_logits):
    return pl.pallas_call(
        _lower_bound_kernel,
        out_shape=jax.ShapeDtypeStruct(lb_logits.shape, F32),
        name="hgrn_lower_bounds",
    )(lb_logits)


def _neg_abs(x):
    bits = lax.bitcast_convert_type(x, jnp.uint32) | jnp.uint32(0x80000000)
    return lax.bitcast_convert_type(bits, F32)


def _group_ref_row(b, s):
    rows, width = b.shape
    gsz = 2 * s
    if gsz >= 8:
        parts = [jnp.broadcast_to(b[i * gsz + s - 1:i * gsz + s, :], (gsz, width)) for i in range(rows // gsz)]
        return parts[0] if len(parts) == 1 else jnp.concatenate(parts, axis=0)
    pos = lax.broadcasted_iota(jnp.int32, b.shape, 0) & (gsz - 1)
    out = b
    for p in range(gsz):
        d = p - (s - 1)
        if d != 0:
            out = jnp.where(pos == p, pltpu.roll(b, d % rows, 0), out)
    return out


def _hgrn_wide(q, zf, vb, lb, states, seg):
    C, W = q.shape
    heads = [slice(h * LANE, (h + 1) * LANE) for h in range(W // LANE)]
    nseg = C // seg
    fg = lb + (1.0 - lb) * jax.nn.sigmoid(zf)
    kk = 1.0 - fg
    g = jnp.log2(fg)

    row = lax.broadcasted_iota(jnp.int32, (C, W), 0)
    rr = lax.broadcasted_iota(jnp.int32, (C, C), 0)
    cc = lax.broadcasted_iota(jnp.int32, (C, C), 1)
    lseg = seg.bit_length() - 1
    tri = jnp.where((cc <= rr) & ((rr >> lseg) == (cc >> lseg)), 1.0, 0.0).astype(F32)
    b = jnp.dot(tri, g, precision=lax.Precision.HIGHEST, preferred_element_type=F32)

    qb, kb = q.astype(BF16), kk.astype(BF16)
    a = [jnp.where(rr == cc, _dot_nt(qb[:, hs], kb[:, hs]), 0.0) for hs in heads]
    s = seg // 2
    while s >= 1:
        ls = s.bit_length() - 1
        same = (rr >> (ls + 1)) == (cc >> (ls + 1))
        if s == 1:
            right = (row & 1) == 1
            qs, ks = jnp.where(right, q * fg, 0.0).astype(BF16), jnp.where(right, 0.0, kk).astype(BF16)
            for h, hs in enumerate(heads):
                a[h] = a[h] + jnp.where(same, _dot_nt(qs[:, hs], ks[:, hs]), 0.0)
            break
        e = jnp.exp2(_neg_abs(b - _group_ref_row(b, s)))
        if s % SUBLANE == 0:
            nblk = C // s
            zero = jnp.zeros((s, W), F32)
            blk = lambda x, i: x[i * s:(i + 1) * s, :]
            qs = jnp.concatenate([blk(q, i) * blk(e, i) if i % 2 else zero for i in range(nblk)], axis=0).astype(BF16)
            ks = jnp.concatenate([zero if i % 2 else blk(kk, i) * blk(e, i) for i in range(nblk)], axis=0).astype(BF16)
            for h, hs in enumerate(heads):
                p = _dot_nt(qs[:, hs], ks[:, hs])
                pieces = []
                for i in range(nblk):
                    if i % 2 == 0:
                        pieces.append(blk(a[h], i))
                    elif 2 * s == C:
                        pieces.append(blk(a[h], i) + blk(p, i))
                    else:
                        pieces.append(blk(a[h], i) + jnp.where(blk(same, i), blk(p, i), 0.0))
                a[h] = jnp.concatenate(pieces, axis=0)
        else:
            right = ((row >> ls) & 1) == 1
            eq = jnp.where(right, e, 0.0)
            qs, ks = (q * eq).astype(BF16), (kk * (e - eq)).astype(BF16)
            for h, hs in enumerate(heads):
                a[h] = a[h] + jnp.where(same, _dot_nt(qs[:, hs], ks[:, hs]), 0.0)
        s //= 2
    o = [_dot(a[h].astype(BF16), vb[:, hs]) for h, hs in enumerate(heads)]

    qe = q * jnp.exp2(b)
    new_states = [[] for _ in heads]
    for k in range(nseg):
        bl = b[k * seg + seg - 1:k * seg + seg, :]
        if nseg == 1:
            qk = qe
            ke = kk * jnp.exp2(bl - b)
        else:
            mine = (row >> lseg) == k
            qk = jnp.where(mine, qe, 0.0)
            ke = jnp.where(mine, kk * jnp.exp2(jnp.where(mine, bl - b, 0.0)), 0.0)
        qkb, keb, ebl = qk.astype(BF16), ke.astype(BF16), jnp.exp2(bl)
        for h, hs in enumerate(heads):
            st = states[h][k]
            o[h] = o[h] + _dot(qkb[:, hs], st.astype(BF16))
            decay = jnp.transpose(jnp.broadcast_to(ebl[:, hs], (LANE, LANE)))
            new_states[h].append(st * decay + _dot_tn(keb[:, hs], vb[:, hs]))
    return o, new_states


def _hgrn_finish(o, zog, hgn):
    return (_rms(o, hgn) * _silu(zog.astype(F32))).astype(BF16)


def _hgrn_prompt_kernel(q_ref, zf_ref, v_ref, og_ref, lb_ref, hgn_ref, oh_ref, s_ref):
    @pl.when(pl.program_id(1) == 0)
    def _():
        s_ref[...] = jnp.zeros_like(s_ref)

    states = [[s_ref[h]] for h in range(HG_HEADS)]
    for sub in range(q_ref.shape[0] // HGRN_CHUNK):
        rs = slice(sub * HGRN_CHUNK, (sub + 1) * HGRN_CHUNK)
        q = _silu(q_ref[rs, :].astype(F32))
        o, states = _hgrn_wide(q, zf_ref[rs, :], v_ref[rs, :], lb_ref[...], states, HGRN_CHUNK)
        for h in range(HG_HEADS):
            cs = slice(h * LANE, (h + 1) * LANE)
            oh_ref[rs, cs] = _hgrn_finish(o[h], og_ref[rs, cs], hgn_ref[:, cs])
    for h in range(HG_HEADS):
        s_ref[h] = states[h][0]


def _hgrn_decode_kernel(seq_len, q_ref, zf_ref, v_ref, og_ref, lb_ref, hgn_ref, s_ref, xq_ref, k_ref, vc_ref, *rest):
    oh_ref, so_ref, ox_ref = rest[-3:]
    _xattn_decode_kernel(seq_len, xq_ref, k_ref, vc_ref, ox_ref)
    rows = DEC_GROUP * seq_len
    for grp in range(s_ref.shape[0] // DEC_GROUP):
        rs = slice(grp * rows, (grp + 1) * rows)
        states = [[s_ref[grp * DEC_GROUP + k, h] for k in range(DEC_GROUP)] for h in range(HG_HEADS)]
        o, new = _hgrn_wide(_silu(q_ref[rs, :].astype(F32)), zf_ref[rs, :], v_ref[rs, :], lb_ref[...], states, seq_len)
        for h in range(HG_HEADS):
            cs = slice(h * LANE, (h + 1) * LANE)
            for k in range(DEC_GROUP):
                so_ref[grp * DEC_GROUP + k, h] = new[h][k]
            oh_ref[rs, cs] = _hgrn_finish(o[h], og_ref[rs, cs], hgn_ref[:, cs])


def _recurrent_mixers_kernel(seq_len, has_alias, qp, zfp, vp, ogp, lb, hgn, qs, zfs, vs, ogs, s_ref, xq, k, vc, *rest):
    ohp, sp, ohs, so, oxs = rest[int(has_alias):]
    _hgrn_prompt_kernel(qp, zfp, vp, ogp, lb, hgn, ohp, sp)
    _hgrn_decode_kernel(seq_len, qs, zfs, vs, ogs, lb, hgn, s_ref, xq, k, vc, ohs, so, oxs)


def _recurrent_mixers(zpb, zpf, zsb, zsf, lbs3, hgn3, state, stacked, cache_k, cache_v, layer, batch, nseq):
    T = zpb.shape[0] // batch
    C = HGRN_STEP_CHUNKS * HGRN_CHUNK
    nc = T // C
    nb = nseq // (batch * nc)
    assert nb * batch * nc == nseq and nb % DEC_GROUP == 0
    seq_len = zsb.shape[0] // nseq
    W = HG_HEADS * LANE
    XW = XA_HEADS * LANE
    pspec = lambda col: pl.BlockSpec((C, W), lambda b, c: (b * nc + c, col))
    dspec = lambda width, col: pl.BlockSpec((nb * seq_len, width), lambda b, c: (b * nc + c, col))
    vec = pl.BlockSpec((None, 1, W), lambda b, c: (layer, 0, 0))
    sspec = pl.BlockSpec((None, nb, HG_HEADS, LANE, LANE), lambda b, c: (layer, b * nc + c, 0, 0, 0))
    cspec = pl.BlockSpec((None, nb) + cache_k.shape[2:], lambda b, c: (layer, b * nc + c, 0, 0))
    in_specs = [pspec(ZB_Q), pspec(ZF_F), pspec(ZB_V), pspec(ZB_OG), vec, vec,
                dspec(W, ZB_Q), dspec(W, ZF_F), dspec(W, ZB_V), dspec(W, ZB_OG), sspec, dspec(XW, ZF_X), cspec, cspec]
    args = [zpb, zpf, zpb, zpb, lbs3, hgn3, zsb, zsf, zsb, zsb, state, zsf, cache_k, cache_v]
    aliases = {}
    if stacked is not None:
        in_specs.append(pl.BlockSpec(memory_space=pl.ANY))
        args.append(stacked)
        aliases = {len(args) - 1: 3}
    return pl.pallas_call(
        functools.partial(_recurrent_mixers_kernel, seq_len, stacked is not None),
        grid=(batch, nc),
        in_specs=in_specs,
        out_specs=[
            pl.BlockSpec((C, W), lambda b, c: (b * nc + c, 0)),
            pl.BlockSpec((None, HG_HEADS, LANE, LANE), lambda b, c: (b, 0, 0, 0)),
            dspec(W, 0), sspec, dspec(XW, 0),
        ],
        out_shape=[
            jax.ShapeDtypeStruct((batch * T, W), BF16),
            jax.ShapeDtypeStruct((batch, HG_HEADS, LANE, LANE), F32),
            jax.ShapeDtypeStruct((nseq * seq_len, W), BF16),
            jax.ShapeDtypeStruct(state.shape, F32),
            jax.ShapeDtypeStruct((nseq * seq_len, XW), BF16),
        ],
        input_output_aliases=aliases,
        compiler_params=_params(("arbitrary", "arbitrary"), 48),
        name="recurrent_mixers",
    )(*args)


def _pool_prompt_kernel(u_ref, wp_ref, sc_ref, op_ref, nb_ref):
    u = u_ref[...]
    T = u.shape[0]
    row = lax.broadcasted_iota(jnp.int32, (T, LANE), 0)
    for g, w in enumerate(POOL_WINDOWS):
        cs = slice(g * LANE, (g + 1) * LANE)
        ug = u[:, cs]
        s = ug
        d = 1
        while d < w:
            s = s + jnp.where(row >= d, pltpu.roll(s, d, 0), 0.0)
            d *= 2
        cnt = jnp.minimum(row + 1, w).astype(F32)
        dv = s / cnt - ug
        y = _dot(dv.astype(BF16), wp_ref[g].astype(BF16)) * sc_ref[:, cs]
        op_ref[:, cs] = y.astype(BF16)
    tail = u_ref[T - 16:T, :]
    nb_ref[...] = pltpu.roll(tail, 15, 0)[0:POOL_BUF, :]


def _pool_prompt(zf, w_pool, scale3, layer, batch):
    T = zf.shape[0] // batch
    G = len(POOL_WINDOWS)
    W = G * LANE
    return pl.pallas_call(
        _pool_prompt_kernel,
        grid=(batch,),
        in_specs=[
            pl.BlockSpec((T, W), lambda b: (b, ZF_U)),
            pl.BlockSpec((None, G, LANE, LANE), lambda b: (layer, 0, 0, 0)),
            pl.BlockSpec((None, 1, W), lambda b: (layer, 0, 0)),
        ],
        out_specs=[
            pl.BlockSpec((T, W), lambda b: (b, 0)),
            pl.BlockSpec((None, POOL_BUF, W), lambda b: (b, 0, 0)),
        ],
        out_shape=[
            jax.ShapeDtypeStruct((batch * T, W), BF16),
            jax.ShapeDtypeStruct((batch, POOL_BUF, W), F32),
        ],
        compiler_params=_params(("parallel",), 48),
        name="pool_prompt",
    )(zf, w_pool, scale3)


def _pool_decode_kernel(u_ref, buf_ref, wp_ref, sc_ref, op_ref, nb_ref):
    steps = u_ref.shape[0]
    for g, w in enumerate(POOL_WINDOWS):
        cs = slice(g * LANE, (g + 1) * LANE)
        wg = wp_ref[g].astype(BF16)
        for t in range(steps):
            n_u = min(t + 1, w)
            acc = u_ref[t, :, cs]
            for j in range(t - n_u + 1, t):
                acc = acc + u_ref[j, :, cs]
            for i in range(POOL_BUF - (w - n_u), POOL_BUF):
                acc = acc + buf_ref[i, :, cs]
            dv = acc * (1.0 / w) - u_ref[t, :, cs]
            op_ref[t, :, cs] = _dot(dv.astype(BF16), wg) * sc_ref[:, cs]
    for i in range(POOL_BUF - steps):
        nb_ref[i] = buf_ref[i + steps]
    for t in range(steps):
        nb_ref[POOL_BUF - steps + t] = u_ref[t]


def _pool_decode(u_t, buf_t, w_pool, scale3, layer):
    steps, nseq, W = u_t.shape
    G = len(POOL_WINDOWS)
    return pl.pallas_call(
        _pool_decode_kernel,
        grid=(1,),
        in_specs=[
            pl.BlockSpec((steps, nseq, W), lambda i: (0, 0, 0)),
            pl.BlockSpec((POOL_BUF, nseq, W), lambda i: (0, 0, 0)),
            pl.BlockSpec((None, G, LANE, LANE), lambda i: (layer, 0, 0, 0)),
            pl.BlockSpec((None, 1, W), lambda i: (layer, 0, 0)),
        ],
        out_specs=[
            pl.BlockSpec((steps, nseq, W), lambda i: (0, 0, 0)),
            pl.BlockSpec((POOL_BUF, nseq, W), lambda i: (0, 0, 0)),
        ],
        out_shape=[
            jax.ShapeDtypeStruct((steps, nseq, W), F32),
            jax.ShapeDtypeStruct((POOL_BUF, nseq, W), F32),
        ],
        compiler_params=_params(("arbitrary",), 32),
        name="pool_decode",
    )(u_t, buf_t, w_pool, scale3)


def _softmax_rows(s):
    e = jnp.exp(s - jnp.max(s, axis=-1, keepdims=True))
    return e / jnp.sum(e, axis=-1, keepdims=True)


def _xattn_prompt_kernel(q_ref, k_ref, v_ref, o_ref):
    scale = LANE ** -0.5
    for h in range(XA_HEADS):
        cs = slice(h * LANE, (h + 1) * LANE)
        s = _dot_nt(q_ref[:, cs].astype(BF16), k_ref[:, cs].astype(BF16)) * scale
        p = _softmax_rows(s)
        o_ref[:, cs] = _dot(p.astype(BF16), v_ref[:, cs].astype(BF16)).astype(BF16)


def _xattn_prompt(zf, mk, mv, batch, tq):
    T = zf.shape[0] // batch
    W = XA_HEADS * LANE
    nq = T // tq
    n_mem = mk.shape[1]
    mspec = pl.BlockSpec((None, n_mem, W), lambda b, i: (b, 0, 0))
    return pl.pallas_call(
        _xattn_prompt_kernel,
        grid=(batch, nq),
        in_specs=[pl.BlockSpec((tq, W), lambda b, i: (b * nq + i, ZF_X)), mspec, mspec],
        out_specs=pl.BlockSpec((tq, W), lambda b, i: (b * nq + i, 0)),
        out_shape=jax.ShapeDtypeStruct((batch * T, W), BF16),
        compiler_params=_params(("parallel", "parallel"), 48),
        name="xattn_prompt",
    )(zf, mk, mv)


def _xattn_decode_kernel(seq_len, q_ref, k_ref, v_ref, o_ref):
    scale = LANE ** -0.5
    rows = DEC_GROUP * seq_len
    lseq = seq_len.bit_length() - 1
    lrows = rows.bit_length() - 1
    nk = k_ref.shape[1]
    row_s = lax.broadcasted_iota(jnp.int32, (XA_HEADS * rows, nk), 0)
    col_s = lax.broadcasted_iota(jnp.int32, (XA_HEADS * rows, nk), 1)
    own_head = (col_s & (XA_HEADS - 1)) == (row_s >> lrows)
    seq_s = (row_s & (rows - 1)) >> lseq
    seq_o = (lax.broadcasted_iota(jnp.int32, (XA_HEADS * rows, LANE), 0) & (rows - 1)) >> lseq
    for grp in range(k_ref.shape[0] // DEC_GROUP):
        rs = slice(grp * rows, (grp + 1) * rows)
        qb = jnp.concatenate([q_ref[rs, h * LANE:(h + 1) * LANE] for h in range(XA_HEADS)], axis=0).astype(BF16)
        s = jnp.full((XA_HEADS * rows, nk), -jnp.inf, F32)
        for k in range(DEC_GROUP):
            sk = _dot_nt(qb, k_ref[grp * DEC_GROUP + k].astype(BF16))
            s = jnp.where(own_head & (seq_s == k), sk, s)
        p = _softmax_rows(s * scale).astype(BF16)
        o = jnp.zeros((XA_HEADS * rows, LANE), F32)
        for k in range(DEC_GROUP):
            o = jnp.where(seq_o == k, _dot(p, v_ref[grp * DEC_GROUP + k].astype(BF16)), o)
        for h in range(XA_HEADS):
            o_ref[rs, h * LANE:(h + 1) * LANE] = o[h * rows:(h + 1) * rows, :].astype(BF16)


def _merge_kernel(x_ref, oh_ref, op_ref, ox_ref, g0_ref, g1_ref, g2_ref, wb_ref, wo_ref, o_ref, *rest):
    y_ref = rest[-1]
    wbb_ref, wob_ref = rest[:2] if len(rest) == 3 else (None, None)
    j = pl.program_id(1)
    nj, _, tn = y_ref.shape

    @pl.when(j < nj)
    def _():
        wh = oh_ref.shape[1]
        wp = op_ref.shape[1]
        oh, op, ox = oh_ref[...].astype(BF16), op_ref[...].astype(BF16), ox_ref[...].astype(BF16)
        cw = min(MXU_COLS, tn)
        for c in range(tn // cw):
            cs = slice(c * cw, (c + 1) * cw)
            w = wb_ref[:, cs].astype(BF16)
            if wbb_ref is not None:
                wbb_ref[:, cs] = w
            y = _sigmoid(g0_ref[:, cs].astype(F32)) * _dot(oh, w[0:wh, :])
            y += _sigmoid(g1_ref[:, cs].astype(F32)) * _dot(op, w[wh:wh + wp, :])
            y += _sigmoid(g2_ref[:, cs].astype(F32)) * _dot(ox, w[wh + wp:, :])
            y_ref[j, :, cs] = y.astype(BF16)

    @pl.when(j >= nj)
    def _():
        acc = x_ref[...]
        for c in range(nj):
            rs = slice(c * tn, (c + 1) * tn)
            w = wo_ref[rs, :].astype(BF16)
            if wob_ref is not None:
                wob_ref[rs, :] = w
            acc += _dot(y_ref[c], w)
        o_ref[...] = acc


def _merge_resident_kernel(x_ref, oh_ref, op_ref, ox_ref, g_ref, wb_ref, wo_ref, o_ref, y_ref):
    D = x_ref.shape[1]
    wh = oh_ref.shape[1]
    wp = op_ref.shape[1]
    oh, op, ox = oh_ref[...].astype(BF16), op_ref[...].astype(BF16), ox_ref[...].astype(BF16)
    cw = min(2 * MXU_COLS, D)
    for c in range(D // cw):
        cs = slice(c * cw, (c + 1) * cw)
        gate = lambda k: _sigmoid(g_ref[:, k * D + c * cw:k * D + (c + 1) * cw].astype(F32))
        y = gate(0) * _dot(oh, wb_ref[0:wh, cs])
        y += gate(1) * _dot(op, wb_ref[wh:wh + wp, cs])
        y += gate(2) * _dot(ox, wb_ref[wh + wp:, cs])
        y_ref[:, cs] = y.astype(BF16)
    for c in range(D // cw):
        cs = slice(c * cw, (c + 1) * cw)
        o_ref[:, cs] = x_ref[:, cs] + _dot(y_ref[...], wo_ref[:, cs])


def _merge_resident(x, oh, op, ox, zg, wb, wo, tm):
    T, D = x.shape
    row = lambda a: pl.BlockSpec((tm, a.shape[1]), lambda i: (i, 0))
    held = lambda a: pl.BlockSpec(a.shape, lambda i: (0, 0), pipeline_mode=pl.Buffered(1))
    return pl.pallas_call(
        _merge_resident_kernel,
        grid=(T // tm,),
        in_specs=[row(x), row(oh), row(op), row(ox), row(zg), held(wb), held(wo)],
        out_specs=row(x),
        out_shape=jax.ShapeDtypeStruct((T, D), F32),
        scratch_shapes=[pltpu.VMEM((tm, D), BF16)],
        compiler_params=_params(("parallel",), 56),
        name="merge_resident",
    )(x, oh, op, ox, zg, wb, wo)


def _merge(x, oh, op, ox, zb, wb, wo, layer, tm, tn, emit=False):
    T, D = x.shape
    assert not emit or T == tm
    nj = D // tn
    first = lambda j: jnp.minimum(j, nj - 1)
    second = lambda j: jnp.maximum(j - nj, 0)
    full = lambda a: pl.BlockSpec((tm, a.shape[1]), lambda i, j: (i, 0))
    gate = lambda k: pl.BlockSpec((tm, tn), lambda i, j: (i, k * nj + first(j)))
    wb_idx = lambda i, j: (0, first(j))
    wo_idx = lambda i, j: (0, second(j))
    rows_b = wb.shape[-2]
    out_specs = [pl.BlockSpec((tm, tn), lambda i, j: (i, second(j)))]
    out_shape = [jax.ShapeDtypeStruct((T, D), F32)]
    if emit:
        out_specs += [pl.BlockSpec((rows_b, tn), wb_idx), pl.BlockSpec((D, tn), wo_idx)]
        out_shape += [jax.ShapeDtypeStruct((rows_b, D), BF16), jax.ShapeDtypeStruct((D, D), BF16)]
    out = pl.pallas_call(
        _merge_kernel,
        grid=(T // tm, 2 * nj),
        in_specs=[
            pl.BlockSpec((tm, tn), lambda i, j: (i, second(j))),
            full(oh), full(op), full(ox), gate(0), gate(1), gate(2),
            _layer_spec(wb, layer, (rows_b, tn), wb_idx),
            _layer_spec(wo, layer, (D, tn), wo_idx),
        ],
        out_specs=out_specs,
        out_shape=out_shape,
        scratch_shapes=[pltpu.VMEM((nj, tm, tn), BF16)],
        compiler_params=_params(("parallel", "arbitrary"), 48),
        name="merge_out",
    )(x, oh, op, ox, zb, zb, zb, wb, wo)
    return out if emit else out[0]


def kernel(x_prompt, x_sample, state_hgrn, state_pool, cache_mem_k, cache_mem_v, mem_prompt, ffn1_norm, ffn1_w1, ffn1_w3, ffn1_w2, mix_norm, w_in, lb_logits, hg_norm, w_pool, pool_scale, mem_norm, w_mk, w_mv, w_branch, w_out, ffn2_norm, ffn2_w1, ffn2_w3, ffn2_w2, final_norm):
    B, T, D = x_prompt.shape
    nseq, steps, _ = x_sample.shape
    depth = w_in.shape[0]
    n_mem = mem_prompt.shape[1]
    xw = XA_HEADS * LANE
    pw = len(POOL_WINDOWS) * LANE

    tm_p = min(1024, B * T)
    tm_s = min(512, nseq * steps)

    vec3 = lambda a: a.reshape(a.shape[0], 1, a.shape[1])
    f1n, f2n = vec3(ffn1_norm), vec3(ffn2_norm)
    mix3, mem3, hgn3, psc3 = vec3(mix_norm), vec3(mem_norm), vec3(hg_norm), vec3(pool_scale)
    lbs3 = vec3(_lower_bounds(lb_logits))

    xp = x_prompt.reshape(B * T, D)
    xs = x_sample.reshape(nseq * steps, D)
    mem2 = mem_prompt.reshape(B * n_mem, D)
    ck = cache_mem_k.reshape(depth, nseq, n_mem * XA_HEADS, LANE)
    cv = cache_mem_v.reshape(depth, nseq, n_mem * XA_HEADS, LANE)

    tf = _tile(ffn1_w1.shape[-1], 512)
    tf_s = _tile(ffn1_w1.shape[-1], 256)
    tn_in = _tile(math.gcd(HG_HEADS * LANE, N_BRANCH * D), 1024)
    tn_mrg = _tile(D, 512)
    tm_mem = min(512, B * n_mem)
    hs_p, pb_p, mk_p, mv_p, pb_s = [], [], [], [], []
    hs_s = None

    blocks = []
    for l in range(depth):
        blocks.append((f1n, ffn1_w1, ffn1_w3, ffn1_w2, l))
        blocks.append((f2n, ffn2_w1, ffn2_w3, ffn2_w2, l))

    def swiglu_pair(xp, xs, wbf, k, final_g):
        norm, w1, w3, w2, l = blocks[k]
        if wbf is None:
            xs, *wbf = _ffn(xs, norm, w1, w3, w2, l, tm_s, tf_s, final_g)
        else:
            xs = _ffn(xs, norm, *wbf, l, tm_s, tf, final_g)
        if k + 1 < len(blocks):
            xp, *nxt = _ffn(xp, norm, *wbf, l, tm_p, tf, final_g, side=blocks[k + 1][1:])
        else:
            xp, nxt = _ffn(xp, norm, *wbf, l, tm_p, tf, final_g), None
        return xp, xs, nxt

    wbf = w_in_b = None
    for l in range(depth):
        mk = _rms_matmul(mem2, mem3, w_mk, l, tm_mem, xw)
        mv = _rms_matmul(mem2, mem3, w_mv, l, tm_mem, xw)
        mk_p.append(mk.reshape(B, n_mem, XA_HEADS, LANE))
        mv_p.append(mv.reshape(B, n_mem, XA_HEADS, LANE))

        xp, xs, wbf = swiglu_pair(xp, xs, wbf, 2 * l, None)

        if w_in_b is None:
            zsb, zsf, zsg, w_in_b = _in_proj(xs, mix3, w_in, l, tm_s, tn_in, emit=True)
        else:
            zsb, zsf, zsg = _in_proj(xs, mix3, w_in_b, l, tm_s, tn_in)
        if l + 1 < depth:
            zpb, zpf, zpg, w_in_next = _in_proj(xp, mix3, w_in_b, l, tm_p, tn_in, side=(w_in, l + 1))
        else:
            (zpb, zpf, zpg), w_in_next = _in_proj(xp, mix3, w_in_b, l, tm_p, tn_in), None
        w_in_b = w_in_next

        ohp, sp, ohs, hs_s, oxs = _recurrent_mixers(zpb, zpf, zsb, zsf, lbs3, hgn3, state_hgrn, hs_s, ck, cv, l, B, nseq)
        hs_p.append(sp)

        opp, bp = _pool_prompt(zpf, w_pool, psc3, l, B)
        pb_p.append(bp)
        u_t = zsf[:, ZF_U * pw:(ZF_U + 1) * pw].reshape(nseq, steps, pw).transpose(1, 0, 2)
        buf_t = state_pool[l].transpose(1, 0, 2)
        ops_t, nb_t = _pool_decode(u_t, buf_t, w_pool, psc3, l)
        ops = ops_t.transpose(1, 0, 2).reshape(nseq * steps, pw)
        pb_s.append(nb_t.transpose(1, 0, 2))

        oxp = _xattn_prompt(zpf, mk.reshape(B, n_mem, xw), mv.reshape(B, n_mem, xw), B, min(1024, T))

        xs, w_br_b, w_out_b = _merge(xs, ohs, ops, oxs, zsg, w_branch, w_out, l, tm_s, tn_mrg, emit=True)
        xp = _merge_resident(xp, ohp, opp, oxp, zpg, w_br_b, w_out_b, min(512, B * T))

        fn2 = final_norm.reshape(1, D) if l == depth - 1 else None
        xp, xs, wbf = swiglu_pair(xp, xs, wbf, 2 * l + 1, fn2)

    y_prompt = xp.reshape(B, T, D)
    y_sample = xs.reshape(nseq, steps, D)
    return (y_prompt, y_sample, jnp.stack(hs_p), jnp.stack(pb_p), jnp.stack(mk_p), jnp.stack(mv_p),
            hs_s, jnp.stack(pb_s))
```
